```python
import jax, jax.numpy as jnp
from jax import lax
import numpy as np

D_MODEL = 1024
BATCH = 4
SEQ = 4096
DEPTH = 2
DEC_BATCH = 128
DEC_SEQ = 8
PAST_LEN = 16384
PAGE_SIZE = 128

N_MIXERS = 2
N_ATTN_LAYERS = (DEPTH + 1) // 2
N_RET_LAYERS = DEPTH // 2
PLE_DIM = 256
ROPE_THETA = 10000.0
NORM_EPS = 1e-6
NEG_INF = -1e30
WINDOW = 128
ATTN_HEADS = 16
ATTN_KV_HEADS = 4
ATTN_HEAD_DIM = 64
ATTN_GROUP = ATTN_HEADS // ATTN_KV_HEADS
ATTN_Q_DIM = ATTN_HEADS * ATTN_HEAD_DIM
ATTN_KV_DIM = ATTN_KV_HEADS * ATTN_HEAD_DIM
ATTN_IN_DIM = 2 * ATTN_Q_DIM + 2 * ATTN_KV_DIM
RET_HEADS = 4
RET_QK_DIM = D_MODEL // RET_HEADS
RET_V_DIM = 2 * RET_QK_DIM
RET_VW = RET_HEADS * RET_V_DIM
RET_IN_DIM = 2 * D_MODEL + 2 * RET_VW
RET_CHUNK = 128

kernel_name = 'hybrid_swa_sink_retention_step'


def rms_norm(x, g):
    xf = x.astype(jnp.float32)
    y = xf * lax.rsqrt(jnp.mean(xf * xf, axis=-1, keepdims=True) + NORM_EPS)
    return (y * g.astype(jnp.float32)).astype(x.dtype)


def rope(x, pos):
    half = x.shape[-1] // 2
    inv = ROPE_THETA ** (-jnp.arange(half, dtype=jnp.float32) / half)
    ang = pos.astype(jnp.float32)[:, None] * inv[None, :]
    cos = jnp.cos(ang)[None, :, None, :]
    sin = jnp.sin(ang)[None, :, None, :]
    xf = x.astype(jnp.float32)
    x1, x2 = xf[..., :half], xf[..., half:]
    return jnp.concatenate([x1 * cos - x2 * sin, x2 * cos + x1 * sin], axis=-1).astype(x.dtype)


def sink_softmax(s, sink, mask):
    s = jnp.where(mask, s, NEG_INF)
    m = jnp.maximum(jnp.max(s, axis=-1, keepdims=True), sink)
    p = jnp.exp(s - m)
    return p / (jnp.sum(p, axis=-1, keepdims=True) + jnp.exp(sink - m))


def swa_project(h, w_in, pos):
    B, T, _ = h.shape
    z = h @ w_in
    q, k, v, g = jnp.split(z, [ATTN_Q_DIM, ATTN_Q_DIM + ATTN_KV_DIM, ATTN_Q_DIM + 2 * ATTN_KV_DIM], axis=-1)
    q = rope(q.reshape(B, T, ATTN_HEADS, ATTN_HEAD_DIM), pos)
    k = rope(k.reshape(B, T, ATTN_KV_HEADS, ATTN_HEAD_DIM), pos)
    v = v.reshape(B, T, ATTN_KV_HEADS, ATTN_HEAD_DIM)
    return q, k, v, g


def swa_prompt(h, w_in, sinks, w_out):
    B, T, _ = h.shape
    nb = T // WINDOW
    q, k, v, g = swa_project(h, w_in, jnp.arange(T))
    qb = q.reshape(B, nb, WINDOW, ATTN_KV_HEADS, ATTN_GROUP, ATTN_HEAD_DIM)
    kb = k.reshape(B, nb, WINDOW, ATTN_KV_HEADS, ATTN_HEAD_DIM)
    vb = v.reshape(B, nb, WINDOW, ATTN_KV_HEADS, ATTN_HEAD_DIM)
    kp = jnp.concatenate([jnp.zeros_like(kb[:, :1]), kb], axis=1)
    vp = jnp.concatenate([jnp.zeros_like(vb[:, :1]), vb], axis=1)
    k_band = jnp.concatenate([kp[:, :-1], kp[:, 1:]], axis=2)
    v_band = jnp.concatenate([vp[:, :-1], vp[:, 1:]], axis=2)
    s = jnp.einsum('bnqkgd,bnskd->bnkgqs', qb, k_band, preferred_element_type=jnp.float32) * (ATTN_HEAD_DIM ** -0.5)
    qi = jnp.arange(WINDOW)[:, None] + WINDOW
    kj = jnp.arange(2 * WINDOW)[None, :]
    diff = qi - kj
    band = (diff >= 0) & (diff < WINDOW)
    not_before_start = (jnp.arange(nb)[:, None, None] > 0) | (kj[None] >= WINDOW)
    mask = (band[None] & not_before_start)[None, :, None, None]
    sink = sinks.astype(jnp.float32).reshape(ATTN_KV_HEADS, ATTN_GROUP, 1, 1)
    p = sink_softmax(s, sink, mask).astype(v.dtype)
    o = jnp.einsum('bnkgqs,bnskd->bnqkgd', p, v_band).reshape(B, T, ATTN_Q_DIM)
    y = (o * jax.nn.silu(g)) @ w_out
    return y, k[:, T - WINDOW:], v[:, T - WINDOW:]


def swa_sample(h, k_buf, v_buf, w_in, sinks, w_out):
    B, L, _ = h.shape
    pos = PAST_LEN + jnp.arange(L)
    q, k, v, g = swa_project(h, w_in, pos)
    kc = jnp.concatenate([k_buf, k], axis=1)
    vc = jnp.concatenate([v_buf, v], axis=1)
    kpos = jnp.concatenate([PAST_LEN - WINDOW + jnp.arange(WINDOW), pos])
    diff = pos[:, None] - kpos[None, :]
    mask = ((diff >= 0) & (diff < WINDOW))[None, None, None]
    qg = q.reshape(B, L, ATTN_KV_HEADS, ATTN_GROUP, ATTN_HEAD_DIM)
    s = jnp.einsum('bqkgd,bskd->bkgqs', qg, kc, preferred_element_type=jnp.float32) * (ATTN_HEAD_DIM ** -0.5)
    sink = sinks.astype(jnp.float32).reshape(ATTN_KV_HEADS, ATTN_GROUP, 1, 1)
    p = sink_softmax(s, sink, mask).astype(vc.dtype)
    o = jnp.einsum('bkgqs,bskd->bqkgd', p, vc).reshape(B, L, ATTN_Q_DIM)
    y = (o * jax.nn.silu(g)) @ w_out
    return y, kc[:, -WINDOW:], vc[:, -WINDOW:]


def ret_log_gamma():
    return jnp.log1p(-(2.0 ** (-5.0 - jnp.arange(RET_HEADS, dtype=jnp.float32))))


def ret_project(h, w_in, pos):
    B, T, _ = h.shape
    z = h @ w_in
    q, k, v, g = jnp.split(z, [D_MODEL, 2 * D_MODEL, 2 * D_MODEL + RET_VW], axis=-1)
    q = rope(q.reshape(B, T, RET_HEADS, RET_QK_DIM), pos)
    k = rope(k.reshape(B, T, RET_HEADS, RET_QK_DIM), pos) * (RET_QK_DIM ** -0.5)
    v = v.reshape(B, T, RET_HEADS, RET_V_DIM)
    return q, k, v, g


def retention_chunk(state, q, k, v, lg):
    L = q.shape[1]
    q = q.astype(jnp.float32)
    k = k.astype(jnp.float32)
    v = v.astype(jnp.float32)
    idx = jnp.arange(L, dtype=jnp.float32)
    diff = idx[:, None] - idx[None, :]
    dmask = jnp.where(diff[None] >= 0, jnp.exp(jnp.maximum(diff, 0.0)[None] * lg[:, None, None]), 0.0)
    a = jnp.einsum('bqhd,bshd->bhqs', q, k) * dmask[None]
    inner = jnp.einsum('bhqs,bshe->bqhe', a, v)
    cross = jnp.einsum('bqhd,bhde->bqhe', q, state) * jnp.exp((idx + 1.0)[:, None] * lg[None])[None, :, :, None]
    k_dec = k * jnp.exp((L - 1.0 - idx)[:, None] * lg[None])[None, :, :, None]
    new_state = jnp.exp(L * lg)[None, :, None, None] * state + jnp.einsum('bshd,bshe->bhde', k_dec, v)
    return new_state, inner + cross


def ret_output(o, g, w_out, dtype):
    B, T = o.shape[0], o.shape[1]
    mu = jnp.mean(o, axis=-1, keepdims=True)
    var = jnp.mean(jnp.square(o - mu), axis=-1, keepdims=True)
    on = ((o - mu) * lax.rsqrt(var + NORM_EPS)).reshape(B, T, RET_VW).astype(dtype)
    return (on * jax.nn.silu(g)) @ w_out


def ret_prompt(h, w_in, w_out):
    B, T, _ = h.shape
    nc = T // RET_CHUNK
    q, k, v, g = ret_project(h, w_in, jnp.arange(T))
    lg = ret_log_gamma()
    to_chunks = lambda a: jnp.moveaxis(a.reshape(B, nc, RET_CHUNK, a.shape[2], a.shape[3]), 1, 0)

    def step(state, qkv):
        qc, kc, vc = qkv
        return retention_chunk(state, qc, kc, vc, lg)

    state0 = jnp.zeros((B, RET_HEADS, RET_QK_DIM, RET_V_DIM), jnp.float32)
    state, o = lax.scan(step, state0, (to_chunks(q), to_chunks(k), to_chunks(v)))
    o = jnp.moveaxis(o, 0, 1).reshape(B, T, RET_HEADS, RET_V_DIM)
    return ret_output(o, g, w_out, h.dtype), state


def ret_sample(h, state, w_in, w_out):
    L = h.shape[1]
    q, k, v, g = ret_project(h, w_in, PAST_LEN + jnp.arange(L))
    new_state, o = retention_chunk(state.astype(jnp.float32), q, k, v, ret_log_gamma())
    return ret_output(o, g, w_out, h.dtype), new_state.astype(state.dtype)


def residual_update(x, y, p, g_post, w_ple, w_gate):
    x = x + rms_norm(y, g_post)
    return x + jax.nn.sigmoid(x @ w_gate) * (p @ w_ple)


def setup_inputs(seed: int = 0) -> dict:
    key = jax.random.key(seed)
    ks = jax.random.split(key, 18)
    f32 = jnp.float32
    nrm = lambda k, shape, scale: jax.random.normal(k, shape, f32) * scale
    return {
        'x_prompt': nrm(ks[0], (BATCH, SEQ, D_MODEL), 1.0),
        'x_sample': nrm(ks[1], (DEC_BATCH, DEC_SEQ, D_MODEL), 1.0),
        'cache_k_win': nrm(ks[2], (N_ATTN_LAYERS, DEC_BATCH, WINDOW, ATTN_KV_HEADS, ATTN_HEAD_DIM), 1.0),
        'cache_v_win': nrm(ks[3], (N_ATTN_LAYERS, DEC_BATCH, WINDOW, ATTN_KV_HEADS, ATTN_HEAD_DIM), 1.0),
        'state_ret': nrm(ks[4], (N_RET_LAYERS, DEC_BATCH, RET_HEADS, RET_QK_DIM, RET_V_DIM), 0.1),
        'p_prompt': nrm(ks[5], (DEPTH, BATCH, SEQ, PLE_DIM), 1.0),
        'p_sample': nrm(ks[6], (DEPTH, DEC_BATCH, DEC_SEQ, PLE_DIM), 1.0),
        'pre_norm': 1.0 + nrm(ks[7], (DEPTH, D_MODEL), 0.1),
        'post_norm': 1.0 + nrm(ks[8], (DEPTH, D_MODEL), 0.1),
        'w_in_attn': nrm(ks[9], (N_ATTN_LAYERS, D_MODEL, ATTN_IN_DIM), D_MODEL ** -0.5),
        'attn_sinks': nrm(ks[10], (N_ATTN_LAYERS, ATTN_HEADS), 0.5),
        'w_out_attn': nrm(ks[11], (N_ATTN_LAYERS, ATTN_Q_DIM, D_MODEL), ATTN_Q_DIM ** -0.5),
        'w_in_ret': nrm(ks[12], (N_RET_LAYERS, D_MODEL, RET_IN_DIM), D_MODEL ** -0.5),
        'w_out_ret': nrm(ks[13], (N_RET_LAYERS, RET_VW, D_MODEL), RET_VW ** -0.5),
        'w_ple': nrm(ks[14], (DEPTH, PLE_DIM, D_MODEL), PLE_DIM ** -0.5),
        'w_ple_gate': nrm(ks[15], (DEPTH, D_MODEL, D_MODEL), D_MODEL ** -0.5),
    }


def reference(x_prompt, x_sample, cache_k_win, cache_v_win, state_ret, p_prompt, p_sample,
              pre_norm, post_norm, w_in_attn, attn_sinks, w_out_attn, w_in_ret, w_out_ret,
              w_ple, w_ple_gate):
    xp, xs = x_prompt, x_sample
    kwp, vwp, kws, vws, srp, srs = [], [], [], [], [], []
    for i in range(DEPTH):
        j = i // N_MIXERS
        hp = rms_norm(xp, pre_norm[i])
        hs = rms_norm(xs, pre_norm[i])
        if i % N_MIXERS == 0:
            yp, kp, vp = swa_prompt(hp, w_in_attn[j], attn_sinks[j], w_out_attn[j])
            ys, kn, vn = swa_sample(hs, cache_k_win[j], cache_v_win[j], w_in_attn[j], attn_sinks[j], w_out_attn[j])
            kwp.append(kp); vwp.append(vp); kws.append(kn); vws.append(vn)
        else:
            yp, sp = ret_prompt(hp, w_in_ret[j], w_out_ret[j])
            ys, sn = ret_sample(hs, state_ret[j], w_in_ret[j], w_out_ret[j])
            srp.append(sp.astype(state_ret.dtype)); srs.append(sn)
        xp = residual_update(xp, yp, p_prompt[i], post_norm[i], w_ple[i], w_ple_gate[i])
        xs = residual_update(xs, ys, p_sample[i], post_norm[i], w_ple[i], w_ple_gate[i])
    k_win_prompt = jnp.stack(kwp)
    v_win_prompt = jnp.stack(vwp)
    k_win_sample = jnp.stack(kws)
    v_win_sample = jnp.stack(vws)
    ret_state_prompt = jnp.stack(srp)
    ret_state_sample = jnp.stack(srs)
    return (xp, xs, k_win_prompt, v_win_prompt, k_win_sample, v_win_sample, ret_state_prompt, ret_state_sample)
```

```python
import functools
import math

import jax
import jax.numpy as jnp
from jax import lax
from jax.experimental import pallas as pl
from jax.experimental.pallas import tpu as pltpu

F32 = jnp.float32
BF16 = jnp.bfloat16

PAST_LEN = 16384
ROPE_THETA = 10000.0
NORM_EPS = 1e-6
NEG_INF = -1e30
WINDOW = 128
ATTN_HEADS = 16
ATTN_KV_HEADS = 4
ATTN_GROUP = ATTN_HEADS // ATTN_KV_HEADS
ATTN_HEAD_DIM = 64
ATTN_Q_DIM = ATTN_HEADS * ATTN_HEAD_DIM
ATTN_KV_DIM = ATTN_KV_HEADS * ATTN_HEAD_DIM
RET_HEADS = 4
RET_QK_DIM = 256
RET_V_DIM = 512
RET_QK_W = RET_HEADS * RET_QK_DIM
RET_VW = RET_HEADS * RET_V_DIM
RET_CHUNK = 128
LANES = 128

RET_LOG_GAMMA = tuple(math.log1p(-(2.0 ** (-5.0 - h))) for h in range(RET_HEADS))

VMEM_LIMIT_BYTES = 56 * 1024 * 1024


def _dot(a, b):
    return jnp.dot(a, b, preferred_element_type=F32)


def _dot_nt(a, b):
    return lax.dot_general(a, b, (((1,), (1,)), ((), ())), preferred_element_type=F32)


def _dot_tn(a, b):
    return lax.dot_general(a, b, (((0,), (0,)), ((), ())), preferred_element_type=F32)


def _rms_norm(x, g):
    return x * lax.rsqrt(jnp.mean(x * x, axis=-1, keepdims=True) + NORM_EPS) * g


def _silu(g):
    return g * jax.nn.sigmoid(g)


def _residual_tail(x, y, p, gpost, wgate_ref, wple_ref):
    x1 = x + _rms_norm(y, gpost)
    gate = jax.nn.sigmoid(_dot(x1.astype(BF16), wgate_ref[...]))
    return x1 + gate * _dot(p.astype(BF16), wple_ref[...])


def _rope_attn(x, cos, sin_signed):
    lane = lax.broadcasted_iota(jnp.int32, (1, LANES), 1)
    first_half = (lane % ATTN_HEAD_DIM) < (ATTN_HEAD_DIM // 2)
    outs = []
    for j in range(x.shape[1] // LANES):
        xj = x[:, j * LANES:(j + 1) * LANES]
        fwd = pltpu.roll(xj, LANES - ATTN_HEAD_DIM // 2, axis=1)
        bwd = pltpu.roll(xj, ATTN_HEAD_DIM // 2, axis=1)
        outs.append(xj * cos + jnp.where(first_half, fwd, bwd) * sin_signed)
    return jnp.concatenate(outs, axis=1)


def _rope_ret(x, cos, sin):
    half = RET_QK_DIM // 2
    outs = []
    for h in range(x.shape[1] // RET_QK_DIM):
        x1 = x[:, h * RET_QK_DIM:h * RET_QK_DIM + half]
        x2 = x[:, h * RET_QK_DIM + half:(h + 1) * RET_QK_DIM]
        outs.append(x1 * cos - x2 * sin)
        outs.append(x2 * cos + x1 * sin)
    return jnp.concatenate(outs, axis=1)


def _attn_prompt_kernel(sink_ref, x_ref, p_ref, cos_ref, sin_ref, gpre_ref, gpost_ref,
                        win_ref, wout_ref, wgate_ref, wple_ref,
                        xo_ref, kw_ref, vw_ref,
                        q_scr, kext_scr, vext_scr, o_scr, *, tq):
    t = pl.program_id(1)
    nt = pl.num_programs(1)

    @pl.when(t == 0)
    def _():
        kext_scr[0:WINDOW, :] = jnp.zeros((WINDOW, ATTN_KV_DIM), BF16)
        vext_scr[0:WINDOW, :] = jnp.zeros((WINDOW, ATTN_KV_DIM), BF16)

    x = x_ref[...]
    h = _rms_norm(x, gpre_ref[...]).astype(BF16)
    cos = cos_ref[...]
    sin = sin_ref[...]
    q0, k0, v0, g0 = 0, ATTN_Q_DIM, ATTN_Q_DIM + ATTN_KV_DIM, ATTN_Q_DIM + 2 * ATTN_KV_DIM
    q = _rope_attn(_dot(h, win_ref[:, q0:k0]), cos, sin) * (ATTN_HEAD_DIM ** -0.5)
    q_scr[...] = q.astype(BF16)
    k = _rope_attn(_dot(h, win_ref[:, k0:v0]), cos, sin)
    v = _dot(h, win_ref[:, v0:g0])
    kext_scr[WINDOW:, :] = k.astype(BF16)
    vext_scr[WINDOW:, :] = v.astype(BF16)

    @pl.when(t == nt - 1)
    def _():
        kw_ref[...] = k[tq - WINDOW:, :]
        vw_ref[...] = v[tq - WINDOW:, :]

    qi = lax.broadcasted_iota(jnp.int32, (WINDOW, 2 * WINDOW), 0) + WINDOW
    kj = lax.broadcasted_iota(jnp.int32, (WINDOW, 2 * WINDOW), 1)
    diff = qi - kj
    band = (diff >= 0) & (diff < WINDOW)

    def block(j, carry):
        r0 = pl.multiple_of(j * WINDOW, WINDOW)
        qb = q_scr[pl.ds(r0, WINDOW), :]
        kb = kext_scr[pl.ds(r0, 2 * WINDOW), :]
        vb = vext_scr[pl.ds(r0, 2 * WINDOW), :]
        has_prev = jnp.logical_or(t > 0, j > 0)
        mask = band & (has_prev | (kj >= WINDOW))
        outs = []
        for kh in range(ATTN_KV_HEADS):
            kk = kb[:, kh * ATTN_HEAD_DIM:(kh + 1) * ATTN_HEAD_DIM]
            vv = vb[:, kh * ATTN_HEAD_DIM:(kh + 1) * ATTN_HEAD_DIM]
            for gi in range(ATTN_GROUP):
                hd = kh * ATTN_GROUP + gi
                s = _dot_nt(qb[:, hd * ATTN_HEAD_DIM:(hd + 1) * ATTN_HEAD_DIM], kk)
                s = jnp.where(mask, s, NEG_INF)
                sink = sink_ref[hd]
                m = jnp.maximum(jnp.max(s, axis=-1, keepdims=True), sink)
                pr = jnp.exp(s - m)
                den = jnp.sum(pr, axis=-1, keepdims=True) + jnp.exp(sink - m)
                outs.append(_dot(pr.astype(BF16), vv) / den)
        o_scr[pl.ds(r0, WINDOW), :] = jnp.concatenate(outs, axis=1)
        return carry

    lax.fori_loop(0, tq // WINDOW, block, 0)

    g = _dot(h, win_ref[:, g0:])
    og = (o_scr[...] * _silu(g)).astype(BF16)
    y = _dot(og, wout_ref[...])
    xo_ref[...] = _residual_tail(x, y, p_ref[...], gpost_ref[...], wgate_ref, wple_ref)

    kext_scr[0:WINDOW, :] = kext_scr[tq:tq + WINDOW, :]
    vext_scr[0:WINDOW, :] = vext_scr[tq:tq + WINDOW, :]


def _const_spec(shape):
    nd = len(shape)
    return pl.BlockSpec(shape, lambda *_: (0,) * nd)


def _attn_prompt_layer(layer, x, p_all, cos, sin, gpre, gpost, sinks, w_in, w_out, w_gate, w_ple):
    nb, seq, d = x.shape
    tq = min(512, seq)
    ple = p_all.shape[-1]
    kernel = functools.partial(_attn_prompt_kernel, tq=tq)
    tok = lambda b, t: (b, t, 0)
    win_blk = lambda b, t: (b, 0, 0)
    return pl.pallas_call(
        kernel,
        grid=(nb, seq // tq),
        in_specs=[
            pl.BlockSpec(memory_space=pltpu.SMEM),
            pl.BlockSpec((None, tq, d), tok),
            pl.BlockSpec((None, None, tq, ple), lambda b, t: (layer, b, t, 0)),
            pl.BlockSpec((tq, LANES), lambda b, t: (t, 0)),
            pl.BlockSpec((tq, LANES), lambda b, t: (t, 0)),
            _const_spec(gpre.shape), _const_spec(gpost.shape),
            _const_spec(w_in.shape), _const_spec(w_out.shape),
            _const_spec(w_gate.shape), _const_spec(w_ple.shape),
        ],
        out_specs=[
            pl.BlockSpec((None, tq, d), tok),
            pl.BlockSpec((None, WINDOW, ATTN_KV_DIM), win_blk),
            pl.BlockSpec((None, WINDOW, ATTN_KV_DIM), win_blk),
        ],
        out_shape=[
            jax.ShapeDtypeStruct(x.shape, F32),
            jax.ShapeDtypeStruct((nb, WINDOW, ATTN_KV_DIM), F32),
            jax.ShapeDtypeStruct((nb, WINDOW, ATTN_KV_DIM), F32),
        ],
        scratch_shapes=[
            pltpu.VMEM((tq, ATTN_Q_DIM), BF16),
            pltpu.VMEM((tq + WINDOW, ATTN_KV_DIM), BF16),
            pltpu.VMEM((tq + WINDOW, ATTN_KV_DIM), BF16),
            pltpu.VMEM((tq, ATTN_Q_DIM), F32),
        ],
        compiler_params=pltpu.CompilerParams(
            dimension_semantics=("arbitrary", "arbitrary"),
            vmem_limit_bytes=VMEM_LIMIT_BYTES),
        name="attn_prompt_layer",
    )(sinks, x, p_all, cos, sin, gpre, gpost, w_in, w_out, w_gate, w_ple)


def _attn_sample_kernel(sink_ref, x_ref, p_ref, cos_ref, sin_ref, gpre_ref, gpost_ref,
                        win_ref, wout_ref, wgate_ref, wple_ref, kc_ref, vc_ref,
                        xo_ref, ko_ref, vo_ref,
                        q_scr, kn_scr, vn_scr, o_scr, *, n_seq, dec_len):
    x = x_ref[...]
    h = _rms_norm(x, gpre_ref[...]).astype(BF16)
    cos = cos_ref[...]
    sin = sin_ref[...]
    q0, k0, v0, g0 = 0, ATTN_Q_DIM, ATTN_Q_DIM + ATTN_KV_DIM, ATTN_Q_DIM + 2 * ATTN_KV_DIM
    q_scr[...] = _rope_attn(_dot(h, win_ref[:, q0:k0]), cos, sin) * (ATTN_HEAD_DIM ** -0.5)
    kn_scr[...] = _rope_attn(_dot(h, win_ref[:, k0:v0]), cos, sin)
    vn_scr[...] = _dot(h, win_ref[:, v0:g0])

    rows = ATTN_GROUP * dec_len
    n_keys = 2 * WINDOW
    tok = lax.broadcasted_iota(jnp.int32, (rows, n_keys), 0) % dec_len
    col = lax.broadcasted_iota(jnp.int32, (rows, n_keys), 1)
    mask = ((col < WINDOW) & (col > tok)) | ((col >= WINDOW) & ((col - WINDOW) <= tok))
    row_head = lax.broadcasted_iota(jnp.int32, (rows, 1), 0) // dec_len
    pad = jnp.zeros((WINDOW - dec_len, ATTN_KV_DIM), F32)

    def seq_body(s, carry):
        r0 = pl.multiple_of(s * dec_len, dec_len)
        q8 = q_scr[pl.ds(r0, dec_len), :]
        kn = kn_scr[pl.ds(r0, dec_len), :]
        vn = vn_scr[pl.ds(r0, dec_len), :]
        kc = kc_ref[s]
        vc = vc_ref[s]
        ko_ref[s, 0:WINDOW - dec_len, :] = kc[dec_len:, :]
        ko_ref[s, WINDOW - dec_len:, :] = kn
        vo_ref[s, 0:WINDOW - dec_len, :] = vc[dec_len:, :]
        vo_ref[s, WINDOW - dec_len:, :] = vn
        kall = jnp.concatenate([kc, kn, pad], axis=0).astype(BF16)
        vall = jnp.concatenate([vc, vn, pad], axis=0).astype(BF16)
        outs = []
        for kh in range(ATTN_KV_HEADS):
            heads = [kh * ATTN_GROUP + gi for gi in range(ATTN_GROUP)]
            qs = jnp.concatenate(
                [q8[:, hd * ATTN_HEAD_DIM:(hd + 1) * ATTN_HEAD_DIM] for hd in heads], axis=0)
            sinkv = jnp.zeros((rows, 1), F32)
            for gi, hd in enumerate(heads):
                sinkv = jnp.where(row_head == gi, sink_ref[hd], sinkv)
            sc = _dot_nt(qs.astype(BF16), kall[:, kh * ATTN_HEAD_DIM:(kh + 1) * ATTN_HEAD_DIM])
            sc = jnp.where(mask, sc, NEG_INF)
            m = jnp.maximum(jnp.max(sc, axis=-1, keepdims=True), sinkv)
            pr = jnp.exp(sc - m)
            den = jnp.sum(pr, axis=-1, keepdims=True) + jnp.exp(sinkv - m)
            o = _dot(pr.astype(BF16), vall[:, kh * ATTN_HEAD_DIM:(kh + 1) * ATTN_HEAD_DIM]) / den
            outs.extend(o[gi * dec_len:(gi + 1) * dec_len, :] for gi in range(ATTN_GROUP))
        o_scr[pl.ds(r0, dec_len), :] = jnp.concatenate(outs, axis=1)
        return carry

    lax.fori_loop(0, n_seq, seq_body, 0)

    g = _dot(h, win_ref[:, g0:])
    og = (o_scr[...] * _silu(g)).astype(BF16)
    y = _dot(og, wout_ref[...])
    xo_ref[...] = _residual_tail(x, y, p_ref[...], gpost_ref[...], wgate_ref, wple_ref)


def _attn_sample_layer(layer, x, p_all, cos, sin, gpre, gpost, sinks, w_in, w_out, w_gate, w_ple,
                       cache_k, cache_v, dec_len):
    n_tok, d = x.shape
    n_all = cache_k.shape[0]
    n_seq = min(16, n_all)
    tq = n_seq * dec_len
    ple = p_all.shape[-1]
    kernel = functools.partial(_attn_sample_kernel, n_seq=n_seq, dec_len=dec_len)
    tok = lambda i: (i, 0)
    cache_blk = lambda i: (i, 0, 0)
    return pl.pallas_call(
        kernel,
        grid=(n_all // n_seq,),
        in_specs=[
            pl.BlockSpec(memory_space=pltpu.SMEM),
            pl.BlockSpec((tq, d), tok),
            pl.BlockSpec((None, tq, ple), lambda i: (layer, i, 0)),
            _const_spec((tq, LANES)), _const_spec((tq, LANES)),
            _const_spec(gpre.shape), _const_spec(gpost.shape),
            _const_spec(w_in.shape), _const_spec(w_out.shape),
            _const_spec(w_gate.shape), _const_spec(w_ple.shape),
            pl.BlockSpec((n_seq, WINDOW, ATTN_KV_DIM), cache_blk),
            pl.BlockSpec((n_seq, WINDOW, ATTN_KV_DIM), cache_blk),
        ],
        out_specs=[
            pl.BlockSpec((tq, d), tok),
            pl.BlockSpec((n_seq, WINDOW, ATTN_KV_DIM), cache_blk),
            pl.BlockSpec((n_seq, WINDOW, ATTN_KV_DIM), cache_blk),
        ],
        out_shape=[
            jax.ShapeDtypeStruct(x.shape, F32),
            jax.ShapeDtypeStruct(cache_k.shape, F32),
            jax.ShapeDtypeStruct(cache_v.shape, F32),
        ],
        scratch_shapes=[
            pltpu.VMEM((tq, ATTN_Q_DIM), F32),
            pltpu.VMEM((tq, ATTN_KV_DIM), F32),
            pltpu.VMEM((tq, ATTN_KV_DIM), F32),
            pltpu.VMEM((tq, ATTN_Q_DIM), F32),
        ],
        compiler_params=pltpu.CompilerParams(
            dimension_semantics=("arbitrary",),
            vmem_limit_bytes=VMEM_LIMIT_BYTES),
        name="attn_sample_layer",
    )(sinks, x, p_all, cos, sin, gpre, gpost, w_in, w_out, w_gate, w_ple, cache_k, cache_v)


def _ret_project(h, cos, sin, win_ref, chunk_len):
    q0, k0, v0, g0 = 0, RET_QK_W, 2 * RET_QK_W, 2 * RET_QK_W + RET_VW
    q = _rope_ret(_dot(h, win_ref[:, q0:k0]), cos, sin)
    k = _rope_ret(_dot(h, win_ref[:, k0:v0]), cos, sin) * (RET_QK_DIM ** -0.5)
    v = _dot(h, win_ref[:, v0:g0])
    g = _dot(h, win_ref[:, g0:])
    n = h.shape[0]
    idx = (lax.broadcasted_iota(jnp.int32, (n, 1), 0) % chunk_len).astype(F32)
    kd = jnp.concatenate(
        [k[:, hd * RET_QK_DIM:(hd + 1) * RET_QK_DIM]
         * jnp.exp((chunk_len - 1.0 - idx) * RET_LOG_GAMMA[hd]) for hd in range(RET_HEADS)], axis=1)
    return q, k, kd, v, g


def _ret_decay_mask(n, chunk_len, hd):
    ri = lax.broadcasted_iota(jnp.int32, (n, n), 0)
    ci = lax.broadcasted_iota(jnp.int32, (n, n), 1)
    diff = ri - ci
    ok = (diff >= 0) & ((ri // chunk_len) == (ci // chunk_len))
    return jnp.where(ok, jnp.exp(jnp.maximum(diff, 0).astype(F32) * RET_LOG_GAMMA[hd]), 0.0)


def _ret_finish(o, g, x, p, gpost, wout_ref, wgate_ref, wple_ref):
    outs = []
    for hd in range(RET_HEADS):
        oh = o[:, hd * RET_V_DIM:(hd + 1) * RET_V_DIM]
        mu = jnp.mean(oh, axis=-1, keepdims=True)
        var = jnp.mean(jnp.square(oh - mu), axis=-1, keepdims=True)
        outs.append((oh - mu) * lax.rsqrt(var + NORM_EPS))
    on = jnp.concatenate(outs, axis=1)
    og = (on * _silu(g)).astype(BF16)
    y = _dot(og, wout_ref[...])
    return _residual_tail(x, y, p, gpost, wgate_ref, wple_ref)


def _ret_prompt_kernel(x_ref, p_ref, cos_ref, sin_ref, gpre_ref, gpost_ref,
                       win_ref, wout_ref, wgate_ref, wple_ref,
                       xo_ref, st_ref,
                       q_scr, k_scr, kd_scr, v_scr, o_scr, *, tq):
    t = pl.program_id(1)

    @pl.when(t == 0)
    def _():
        st_ref[...] = jnp.zeros(st_ref.shape, F32)

    x = x_ref[...]
    h = _rms_norm(x, gpre_ref[...]).astype(BF16)
    q, k, kd, v, g = _ret_project(h, cos_ref[...], sin_ref[...], win_ref, RET_CHUNK)
    q_scr[...] = q.astype(BF16)
    k_scr[...] = k.astype(BF16)
    kd_scr[...] = kd.astype(BF16)
    v_scr[...] = v.astype(BF16)

    idx = lax.broadcasted_iota(jnp.int32, (RET_CHUNK, 1), 0).astype(F32)

    def chunk(c, carry):
        r0 = pl.multiple_of(c * RET_CHUNK, RET_CHUNK)
        for hd in range(RET_HEADS):
            lg = RET_LOG_GAMMA[hd]
            qs = slice(hd * RET_QK_DIM, (hd + 1) * RET_QK_DIM)
            vs = slice(hd * RET_V_DIM, (hd + 1) * RET_V_DIM)
            qh = q_scr[pl.ds(r0, RET_CHUNK), qs]
            kh = k_scr[pl.ds(r0, RET_CHUNK), qs]
            kdh = kd_scr[pl.ds(r0, RET_CHUNK), qs]
            vh = v_scr[pl.ds(r0, RET_CHUNK), vs]
            a = _dot_nt(qh, kh) * _ret_decay_mask(RET_CHUNK, RET_CHUNK, hd)
            inner = _dot(a.astype(BF16), vh)
            state = st_ref[hd]
            cross = _dot(qh, state.astype(BF16)) * jnp.exp((idx + 1.0) * lg)
            o_scr[pl.ds(r0, RET_CHUNK), vs] = inner + cross
            st_ref[hd] = math.exp(RET_CHUNK * lg) * state + _dot_tn(kdh, vh)
        return carry

    lax.fori_loop(0, tq // RET_CHUNK, chunk, 0)

    xo_ref[...] = _ret_finish(o_scr[...], g, x, p_ref[...], gpost_ref[...],
                              wout_ref, wgate_ref, wple_ref)


def _ret_prompt_layer(layer, x, p_all, cos, sin, gpre, gpost, w_in, w_out, w_gate, w_ple):
    nb, seq, d = x.shape
    tq = min(256, seq)
    ple = p_all.shape[-1]
    kernel = functools.partial(_ret_prompt_kernel, tq=tq)
    tok = lambda b, t: (b, t, 0)
    return pl.pallas_call(
        kernel,
        grid=(nb, seq // tq),
        in_specs=[
            pl.BlockSpec((None, tq, d), tok),
            pl.BlockSpec((None, None, tq, ple), lambda b, t: (layer, b, t, 0)),
            pl.BlockSpec((tq, LANES), lambda b, t: (t, 0)),
            pl.BlockSpec((tq, LANES), lambda b, t: (t, 0)),
            _const_spec(gpre.shape), _const_spec(gpost.shape),
            _const_spec(w_in.shape), _const_spec(w_out.shape),
            _const_spec(w_gate.shape), _const_spec(w_ple.shape),
        ],
        out_specs=[
            pl.BlockSpec((None, tq, d), tok),
            pl.BlockSpec((None, RET_HEADS, RET_QK_DIM, RET_V_DIM), lambda b, t: (b, 0, 0, 0)),
        ],
        out_shape=[
            jax.ShapeDtypeStruct(x.shape, F32),
            jax.ShapeDtypeStruct((nb, RET_HEADS, RET_QK_DIM, RET_V_DIM), F32),
        ],
        scratch_shapes=[
            pltpu.VMEM((tq, RET_QK_W), BF16),
            pltpu.VMEM((tq, RET_QK_W), BF16),
            pltpu.VMEM((tq, RET_QK_W), BF16),
            pltpu.VMEM((tq, RET_VW), BF16),
            pltpu.VMEM((tq, RET_VW), F32),
        ],
        compiler_params=pltpu.CompilerParams(
            dimension_semantics=("arbitrary", "arbitrary"),
            vmem_limit_bytes=VMEM_LIMIT_BYTES),
        name="ret_prompt_layer",
    )(x, p_all, cos, sin, gpre, gpost, w_in, w_out, w_gate, w_ple)


def _ret_sample_proj_kernel(x_ref, cos_ref, sin_ref, gpre_ref, win_ref,
                            q_ref, k_ref, kd_ref, v_ref, g_ref, *, dec_len):
    h = _rms_norm(x_ref[...], gpre_ref[...]).astype(BF16)
    q, k, kd, v, g = _ret_project(h, cos_ref[...], sin_ref[...], win_ref, dec_len)
    q_ref[...] = q
    k_ref[...] = k
    kd_ref[...] = kd
    v_ref[...] = v
    g_ref[...] = g


def _ret_sample_proj(x, cos, sin, gpre, w_in, dec_len):
    n_tok, d = x.shape
    tq = min(512, n_tok)
    kernel = functools.partial(_ret_sample_proj_kernel, dec_len=dec_len)
    tok = lambda i: (i, 0)
    widths = (RET_QK_W, RET_QK_W, RET_QK_W, RET_VW, RET_VW)
    return pl.pallas_call(
        kernel,
        grid=(n_tok // tq,),
        in_specs=[
            pl.BlockSpec((tq, d), tok),
            _const_spec((tq, LANES)), _const_spec((tq, LANES)),
            _const_spec(gpre.shape), _const_spec(w_in.shape),
        ],
        out_specs=[pl.BlockSpec((tq, w), tok) for w in widths],
        out_shape=[jax.ShapeDtypeStruct((n_tok, w), F32) for w in widths],
        compiler_params=pltpu.CompilerParams(
            dimension_semantics=("arbitrary",),
            vmem_limit_bytes=VMEM_LIMIT_BYTES),
        name="ret_sample_proj",
    )(x, cos, sin, gpre, w_in)


def _ret_sample_mix_kernel(q_ref, k_ref, kd_ref, v_ref, st_ref, o_ref, sto_ref, *, n_seq, dec_len):
    pad_rows = 2 * dec_len
    idx = lax.broadcasted_iota(jnp.int32, (pad_rows, 1), 0).astype(F32)

    def padded(ref, r0, cols):
        val = ref[pl.ds(r0, dec_len), cols]
        return jnp.concatenate([val, jnp.zeros_like(val)], axis=0).astype(BF16)

    def seq_body(s, carry):
        r0 = pl.multiple_of(s * dec_len, dec_len)
        for hd in range(RET_HEADS):
            lg = RET_LOG_GAMMA[hd]
            qs = slice(hd * RET_QK_DIM, (hd + 1) * RET_QK_DIM)
            vs = slice(hd * RET_V_DIM, (hd + 1) * RET_V_DIM)
            qh = padded(q_ref, r0, qs)
            kh = padded(k_ref, r0, qs)
            kdh = padded(kd_ref, r0, qs)
            vh = padded(v_ref, r0, vs)
            a = _dot_nt(qh, kh) * _ret_decay_mask(pad_rows, dec_len, hd)
            inner = _dot(a.astype(BF16), vh)
            state = st_ref[s, hd]
            cross = _dot(qh, state.astype(BF16)) * jnp.exp((idx + 1.0) * lg)
            o_ref[pl.ds(r0, dec_len), vs] = (inner + cross)[0:dec_len, :]
            sto_ref[s, hd] = math.exp(dec_len * lg) * state + _dot_tn(kdh, vh)
        return carry

    lax.fori_loop(0, n_seq, seq_body, 0)


def _ret_sample_mix(q, k, kd, v, state, dec_len):
    n_all = state.shape[0]
    n_seq = min(4, n_all)
    tq = n_seq * dec_len
    kernel = functools.partial(_ret_sample_mix_kernel, n_seq=n_seq, dec_len=dec_len)
    tok = lambda i: (i, 0)
    st_blk = lambda i: (i, 0, 0, 0)
    st_shape = (n_seq, RET_HEADS, RET_QK_DIM, RET_V_DIM)
    return pl.pallas_call(
        kernel,
        grid=(n_all // n_seq,),
        in_specs=[
            pl.BlockSpec((tq, RET_QK_W), tok), pl.BlockSpec((tq, RET_QK_W), tok),
            pl.BlockSpec((tq, RET_QK_W), tok), pl.BlockSpec((tq, RET_VW), tok),
            pl.BlockSpec(st_shape, st_blk),
        ],
        out_specs=[pl.BlockSpec((tq, RET_VW), tok), pl.BlockSpec(st_shape, st_blk)],
        out_shape=[
            jax.ShapeDtypeStruct((q.shape[0], RET_VW), F32),
            jax.ShapeDtypeStruct(state.shape, F32),
        ],
        compiler_params=pltpu.CompilerParams(
            dimension_semantics=("arbitrary",),
            vmem_limit_bytes=VMEM_LIMIT_BYTES),
        name="ret_sample_mix",
    )(q, k, kd, v, state)


def _ret_sample_finish_kernel(o_ref, g_ref, x_ref, p_ref, gpost_ref, wout_ref, wgate_ref, wple_ref,
                              xo_ref):
    xo_ref[...] = _ret_finish(o_ref[...], g_ref[...], x_ref[...], p_ref[...], gpost_ref[...],
                              wout_ref, wgate_ref, wple_ref)


def _ret_sample_finish(layer, o, g, x, p_all, gpost, w_out, w_gate, w_ple):
    n_tok, d = x.shape
    tq = min(512, n_tok)
    ple = p_all.shape[-1]
    tok = lambda i: (i, 0)
    return pl.pallas_call(
        _ret_sample_finish_kernel,
        grid=(n_tok // tq,),
        in_specs=[
            pl.BlockSpec((tq, RET_VW), tok), pl.BlockSpec((tq, RET_VW), tok),
            pl.BlockSpec((tq, d), tok),
            pl.BlockSpec((None, tq, ple), lambda i: (layer, i, 0)),
            _const_spec(gpost.shape), _const_spec(w_out.shape),
            _const_spec(w_gate.shape), _const_spec(w_ple.shape),
        ],
        out_specs=pl.BlockSpec((tq, d), tok),
        out_shape=jax.ShapeDtypeStruct(x.shape, F32),
        compiler_params=pltpu.CompilerParams(
            dimension_semantics=("arbitrary",),
            vmem_limit_bytes=VMEM_LIMIT_BYTES),
        name="ret_sample_finish",
    )(o, g, x, p_all, gpost, w_out, w_gate, w_ple)


def _rope_tables(pos, head_dim):
    half = head_dim // 2
    inv = ROPE_THETA ** (-jnp.arange(half, dtype=F32) / half)
    ang = pos.astype(F32)[:, None] * inv[None, :]
    return jnp.cos(ang), jnp.sin(ang)


def _attn_tables(pos):
    cos, sin = _rope_tables(pos, ATTN_HEAD_DIM)
    reps = LANES // ATTN_HEAD_DIM
    return (jnp.tile(jnp.concatenate([cos, cos], axis=1), (1, reps)),
            jnp.tile(jnp.concatenate([-sin, sin], axis=1), (1, reps)))


def kernel(x_prompt, x_sample, cache_k_win, cache_v_win, state_ret, p_prompt, p_sample, pre_norm,
           post_norm, w_in_attn, attn_sinks, w_out_attn, w_in_ret, w_out_ret, w_ple, w_ple_gate):
    depth = p_prompt.shape[0]
    nb, seq, d = x_prompt.shape
    n_dec, dec_len, _ = x_sample.shape
    n_stok = n_dec * dec_len
    assert seq % 512 == 0 or seq in (128, 256), seq

    pos_p = jnp.arange(seq)
    pos_s = PAST_LEN + jnp.arange(dec_len)
    cos_ap, sin_ap = _attn_tables(pos_p)
    cos_as, sin_as = _attn_tables(pos_s)
    cos_rp, sin_rp = _rope_tables(pos_p, RET_QK_DIM)
    cos_rs, sin_rs = _rope_tables(pos_s, RET_QK_DIM)
    attn_s_tile = min(16, n_dec)
    ret_s_tile = min(512, n_stok) // dec_len
    cos_as, sin_as = (jnp.tile(a, (attn_s_tile, 1)) for a in (cos_as, sin_as))
    cos_rs, sin_rs = (jnp.tile(a, (ret_s_tile, 1)) for a in (cos_rs, sin_rs))

    xp = x_prompt
    xs = x_sample.reshape(n_stok, d)
    p_s = p_sample.reshape(depth, n_stok, p_sample.shape[-1])
    kwp, vwp, kws, vws, srp, srs = [], [], [], [], [], []
    for i in range(depth):
        j = i // 2
        gpre = pre_norm[i][None, :]
        gpost = post_norm[i][None, :]
        w_gate = w_ple_gate[i].astype(BF16)
        w_pl = w_ple[i].astype(BF16)
        if i % 2 == 0:
            w_in = w_in_attn[j].astype(BF16)
            w_out = w_out_attn[j].astype(BF16)
            sinks = attn_sinks[j]
            xp, kp, vp = _attn_prompt_layer(i, xp, p_prompt, cos_ap, sin_ap, gpre, gpost, sinks,
                                            w_in, w_out, w_gate, w_pl)
            ck = cache_k_win[j].reshape(n_dec, WINDOW, ATTN_KV_DIM)
            cv = cache_v_win[j].reshape(n_dec, WINDOW, ATTN_KV_DIM)
            xs, kn, vn = _attn_sample_layer(i, xs, p_s, cos_as, sin_as, gpre, gpost, sinks,
                                            w_in, w_out, w_gate, w_pl, ck, cv, dec_len)
            win_shape = (WINDOW, ATTN_KV_HEADS, ATTN_HEAD_DIM)
            kwp.append(kp.reshape((nb,) + win_shape))
            vwp.append(vp.reshape((nb,) + win_shape))
            kws.append(kn.reshape((n_dec,) + win_shape))
            vws.append(vn.reshape((n_dec,) + win_shape))
        else:
            w_in = w_in_ret[j].astype(BF16)
            w_out = w_out_ret[j].astype(BF16)
            xp, sp = _ret_prompt_layer(i, xp, p_prompt, cos_rp, sin_rp, gpre, gpost,
                                       w_in, w_out, w_gate, w_pl)
            q, k, kd, v, g = _ret_sample_proj(xs, cos_rs, sin_rs, gpre, w_in, dec_len)
            o, sn = _ret_sample_mix(q, k, kd, v, state_ret[j], dec_len)
            xs = _ret_sample_finish(i, o, g, xs, p_s, gpost, w_out, w_gate, w_pl)
            srp.append(sp)
            srs.append(sn)
    return (xp, xs.reshape(n_dec, dec_len, d), jnp.stack(kwp), jnp.stack(vwp), jnp.stack(kws),
            jnp.stack(vws), jnp.stack(srp), jnp.stack(srs))
```

```python
import functools
import math

import jax
import jax.numpy as jnp
from jax import lax
from jax.experimental import pallas as pl
from jax.experimental.pallas import tpu as pltpu

F32 = jnp.float32
BF16 = jnp.bfloat16

PAST_LEN = 16384
ROPE_THETA = 10000.0
NORM_EPS = 1e-6
NEG_INF = -1e30
WINDOW = 128
ATTN_HEADS = 16
ATTN_KV_HEADS = 4
ATTN_GROUP = ATTN_HEADS // ATTN_KV_HEADS
ATTN_HEAD_DIM = 64
ATTN_Q_DIM = ATTN_HEADS * ATTN_HEAD_DIM
ATTN_KV_DIM = ATTN_KV_HEADS * ATTN_HEAD_DIM
RET_HEADS = 4
RET_QK_DIM = 256
RET_V_DIM = 512
RET_QK_W = RET_HEADS * RET_QK_DIM
RET_VW = RET_HEADS * RET_V_DIM
RET_CHUNK = 128
LANES = 128

LOG2_E = math.log2(math.e)
RET_LOG_GAMMA = tuple(math.log1p(-(2.0 ** (-5.0 - h))) for h in range(RET_HEADS))

VMEM_LIMIT_BYTES = 56 * 1024 * 1024


def _dot(a, b):
    return jnp.dot(a, b, preferred_element_type=F32)


def _dot_nt(a, b):
    return lax.dot_general(a, b, (((1,), (1,)), ((), ())), preferred_element_type=F32)


def _dot_tn(a, b):
    return lax.dot_general(a, b, (((0,), (0,)), ((), ())), preferred_element_type=F32)


def _rms_norm(x, g):
    return x * lax.rsqrt(jnp.mean(x * x, axis=-1, keepdims=True) + NORM_EPS) * g


def _silu(g):
    return g * jax.nn.sigmoid(g)


def _residual_tail(x, y, p, gpost, wgate_ref, wple_ref):
    x1 = x + _rms_norm(y, gpost)
    gate = jax.nn.sigmoid(_dot(x1.astype(BF16), wgate_ref[...]))
    return x1 + gate * _dot(p.astype(BF16), wple_ref[...])


def _rope_attn(x, cos, sin_signed):
    lane = lax.broadcasted_iota(jnp.int32, (1, LANES), 1)
    first_half = (lane % ATTN_HEAD_DIM) < (ATTN_HEAD_DIM // 2)
    outs = []
    for j in range(x.shape[1] // LANES):
        xj = x[:, j * LANES:(j + 1) * LANES]
        fwd = pltpu.roll(xj, LANES - ATTN_HEAD_DIM // 2, axis=1)
        bwd = pltpu.roll(xj, ATTN_HEAD_DIM // 2, axis=1)
        outs.append(xj * cos + jnp.where(first_half, fwd, bwd) * sin_signed)
    return jnp.concatenate(outs, axis=1)


def _rope_ret(x, cos, sin):
    half = RET_QK_DIM // 2
    outs = []
    for h in range(x.shape[1] // RET_QK_DIM):
        x1 = x[:, h * RET_QK_DIM:h * RET_QK_DIM + half]
        x2 = x[:, h * RET_QK_DIM + half:(h + 1) * RET_QK_DIM]
        outs.append(x1 * cos - x2 * sin)
        outs.append(x2 * cos + x1 * sin)
    return jnp.concatenate(outs, axis=1)


def _attn_prompt_kernel(sink_ref, x_ref, p_ref, cos_ref, sin_ref, cost_ref, sint_ref,
                        gpre_ref, gpost_ref, wqt_ref, wk_ref, wvt_ref, wg_ref,
                        wout_ref, wgate_ref, wple_ref,
                        xo_ref, kw_ref, vw_ref,
                        qt_scr, kext_scr, vt_scr, st_scr, pt_scr, inv_scr, ot_scr, *, tq):
    nblk = tq // WINDOW
    half = ATTN_HEAD_DIM // 2
    t = pl.program_id(1)
    nt = pl.num_programs(1)

    @pl.when(t == 0)
    def _():
        kext_scr[0:WINDOW, :] = jnp.zeros((WINDOW, ATTN_KV_DIM), BF16)
        vt_scr[0] = jnp.zeros((ATTN_KV_DIM, WINDOW), BF16)

    x = x_ref[...]
    h = _rms_norm(x, gpre_ref[...]).astype(BF16)

    qt = _dot_nt(wqt_ref[...], h)
    cost = cost_ref[...]
    sint = sint_ref[...]
    pieces = []
    for hd in range(ATTN_HEADS):
        x1 = qt[hd * ATTN_HEAD_DIM:hd * ATTN_HEAD_DIM + half, :]
        x2 = qt[hd * ATTN_HEAD_DIM + half:(hd + 1) * ATTN_HEAD_DIM, :]
        pieces.append(x1 * cost - x2 * sint)
        pieces.append(x2 * cost + x1 * sint)
    qrot = (jnp.concatenate(pieces, axis=0) * (ATTN_HEAD_DIM ** -0.5 * LOG2_E)).astype(BF16)
    for jb in range(nblk):
        qt_scr[jb] = qrot[:, jb * WINDOW:(jb + 1) * WINDOW]

    k = _rope_attn(_dot(h, wk_ref[...]), cos_ref[...], sin_ref[...])
    kext_scr[WINDOW:, :] = k.astype(BF16)
    vt = _dot_nt(wvt_ref[...], h)
    vtb = vt.astype(BF16)
    for jb in range(nblk):
        vt_scr[jb + 1] = vtb[:, jb * WINDOW:(jb + 1) * WINDOW]

    @pl.when(t == nt - 1)
    def _():
        kw_ref[...] = k[tq - WINDOW:, :].T
        vw_ref[...] = vt[:, tq - WINDOW:]

    kj = lax.broadcasted_iota(jnp.int32, (2 * WINDOW, WINDOW), 0)
    qi = lax.broadcasted_iota(jnp.int32, (2 * WINDOW, WINDOW), 1) + WINDOW
    diff = qi - kj
    band = (diff >= 0) & (diff < WINDOW)
    zero_rows = jnp.zeros((ATTN_HEAD_DIM, ATTN_GROUP * WINDOW), BF16)
    groups = [(jb, kh) for jb in range(nblk) for kh in range(ATTN_KV_HEADS)]

    for gidx, (jb, kh) in enumerate(groups):
        heads = [kh * ATTN_GROUP + gi for gi in range(ATTN_GROUP)]
        pair = kh // 2
        kb = kext_scr[jb * WINDOW:(jb + 2) * WINDOW, pair * LANES:(pair + 1) * LANES]
        qg = jnp.concatenate(
            [qt_scr[jb, hd * ATTN_HEAD_DIM:(hd + 1) * ATTN_HEAD_DIM, :] for hd in heads], axis=1)
        qg = jnp.concatenate([qg, zero_rows] if kh % 2 == 0 else [zero_rows, qg], axis=0)
        st_scr[gidx] = _dot(kb, qg)

    for gidx, (jb, kh) in enumerate(groups):
        mask = band & ((t > 0) | (kj >= WINDOW)) if jb == 0 else band
        for gi in range(ATTN_GROUP):
            hd = kh * ATTN_GROUP + gi
            cols = slice(gi * WINDOW, (gi + 1) * WINDOW)
            s = jnp.where(mask, st_scr[gidx, :, cols], NEG_INF)
            sink = sink_ref[hd] * LOG2_E
            m = jnp.maximum(jnp.max(s, axis=0, keepdims=True), sink)
            pr = jnp.exp2(s - m)
            den = jnp.sum(pr, axis=0, keepdims=True) + jnp.exp2(sink - m)
            pt_scr[gidx, :, cols] = pr.astype(BF16)
            inv_scr[jb, hd:hd + 1, :] = 1.0 / den

    for gidx, (jb, kh) in enumerate(groups):
        vband = jnp.concatenate([vt_scr[jb, kh * ATTN_HEAD_DIM:(kh + 1) * ATTN_HEAD_DIM, :],
                                 vt_scr[jb + 1, kh * ATTN_HEAD_DIM:(kh + 1) * ATTN_HEAD_DIM, :]],
                                axis=1)
        otg = _dot(vband, pt_scr[gidx])
        for gi in range(ATTN_GROUP):
            hd = kh * ATTN_GROUP + gi
            ot_scr[jb, hd * ATTN_HEAD_DIM:(hd + 1) * ATTN_HEAD_DIM, :] = (
                otg[:, gi * WINDOW:(gi + 1) * WINDOW] * inv_scr[jb, hd:hd + 1, :])

    o = jnp.concatenate([ot_scr[jb].T for jb in range(nblk)], axis=0)
    g = _dot(h, wg_ref[...])
    og = (o * _silu(g)).astype(BF16)
    y = _dot(og, wout_ref[...])
    xo_ref[...] = _residual_tail(x, y, p_ref[...], gpost_ref[...], wgate_ref, wple_ref)

    kext_scr[0:WINDOW, :] = kext_scr[tq:tq + WINDOW, :]
    vt_scr[0] = vt_scr[nblk]


def _const_spec(shape):
    nd = len(shape)
    return pl.BlockSpec(shape, lambda *_: (0,) * nd)


def _attn_prompt_layer(layer, x, p_all, tables, gpre, gpost, sinks, w_qt, w_k, w_vt, w_g,
                       w_out, w_gate, w_ple):
    nb, seq, d = x.shape
    tq = min(512, seq)
    nblk = tq // WINDOW
    ple = p_all.shape[-1]
    cos, sin, cost, sint = tables
    kernel = functools.partial(_attn_prompt_kernel, tq=tq)
    tok = lambda b, t: (b, t, 0)
    win_blk = lambda b, t: (b, 0, 0)
    consts = (gpre, gpost, w_qt, w_k, w_vt, w_g, w_out, w_gate, w_ple)
    return pl.pallas_call(
        kernel,
        grid=(nb, seq // tq),
        in_specs=[
            pl.BlockSpec(memory_space=pltpu.SMEM),
            pl.BlockSpec((None, tq, d), tok),
            pl.BlockSpec((None, None, tq, ple), lambda b, t: (layer, b, t, 0)),
            pl.BlockSpec((tq, LANES), lambda b, t: (t, 0)),
            pl.BlockSpec((tq, LANES), lambda b, t: (t, 0)),
            pl.BlockSpec((ATTN_HEAD_DIM // 2, tq), lambda b, t: (0, t)),
            pl.BlockSpec((ATTN_HEAD_DIM // 2, tq), lambda b, t: (0, t)),
        ] + [_const_spec(c.shape) for c in consts],
        out_specs=[
            pl.BlockSpec((None, tq, d), tok),
            pl.BlockSpec((None, ATTN_KV_DIM, WINDOW), win_blk),
            pl.BlockSpec((None, ATTN_KV_DIM, WINDOW), win_blk),
        ],
        out_shape=[
            jax.ShapeDtypeStruct(x.shape, F32),
            jax.ShapeDtypeStruct((nb, ATTN_KV_DIM, WINDOW), F32),
            jax.ShapeDtypeStruct((nb, ATTN_KV_DIM, WINDOW), F32),
        ],
        scratch_shapes=[
            pltpu.VMEM((nblk, ATTN_Q_DIM, WINDOW), BF16),
            pltpu.VMEM((tq + WINDOW, ATTN_KV_DIM), BF16),
            pltpu.VMEM((nblk + 1, ATTN_KV_DIM, WINDOW), BF16),
            pltpu.VMEM((nblk * ATTN_KV_HEADS, 2 * WINDOW, ATTN_GROUP * WINDOW), F32),
            pltpu.VMEM((nblk * ATTN_KV_HEADS, 2 * WINDOW, ATTN_GROUP * WINDOW), BF16),
            pltpu.VMEM((nblk, ATTN_HEADS, WINDOW), F32),
            pltpu.VMEM((nblk, ATTN_Q_DIM, WINDOW), F32),
        ],
        compiler_params=pltpu.CompilerParams(
            dimension_semantics=("arbitrary", "arbitrary"),
            vmem_limit_bytes=VMEM_LIMIT_BYTES),
        name="attn_prompt_layer",
    )(sinks, x, p_all, cos, sin, cost, sint, *consts)


def _attn_sample_kernel(sink_ref, x_ref, p_ref, cos_ref, sin_ref, gpre_ref, gpost_ref,
                        win_ref, wout_ref, wgate_ref, wple_ref, kc_ref, vc_ref,
                        xo_ref, ko_ref, vo_ref,
                        q_scr, kn_scr, vn_scr, o_scr, *, n_seq, dec_len):
    x = x_ref[...]
    h = _rms_norm(x, gpre_ref[...]).astype(BF16)
    cos = cos_ref[...]
    sin = sin_ref[...]
    q0, k0, v0, g0 = 0, ATTN_Q_DIM, ATTN_Q_DIM + ATTN_KV_DIM, ATTN_Q_DIM + 2 * ATTN_KV_DIM
    q_scr[...] = _rope_attn(_dot(h, win_ref[:, q0:k0]), cos, sin) * (ATTN_HEAD_DIM ** -0.5)
    kn_scr[...] = _rope_attn(_dot(h, win_ref[:, k0:v0]), cos, sin)
    vn_scr[...] = _dot(h, win_ref[:, v0:g0])

    rows = ATTN_GROUP * dec_len
    n_keys = 2 * WINDOW
    tok = lax.broadcasted_iota(jnp.int32, (rows, n_keys), 0) % dec_len
    col = lax.broadcasted_iota(jnp.int32, (rows, n_keys), 1)
    mask = ((col < WINDOW) & (col > tok)) | ((col >= WINDOW) & ((col - WINDOW) <= tok))
    row_head = lax.broadcasted_iota(jnp.int32, (rows, 1), 0) // dec_len
    pad = jnp.zeros((WINDOW - dec_len, ATTN_KV_DIM), F32)

    def seq_body(s, carry):
        r0 = pl.multiple_of(s * dec_len, dec_len)
        q8 = q_scr[pl.ds(r0, dec_len), :]
        kn = kn_scr[pl.ds(r0, dec_len), :]
        vn = vn_scr[pl.ds(r0, dec_len), :]
        kc = kc_ref[s]
        vc = vc_ref[s]
        ko_ref[s, 0:WINDOW - dec_len, :] = kc[dec_len:, :]
        ko_ref[s, WINDOW - dec_len:, :] = kn
        vo_ref[s, 0:WINDOW - dec_len, :] = vc[dec_len:, :]
        vo_ref[s, WINDOW - dec_len:, :] = vn
        kall = jnp.concatenate([kc, kn, pad], axis=0).astype(BF16)
        vall = jnp.concatenate([vc, vn, pad], axis=0).astype(BF16)
        outs = []
        for kh in range(ATTN_KV_HEADS):
            heads = [kh * ATTN_GROUP + gi for gi in range(ATTN_GROUP)]
            qs = jnp.concatenate(
                [q8[:, hd * ATTN_HEAD_DIM:(hd + 1) * ATTN_HEAD_DIM] for hd in heads], axis=0)
            sinkv = jnp.zeros((rows, 1), F32)
            for gi, hd in enumerate(heads):
                sinkv = jnp.where(row_head == gi, sink_ref[hd], sinkv)
            sc = _dot_nt(qs.astype(BF16), kall[:, kh * ATTN_HEAD_DIM:(kh + 1) * ATTN_HEAD_DIM])
            sc = jnp.where(mask, sc, NEG_INF)
            m = jnp.maximum(jnp.max(sc, axis=-1, keepdims=True), sinkv)
            pr = jnp.exp(sc - m)
            den = jnp.sum(pr, axis=-1, keepdims=True) + jnp.exp(sinkv - m)
            o = _dot(pr.astype(BF16), vall[:, kh * ATTN_HEAD_DIM:(kh + 1) * ATTN_HEAD_DIM]) / den
            outs.extend(o[gi * dec_len:(gi + 1) * dec_len, :] for gi in range(ATTN_GROUP))
        o_scr[pl.ds(r0, dec_len), :] = jnp.concatenate(outs, axis=1)
        return carry

    lax.fori_loop(0, n_seq, seq_body, 0)

    g = _dot(h, win_ref[:, g0:])
    og = (o_scr[...] * _silu(g)).astype(BF16)
    y = _dot(og, wout_ref[...])
    xo_ref[...] = _residual_tail(x, y, p_ref[...], gpost_ref[...], wgate_ref, wple_ref)


def _attn_sample_layer(layer, x, p_all, cos, sin, gpre, gpost, sinks, w_in, w_out, w_gate, w_ple,
                       cache_k, cache_v, dec_len):
    n_tok, d = x.shape
    n_all = cache_k.shape[0]
    n_seq = min(16, n_all)
    tq = n_seq * dec_len
    ple = p_all.shape[-1]
    kernel = functools.partial(_attn_sample_kernel, n_seq=n_seq, dec_len=dec_len)
    tok = lambda i: (i, 0)
    cache_blk = lambda i: (i, 0, 0)
    return pl.pallas_call(
        kernel,
        grid=(n_all // n_seq,),
        in_specs=[
            pl.BlockSpec(memory_space=pltpu.SMEM),
            pl.BlockSpec((tq, d), tok),
            pl.BlockSpec((None, tq, ple), lambda i: (layer, i, 0)),
            _const_spec((tq, LANES)), _const_spec((tq, LANES)),
            _const_spec(gpre.shape), _const_spec(gpost.shape),
            _const_spec(w_in.shape), _const_spec(w_out.shape),
            _const_spec(w_gate.shape), _const_spec(w_ple.shape),
            pl.BlockSpec((n_seq, WINDOW, ATTN_KV_DIM), cache_blk),
            pl.BlockSpec((n_seq, WINDOW, ATTN_KV_DIM), cache_blk),
        ],
        out_specs=[
            pl.BlockSpec((tq, d), tok),
            pl.BlockSpec((n_seq, WINDOW, ATTN_KV_DIM), cache_blk),
            pl.BlockSpec((n_seq, WINDOW, ATTN_KV_DIM), cache_blk),
        ],
        out_shape=[
            jax.ShapeDtypeStruct(x.shape, F32),
            jax.ShapeDtypeStruct(cache_k.shape, F32),
            jax.ShapeDtypeStruct(cache_v.shape, F32),
        ],
        scratch_shapes=[
            pltpu.VMEM((tq, ATTN_Q_DIM), F32),
            pltpu.VMEM((tq, ATTN_KV_DIM), F32),
            pltpu.VMEM((tq, ATTN_KV_DIM), F32),
            pltpu.VMEM((tq, ATTN_Q_DIM), F32),
        ],
        compiler_params=pltpu.CompilerParams(
            dimension_semantics=("arbitrary",),
            vmem_limit_bytes=VMEM_LIMIT_BYTES),
        name="attn_sample_layer",
    )(sinks, x, p_all, cos, sin, gpre, gpost, w_in, w_out, w_gate, w_ple, cache_k, cache_v)


def _ret_project(h, cos, sin, win_ref, chunk_len):
    q0, k0, v0, g0 = 0, RET_QK_W, 2 * RET_QK_W, 2 * RET_QK_W + RET_VW
    q = _rope_ret(_dot(h, win_ref[:, q0:k0]), cos, sin)
    k = _rope_ret(_dot(h, win_ref[:, k0:v0]), cos, sin) * (RET_QK_DIM ** -0.5)
    v = _dot(h, win_ref[:, v0:g0])
    g = _dot(h, win_ref[:, g0:])
    n = h.shape[0]
    idx = (lax.broadcasted_iota(jnp.int32, (n, 1), 0) % chunk_len).astype(F32)
    kd = jnp.concatenate(
        [k[:, hd * RET_QK_DIM:(hd + 1) * RET_QK_DIM]
         * jnp.exp((chunk_len - 1.0 - idx) * RET_LOG_GAMMA[hd]) for hd in range(RET_HEADS)], axis=1)
    return q, k, kd, v, g


def _ret_decay_mask(n, chunk_len, hd):
    ri = lax.broadcasted_iota(jnp.int32, (n, n), 0)
    ci = lax.broadcasted_iota(jnp.int32, (n, n), 1)
    diff = ri - ci
    ok = (diff >= 0) & ((ri // chunk_len) == (ci // chunk_len))
    return jnp.where(ok, jnp.exp(jnp.maximum(diff, 0).astype(F32) * RET_LOG_GAMMA[hd]), 0.0)


def _ret_finish(o, g, x, p, gpost, wout_ref, wgate_ref, wple_ref):
    outs = []
    for hd in range(RET_HEADS):
        oh = o[:, hd * RET_V_DIM:(hd + 1) * RET_V_DIM]
        mu = jnp.mean(oh, axis=-1, keepdims=True)
        var = jnp.mean(jnp.square(oh - mu), axis=-1, keepdims=True)
        outs.append((oh - mu) * lax.rsqrt(var + NORM_EPS))
    on = jnp.concatenate(outs, axis=1)
    og = (on * _silu(g)).astype(BF16)
    y = _dot(og, wout_ref[...])
    return _residual_tail(x, y, p, gpost, wgate_ref, wple_ref)


def _ret_prompt_kernel(x_ref, p_ref, cos_ref, sin_ref, gpre_ref, gpost_ref,
                       win_ref, wout_ref, wgate_ref, wple_ref,
                       xo_ref, st_ref,
                       q_scr, k_scr, kd_scr, v_scr, o_scr, *, tq):
    t = pl.program_id(1)

    @pl.when(t == 0)
    def _():
        st_ref[...] = jnp.zeros(st_ref.shape, F32)

    x = x_ref[...]
    h = _rms_norm(x, gpre_ref[...]).astype(BF16)
    q, k, kd, v, g = _ret_project(h, cos_ref[...], sin_ref[...], win_ref, RET_CHUNK)
    q_scr[...] = q.astype(BF16)
    k_scr[...] = k.astype(BF16)
    kd_scr[...] = kd.astype(BF16)
    v_scr[...] = v.astype(BF16)

    idx = lax.broadcasted_iota(jnp.int32, (RET_CHUNK, 1), 0).astype(F32)

    def chunk(c, carry):
        r0 = pl.multiple_of(c * RET_CHUNK, RET_CHUNK)
        for hd in range(RET_HEADS):
            lg = RET_LOG_GAMMA[hd]
            qs = slice(hd * RET_QK_DIM, (hd + 1) * RET_QK_DIM)
            vs = slice(hd * RET_V_DIM, (hd + 1) * RET_V_DIM)
            qh = q_scr[pl.ds(r0, RET_CHUNK), qs]
            kh = k_scr[pl.ds(r0, RET_CHUNK), qs]
            kdh = kd_scr[pl.ds(r0, RET_CHUNK), qs]
            vh = v_scr[pl.ds(r0, RET_CHUNK), vs]
            a = _dot_nt(qh, kh) * _ret_decay_mask(RET_CHUNK, RET_CHUNK, hd)
            inner = _dot(a.astype(BF16), vh)
            state = st_ref[hd]
            cross = _dot(qh, state.astype(BF16)) * jnp.exp((idx + 1.0) * lg)
            o_scr[pl.ds(r0, RET_CHUNK), vs] = inner + cross
            st_ref[hd] = math.exp(RET_CHUNK * lg) * state + _dot_tn(kdh, vh)
        return carry

    lax.fori_loop(0, tq // RET_CHUNK, chunk, 0, unroll=True)

    xo_ref[...] = _ret_finish(o_scr[...], g, x, p_ref[...], gpost_ref[...],
                              wout_ref, wgate_ref, wple_ref)


def _ret_prompt_layer(layer, x, p_all, cos, sin, gpre, gpost, w_in, w_out, w_gate, w_ple):
    nb, seq, d = x.shape
    tq = min(512, seq)
    ple = p_all.shape[-1]
    kernel = functools.partial(_ret_prompt_kernel, tq=tq)
    tok = lambda b, t: (b, t, 0)
    return pl.pallas_call(
        kernel,
        grid=(nb, seq // tq),
        in_specs=[
            pl.BlockSpec((None, tq, d), tok),
            pl.BlockSpec((None, None, tq, ple), lambda b, t: (layer, b, t, 0)),
            pl.BlockSpec((tq, LANES), lambda b, t: (t, 0)),
            pl.BlockSpec((tq, LANES), lambda b, t: (t, 0)),
            _const_spec(gpre.shape), _const_spec(gpost.shape),
            _const_spec(w_in.shape), _const_spec(w_out.shape),
            _const_spec(w_gate.shape), _const_spec(w_ple.shape),
        ],
        out_specs=[
            pl.BlockSpec((None, tq, d), tok),
            pl.BlockSpec((None, RET_HEADS, RET_QK_DIM, RET_V_DIM), lambda b, t: (b, 0, 0, 0)),
        ],
        out_shape=[
            jax.ShapeDtypeStruct(x.shape, F32),
            jax.ShapeDtypeStruct((nb, RET_HEADS, RET_QK_DIM, RET_V_DIM), F32),
        ],
        scratch_shapes=[
            pltpu.VMEM((tq, RET_QK_W), BF16),
            pltpu.VMEM((tq, RET_QK_W), BF16),
            pltpu.VMEM((tq, RET_QK_W), BF16),
            pltpu.VMEM((tq, RET_VW), BF16),
            pltpu.VMEM((tq, RET_VW), F32),
        ],
        compiler_params=pltpu.CompilerParams(
            dimension_semantics=("arbitrary", "arbitrary"),
            vmem_limit_bytes=VMEM_LIMIT_BYTES),
        name="ret_prompt_layer",
    )(x, p_all, cos, sin, gpre, gpost, w_in, w_out, w_gate, w_ple)


def _ret_sample_proj_kernel(x_ref, cos_ref, sin_ref, gpre_ref, win_ref,
                            q_ref, k_ref, kd_ref, v_ref, g_ref, *, dec_len):
    h = _rms_norm(x_ref[...], gpre_ref[...]).astype(BF16)
    q, k, kd, v, g = _ret_project(h, cos_ref[...], sin_ref[...], win_ref, dec_len)
    q_ref[...] = q
    k_ref[...] = k
    kd_ref[...] = kd
    v_ref[...] = v
    g_ref[...] = g


def _ret_sample_proj(x, cos, sin, gpre, w_in, dec_len):
    n_tok, d = x.shape
    tq = min(512, n_tok)
    kernel = functools.partial(_ret_sample_proj_kernel, dec_len=dec_len)
    tok = lambda i: (i, 0)
    widths = (RET_QK_W, RET_QK_W, RET_QK_W, RET_VW, RET_VW)
    return pl.pallas_call(
        kernel,
        grid=(n_tok // tq,),
        in_specs=[
            pl.BlockSpec((tq, d), tok),
            _const_spec((tq, LANES)), _const_spec((tq, LANES)),
            _const_spec(gpre.shape), _const_spec(w_in.shape),
        ],
        out_specs=[pl.BlockSpec((tq, w), tok) for w in widths],
        out_shape=[jax.ShapeDtypeStruct((n_tok, w), F32) for w in widths],
        compiler_params=pltpu.CompilerParams(
            dimension_semantics=("arbitrary",),
            vmem_limit_bytes=VMEM_LIMIT_BYTES),
        name="ret_sample_proj",
    )(x, cos, sin, gpre, w_in)


def _ret_sample_mix_kernel(q_ref, k_ref, kd_ref, v_ref, st_ref, o_ref, sto_ref, *, n_seq, dec_len):
    pad_rows = 2 * dec_len
    idx = lax.broadcasted_iota(jnp.int32, (pad_rows, 1), 0).astype(F32)

    def padded(ref, r0, cols):
        val = ref[pl.ds(r0, dec_len), cols]
        return jnp.concatenate([val, jnp.zeros_like(val)], axis=0).astype(BF16)

    def seq_body(s, carry):
        r0 = pl.multiple_of(s * dec_len, dec_len)
        for hd in range(RET_HEADS):
            lg = RET_LOG_GAMMA[hd]
            qs = slice(hd * RET_QK_DIM, (hd + 1) * RET_QK_DIM)
            vs = slice(hd * RET_V_DIM, (hd + 1) * RET_V_DIM)
            qh = padded(q_ref, r0, qs)
            kh = padded(k_ref, r0, qs)
            kdh = padded(kd_ref, r0, qs)
            vh = padded(v_ref, r0, vs)
            a = _dot_nt(qh, kh) * _ret_decay_mask(pad_rows, dec_len, hd)
            inner = _dot(a.astype(BF16), vh)
            state = st_ref[s, hd]
            cross = _dot(qh, state.astype(BF16)) * jnp.exp((idx + 1.0) * lg)
            o_ref[pl.ds(r0, dec_len), vs] = (inner + cross)[0:dec_len, :]
            sto_ref[s, hd] = math.exp(dec_len * lg) * state + _dot_tn(kdh, vh)
        return carry

    lax.fori_loop(0, n_seq, seq_body, 0)


def _ret_sample_mix(q, k, kd, v, state, dec_len):
    n_all = state.shape[0]
    n_seq = min(4, n_all)
    tq = n_seq * dec_len
    kernel = functools.partial(_ret_sample_mix_kernel, n_seq=n_seq, dec_len=dec_len)
    tok = lambda i: (i, 0)
    st_blk = lambda i: (i, 0, 0, 0)
    st_shape = (n_seq, RET_HEADS, RET_QK_DIM, RET_V_DIM)
    return pl.pallas_call(
        kernel,
        grid=(n_all // n_seq,),
        in_specs=[
            pl.BlockSpec((tq, RET_QK_W), tok), pl.BlockSpec((tq, RET_QK_W), tok),
            pl.BlockSpec((tq, RET_QK_W), tok), pl.BlockSpec((tq, RET_VW), tok),
            pl.BlockSpec(st_shape, st_blk),
        ],
        out_specs=[pl.BlockSpec((tq, RET_VW), tok), pl.BlockSpec(st_shape, st_blk)],
        out_shape=[
            jax.ShapeDtypeStruct((q.shape[0], RET_VW), F32),
            jax.ShapeDtypeStruct(state.shape, F32),
        ],
        compiler_params=pltpu.CompilerParams(
            dimension_semantics=("arbitrary",),
            vmem_limit_bytes=VMEM_LIMIT_BYTES),
        name="ret_sample_mix",
    )(q, k, kd, v, state)


def _ret_sample_finish_kernel(o_ref, g_ref, x_ref, p_ref, gpost_ref, wout_ref, wgate_ref, wple_ref,
                              xo_ref):
    xo_ref[...] = _ret_finish(o_ref[...], g_ref[...], x_ref[...], p_ref[...], gpost_ref[...],
                              wout_ref, wgate_ref, wple_ref)


def _ret_sample_finish(layer, o, g, x, p_all, gpost, w_out, w_gate, w_ple):
    n_tok, d = x.shape
    tq = min(512, n_tok)
    ple = p_all.shape[-1]
    tok = lambda i: (i, 0)
    return pl.pallas_call(
        _ret_sample_finish_kernel,
        grid=(n_tok // tq,),
        in_specs=[
            pl.BlockSpec((tq, RET_VW), tok), pl.BlockSpec((tq, RET_VW), tok),
            pl.BlockSpec((tq, d), tok),
            pl.BlockSpec((None, tq, ple), lambda i: (layer, i, 0)),
            _const_spec(gpost.shape), _const_spec(w_out.shape),
            _const_spec(w_gate.shape), _const_spec(w_ple.shape),
        ],
        out_specs=pl.BlockSpec((tq, d), tok),
        out_shape=jax.ShapeDtypeStruct(x.shape, F32),
        compiler_params=pltpu.CompilerParams(
            dimension_semantics=("arbitrary",),
            vmem_limit_bytes=VMEM_LIMIT_BYTES),
        name="ret_sample_finish",
    )(o, g, x, p_all, gpost, w_out, w_gate, w_ple)


def _rope_tables(pos, head_dim):
    half = head_dim // 2
    inv = ROPE_THETA ** (-jnp.arange(half, dtype=F32) / half)
    ang = pos.astype(F32)[:, None] * inv[None, :]
    return jnp.cos(ang), jnp.sin(ang)


def _attn_tables(pos):
    cos, sin = _rope_tables(pos, ATTN_HEAD_DIM)
    reps = LANES // ATTN_HEAD_DIM
    return (jnp.tile(jnp.concatenate([cos, cos], axis=1), (1, reps)),
            jnp.tile(jnp.concatenate([-sin, sin], axis=1), (1, reps)))


def kernel(x_prompt, x_sample, cache_k_win, cache_v_win, state_ret, p_prompt, p_sample, pre_norm,
           post_norm, w_in_attn, attn_sinks, w_out_attn, w_in_ret, w_out_ret, w_ple, w_ple_gate):
    depth = p_prompt.shape[0]
    nb, seq, d = x_prompt.shape
    n_dec, dec_len, _ = x_sample.shape
    n_stok = n_dec * dec_len
    assert seq % 512 == 0 or seq in (128, 256), seq

    pos_p = jnp.arange(seq)
    pos_s = PAST_LEN + jnp.arange(dec_len)
    cos_ap, sin_ap = _attn_tables(pos_p)
    cos_apt, sin_apt = (a.T for a in _rope_tables(pos_p, ATTN_HEAD_DIM))
    cos_as, sin_as = _attn_tables(pos_s)
    cos_rp, sin_rp = _rope_tables(pos_p, RET_QK_DIM)
    cos_rs, sin_rs = _rope_tables(pos_s, RET_QK_DIM)
    attn_s_tile = min(16, n_dec)
    ret_s_tile = min(512, n_stok) // dec_len
    cos_as, sin_as = (jnp.tile(a, (attn_s_tile, 1)) for a in (cos_as, sin_as))
    cos_rs, sin_rs = (jnp.tile(a, (ret_s_tile, 1)) for a in (cos_rs, sin_rs))

    xp = x_prompt
    xs = x_sample.reshape(n_stok, d)
    p_s = p_sample.reshape(depth, n_stok, p_sample.shape[-1])
    kwp, vwp, kws, vws, srp, srs = [], [], [], [], [], []
    for i in range(depth):
        j = i // 2
        gpre = pre_norm[i][None, :]
        gpost = post_norm[i][None, :]
        w_gate = w_ple_gate[i].astype(BF16)
        w_pl = w_ple[i].astype(BF16)
        if i % 2 == 0:
            w_in = w_in_attn[j].astype(BF16)
            w_out = w_out_attn[j].astype(BF16)
            sinks = attn_sinks[j]
            k0, v0, g0 = ATTN_Q_DIM, ATTN_Q_DIM + ATTN_KV_DIM, ATTN_Q_DIM + 2 * ATTN_KV_DIM
            xp, kpt, vpt = _attn_prompt_layer(
                i, xp, p_prompt, (cos_ap, sin_ap, cos_apt, sin_apt), gpre, gpost, sinks,
                w_in[:, :k0].T, w_in[:, k0:v0], w_in[:, v0:g0].T, w_in[:, g0:],
                w_out, w_gate, w_pl)
            ck = cache_k_win[j].reshape(n_dec, WINDOW, ATTN_KV_DIM)
            cv = cache_v_win[j].reshape(n_dec, WINDOW, ATTN_KV_DIM)
            xs, kn, vn = _attn_sample_layer(i, xs, p_s, cos_as, sin_as, gpre, gpost, sinks,
                                            w_in, w_out, w_gate, w_pl, ck, cv, dec_len)
            win_shape = (WINDOW, ATTN_KV_HEADS, ATTN_HEAD_DIM)
            from_t = lambda a: a.reshape(nb, ATTN_KV_HEADS, ATTN_HEAD_DIM, WINDOW).transpose(0, 3, 1, 2)
            kwp.append(from_t(kpt))
            vwp.append(from_t(vpt))
            kws.append(kn.reshape((n_dec,) + win_shape))
            vws.append(vn.reshape((n_dec,) + win_shape))
        else:
            w_in = w_in_ret[j].astype(BF16)
            w_out = w_out_ret[j].astype(BF16)
            xp, sp = _ret_prompt_layer(i, xp, p_prompt, cos_rp, sin_rp, gpre, gpost,
                                       w_in, w_out, w_gate, w_pl)
            q, k, kd, v, g = _ret_sample_proj(xs, cos_rs, sin_rs, gpre, w_in, dec_len)
            o, sn = _ret_sample_mix(q, k, kd, v, state_ret[j], dec_len)
            xs = _ret_sample_finish(i, o, g, xs, p_s, gpost, w_out, w_gate, w_pl)
            srp.append(sp)
            srs.append(sn)
    return (xp, xs.reshape(n_dec, dec_len, d), jnp.stack(kwp), jnp.stack(vwp), jnp.stack(kws),
            jnp.stack(vws), jnp.stack(srp), jnp.stack(srs))
```

```python
import functools
import math

import numpy as np
import jax
import jax.numpy as jnp
from jax import lax
from jax.experimental import pallas as pl
from jax.experimental.pallas import tpu as pltpu

F32 = jnp.float32
BF16 = jnp.bfloat16

PAST_LEN = 16384
ROPE_THETA = 10000.0
NORM_EPS = 1e-6
NEG_INF = -1e30
WINDOW = 128
ATTN_HEADS = 16
ATTN_KV_HEADS = 4
ATTN_GROUP = ATTN_HEADS // ATTN_KV_HEADS
ATTN_HEAD_DIM = 64
ATTN_Q_DIM = ATTN_HEADS * ATTN_HEAD_DIM
ATTN_KV_DIM = ATTN_KV_HEADS * ATTN_HEAD_DIM
RET_HEADS = 4
RET_QK_DIM = 256
RET_V_DIM = 512
RET_QK_W = RET_HEADS * RET_QK_DIM
RET_VW = RET_HEADS * RET_V_DIM
RET_CHUNK = 256
LANES = 128

LOG2_E = math.log2(math.e)
RET_LOG_GAMMA = tuple(math.log1p(-(2.0 ** (-5.0 - h))) for h in range(RET_HEADS))

VMEM_LIMIT_BYTES = 56 * 1024 * 1024
SAMPLE_ATTN_PIPELINE_GROUP = 8


def _dot(a, b):
    return jnp.dot(a, b, preferred_element_type=F32)


def _dot_nt(a, b):
    return lax.dot_general(a, b, (((1,), (1,)), ((), ())), preferred_element_type=F32)


def _dot_tn(a, b):
    return lax.dot_general(a, b, (((0,), (0,)), ((), ())), preferred_element_type=F32)


def _rms_norm(x, g):
    return x * lax.rsqrt(jnp.mean(x * x, axis=-1, keepdims=True) + NORM_EPS) * g


def _silu(g):
    return g * jax.nn.sigmoid(g)


def _residual_tail(x, y, p, gpost, wgate_ref, wple_ref):
    x1 = x + _rms_norm(y, gpost)
    gate = jax.nn.sigmoid(_dot(x1.astype(BF16), wgate_ref[...]))
    return x1 + gate * _dot(p.astype(BF16), wple_ref[...])


def _rope_attn(x, cos, sin_signed):
    lane = lax.broadcasted_iota(jnp.int32, (1, LANES), 1)
    first_half = (lane % ATTN_HEAD_DIM) < (ATTN_HEAD_DIM // 2)
    outs = []
    for j in range(x.shape[1] // LANES):
        xj = x[:, j * LANES:(j + 1) * LANES]
        fwd = pltpu.roll(xj, LANES - ATTN_HEAD_DIM // 2, axis=1)
        bwd = pltpu.roll(xj, ATTN_HEAD_DIM // 2, axis=1)
        outs.append(xj * cos + jnp.where(first_half, fwd, bwd) * sin_signed)
    return jnp.concatenate(outs, axis=1)


def _rope_ret(x, cos, sin):
    half = RET_QK_DIM // 2
    outs = []
    for h in range(x.shape[1] // RET_QK_DIM):
        x1 = x[:, h * RET_QK_DIM:h * RET_QK_DIM + half]
        x2 = x[:, h * RET_QK_DIM + half:(h + 1) * RET_QK_DIM]
        outs.append(x1 * cos - x2 * sin)
        outs.append(x2 * cos + x1 * sin)
    return jnp.concatenate(outs, axis=1)


def _attn_prompt_kernel(sink_ref, x_ref, p_ref, cos_ref, sin_ref, cost_ref, sint_ref,
                        gpre_ref, gpost_ref, wqt_ref, wk_ref, wvt_ref, wg_ref,
                        wout_ref, wgate_ref, wple_ref,
                        xo_ref, kw_ref, vw_ref,
                        qt_scr, kext_scr, vt_scr, st_scr, pt_scr, inv_scr, ot_scr, *, tq):
    nblk = tq // WINDOW
    half = ATTN_HEAD_DIM // 2
    t = pl.program_id(1)
    nt = pl.num_programs(1)

    @pl.when(t == 0)
    def _():
        kext_scr[0:WINDOW, :] = jnp.zeros((WINDOW, ATTN_KV_DIM), BF16)
        vt_scr[0] = jnp.zeros((ATTN_KV_DIM, WINDOW), BF16)

    x = x_ref[...]
    h = _rms_norm(x, gpre_ref[...]).astype(BF16)

    qt = _dot_nt(wqt_ref[...], h)
    cost = cost_ref[...]
    sint = sint_ref[...]
    pieces = []
    for hd in range(ATTN_HEADS):
        x1 = qt[hd * ATTN_HEAD_DIM:hd * ATTN_HEAD_DIM + half, :]
        x2 = qt[hd * ATTN_HEAD_DIM + half:(hd + 1) * ATTN_HEAD_DIM, :]
        pieces.append(x1 * cost - x2 * sint)
        pieces.append(x2 * cost + x1 * sint)
    qrot = (jnp.concatenate(pieces, axis=0) * (ATTN_HEAD_DIM ** -0.5 * LOG2_E)).astype(BF16)
    for jb in range(nblk):
        qt_scr[jb] = qrot[:, jb * WINDOW:(jb + 1) * WINDOW]

    k = _rope_attn(_dot(h, wk_ref[...]), cos_ref[...], sin_ref[...])
    kext_scr[WINDOW:, :] = k.astype(BF16)
    vt = _dot_nt(wvt_ref[...], h)
    vtb = vt.astype(BF16)
    for jb in range(nblk):
        vt_scr[jb + 1] = vtb[:, jb * WINDOW:(jb + 1) * WINDOW]

    @pl.when(t == nt - 1)
    def _():
        kw_ref[...] = k[tq - WINDOW:, :].T
        vw_ref[...] = vt[:, tq - WINDOW:]

    kj = lax.broadcasted_iota(jnp.int32, (2 * WINDOW, WINDOW), 0)
    qi = lax.broadcasted_iota(jnp.int32, (2 * WINDOW, WINDOW), 1) + WINDOW
    diff = qi - kj
    band = (diff >= 0) & (diff < WINDOW)
    zero_rows = jnp.zeros((ATTN_HEAD_DIM, ATTN_GROUP * WINDOW), BF16)
    groups = [(jb, kh) for jb in range(nblk) for kh in range(ATTN_KV_HEADS)]

    for gidx, (jb, kh) in enumerate(groups):
        heads = [kh * ATTN_GROUP + gi for gi in range(ATTN_GROUP)]
        pair = kh // 2
        kb = kext_scr[jb * WINDOW:(jb + 2) * WINDOW, pair * LANES:(pair + 1) * LANES]
        qg = jnp.concatenate(
            [qt_scr[jb, hd * ATTN_HEAD_DIM:(hd + 1) * ATTN_HEAD_DIM, :] for hd in heads], axis=1)
        qg = jnp.concatenate([qg, zero_rows] if kh % 2 == 0 else [zero_rows, qg], axis=0)
        st_scr[gidx] = _dot(kb, qg)

    for gidx, (jb, kh) in enumerate(groups):
        mask = band & ((t > 0) | (kj >= WINDOW)) if jb == 0 else band
        for gi in range(ATTN_GROUP):
            hd = kh * ATTN_GROUP + gi
            cols = slice(gi * WINDOW, (gi + 1) * WINDOW)
            s = jnp.where(mask, st_scr[gidx, :, cols], NEG_INF)
            sink = sink_ref[hd] * LOG2_E
            m = jnp.maximum(jnp.max(s, axis=0, keepdims=True), sink)
            pr = jnp.exp2(s - m)
            den = jnp.sum(pr, axis=0, keepdims=True) + jnp.exp2(sink - m)
            pt_scr[gidx, :, cols] = pr.astype(BF16)
            inv_scr[jb, hd:hd + 1, :] = 1.0 / den

    for gidx, (jb, kh) in enumerate(groups):
        vband = jnp.concatenate([vt_scr[jb, kh * ATTN_HEAD_DIM:(kh + 1) * ATTN_HEAD_DIM, :],
                                 vt_scr[jb + 1, kh * ATTN_HEAD_DIM:(kh + 1) * ATTN_HEAD_DIM, :]],
                                axis=1)
        otg = _dot(vband, pt_scr[gidx])
        for gi in range(ATTN_GROUP):
            hd = kh * ATTN_GROUP + gi
            ot_scr[jb, hd * ATTN_HEAD_DIM:(hd + 1) * ATTN_HEAD_DIM, :] = (
                otg[:, gi * WINDOW:(gi + 1) * WINDOW] * inv_scr[jb, hd:hd + 1, :])

    o = jnp.concatenate([ot_scr[jb].T for jb in range(nblk)], axis=0)
    g = _dot(h, wg_ref[...])
    og = (o * _silu(g)).astype(BF16)
    y = _dot(og, wout_ref[...])
    xo_ref[...] = _residual_tail(x, y, p_ref[...], gpost_ref[...], wgate_ref, wple_ref)

    kext_scr[0:WINDOW, :] = kext_scr[tq:tq + WINDOW, :]
    vt_scr[0] = vt_scr[nblk]


def _const_spec(shape):
    nd = len(shape)
    return pl.BlockSpec(shape, lambda *_: (0,) * nd)


def _attn_prompt_layer(layer, x, p_all, tables, gpre, gpost, sinks, w_qt, w_k, w_vt, w_g,
                       w_out, w_gate, w_ple):
    nb, seq, d = x.shape
    tq = min(512, seq)
    nblk = tq // WINDOW
    ple = p_all.shape[-1]
    cos, sin, cost, sint = tables
    kernel = functools.partial(_attn_prompt_kernel, tq=tq)
    tok = lambda b, t: (b, t, 0)
    win_blk = lambda b, t: (b, 0, 0)
    consts = (gpre, gpost, w_qt, w_k, w_vt, w_g, w_out, w_gate, w_ple)
    return pl.pallas_call(
        kernel,
        grid=(nb, seq // tq),
        in_specs=[
            pl.BlockSpec(memory_space=pltpu.SMEM),
            pl.BlockSpec((None, tq, d), tok),
            pl.BlockSpec((None, None, tq, ple), lambda b, t: (layer, b, t, 0)),
            pl.BlockSpec((tq, LANES), lambda b, t: (t, 0)),
            pl.BlockSpec((tq, LANES), lambda b, t: (t, 0)),
            pl.BlockSpec((ATTN_HEAD_DIM // 2, tq), lambda b, t: (0, t)),
            pl.BlockSpec((ATTN_HEAD_DIM // 2, tq), lambda b, t: (0, t)),
        ] + [_const_spec(c.shape) for c in consts],
        out_specs=[
            pl.BlockSpec((None, tq, d), tok),
            pl.BlockSpec((None, ATTN_KV_DIM, WINDOW), win_blk),
            pl.BlockSpec((None, ATTN_KV_DIM, WINDOW), win_blk),
        ],
        out_shape=[
            jax.ShapeDtypeStruct(x.shape, F32),
            jax.ShapeDtypeStruct((nb, ATTN_KV_DIM, WINDOW), F32),
            jax.ShapeDtypeStruct((nb, ATTN_KV_DIM, WINDOW), F32),
        ],
        scratch_shapes=[
            pltpu.VMEM((nblk, ATTN_Q_DIM, WINDOW), BF16),
            pltpu.VMEM((tq + WINDOW, ATTN_KV_DIM), BF16),
            pltpu.VMEM((nblk + 1, ATTN_KV_DIM, WINDOW), BF16),
            pltpu.VMEM((nblk * ATTN_KV_HEADS, 2 * WINDOW, ATTN_GROUP * WINDOW), F32),
            pltpu.VMEM((nblk * ATTN_KV_HEADS, 2 * WINDOW, ATTN_GROUP * WINDOW), BF16),
            pltpu.VMEM((nblk, ATTN_HEADS, WINDOW), F32),
            pltpu.VMEM((nblk, ATTN_Q_DIM, WINDOW), F32),
        ],
        compiler_params=pltpu.CompilerParams(
            dimension_semantics=("arbitrary", "arbitrary"),
            vmem_limit_bytes=VMEM_LIMIT_BYTES),
        name="attn_prompt_layer",
    )(sinks, x, p_all, cos, sin, cost, sint, *consts)


def _attn_sample_kernel(sink_ref, x_ref, p_ref, cos_ref, sin_ref, cost_ref, sint_ref,
                        gpre_ref, gpost_ref, wq_ref, wkt_ref, wvt_ref, wg_ref,
                        wout_ref, wgate_ref, wple_ref, kc_ref, vc_ref,
                        xo_ref, ko_ref, vo_ref,
                        q_scr, knt_scr, vnt_scr, o_scr, *, n_seq, dec_len):
    half = ATTN_HEAD_DIM // 2
    x = x_ref[...]
    h = _rms_norm(x, gpre_ref[...]).astype(BF16)
    q_scr[...] = (_rope_attn(_dot(h, wq_ref[...]), cos_ref[...], sin_ref[...])
                  * (ATTN_HEAD_DIM ** -0.5 * LOG2_E))
    knt = _dot_nt(wkt_ref[...], h)
    cost = cost_ref[...]
    sint = sint_ref[...]
    pieces = []
    for kh in range(ATTN_KV_HEADS):
        x1 = knt[kh * ATTN_HEAD_DIM:kh * ATTN_HEAD_DIM + half, :]
        x2 = knt[kh * ATTN_HEAD_DIM + half:(kh + 1) * ATTN_HEAD_DIM, :]
        pieces.append(x1 * cost - x2 * sint)
        pieces.append(x2 * cost + x1 * sint)
    knt_scr[...] = jnp.concatenate(pieces, axis=0)
    vnt_scr[...] = _dot_nt(wvt_ref[...], h)

    rows = 2 * dec_len
    tok = lax.broadcasted_iota(jnp.int32, (rows, 2 * WINDOW), 0) % dec_len
    col = lax.broadcasted_iota(jnp.int32, (rows, 2 * WINDOW), 1)
    upper_row = lax.broadcasted_iota(jnp.int32, (rows, 1), 0) >= dec_len
    lane = lax.broadcasted_iota(jnp.int32, (ATTN_HEAD_DIM, WINDOW), 1)
    keep_old = lane < WINDOW - dec_len
    zeros = jnp.zeros((ATTN_HEAD_DIM, 2 * WINDOW), BF16)

    def select_head(t, parity):
        return jnp.concatenate([t, zeros] if parity == 0 else [zeros, t], axis=0)

    def scores(s):
        tok_rows = slice(s * dec_len, (s + 1) * dec_len)
        new_col = col - (WINDOW + s * dec_len)
        mask = ((col < WINDOW) & (col > tok)) | ((new_col >= 0) & (new_col <= tok))
        new_shift = (WINDOW - dec_len - s * dec_len) % WINDOW
        out = []
        for kh in range(ATTN_KV_HEADS):
            hrows = slice(kh * ATTN_HEAD_DIM, (kh + 1) * ATTN_HEAD_DIM)
            kc = kc_ref[s, kh]
            vc = vc_ref[s, kh]
            kn = knt_scr[hrows, :]
            vn = vnt_scr[hrows, :]
            kn_at_end = pltpu.roll(kn, new_shift, axis=1) if new_shift else kn
            vn_at_end = pltpu.roll(vn, new_shift, axis=1) if new_shift else vn
            ko_ref[s, kh] = jnp.where(keep_old, pltpu.roll(kc, WINDOW - dec_len, axis=1), kn_at_end)
            vo_ref[s, kh] = jnp.where(keep_old, pltpu.roll(vc, WINDOW - dec_len, axis=1), vn_at_end)

            kt = jnp.concatenate([kc, kn], axis=1).astype(BF16)
            vt = jnp.concatenate([vc, vn], axis=1).astype(BF16)
            qp = jnp.concatenate([q_scr[tok_rows, (2 * kh + c) * LANES:(2 * kh + c + 1) * LANES]
                                  for c in range(2)], axis=0).astype(BF16)
            sc = [jnp.where(mask, _dot(qp, select_head(kt, parity)), NEG_INF)
                  for parity in range(2)]
            out.append((sc, vt))
        return out

    def values(s, per_head):
        tok_rows = slice(s * dec_len, (s + 1) * dec_len)
        probs = []
        for kh, (scs, vt) in enumerate(per_head):
            for parity, sc in enumerate(scs):
                hd_lo, hd_hi = 4 * kh + parity, 4 * kh + 2 + parity
                sink = jnp.where(upper_row, sink_ref[hd_hi], sink_ref[hd_lo]) * LOG2_E
                m = jnp.maximum(jnp.max(sc, axis=-1, keepdims=True), sink)
                pr = jnp.exp2(sc - m)
                den = jnp.sum(pr, axis=-1, keepdims=True) + jnp.exp2(sink - m)
                probs.append((pr / den).astype(BF16))
        for kh, (scs, vt) in enumerate(per_head):
            o_pair = (_dot_nt(probs[2 * kh], select_head(vt, 0))
                      + _dot_nt(probs[2 * kh + 1], select_head(vt, 1)))
            for c in range(2):
                o_scr[tok_rows, (2 * kh + c) * LANES:(2 * kh + c + 1) * LANES] = (
                    o_pair[c * dec_len:(c + 1) * dec_len, :])

    group = min(SAMPLE_ATTN_PIPELINE_GROUP, n_seq)
    pending = [scores(s) for s in range(group)]
    for s0 in range(0, n_seq, group):
        upcoming = [scores(s) for s in range(s0 + group, min(s0 + 2 * group, n_seq))]
        for i, per_head in enumerate(pending):
            values(s0 + i, per_head)
        pending = upcoming

    g = _dot(h, wg_ref[...])
    og = (o_scr[...] * _silu(g)).astype(BF16)
    y = _dot(og, wout_ref[...])
    xo_ref[...] = _residual_tail(x, y, p_ref[...], gpost_ref[...], wgate_ref, wple_ref)


def _attn_sample_layer(layer, x, p_all, tables, gpre, gpost, sinks, w_q, w_kt, w_vt, w_g,
                       w_out, w_gate, w_ple, cache_kt, cache_vt, dec_len):
    n_tok, d = x.shape
    n_all = cache_kt.shape[0]
    n_seq = min(16, n_all)
    tq = n_seq * dec_len
    ple = p_all.shape[-1]
    cos, sin, cost, sint = tables
    kernel = functools.partial(_attn_sample_kernel, n_seq=n_seq, dec_len=dec_len)
    tok = lambda i: (i, 0)
    cache_shape = (n_seq, ATTN_KV_HEADS, ATTN_HEAD_DIM, WINDOW)
    cache_blk = lambda i: (i, 0, 0, 0)
    consts = (cos, sin, cost, sint, gpre, gpost, w_q, w_kt, w_vt, w_g, w_out, w_gate, w_ple)
    return pl.pallas_call(
        kernel,
        grid=(n_all // n_seq,),
        in_specs=[
            pl.BlockSpec(memory_space=pltpu.SMEM),
            pl.BlockSpec((tq, d), tok),
            pl.BlockSpec((None, tq, ple), lambda i: (layer, i, 0)),
        ] + [_const_spec(c.shape) for c in consts] + [
            pl.BlockSpec(cache_shape, cache_blk),
            pl.BlockSpec(cache_shape, cache_blk),
        ],
        out_specs=[
            pl.BlockSpec((tq, d), tok),
            pl.BlockSpec(cache_shape, cache_blk),
            pl.BlockSpec(cache_shape, cache_blk),
        ],
        out_shape=[
            jax.ShapeDtypeStruct(x.shape, F32),
            jax.ShapeDtypeStruct(cache_kt.shape, F32),
            jax.ShapeDtypeStruct(cache_vt.shape, F32),
        ],
        scratch_shapes=[
            pltpu.VMEM((tq, ATTN_Q_DIM), F32),
            pltpu.VMEM((ATTN_KV_DIM, tq), F32),
            pltpu.VMEM((ATTN_KV_DIM, tq), F32),
            pltpu.VMEM((tq, ATTN_Q_DIM), F32),
        ],
        compiler_params=pltpu.CompilerParams(
            dimension_semantics=("arbitrary",),
            vmem_limit_bytes=VMEM_LIMIT_BYTES),
        name="attn_sample_layer",
    )(sinks, x, p_all, *consts, cache_kt, cache_vt)


def _ret_project(h, cos, sin, win_ref, chunk_len):
    q0, k0, v0, g0 = 0, RET_QK_W, 2 * RET_QK_W, 2 * RET_QK_W + RET_VW
    q = _rope_ret(_dot(h, win_ref[:, q0:k0]), cos, sin)
    k = _rope_ret(_dot(h, win_ref[:, k0:v0]), cos, sin) * (RET_QK_DIM ** -0.5)
    v = _dot(h, win_ref[:, v0:g0])
    g = _dot(h, win_ref[:, g0:])
    n = h.shape[0]
    idx = (lax.broadcasted_iota(jnp.int32, (n, 1), 0) % chunk_len).astype(F32)
    kd = jnp.concatenate(
        [k[:, hd * RET_QK_DIM:(hd + 1) * RET_QK_DIM]
         * jnp.exp((chunk_len - 1.0 - idx) * RET_LOG_GAMMA[hd]) for hd in range(RET_HEADS)], axis=1)
    return q, k, kd, v, g


def _ret_decay_mask(n, chunk_len, hd):
    ri = lax.broadcasted_iota(jnp.int32, (n, n), 0)
    ci = lax.broadcasted_iota(jnp.int32, (n, n), 1)
    diff = ri - ci
    ok = (diff >= 0) & ((ri // chunk_len) == (ci // chunk_len))
    return jnp.where(ok, jnp.exp(jnp.maximum(diff, 0).astype(F32) * RET_LOG_GAMMA[hd]), 0.0)


def _ret_finish(o, g, x, p, gpost, wout_ref, wgate_ref, wple_ref):
    outs = []
    for hd in range(RET_HEADS):
        oh = o[:, hd * RET_V_DIM:(hd + 1) * RET_V_DIM]
        mu = jnp.mean(oh, axis=-1, keepdims=True)
        var = jnp.mean(jnp.square(oh - mu), axis=-1, keepdims=True)
        outs.append((oh - mu) * lax.rsqrt(var + NORM_EPS))
    on = jnp.concatenate(outs, axis=1)
    og = (on * _silu(g)).astype(BF16)
    y = _dot(og, wout_ref[...])
    return _residual_tail(x, y, p, gpost, wgate_ref, wple_ref)


def _ret_prompt_kernel(x_ref, p_ref, cos_ref, sin_ref, gpre_ref, gpost_ref,
                       win_ref, wout_ref, wgate_ref, wple_ref,
                       xo_ref, st_ref,
                       q_scr, k_scr, kd_scr, v_scr, o_scr, dm_scr, *, tq):
    b = pl.program_id(0)
    t = pl.program_id(1)

    @pl.when((b == 0) & (t == 0))
    def _():
        for hd in range(RET_HEADS):
            dm_scr[hd] = _ret_decay_mask(RET_CHUNK, RET_CHUNK, hd)

    @pl.when(t == 0)
    def _():
        st_ref[...] = jnp.zeros(st_ref.shape, F32)

    x = x_ref[...]
    h = _rms_norm(x, gpre_ref[...]).astype(BF16)
    q, k, kd, v, g = _ret_project(h, cos_ref[...], sin_ref[...], win_ref, RET_CHUNK)
    q_scr[...] = q.astype(BF16)
    k_scr[...] = k.astype(BF16)
    kd_scr[...] = kd.astype(BF16)
    v_scr[...] = v.astype(BF16)

    idx = lax.broadcasted_iota(jnp.int32, (RET_CHUNK, 1), 0).astype(F32)
    for c in range(tq // RET_CHUNK):
        rows = slice(c * RET_CHUNK, (c + 1) * RET_CHUNK)
        for hd in range(RET_HEADS):
            lg = RET_LOG_GAMMA[hd]
            qs = slice(hd * RET_QK_DIM, (hd + 1) * RET_QK_DIM)
            vs = slice(hd * RET_V_DIM, (hd + 1) * RET_V_DIM)
            qh = q_scr[rows, qs]
            vh = v_scr[rows, vs]
            a = _dot_nt(qh, k_scr[rows, qs]) * dm_scr[hd]
            inner = _dot(a.astype(BF16), vh)
            state = st_ref[hd]
            cross = _dot(qh, state.astype(BF16)) * jnp.exp((idx + 1.0) * lg)
            o_scr[rows, vs] = inner + cross
            st_ref[hd] = math.exp(RET_CHUNK * lg) * state + _dot_tn(kd_scr[rows, qs], vh)

    xo_ref[...] = _ret_finish(o_scr[...], g, x, p_ref[...], gpost_ref[...],
                              wout_ref, wgate_ref, wple_ref)


def _ret_prompt_layer(layer, x, p_all, cos, sin, gpre, gpost, w_in, w_out, w_gate, w_ple):
    nb, seq, d = x.shape
    tq = min(512, seq)
    ple = p_all.shape[-1]
    kernel = functools.partial(_ret_prompt_kernel, tq=tq)
    tok = lambda b, t: (b, t, 0)
    return pl.pallas_call(
        kernel,
        grid=(nb, seq // tq),
        in_specs=[
            pl.BlockSpec((None, tq, d), tok),
            pl.BlockSpec((None, None, tq, ple), lambda b, t: (layer, b, t, 0)),
            pl.BlockSpec((tq, LANES), lambda b, t: (t, 0)),
            pl.BlockSpec((tq, LANES), lambda b, t: (t, 0)),
            _const_spec(gpre.shape), _const_spec(gpost.shape),
            _const_spec(w_in.shape), _const_spec(w_out.shape),
            _const_spec(w_gate.shape), _const_spec(w_ple.shape),
        ],
        out_specs=[
            pl.BlockSpec((None, tq, d), tok),
            pl.BlockSpec((None, RET_HEADS, RET_QK_DIM, RET_V_DIM), lambda b, t: (b, 0, 0, 0)),
        ],
        out_shape=[
            jax.ShapeDtypeStruct(x.shape, F32),
            jax.ShapeDtypeStruct((nb, RET_HEADS, RET_QK_DIM, RET_V_DIM), F32),
        ],
        scratch_shapes=[
            pltpu.VMEM((tq, RET_QK_W), BF16),
            pltpu.VMEM((tq, RET_QK_W), BF16),
            pltpu.VMEM((tq, RET_QK_W), BF16),
            pltpu.VMEM((tq, RET_VW), BF16),
            pltpu.VMEM((tq, RET_VW), F32),
            pltpu.VMEM((RET_HEADS, RET_CHUNK, RET_CHUNK), F32),
        ],
        compiler_params=pltpu.CompilerParams(
            dimension_semantics=("arbitrary", "arbitrary"),
            vmem_limit_bytes=VMEM_LIMIT_BYTES),
        name="ret_prompt_layer",
    )(x, p_all, cos, sin, gpre, gpost, w_in, w_out, w_gate, w_ple)


def _ret_sample_proj_kernel(x_ref, cos_ref, sin_ref, gpre_ref, win_ref,
                            q_ref, k_ref, kd_ref, v_ref, g_ref, *, dec_len):
    h = _rms_norm(x_ref[...], gpre_ref[...]).astype(BF16)
    q, k, kd, v, g = _ret_project(h, cos_ref[...], sin_ref[...], win_ref, dec_len)
    q_ref[...] = q
    k_ref[...] = k
    kd_ref[...] = kd
    v_ref[...] = v
    g_ref[...] = g


def _ret_sample_proj(x, cos, sin, gpre, w_in, dec_len):
    n_tok, d = x.shape
    tq = min(512, n_tok)
    kernel = functools.partial(_ret_sample_proj_kernel, dec_len=dec_len)
    tok = lambda i: (i, 0)
    widths = (RET_QK_W, RET_QK_W, RET_QK_W, RET_VW, RET_VW)
    return pl.pallas_call(
        kernel,
        grid=(n_tok // tq,),
        in_specs=[
            pl.BlockSpec((tq, d), tok),
            _const_spec((tq, LANES)), _const_spec((tq, LANES)),
            _const_spec(gpre.shape), _const_spec(w_in.shape),
        ],
        out_specs=[pl.BlockSpec((tq, w), tok) for w in widths],
        out_shape=[jax.ShapeDtypeStruct((n_tok, w), F32) for w in widths],
        compiler_params=pltpu.CompilerParams(
            dimension_semantics=("arbitrary",),
            vmem_limit_bytes=VMEM_LIMIT_BYTES),
        name="ret_sample_proj",
    )(x, cos, sin, gpre, w_in)


def _ret_sample_mix_kernel(q_ref, k_ref, kd_ref, v_ref, st_ref, o_ref, sto_ref, *, n_seq, dec_len):
    pad_rows = 2 * dec_len
    idx = lax.broadcasted_iota(jnp.int32, (pad_rows, 1), 0).astype(F32)

    def padded(ref, r0, cols):
        val = ref[pl.ds(r0, dec_len), cols]
        return jnp.concatenate([val, jnp.zeros_like(val)], axis=0).astype(BF16)

    def seq_body(s, carry):
        r0 = pl.multiple_of(s * dec_len, dec_len)
        for hd in range(RET_HEADS):
            lg = RET_LOG_GAMMA[hd]
            qs = slice(hd * RET_QK_DIM, (hd + 1) * RET_QK_DIM)
            vs = slice(hd * RET_V_DIM, (hd + 1) * RET_V_DIM)
            qh = padded(q_ref, r0, qs)
            kh = padded(k_ref, r0, qs)
            kdh = padded(kd_ref, r0, qs)
            vh = padded(v_ref, r0, vs)
            a = _dot_nt(qh, kh) * _ret_decay_mask(pad_rows, dec_len, hd)
            inner = _dot(a.astype(BF16), vh)
            state = st_ref[s, hd]
            cross = _dot(qh, state.astype(BF16)) * jnp.exp((idx + 1.0) * lg)
            o_ref[pl.ds(r0, dec_len), vs] = (inner + cross)[0:dec_len, :]
            sto_ref[s, hd] = math.exp(dec_len * lg) * state + _dot_tn(kdh, vh)
        return carry

    lax.fori_loop(0, n_seq, seq_body, 0)


def _ret_sample_mix(q, k, kd, v, state, dec_len):
    n_all = state.shape[0]
    n_seq = min(4, n_all)
    tq = n_seq * dec_len
    kernel = functools.partial(_ret_sample_mix_kernel, n_seq=n_seq, dec_len=dec_len)
    tok = lambda i: (i, 0)
    st_blk = lambda i: (i, 0, 0, 0)
    st_shape = (n_seq, RET_HEADS, RET_QK_DIM, RET_V_DIM)
    return pl.pallas_call(
        kernel,
        grid=(n_all // n_seq,),
        in_specs=[
            pl.BlockSpec((tq, RET_QK_W), tok), pl.BlockSpec((tq, RET_QK_W), tok),
            pl.BlockSpec((tq, RET_QK_W), tok), pl.BlockSpec((tq, RET_VW), tok),
            pl.BlockSpec(st_shape, st_blk),
        ],
        out_specs=[pl.BlockSpec((tq, RET_VW), tok), pl.BlockSpec(st_shape, st_blk)],
        out_shape=[
            jax.ShapeDtypeStruct((q.shape[0], RET_VW), F32),
            jax.ShapeDtypeStruct(state.shape, F32),
        ],
        compiler_params=pltpu.CompilerParams(
            dimension_semantics=("arbitrary",),
            vmem_limit_bytes=VMEM_LIMIT_BYTES),
        name="ret_sample_mix",
    )(q, k, kd, v, state)


def _ret_sample_finish_kernel(o_ref, g_ref, x_ref, p_ref, gpost_ref, wout_ref, wgate_ref, wple_ref,
                              xo_ref):
    xo_ref[...] = _ret_finish(o_ref[...], g_ref[...], x_ref[...], p_ref[...], gpost_ref[...],
                              wout_ref, wgate_ref, wple_ref)


def _ret_sample_finish(layer, o, g, x, p_all, gpost, w_out, w_gate, w_ple):
    n_tok, d = x.shape
    tq = min(512, n_tok)
    ple = p_all.shape[-1]
    tok = lambda i: (i, 0)
    return pl.pallas_call(
        _ret_sample_finish_kernel,
        grid=(n_tok // tq,),
        in_specs=[
            pl.BlockSpec((tq, RET_VW), tok), pl.BlockSpec((tq, RET_VW), tok),
            pl.BlockSpec((tq, d), tok),
            pl.BlockSpec((None, tq, ple), lambda i: (layer, i, 0)),
            _const_spec(gpost.shape), _const_spec(w_out.shape),
            _const_spec(w_gate.shape), _const_spec(w_ple.shape),
        ],
        out_specs=pl.BlockSpec((tq, d), tok),
        out_shape=jax.ShapeDtypeStruct(x.shape, F32),
        compiler_params=pltpu.CompilerParams(
            dimension_semantics=("arbitrary",),
            vmem_limit_bytes=VMEM_LIMIT_BYTES),
        name="ret_sample_finish",
    )(o, g, x, p_all, gpost, w_out, w_gate, w_ple)


def _rope_tables(pos, head_dim):
    half = head_dim // 2
    inv = ROPE_THETA ** (-np.arange(half, dtype=np.float64) / half)
    ang = np.asarray(pos, np.float64)[:, None] * inv[None, :]
    return np.cos(ang).astype(np.float32), np.sin(ang).astype(np.float32)


def _attn_tables(pos):
    cos, sin = _rope_tables(pos, ATTN_HEAD_DIM)
    reps = LANES // ATTN_HEAD_DIM
    return (np.tile(np.concatenate([cos, cos], axis=1), (1, reps)),
            np.tile(np.concatenate([-sin, sin], axis=1), (1, reps)))


def kernel(x_prompt, x_sample, cache_k_win, cache_v_win, state_ret, p_prompt, p_sample, pre_norm,
           post_norm, w_in_attn, attn_sinks, w_out_attn, w_in_ret, w_out_ret, w_ple, w_ple_gate):
    depth = p_prompt.shape[0]
    nb, seq, d = x_prompt.shape
    n_dec, dec_len, _ = x_sample.shape
    n_stok = n_dec * dec_len
    assert seq % 512 == 0 or seq in (128, 256), seq

    pos_p = np.arange(seq)
    pos_s = PAST_LEN + np.arange(dec_len)
    cos_ap, sin_ap = _attn_tables(pos_p)
    cos_apt, sin_apt = (np.ascontiguousarray(a.T) for a in _rope_tables(pos_p, ATTN_HEAD_DIM))
    cos_as, sin_as = _attn_tables(pos_s)
    cos_rp, sin_rp = _rope_tables(pos_p, RET_QK_DIM)
    cos_rs, sin_rs = _rope_tables(pos_s, RET_QK_DIM)
    attn_s_tile = min(16, n_dec)
    ret_s_tile = min(512, n_stok) // dec_len
    cos_as, sin_as = (np.tile(a, (attn_s_tile, 1)) for a in (cos_as, sin_as))
    cos_ast, sin_ast = (np.ascontiguousarray(np.tile(a, (attn_s_tile, 1)).T)
                        for a in _rope_tables(pos_s, ATTN_HEAD_DIM))
    cos_rs, sin_rs = (np.tile(a, (ret_s_tile, 1)) for a in (cos_rs, sin_rs))
    (cos_ap, sin_ap, cos_apt, sin_apt, cos_as, sin_as, cos_ast, sin_ast,
     cos_rp, sin_rp, cos_rs, sin_rs) = (
        jnp.asarray(a) for a in
        (cos_ap, sin_ap, cos_apt, sin_apt, cos_as, sin_as, cos_ast, sin_ast,
         cos_rp, sin_rp, cos_rs, sin_rs))

    xp = x_prompt
    xs = x_sample.reshape(n_stok, d)
    p_s = p_sample.reshape(depth, n_stok, p_sample.shape[-1])
    kwp, vwp, kws, vws, srp, srs = [], [], [], [], [], []
    for i in range(depth):
        j = i // 2
        gpre = pre_norm[i][None, :]
        gpost = post_norm[i][None, :]
        w_gate = w_ple_gate[i].astype(BF16)
        w_pl = w_ple[i].astype(BF16)
        if i % 2 == 0:
            w_in = w_in_attn[j].astype(BF16)
            w_out = w_out_attn[j].astype(BF16)
            sinks = attn_sinks[j]
            k0, v0, g0 = ATTN_Q_DIM, ATTN_Q_DIM + ATTN_KV_DIM, ATTN_Q_DIM + 2 * ATTN_KV_DIM
            w_q, w_k, w_v, w_g = w_in[:, :k0], w_in[:, k0:v0], w_in[:, v0:g0], w_in[:, g0:]
            xp, kpt, vpt = _attn_prompt_layer(
                i, xp, p_prompt, (cos_ap, sin_ap, cos_apt, sin_apt), gpre, gpost, sinks,
                w_q.T, w_k, w_v.T, w_g, w_out, w_gate, w_pl)
            to_t = lambda a: a.transpose(0, 2, 3, 1)
            from_t = lambda a: a.transpose(0, 3, 1, 2)
            xs, knt, vnt = _attn_sample_layer(
                i, xs, p_s, (cos_as, sin_as, cos_ast, sin_ast), gpre, gpost, sinks,
                w_q, w_k.T, w_v.T, w_g, w_out, w_gate, w_pl,
                to_t(cache_k_win[j]), to_t(cache_v_win[j]), dec_len)
            head_shape = (nb, ATTN_KV_HEADS, ATTN_HEAD_DIM, WINDOW)
            kwp.append(from_t(kpt.reshape(head_shape)))
            vwp.append(from_t(vpt.reshape(head_shape)))
            kws.append(from_t(knt))
            vws.append(from_t(vnt))
        else:
            w_in = w_in_ret[j].astype(BF16)
            w_out = w_out_ret[j].astype(BF16)
            xp, sp = _ret_prompt_layer(i, xp, p_prompt, cos_rp, sin_rp, gpre, gpost,
                                       w_in, w_out, w_gate, w_pl)
            q, k, kd, v, g = _ret_sample_proj(xs, cos_rs, sin_rs, gpre, w_in, dec_len)
            o, sn = _ret_sample_mix(q, k, kd, v, state_ret[j], dec_len)
            xs = _ret_sample_finish(i, o, g, xs, p_s, gpost, w_out, w_gate, w_pl)
            srp.append(sp)
            srs.append(sn)
    return (xp, xs.reshape(n_dec, dec_len, d), jnp.stack(kwp), jnp.stack(vwp), jnp.stack(kws),
            jnp.stack(vws), jnp.stack(srp), jnp.stack(srs))
```

```python
import functools
import math

import numpy as np
import jax
import jax.numpy as jnp
from jax import lax
from jax.experimental import pallas as pl
from jax.experimental.pallas import tpu as pltpu

F32 = jnp.float32
BF16 = jnp.bfloat16

PAST_LEN = 16384
ROPE_THETA = 10000.0
NORM_EPS = 1e-6
NEG_INF = -1e30
WINDOW = 128
ATTN_HEADS = 16
ATTN_KV_HEADS = 4
ATTN_GROUP = ATTN_HEADS // ATTN_KV_HEADS
ATTN_HEAD_DIM = 64
ATTN_Q_DIM = ATTN_HEADS * ATTN_HEAD_DIM
ATTN_KV_DIM = ATTN_KV_HEADS * ATTN_HEAD_DIM
RET_HEADS = 4
RET_QK_DIM = 256
RET_V_DIM = 512
RET_QK_W = RET_HEADS * RET_QK_DIM
RET_VW = RET_HEADS * RET_V_DIM
RET_CHUNK = 256
LANES = 128

LOG2_E = math.log2(math.e)
RET_LOG_GAMMA = tuple(math.log1p(-(2.0 ** (-5.0 - h))) for h in range(RET_HEADS))

VMEM_LIMIT_BYTES = 56 * 1024 * 1024
SAMPLE_ATTN_PIPELINE_GROUP = 8


def _dot(a, b):
    return jnp.dot(a, b, preferred_element_type=F32)


def _dot_nt(a, b):
    return lax.dot_general(a, b, (((1,), (1,)), ((), ())), preferred_element_type=F32)


def _dot_tn(a, b):
    return lax.dot_general(a, b, (((0,), (0,)), ((), ())), preferred_element_type=F32)


def _rms_norm(x, g):
    return x * lax.rsqrt(jnp.mean(x * x, axis=-1, keepdims=True) + NORM_EPS) * g


def _silu(g):
    return g * jax.nn.sigmoid(g)


def _row_splits(n, parts):
    step = n // parts
    return [slice(i * step, (i + 1) * step) for i in range(parts)]


def _out_proj_and_residual(rows, ogs, x, emb, gpost, wout_ref, wgate_ref, xo_ref):
    ys = [_dot(og, wout_ref[...]) for og in ogs]
    x1s = [x[r, :] + _rms_norm(y, gpost) for r, y in zip(rows, ys)]
    gates = [_dot(x1.astype(BF16), wgate_ref[...]) for x1 in x1s]
    for r, x1, gate in zip(rows, x1s, gates):
        xo_ref[r, :] = x1 + jax.nn.sigmoid(gate) * emb[r, :]


def _rope_attn(x, cos, sin_signed):
    lane = lax.broadcasted_iota(jnp.int32, (1, LANES), 1)
    first_half = (lane % ATTN_HEAD_DIM) < (ATTN_HEAD_DIM // 2)
    outs = []
    for j in range(x.shape[1] // LANES):
        xj = x[:, j * LANES:(j + 1) * LANES]
        fwd = pltpu.roll(xj, LANES - ATTN_HEAD_DIM // 2, axis=1)
        bwd = pltpu.roll(xj, ATTN_HEAD_DIM // 2, axis=1)
        outs.append(xj * cos + jnp.where(first_half, fwd, bwd) * sin_signed)
    return jnp.concatenate(outs, axis=1)


def _rope_ret(x, cos, sin):
    half = RET_QK_DIM // 2
    outs = []
    for h in range(x.shape[1] // RET_QK_DIM):
        x1 = x[:, h * RET_QK_DIM:h * RET_QK_DIM + half]
        x2 = x[:, h * RET_QK_DIM + half:(h + 1) * RET_QK_DIM]
        outs.append(x1 * cos - x2 * sin)
        outs.append(x2 * cos + x1 * sin)
    return jnp.concatenate(outs, axis=1)


def _attn_prompt_kernel(sink_ref, x_ref, p_ref, cos_ref, sin_ref, cost_ref, sint_ref,
                        gpre_ref, gpost_ref, wqt_ref, wk_ref, wvt_ref, wg_ref,
                        wout_ref, wgate_ref, wple_ref,
                        xo_ref, kw_ref, vw_ref,
                        qt_scr, kext_scr, vt_scr, st_scr, pt_scr, inv_scr, ot_scr, *, tq):
    nblk = tq // WINDOW
    half = ATTN_HEAD_DIM // 2
    t = pl.program_id(1)
    nt = pl.num_programs(1)

    @pl.when(t == 0)
    def _():
        kext_scr[0:WINDOW, :] = jnp.zeros((WINDOW, ATTN_KV_DIM), BF16)
        vt_scr[0] = jnp.zeros((ATTN_KV_DIM, WINDOW), BF16)

    emb = _dot(p_ref[...].astype(BF16), wple_ref[...])
    x = x_ref[...]
    h = _rms_norm(x, gpre_ref[...]).astype(BF16)

    k = _rope_attn(_dot(h, wk_ref[...]), cos_ref[...], sin_ref[...])
    kext_scr[WINDOW:, :] = k.astype(BF16)

    qt = _dot_nt(wqt_ref[...], h)
    cost = cost_ref[...]
    sint = sint_ref[...]
    pieces = []
    for hd in range(ATTN_HEADS):
        x1 = qt[hd * ATTN_HEAD_DIM:hd * ATTN_HEAD_DIM + half, :]
        x2 = qt[hd * ATTN_HEAD_DIM + half:(hd + 1) * ATTN_HEAD_DIM, :]
        pieces.append(x1 * cost - x2 * sint)
        pieces.append(x2 * cost + x1 * sint)
    qrot = (jnp.concatenate(pieces, axis=0) * (ATTN_HEAD_DIM ** -0.5 * LOG2_E)).astype(BF16)
    for jb in range(nblk):
        qt_scr[jb] = qrot[:, jb * WINDOW:(jb + 1) * WINDOW]

    vt = _dot_nt(wvt_ref[...], h)
    vtb = vt.astype(BF16)
    for jb in range(nblk):
        vt_scr[jb + 1] = vtb[:, jb * WINDOW:(jb + 1) * WINDOW]

    @pl.when(t == nt - 1)
    def _():
        kw_ref[...] = k[tq - WINDOW:, :].T
        vw_ref[...] = vt[:, tq - WINDOW:]

    zero_rows = jnp.zeros((ATTN_HEAD_DIM, ATTN_GROUP * WINDOW), BF16)
    groups = [(jb, kh) for jb in range(nblk) for kh in range(ATTN_KV_HEADS)]

    for gidx, (jb, kh) in enumerate(groups):
        heads = [kh * ATTN_GROUP + gi for gi in range(ATTN_GROUP)]
        pair = kh // 2
        kb = kext_scr[jb * WINDOW:(jb + 2) * WINDOW, pair * LANES:(pair + 1) * LANES]
        qg = jnp.concatenate(
            [qt_scr[jb, hd * ATTN_HEAD_DIM:(hd + 1) * ATTN_HEAD_DIM, :] for hd in heads], axis=1)
        qg = jnp.concatenate([qg, zero_rows] if kh % 2 == 0 else [zero_rows, qg], axis=0)
        st_scr[gidx] = _dot(kb, qg)

    g = _dot(h, wg_ref[...])

    key_r = lax.broadcasted_iota(jnp.int32, (WINDOW, WINDOW), 0)
    query_c = lax.broadcasted_iota(jnp.int32, (WINDOW, WINDOW), 1)
    from_prev = key_r > query_c
    prev_w = from_prev.astype(BF16)
    cur_w = 1.0 - prev_w
    for gidx, (jb, kh) in enumerate(groups):
        for gi in range(ATTN_GROUP):
            hd = kh * ATTN_GROUP + gi
            cols = slice(gi * WINDOW, (gi + 1) * WINDOW)
            s_prev = st_scr[gidx, 0:WINDOW, cols]
            if jb == 0:
                s_prev = jnp.where(t > 0, s_prev, NEG_INF)
            s = jnp.where(from_prev, s_prev, st_scr[gidx, WINDOW:, cols])
            sink = sink_ref[hd] * LOG2_E
            m = jnp.maximum(jnp.max(s, axis=0, keepdims=True), sink)
            pr = jnp.exp2(s - m)
            den = jnp.sum(pr, axis=0, keepdims=True) + jnp.exp2(sink - m)
            prb = pr.astype(BF16)
            pt_scr[gidx, 0:WINDOW, cols] = prb * prev_w
            pt_scr[gidx, WINDOW:, cols] = prb * cur_w
            inv_scr[jb, hd:hd + 1, :] = 1.0 / den

    for gidx, (jb, kh) in enumerate(groups):
        vband = jnp.concatenate([vt_scr[jb, kh * ATTN_HEAD_DIM:(kh + 1) * ATTN_HEAD_DIM, :],
                                 vt_scr[jb + 1, kh * ATTN_HEAD_DIM:(kh + 1) * ATTN_HEAD_DIM, :]],
                                axis=1)
        otg = _dot(vband, pt_scr[gidx])
        for gi in range(ATTN_GROUP):
            hd = kh * ATTN_GROUP + gi
            ot_scr[jb, hd * ATTN_HEAD_DIM:(hd + 1) * ATTN_HEAD_DIM, :] = (
                otg[:, gi * WINDOW:(gi + 1) * WINDOW] * inv_scr[jb, hd:hd + 1, :])

    parts = 2 if nblk % 2 == 0 else 1
    rows = _row_splits(tq, parts)
    ogs = []
    for r in rows:
        o = jnp.concatenate([ot_scr[jb].T for jb in range(r.start // WINDOW, r.stop // WINDOW)],
                            axis=0)
        ogs.append((o * _silu(g[r, :])).astype(BF16))
    _out_proj_and_residual(rows, ogs, x, emb, gpost_ref[...], wout_ref, wgate_ref, xo_ref)

    kext_scr[0:WINDOW, :] = kext_scr[tq:tq + WINDOW, :]
    vt_scr[0] = vt_scr[nblk]


def _const_spec(shape):
    nd = len(shape)
    return pl.BlockSpec(shape, lambda *_: (0,) * nd)


def _attn_prompt_layer(layer, x, p_all, tables, gpre, gpost, sinks, w_qt, w_k, w_vt, w_g,
                       w_out, w_gate, w_ple):
    nb, seq, d = x.shape
    tq = min(512, seq)
    nblk = tq // WINDOW
    ple = p_all.shape[-1]
    cos, sin, cost, sint = tables
    kernel = functools.partial(_attn_prompt_kernel, tq=tq)
    tok = lambda b, t: (b, t, 0)
    win_blk = lambda b, t: (b, 0, 0)
    consts = (gpre, gpost, w_qt, w_k, w_vt, w_g, w_out, w_gate, w_ple)
    return pl.pallas_call(
        kernel,
        grid=(nb, seq // tq),
        in_specs=[
            pl.BlockSpec(memory_space=pltpu.SMEM),
            pl.BlockSpec((None, tq, d), tok),
            pl.BlockSpec((None, None, tq, ple), lambda b, t: (layer, b, t, 0)),
            pl.BlockSpec((tq, LANES), lambda b, t: (t, 0)),
            pl.BlockSpec((tq, LANES), lambda b, t: (t, 0)),
            pl.BlockSpec((ATTN_HEAD_DIM // 2, tq), lambda b, t: (0, t)),
            pl.BlockSpec((ATTN_HEAD_DIM // 2, tq), lambda b, t: (0, t)),
        ] + [_const_spec(c.shape) for c in consts],
        out_specs=[
            pl.BlockSpec((None, tq, d), tok),
            pl.BlockSpec((None, ATTN_KV_DIM, WINDOW), win_blk),
            pl.BlockSpec((None, ATTN_KV_DIM, WINDOW), win_blk),
        ],
        out_shape=[
            jax.ShapeDtypeStruct(x.shape, F32),
            jax.ShapeDtypeStruct((nb, ATTN_KV_DIM, WINDOW), F32),
            jax.ShapeDtypeStruct((nb, ATTN_KV_DIM, WINDOW), F32),
        ],
        scratch_shapes=[
            pltpu.VMEM((nblk, ATTN_Q_DIM, WINDOW), BF16),
            pltpu.VMEM((tq + WINDOW, ATTN_KV_DIM), BF16),
            pltpu.VMEM((nblk + 1, ATTN_KV_DIM, WINDOW), BF16),
            pltpu.VMEM((nblk * ATTN_KV_HEADS, 2 * WINDOW, ATTN_GROUP * WINDOW), F32),
            pltpu.VMEM((nblk * ATTN_KV_HEADS, 2 * WINDOW, ATTN_GROUP * WINDOW), BF16),
            pltpu.VMEM((nblk, ATTN_HEADS, WINDOW), F32),
            pltpu.VMEM((nblk, ATTN_Q_DIM, WINDOW), F32),
        ],
        compiler_params=pltpu.CompilerParams(
            dimension_semantics=("arbitrary", "arbitrary"),
            vmem_limit_bytes=VMEM_LIMIT_BYTES),
        name="attn_prompt_layer",
    )(sinks, x, p_all, cos, sin, cost, sint, *consts)


def _attn_sample_kernel(sink_ref, x_ref, p_ref, cos_ref, sin_ref, cost_ref, sint_ref,
                        gpre_ref, gpost_ref, wq_ref, wkt_ref, wvt_ref, wg_ref,
                        wout_ref, wgate_ref, wple_ref, kc_ref, vc_ref,
                        xo_ref, ko_ref, vo_ref,
                        q_scr, knt_scr, vnt_scr, o_scr, *, n_seq, dec_len):
    half = ATTN_HEAD_DIM // 2
    x = x_ref[...]
    h = _rms_norm(x, gpre_ref[...]).astype(BF16)
    q_scr[...] = (_rope_attn(_dot(h, wq_ref[...]), cos_ref[...], sin_ref[...])
                  * (ATTN_HEAD_DIM ** -0.5 * LOG2_E))
    knt = _dot_nt(wkt_ref[...], h)
    cost = cost_ref[...]
    sint = sint_ref[...]
    pieces = []
    for kh in range(ATTN_KV_HEADS):
        x1 = knt[kh * ATTN_HEAD_DIM:kh * ATTN_HEAD_DIM + half, :]
        x2 = knt[kh * ATTN_HEAD_DIM + half:(kh + 1) * ATTN_HEAD_DIM, :]
        pieces.append(x1 * cost - x2 * sint)
        pieces.append(x2 * cost + x1 * sint)
    knt_scr[...] = jnp.concatenate(pieces, axis=0)
    vnt_scr[...] = _dot_nt(wvt_ref[...], h)

    rows = 2 * dec_len
    tok = lax.broadcasted_iota(jnp.int32, (rows, 2 * WINDOW), 0) % dec_len
    col = lax.broadcasted_iota(jnp.int32, (rows, 2 * WINDOW), 1)
    upper_row = lax.broadcasted_iota(jnp.int32, (rows, 1), 0) >= dec_len
    lane = lax.broadcasted_iota(jnp.int32, (ATTN_HEAD_DIM, WINDOW), 1)
    keep_old = lane < WINDOW - dec_len
    zeros = jnp.zeros((ATTN_HEAD_DIM, 2 * WINDOW), BF16)

    def select_head(t, parity):
        return jnp.concatenate([t, zeros] if parity == 0 else [zeros, t], axis=0)

    def scores(s):
        tok_rows = slice(s * dec_len, (s + 1) * dec_len)
        new_col = col - (WINDOW + s * dec_len)
        mask = ((col < WINDOW) & (col > tok)) | ((new_col >= 0) & (new_col <= tok))
        new_shift = (WINDOW - dec_len - s * dec_len) % WINDOW
        out = []
        for kh in range(ATTN_KV_HEADS):
            hrows = slice(kh * ATTN_HEAD_DIM, (kh + 1) * ATTN_HEAD_DIM)
            kc = kc_ref[s, kh]
            vc = vc_ref[s, kh]
            kn = knt_scr[hrows, :]
            vn = vnt_scr[hrows, :]
            kn_at_end = pltpu.roll(kn, new_shift, axis=1) if new_shift else kn
            vn_at_end = pltpu.roll(vn, new_shift, axis=1) if new_shift else vn
            ko_ref[s, kh] = jnp.where(keep_old, pltpu.roll(kc, WINDOW - dec_len, axis=1), kn_at_end)
            vo_ref[s, kh] = jnp.where(keep_old, pltpu.roll(vc, WINDOW - dec_len, axis=1), vn_at_end)

            kt = jnp.concatenate([kc, kn], axis=1).astype(BF16)
            vt = jnp.concatenate([vc, vn], axis=1).astype(BF16)
            qp = jnp.concatenate([q_scr[tok_rows, (2 * kh + c) * LANES:(2 * kh + c + 1) * LANES]
                                  for c in range(2)], axis=0).astype(BF16)
            sc = [jnp.where(mask, _dot(qp, select_head(kt, parity)), NEG_INF)
                  for parity in range(2)]
            out.append((sc, vt))
        return out

    def values(s, per_head):
        tok_rows = slice(s * dec_len, (s + 1) * dec_len)
        probs = []
        for kh, (scs, vt) in enumerate(per_head):
            for parity, sc in enumerate(scs):
                hd_lo, hd_hi = 4 * kh + parity, 4 * kh + 2 + parity
                sink = jnp.where(upper_row, sink_ref[hd_hi], sink_ref[hd_lo]) * LOG2_E
                m = jnp.maximum(jnp.max(sc, axis=-1, keepdims=True), sink)
                pr = jnp.exp2(sc - m)
                den = jnp.sum(pr, axis=-1, keepdims=True) + jnp.exp2(sink - m)
                probs.append((pr / den).astype(BF16))
        for kh, (scs, vt) in enumerate(per_head):
            o_pair = (_dot_nt(probs[2 * kh], select_head(vt, 0))
                      + _dot_nt(probs[2 * kh + 1], select_head(vt, 1)))
            for c in range(2):
                o_scr[tok_rows, (2 * kh + c) * LANES:(2 * kh + c + 1) * LANES] = (
                    o_pair[c * dec_len:(c + 1) * dec_len, :])

    group = min(SAMPLE_ATTN_PIPELINE_GROUP, n_seq)
    pending = [scores(s) for s in range(group)]
    for s0 in range(0, n_seq, group):
        upcoming = [scores(s) for s in range(s0 + group, min(s0 + 2 * group, n_seq))]
        for i, per_head in enumerate(pending):
            values(s0 + i, per_head)
        pending = upcoming

    g = _dot(h, wg_ref[...])
    emb = _dot(p_ref[...].astype(BF16), wple_ref[...])
    og = (o_scr[...] * _silu(g)).astype(BF16)
    _out_proj_and_residual(_row_splits(x.shape[0], 1), [og], x, emb, gpost_ref[...],
                           wout_ref, wgate_ref, xo_ref)


def _attn_sample_layer(layer, x, p_all, tables, gpre, gpost, sinks, w_q, w_kt, w_vt, w_g,
                       w_out, w_gate, w_ple, cache_kt, cache_vt, dec_len):
    n_tok, d = x.shape
    n_all = cache_kt.shape[0]
    n_seq = min(16, n_all)
    tq = n_seq * dec_len
    ple = p_all.shape[-1]
    cos, sin, cost, sint = tables
    kernel = functools.partial(_attn_sample_kernel, n_seq=n_seq, dec_len=dec_len)
    tok = lambda i: (i, 0)
    cache_shape = (n_seq, ATTN_KV_HEADS, ATTN_HEAD_DIM, WINDOW)
    cache_blk = lambda i: (i, 0, 0, 0)
    consts = (cos, sin, cost, sint, gpre, gpost, w_q, w_kt, w_vt, w_g, w_out, w_gate, w_ple)
    return pl.pallas_call(
        kernel,
        grid=(n_all // n_seq,),
        in_specs=[
            pl.BlockSpec(memory_space=pltpu.SMEM),
            pl.BlockSpec((tq, d), tok),
            pl.BlockSpec((None, tq, ple), lambda i: (layer, i, 0)),
        ] + [_const_spec(c.shape) for c in consts] + [
            pl.BlockSpec(cache_shape, cache_blk),
            pl.BlockSpec(cache_shape, cache_blk),
        ],
        out_specs=[
            pl.BlockSpec((tq, d), tok),
            pl.BlockSpec(cache_shape, cache_blk),
            pl.BlockSpec(cache_shape, cache_blk),
        ],
        out_shape=[
            jax.ShapeDtypeStruct(x.shape, F32),
            jax.ShapeDtypeStruct(cache_kt.shape, F32),
            jax.ShapeDtypeStruct(cache_vt.shape, F32),
        ],
        scratch_shapes=[
            pltpu.VMEM((tq, ATTN_Q_DIM), F32),
            pltpu.VMEM((ATTN_KV_DIM, tq), F32),
            pltpu.VMEM((ATTN_KV_DIM, tq), F32),
            pltpu.VMEM((tq, ATTN_Q_DIM), F32),
        ],
        compiler_params=pltpu.CompilerParams(
            dimension_semantics=("arbitrary",),
            vmem_limit_bytes=VMEM_LIMIT_BYTES),
        name="attn_sample_layer",
    )(sinks, x, p_all, *consts, cache_kt, cache_vt)


RET_GATE_COL0 = 2 * RET_QK_W + RET_VW


def _ret_project(h, cos, sin, win_ref, chunk_len):
    q0, k0, v0, g0 = 0, RET_QK_W, 2 * RET_QK_W, RET_GATE_COL0
    q = _rope_ret(_dot(h, win_ref[:, q0:k0]), cos, sin)
    k = _rope_ret(_dot(h, win_ref[:, k0:v0]), cos, sin) * (RET_QK_DIM ** -0.5)
    v = _dot(h, win_ref[:, v0:g0])
    n = h.shape[0]
    idx = (lax.broadcasted_iota(jnp.int32, (n, 1), 0) % chunk_len).astype(F32)
    kd = jnp.concatenate(
        [k[:, hd * RET_QK_DIM:(hd + 1) * RET_QK_DIM]
         * jnp.exp((chunk_len - 1.0 - idx) * RET_LOG_GAMMA[hd]) for hd in range(RET_HEADS)], axis=1)
    return q, k, kd, v


def _ret_decay_mask(n, chunk_len, hd):
    ri = lax.broadcasted_iota(jnp.int32, (n, n), 0)
    ci = lax.broadcasted_iota(jnp.int32, (n, n), 1)
    diff = ri - ci
    ok = (diff >= 0) & ((ri // chunk_len) == (ci // chunk_len))
    return jnp.where(ok, jnp.exp(jnp.maximum(diff, 0).astype(F32) * RET_LOG_GAMMA[hd]), 0.0)


def _ret_gated(o, g):
    outs = []
    for hd in range(RET_HEADS):
        oh = o[:, hd * RET_V_DIM:(hd + 1) * RET_V_DIM]
        mu = jnp.mean(oh, axis=-1, keepdims=True)
        var = jnp.mean(jnp.square(oh - mu), axis=-1, keepdims=True)
        outs.append((oh - mu) * lax.rsqrt(var + NORM_EPS))
    return (jnp.concatenate(outs, axis=1) * _silu(g)).astype(BF16)


def _ret_prompt_kernel(x_ref, p_ref, cos_ref, sin_ref, gpre_ref, gpost_ref,
                       win_ref, wout_ref, wgate_ref, wple_ref,
                       xo_ref, st_ref,
                       q_scr, k_scr, kd_scr, v_scr, o_scr, g_scr, dm_scr, *, tq):
    b = pl.program_id(0)
    t = pl.program_id(1)

    @pl.when((b == 0) & (t == 0))
    def _():
        for hd in range(RET_HEADS):
            dm_scr[hd] = _ret_decay_mask(RET_CHUNK, RET_CHUNK, hd)

    @pl.when(t == 0)
    def _():
        st_ref[...] = jnp.zeros(st_ref.shape, F32)

    emb = _dot(p_ref[...].astype(BF16), wple_ref[...])
    x = x_ref[...]
    h = _rms_norm(x, gpre_ref[...]).astype(BF16)
    q, k, kd, v = _ret_project(h, cos_ref[...], sin_ref[...], win_ref, RET_CHUNK)
    q_scr[...] = q.astype(BF16)
    k_scr[...] = k.astype(BF16)
    kd_scr[...] = kd.astype(BF16)
    v_scr[...] = v.astype(BF16)

    idx = lax.broadcasted_iota(jnp.int32, (RET_CHUNK, 1), 0).astype(F32)
    heads = range(RET_HEADS)
    qs = [slice(hd * RET_QK_DIM, (hd + 1) * RET_QK_DIM) for hd in heads]
    vs = [slice(hd * RET_V_DIM, (hd + 1) * RET_V_DIM) for hd in heads]
    chunks = _row_splits(tq, tq // RET_CHUNK)
    gate_cols = iter(_row_splits(RET_VW, 2 * len(chunks)))

    def gate_piece():
        c = next(gate_cols)
        g_scr[:, c] = _dot(h, win_ref[:, RET_GATE_COL0 + c.start:RET_GATE_COL0 + c.stop])

    for rows in chunks:
        scores = [_dot_nt(q_scr[rows, qs[hd]], k_scr[rows, qs[hd]]) for hd in heads]
        cross = [_dot(q_scr[rows, qs[hd]], st_ref[hd].astype(BF16)) for hd in heads]
        gate_piece()
        for hd in heads:
            st_ref[hd] = (math.exp(RET_CHUNK * RET_LOG_GAMMA[hd]) * st_ref[hd]
                          + _dot_tn(kd_scr[rows, qs[hd]], v_scr[rows, vs[hd]]))
        for hd in heads:
            inner = _dot((scores[hd] * dm_scr[hd]).astype(BF16), v_scr[rows, vs[hd]])
            o_scr[rows, vs[hd]] = inner + cross[hd] * jnp.exp((idx + 1.0) * RET_LOG_GAMMA[hd])
        gate_piece()

    ogs = [_ret_gated(o_scr[r, :], g_scr[r, :]) for r in chunks]
    _out_proj_and_residual(chunks, ogs, x, emb, gpost_ref[...], wout_ref, wgate_ref, xo_ref)


def _ret_prompt_layer(layer, x, p_all, cos, sin, gpre, gpost, w_in, w_out, w_gate, w_ple):
    nb, seq, d = x.shape
    tq = min(512, seq)
    ple = p_all.shape[-1]
    kernel = functools.partial(_ret_prompt_kernel, tq=tq)
    tok = lambda b, t: (b, t, 0)
    return pl.pallas_call(
        kernel,
        grid=(nb, seq // tq),
        in_specs=[
            pl.BlockSpec((None, tq, d), tok),
            pl.BlockSpec((None, None, tq, ple), lambda b, t: (layer, b, t, 0)),
            pl.BlockSpec((tq, LANES), lambda b, t: (t, 0)),
            pl.BlockSpec((tq, LANES), lambda b, t: (t, 0)),
            _const_spec(gpre.shape), _const_spec(gpost.shape),
            _const_spec(w_in.shape), _const_spec(w_out.shape),
            _const_spec(w_gate.shape), _const_spec(w_ple.shape),
        ],
        out_specs=[
            pl.BlockSpec((None, tq, d), tok),
            pl.BlockSpec((None, RET_HEADS, RET_QK_DIM, RET_V_DIM), lambda b, t: (b, 0, 0, 0)),
        ],
        out_shape=[
            jax.ShapeDtypeStruct(x.shape, F32),
            jax.ShapeDtypeStruct((nb, RET_HEADS, RET_QK_DIM, RET_V_DIM), F32),
        ],
        scratch_shapes=[
            pltpu.VMEM((tq, RET_QK_W), BF16),
            pltpu.VMEM((tq, RET_QK_W), BF16),
            pltpu.VMEM((tq, RET_QK_W), BF16),
            pltpu.VMEM((tq, RET_VW), BF16),
            pltpu.VMEM((tq, RET_VW), F32),
            pltpu.VMEM((tq, RET_VW), F32),
            pltpu.VMEM((RET_HEADS, RET_CHUNK, RET_CHUNK), F32),
        ],
        compiler_params=pltpu.CompilerParams(
            dimension_semantics=("arbitrary", "arbitrary"),
            vmem_limit_bytes=VMEM_LIMIT_BYTES),
        name="ret_prompt_layer",
    )(x, p_all, cos, sin, gpre, gpost, w_in, w_out, w_gate, w_ple)


def _ret_sample_proj_kernel(x_ref, cos_ref, sin_ref, gpre_ref, win_ref,
                            q_ref, k_ref, kd_ref, v_ref, g_ref, *, dec_len):
    h = _rms_norm(x_ref[...], gpre_ref[...]).astype(BF16)
    q, k, kd, v = _ret_project(h, cos_ref[...], sin_ref[...], win_ref, dec_len)
    q_ref[...] = q
    k_ref[...] = k
    kd_ref[...] = kd
    v_ref[...] = v
    g_ref[...] = _dot(h, win_ref[:, RET_GATE_COL0:])


def _ret_sample_proj(x, cos, sin, gpre, w_in, dec_len):
    n_tok, d = x.shape
    tq = min(512, n_tok)
    kernel = functools.partial(_ret_sample_proj_kernel, dec_len=dec_len)
    tok = lambda i: (i, 0)
    widths = (RET_QK_W, RET_QK_W, RET_QK_W, RET_VW, RET_VW)
    return pl.pallas_call(
        kernel,
        grid=(n_tok // tq,),
        in_specs=[
            pl.BlockSpec((tq, d), tok),
            _const_spec((tq, LANES)), _const_spec((tq, LANES)),
            _const_spec(gpre.shape), _const_spec(w_in.shape),
        ],
        out_specs=[pl.BlockSpec((tq, w), tok) for w in widths],
        out_shape=[jax.ShapeDtypeStruct((n_tok, w), F32) for w in widths],
        compiler_params=pltpu.CompilerParams(
            dimension_semantics=("arbitrary",),
            vmem_limit_bytes=VMEM_LIMIT_BYTES),
        name="ret_sample_proj",
    )(x, cos, sin, gpre, w_in)


def _ret_sample_mix_kernel(q_ref, k_ref, kd_ref, v_ref, st_ref, o_ref, sto_ref, *, n_seq, dec_len):
    pad_rows = 2 * dec_len
    idx = lax.broadcasted_iota(jnp.int32, (pad_rows, 1), 0).astype(F32)

    def padded(ref, r0, cols):
        val = ref[pl.ds(r0, dec_len), cols]
        return jnp.concatenate([val, jnp.zeros_like(val)], axis=0).astype(BF16)

    def seq_body(s, carry):
        r0 = pl.multiple_of(s * dec_len, dec_len)
        for hd in range(RET_HEADS):
            lg = RET_LOG_GAMMA[hd]
            qs = slice(hd * RET_QK_DIM, (hd + 1) * RET_QK_DIM)
            vs = slice(hd * RET_V_DIM, (hd + 1) * RET_V_DIM)
            qh = padded(q_ref, r0, qs)
            kh = padded(k_ref, r0, qs)
            kdh = padded(kd_ref, r0, qs)
            vh = padded(v_ref, r0, vs)
            a = _dot_nt(qh, kh) * _ret_decay_mask(pad_rows, dec_len, hd)
            inner = _dot(a.astype(BF16), vh)
            state = st_ref[s, hd]
            cross = _dot(qh, state.astype(BF16)) * jnp.exp((idx + 1.0) * lg)
            o_ref[pl.ds(r0, dec_len), vs] = (inner + cross)[0:dec_len, :]
            sto_ref[s, hd] = math.exp(dec_len * lg) * state + _dot_tn(kdh, vh)
        return carry

    lax.fori_loop(0, n_seq, seq_body, 0)


def _ret_sample_mix(q, k, kd, v, state, dec_len):
    n_all = state.shape[0]
    n_seq = min(4, n_all)
    tq = n_seq * dec_len
    kernel = functools.partial(_ret_sample_mix_kernel, n_seq=n_seq, dec_len=dec_len)
    tok = lambda i: (i, 0)
    st_blk = lambda i: (i, 0, 0, 0)
    st_shape = (n_seq, RET_HEADS, RET_QK_DIM, RET_V_DIM)
    return pl.pallas_call(
        kernel,
        grid=(n_all // n_seq,),
        in_specs=[
            pl.BlockSpec((tq, RET_QK_W), tok), pl.BlockSpec((tq, RET_QK_W), tok),
            pl.BlockSpec((tq, RET_QK_W), tok), pl.BlockSpec((tq, RET_VW), tok),
            pl.BlockSpec(st_shape, st_blk),
        ],
        out_specs=[pl.BlockSpec((tq, RET_VW), tok), pl.BlockSpec(st_shape, st_blk)],
        out_shape=[
            jax.ShapeDtypeStruct((q.shape[0], RET_VW), F32),
            jax.ShapeDtypeStruct(state.shape, F32),
        ],
        compiler_params=pltpu.CompilerParams(
            dimension_semantics=("arbitrary",),
            vmem_limit_bytes=VMEM_LIMIT_BYTES),
        name="ret_sample_mix",
    )(q, k, kd, v, state)


def _ret_sample_finish_kernel(o_ref, g_ref, x_ref, p_ref, gpost_ref, wout_ref, wgate_ref, wple_ref,
                              xo_ref):
    emb = _dot(p_ref[...].astype(BF16), wple_ref[...])
    x = x_ref[...]
    rows = _row_splits(x.shape[0], 2)
    ogs = [_ret_gated(o_ref[r, :], g_ref[r, :]) for r in rows]
    _out_proj_and_residual(rows, ogs, x, emb, gpost_ref[...], wout_ref, wgate_ref, xo_ref)


def _ret_sample_finish(layer, o, g, x, p_all, gpost, w_out, w_gate, w_ple):
    n_tok, d = x.shape
    tq = min(512, n_tok)
    ple = p_all.shape[-1]
    tok = lambda i: (i, 0)
    return pl.pallas_call(
        _ret_sample_finish_kernel,
        grid=(n_tok // tq,),
        in_specs=[
            pl.BlockSpec((tq, RET_VW), tok), pl.BlockSpec((tq, RET_VW), tok),
            pl.BlockSpec((tq, d), tok),
            pl.BlockSpec((None, tq, ple), lambda i: (layer, i, 0)),
            _const_spec(gpost.shape), _const_spec(w_out.shape),
            _const_spec(w_gate.shape), _const_spec(w_ple.shape),
        ],
        out_specs=pl.BlockSpec((tq, d), tok),
        out_shape=jax.ShapeDtypeStruct(x.shape, F32),
        compiler_params=pltpu.CompilerParams(
            dimension_semantics=("arbitrary",),
            vmem_limit_bytes=VMEM_LIMIT_BYTES),
        name="ret_sample_finish",
    )(o, g, x, p_all, gpost, w_out, w_gate, w_ple)


def _rope_tables(pos, head_dim):
    half = head_dim // 2
    inv = ROPE_THETA ** (-np.arange(half, dtype=np.float64) / half)
    ang = np.asarray(pos, np.float64)[:, None] * inv[None, :]
    return np.cos(ang).astype(np.float32), np.sin(ang).astype(np.float32)


def _attn_tables(pos):
    cos, sin = _rope_tables(pos, ATTN_HEAD_DIM)
    reps = LANES // ATTN_HEAD_DIM
    return (np.tile(np.concatenate([cos, cos], axis=1), (1, reps)),
            np.tile(np.concatenate([-sin, sin], axis=1), (1, reps)))


def kernel(x_prompt, x_sample, cache_k_win, cache_v_win, state_ret, p_prompt, p_sample, pre_norm,
           post_norm, w_in_attn, attn_sinks, w_out_attn, w_in_ret, w_out_ret, w_ple, w_ple_gate):
    depth = p_prompt.shape[0]
    nb, seq, d = x_prompt.shape
    n_dec, dec_len, _ = x_sample.shape
    n_stok = n_dec * dec_len
    assert seq % 512 == 0 or seq in (128, 256), seq

    pos_p = np.arange(seq)
    pos_s = PAST_LEN + np.arange(dec_len)
    cos_ap, sin_ap = _attn_tables(pos_p)
    cos_apt, sin_apt = (np.ascontiguousarray(a.T) for a in _rope_tables(pos_p, ATTN_HEAD_DIM))
    cos_as, sin_as = _attn_tables(pos_s)
    cos_rp, sin_rp = _rope_tables(pos_p, RET_QK_DIM)
    cos_rs, sin_rs = _rope_tables(pos_s, RET_QK_DIM)
    attn_s_tile = min(16, n_dec)
    ret_s_tile = min(512, n_stok) // dec_len
    cos_as, sin_as = (np.tile(a, (attn_s_tile, 1)) for a in (cos_as, sin_as))
    cos_ast, sin_ast = (np.ascontiguousarray(np.tile(a, (attn_s_tile, 1)).T)
                        for a in _rope_tables(pos_s, ATTN_HEAD_DIM))
    cos_rs, sin_rs = (np.tile(a, (ret_s_tile, 1)) for a in (cos_rs, sin_rs))
    (cos_ap, sin_ap, cos_apt, sin_apt, cos_as, sin_as, cos_ast, sin_ast,
     cos_rp, sin_rp, cos_rs, sin_rs) = (
        jnp.asarray(a) for a in
        (cos_ap, sin_ap, cos_apt, sin_apt, cos_as, sin_as, cos_ast, sin_ast,
         cos_rp, sin_rp, cos_rs, sin_rs))

    xp = x_prompt
    xs = x_sample.reshape(n_stok, d)
    p_s = p_sample.reshape(depth, n_stok, p_sample.shape[-1])
    kwp, vwp, kws, vws, srp, srs = [], [], [], [], [], []
    for i in range(depth):
        j = i // 2
        gpre = pre_norm[i][None, :]
        gpost = post_norm[i][None, :]
        w_gate = w_ple_gate[i].astype(BF16)
        w_pl = w_ple[i].astype(BF16)
        if i % 2 == 0:
            w_in = w_in_attn[j].astype(BF16)
            w_out = w_out_attn[j].astype(BF16)
            sinks = attn_sinks[j]
            k0, v0, g0 = ATTN_Q_DIM, ATTN_Q_DIM + ATTN_KV_DIM, ATTN_Q_DIM + 2 * ATTN_KV_DIM
            w_q, w_k, w_v, w_g = w_in[:, :k0], w_in[:, k0:v0], w_in[:, v0:g0], w_in[:, g0:]
            xp, kpt, vpt = _attn_prompt_layer(
                i, xp, p_prompt, (cos_ap, sin_ap, cos_apt, sin_apt), gpre, gpost, sinks,
                w_q.T, w_k, w_v.T, w_g, w_out, w_gate, w_pl)
            to_t = lambda a: a.transpose(0, 2, 3, 1)
            from_t = lambda a: a.transpose(0, 3, 1, 2)
            xs, knt, vnt = _attn_sample_layer(
                i, xs, p_s, (cos_as, sin_as, cos_ast, sin_ast), gpre, gpost, sinks,
                w_q, w_k.T, w_v.T, w_g, w_out, w_gate, w_pl,
                to_t(cache_k_win[j]), to_t(cache_v_win[j]), dec_len)
            head_shape = (nb, ATTN_KV_HEADS, ATTN_HEAD_DIM, WINDOW)
            kwp.append(from_t(kpt.reshape(head_shape)))
            vwp.append(from_t(vpt.reshape(head_shape)))
            kws.append(from_t(knt))
            vws.append(from_t(vnt))
        else:
            w_in = w_in_ret[j].astype(BF16)
            w_out = w_out_ret[j].astype(BF16)
            xp, sp = _ret_prompt_layer(i, xp, p_prompt, cos_rp, sin_rp, gpre, gpost,
                                       w_in, w_out, w_gate, w_pl)
            q, k, kd, v, g = _ret_sample_proj(xs, cos_rs, sin_rs, gpre, w_in, dec_len)
            o, sn = _ret_sample_mix(q, k, kd, v, state_ret[j], dec_len)
            xs = _ret_sample_finish(i, o, g, xs, p_s, gpost, w_out, w_gate, w_pl)
            srp.append(sp)
            srs.append(sn)
    return (xp, xs.reshape(n_dec, dec_len, d), jnp.stack(kwp), jnp.stack(vwp), jnp.stack(kws),
            jnp.stack(vws), jnp.stack(srp), jnp.stack(srs))
```

```python
import functools
import math

import numpy as np
import jax
import jax.numpy as jnp
from jax import lax
from jax.experimental import pallas as pl
from jax.experimental.pallas import tpu as pltpu

F32 = jnp.float32
BF16 = jnp.bfloat16

PAST_LEN = 16384
ROPE_THETA = 10000.0
NORM_EPS = 1e-6
NEG_INF = -1e30
WINDOW = 128
ATTN_HEADS = 16
ATTN_KV_HEADS = 4
ATTN_GROUP = ATTN_HEADS // ATTN_KV_HEADS
ATTN_HEAD_DIM = 64
ATTN_Q_DIM = ATTN_HEADS * ATTN_HEAD_DIM
ATTN_KV_DIM = ATTN_KV_HEADS * ATTN_HEAD_DIM
RET_HEADS = 4
RET_QK_DIM = 256
RET_V_DIM = 512
RET_QK_W = RET_HEADS * RET_QK_DIM
RET_VW = RET_HEADS * RET_V_DIM
RET_CHUNK = 256
LANES = 128

LOG2_E = math.log2(math.e)
RET_LOG_GAMMA = tuple(math.log1p(-(2.0 ** (-5.0 - h))) for h in range(RET_HEADS))

VMEM_LIMIT_BYTES = 56 * 1024 * 1024
ATTN_PROMPT_VMEM_LIMIT_BYTES = 50 * 1024 * 1024
SAMPLE_ATTN_PIPELINE_GROUP = 8


def _dot(a, b):
    return jnp.dot(a, b, preferred_element_type=F32)


def _dot_nt(a, b):
    return lax.dot_general(a, b, (((1,), (1,)), ((), ())), preferred_element_type=F32)


def _dot_tn(a, b):
    return lax.dot_general(a, b, (((0,), (0,)), ((), ())), preferred_element_type=F32)


def _rms_norm(x, g):
    return x * lax.rsqrt(jnp.mean(x * x, axis=-1, keepdims=True) + NORM_EPS) * g


def _silu(g):
    return g * jax.nn.sigmoid(g)


def _row_splits(n, parts):
    step = n // parts
    return [slice(i * step, (i + 1) * step) for i in range(parts)]


def _out_proj_and_residual(rows, ogs, x, emb, gpost, wout_ref, wgate_ref, xo_ref):
    ys = [_dot(og, wout_ref[...]) for og in ogs]
    x1s = [x[r, :] + _rms_norm(y, gpost) for r, y in zip(rows, ys)]
    gates = [_dot(x1.astype(BF16), wgate_ref[...]) for x1 in x1s]
    for r, x1, gate in zip(rows, x1s, gates):
        xo_ref[r, :] = x1 + jax.nn.sigmoid(gate) * emb[r, :]


def _rope_attn(x, cos, sin_signed):
    lane = lax.broadcasted_iota(jnp.int32, (1, LANES), 1)
    first_half = (lane % ATTN_HEAD_DIM) < (ATTN_HEAD_DIM // 2)
    outs = []
    for j in range(x.shape[1] // LANES):
        xj = x[:, j * LANES:(j + 1) * LANES]
        fwd = pltpu.roll(xj, LANES - ATTN_HEAD_DIM // 2, axis=1)
        bwd = pltpu.roll(xj, ATTN_HEAD_DIM // 2, axis=1)
        outs.append(xj * cos + jnp.where(first_half, fwd, bwd) * sin_signed)
    return jnp.concatenate(outs, axis=1)


def _rope_ret(x, cos, sin):
    half = RET_QK_DIM // 2
    outs = []
    for h in range(x.shape[1] // RET_QK_DIM):
        x1 = x[:, h * RET_QK_DIM:h * RET_QK_DIM + half]
        x2 = x[:, h * RET_QK_DIM + half:(h + 1) * RET_QK_DIM]
        outs.append(x1 * cos - x2 * sin)
        outs.append(x2 * cos + x1 * sin)
    return jnp.concatenate(outs, axis=1)


def _attn_prompt_kernel(sink_ref, x_ref, p_ref, cos_ref, sin_ref, cost_ref, sint_ref,
                        gpre_ref, gpost_ref, wqt_ref, wk_ref, wvt_ref, wg_ref,
                        wout_ref, wgate_ref, wple_ref,
                        rq_ref, rk_ref, rkd_ref, rv_ref, rstate_hbm,
                        xo_ref, kw_ref, vw_ref, ro_ref, rstate_out_hbm,
                        qt_scr, kext_scr, vt_scr, st_scr, pt_scr, inv_scr, ot_scr,
                        rst_in, rst_out, sem_in, sem_out, *, tq, n_seq, dec_len):
    nblk = tq // WINDOW
    half = ATTN_HEAD_DIM // 2
    t = pl.program_id(1)
    nt = pl.num_programs(1)

    @pl.when(t == 0)
    def _():
        kext_scr[0:WINDOW, :] = jnp.zeros((WINDOW, ATTN_KV_DIM), BF16)
        vt_scr[0] = jnp.zeros((ATTN_KV_DIM, WINDOW), BF16)

    seq0 = (pl.program_id(0) * nt + t) * n_seq
    half_seq = n_seq // 2

    def state_copies(part):
        first = seq0 + part * half_seq
        ins = [pltpu.make_async_copy(rstate_hbm.at[first + i], rst_in.at[i], sem_in.at[i])
               for i in range(half_seq)]
        outs = [pltpu.make_async_copy(rst_out.at[i], rstate_out_hbm.at[first + i], sem_out.at[i])
                for i in range(half_seq)]
        return ins, outs

    def state_update(part):
        _ret_state_update_tile(rq_ref, rk_ref, rkd_ref, rv_ref, rst_in, rst_out, ro_ref,
                               first_seq=part * half_seq, n_seq=half_seq, dec_len=dec_len)

    (in_a, out_a), (in_b, out_b) = state_copies(0), state_copies(1)
    for copy in in_a:
        copy.start()

    emb = _dot(p_ref[...].astype(BF16), wple_ref[...])
    x = x_ref[...]
    h = _rms_norm(x, gpre_ref[...]).astype(BF16)

    k = _rope_attn(_dot(h, wk_ref[...]), cos_ref[...], sin_ref[...])
    kext_scr[WINDOW:, :] = k.astype(BF16)

    qt = _dot_nt(wqt_ref[...], h)
    cost = cost_ref[...]
    sint = sint_ref[...]
    pieces = []
    for hd in range(ATTN_HEADS):
        x1 = qt[hd * ATTN_HEAD_DIM:hd * ATTN_HEAD_DIM + half, :]
        x2 = qt[hd * ATTN_HEAD_DIM + half:(hd + 1) * ATTN_HEAD_DIM, :]
        pieces.append(x1 * cost - x2 * sint)
        pieces.append(x2 * cost + x1 * sint)
    qrot = (jnp.concatenate(pieces, axis=0) * (ATTN_HEAD_DIM ** -0.5 * LOG2_E)).astype(BF16)
    for jb in range(nblk):
        qt_scr[jb] = qrot[:, jb * WINDOW:(jb + 1) * WINDOW]

    vt = _dot_nt(wvt_ref[...], h)
    vtb = vt.astype(BF16)
    for jb in range(nblk):
        vt_scr[jb + 1] = vtb[:, jb * WINDOW:(jb + 1) * WINDOW]

    @pl.when(t == nt - 1)
    def _():
        kw_ref[...] = k[tq - WINDOW:, :].T
        vw_ref[...] = vt[:, tq - WINDOW:]

    zero_rows = jnp.zeros((ATTN_HEAD_DIM, ATTN_GROUP * WINDOW), BF16)
    groups = [(jb, kh) for jb in range(nblk) for kh in range(ATTN_KV_HEADS)]

    for gidx, (jb, kh) in enumerate(groups):
        heads = [kh * ATTN_GROUP + gi for gi in range(ATTN_GROUP)]
        pair = kh // 2
        kb = kext_scr[jb * WINDOW:(jb + 2) * WINDOW, pair * LANES:(pair + 1) * LANES]
        qg = jnp.concatenate(
            [qt_scr[jb, hd * ATTN_HEAD_DIM:(hd + 1) * ATTN_HEAD_DIM, :] for hd in heads], axis=1)
        qg = jnp.concatenate([qg, zero_rows] if kh % 2 == 0 else [zero_rows, qg], axis=0)
        st_scr[gidx] = _dot(kb, qg)

    g = _dot(h, wg_ref[...])

    for copy in in_a:
        copy.wait()
    state_update(0)
    for copy in out_a + in_b:
        copy.start()

    key_r = lax.broadcasted_iota(jnp.int32, (WINDOW, WINDOW), 0)
    query_c = lax.broadcasted_iota(jnp.int32, (WINDOW, WINDOW), 1)
    from_prev = key_r > query_c
    prev_w = from_prev.astype(BF16)
    cur_w = 1.0 - prev_w
    for gidx, (jb, kh) in enumerate(groups):
        for gi in range(ATTN_GROUP):
            hd = kh * ATTN_GROUP + gi
            cols = slice(gi * WINDOW, (gi + 1) * WINDOW)
            s_prev = st_scr[gidx, 0:WINDOW, cols]
            if jb == 0:
                s_prev = jnp.where(t > 0, s_prev, NEG_INF)
            s = jnp.where(from_prev, s_prev, st_scr[gidx, WINDOW:, cols])
            sink = sink_ref[hd] * LOG2_E
            m = jnp.maximum(jnp.max(s, axis=0, keepdims=True), sink)
            pr = jnp.exp2(s - m)
            den = jnp.sum(pr, axis=0, keepdims=True) + jnp.exp2(sink - m)
            prb = pr.astype(BF16)
            pt_scr[gidx, 0:WINDOW, cols] = prb * prev_w
            pt_scr[gidx, WINDOW:, cols] = prb * cur_w
            inv_scr[jb, hd:hd + 1, :] = 1.0 / den

    for gidx, (jb, kh) in enumerate(groups):
        vband = jnp.concatenate([vt_scr[jb, kh * ATTN_HEAD_DIM:(kh + 1) * ATTN_HEAD_DIM, :],
                                 vt_scr[jb + 1, kh * ATTN_HEAD_DIM:(kh + 1) * ATTN_HEAD_DIM, :]],
                                axis=1)
        otg = _dot(vband, pt_scr[gidx])
        for gi in range(ATTN_GROUP):
            hd = kh * ATTN_GROUP + gi
            ot_scr[jb, hd * ATTN_HEAD_DIM:(hd + 1) * ATTN_HEAD_DIM, :] = (
                otg[:, gi * WINDOW:(gi + 1) * WINDOW] * inv_scr[jb, hd:hd + 1, :])

    for copy in in_b + out_a:
        copy.wait()
    state_update(1)
    for copy in out_b:
        copy.start()

    parts = 2 if nblk % 2 == 0 else 1
    rows = _row_splits(tq, parts)
    ogs = []
    for r in rows:
        o = jnp.concatenate([ot_scr[jb].T for jb in range(r.start // WINDOW, r.stop // WINDOW)],
                            axis=0)
        ogs.append((o * _silu(g[r, :])).astype(BF16))
    _out_proj_and_residual(rows, ogs, x, emb, gpost_ref[...], wout_ref, wgate_ref, xo_ref)

    kext_scr[0:WINDOW, :] = kext_scr[tq:tq + WINDOW, :]
    vt_scr[0] = vt_scr[nblk]
    for copy in out_b:
        copy.wait()


def _const_spec(shape):
    nd = len(shape)
    return pl.BlockSpec(shape, lambda *_: (0,) * nd)


def _attn_prompt_layer(layer, x, p_all, tables, gpre, gpost, sinks, w_qt, w_k, w_vt, w_g,
                       w_out, w_gate, w_ple, ret_qkv, ret_state, dec_len):
    nb, seq, d = x.shape
    tq = min(512, seq)
    nblk = tq // WINDOW
    nt = seq // tq
    n_seq = ret_state.shape[0] // (nb * nt)
    assert n_seq * nb * nt == ret_state.shape[0] and n_seq % 2 == 0, (ret_state.shape, nb, nt)
    ple = p_all.shape[-1]
    cos, sin, cost, sint = tables
    kernel = functools.partial(_attn_prompt_kernel, tq=tq, n_seq=n_seq, dec_len=dec_len)
    tok = lambda b, t: (b, t, 0)
    win_blk = lambda b, t: (b, 0, 0)
    stok = lambda b, t: (b * nt + t, 0)
    consts = (gpre, gpost, w_qt, w_k, w_vt, w_g, w_out, w_gate, w_ple)
    rq, rk, rkd, rv = ret_qkv
    st_shape = (n_seq // 2,) + ret_state.shape[1:]
    return pl.pallas_call(
        kernel,
        grid=(nb, nt),
        in_specs=[
            pl.BlockSpec(memory_space=pltpu.SMEM),
            pl.BlockSpec((None, tq, d), tok),
            pl.BlockSpec((None, None, tq, ple), lambda b, t: (layer, b, t, 0)),
            pl.BlockSpec((tq, LANES), lambda b, t: (t, 0)),
            pl.BlockSpec((tq, LANES), lambda b, t: (t, 0)),
            pl.BlockSpec((ATTN_HEAD_DIM // 2, tq), lambda b, t: (0, t)),
            pl.BlockSpec((ATTN_HEAD_DIM // 2, tq), lambda b, t: (0, t)),
        ] + [_const_spec(c.shape) for c in consts] + [
            pl.BlockSpec((n_seq * dec_len, RET_QK_W), stok),
            pl.BlockSpec((n_seq * dec_len, RET_QK_W), stok),
            pl.BlockSpec((n_seq * dec_len, RET_QK_W), stok),
            pl.BlockSpec((n_seq * dec_len, RET_VW), stok),
            pl.BlockSpec(memory_space=pl.ANY),
        ],
        out_specs=[
            pl.BlockSpec((None, tq, d), tok),
            pl.BlockSpec((None, ATTN_KV_DIM, WINDOW), win_blk),
            pl.BlockSpec((None, ATTN_KV_DIM, WINDOW), win_blk),
            pl.BlockSpec((n_seq * dec_len, RET_VW), stok),
            pl.BlockSpec(memory_space=pl.ANY),
        ],
        out_shape=[
            jax.ShapeDtypeStruct(x.shape, F32),
            jax.ShapeDtypeStruct((nb, ATTN_KV_DIM, WINDOW), F32),
            jax.ShapeDtypeStruct((nb, ATTN_KV_DIM, WINDOW), F32),
            jax.ShapeDtypeStruct((rq.shape[0], RET_VW), F32),
            jax.ShapeDtypeStruct(ret_state.shape, F32),
        ],
        scratch_shapes=[
            pltpu.VMEM((nblk, ATTN_Q_DIM, WINDOW), BF16),
            pltpu.VMEM((tq + WINDOW, ATTN_KV_DIM), BF16),
            pltpu.VMEM((nblk + 1, ATTN_KV_DIM, WINDOW), BF16),
            pltpu.VMEM((nblk * ATTN_KV_HEADS, 2 * WINDOW, ATTN_GROUP * WINDOW), F32),
            pltpu.VMEM((nblk * ATTN_KV_HEADS, 2 * WINDOW, ATTN_GROUP * WINDOW), BF16),
            pltpu.VMEM((nblk, ATTN_HEADS, WINDOW), F32),
            pltpu.VMEM((nblk, ATTN_Q_DIM, WINDOW), F32),
            pltpu.VMEM(st_shape, F32),
            pltpu.VMEM(st_shape, F32),
            pltpu.SemaphoreType.DMA((n_seq // 2,)),
            pltpu.SemaphoreType.DMA((n_seq // 2,)),
        ],
        compiler_params=pltpu.CompilerParams(
            dimension_semantics=("arbitrary", "arbitrary"),
            vmem_limit_bytes=ATTN_PROMPT_VMEM_LIMIT_BYTES),
        name="attn_prompt_layer",
    )(sinks, x, p_all, cos, sin, cost, sint, *consts, rq, rk, rkd, rv, ret_state)


def _attn_sample_kernel(sink_ref, x_ref, p_ref, cos_ref, sin_ref, cost_ref, sint_ref,
                        gpre_ref, gpost_ref, wq_ref, wkt_ref, wvt_ref, wg_ref,
                        wout_ref, wgate_ref, wple_ref, kc_ref, vc_ref,
                        xo_ref, ko_ref, vo_ref,
                        q_scr, knt_scr, vnt_scr, o_scr, *, n_seq, dec_len):
    half = ATTN_HEAD_DIM // 2
    x = x_ref[...]
    h = _rms_norm(x, gpre_ref[...]).astype(BF16)
    q_scr[...] = (_rope_attn(_dot(h, wq_ref[...]), cos_ref[...], sin_ref[...])
                  * (ATTN_HEAD_DIM ** -0.5 * LOG2_E))
    knt = _dot_nt(wkt_ref[...], h)
    cost = cost_ref[...]
    sint = sint_ref[...]
    pieces = []
    for kh in range(ATTN_KV_HEADS):
        x1 = knt[kh * ATTN_HEAD_DIM:kh * ATTN_HEAD_DIM + half, :]
        x2 = knt[kh * ATTN_HEAD_DIM + half:(kh + 1) * ATTN_HEAD_DIM, :]
        pieces.append(x1 * cost - x2 * sint)
        pieces.append(x2 * cost + x1 * sint)
    knt_scr[...] = jnp.concatenate(pieces, axis=0)
    vnt_scr[...] = _dot_nt(wvt_ref[...], h)

    rows = 2 * dec_len
    tok = lax.broadcasted_iota(jnp.int32, (rows, 2 * WINDOW), 0) % dec_len
    col = lax.broadcasted_iota(jnp.int32, (rows, 2 * WINDOW), 1)
    upper_row = lax.broadcasted_iota(jnp.int32, (rows, 1), 0) >= dec_len
    lane = lax.broadcasted_iota(jnp.int32, (ATTN_HEAD_DIM, WINDOW), 1)
    keep_old = lane < WINDOW - dec_len
    zeros = jnp.zeros((ATTN_HEAD_DIM, 2 * WINDOW), BF16)

    def select_head(t, parity):
        return jnp.concatenate([t, zeros] if parity == 0 else [zeros, t], axis=0)

    def scores(s):
        tok_rows = slice(s * dec_len, (s + 1) * dec_len)
        new_col = col - (WINDOW + s * dec_len)
        mask = ((col < WINDOW) & (col > tok)) | ((new_col >= 0) & (new_col <= tok))
        new_shift = (WINDOW - dec_len - s * dec_len) % WINDOW
        out = []
        for kh in range(ATTN_KV_HEADS):
            hrows = slice(kh * ATTN_HEAD_DIM, (kh + 1) * ATTN_HEAD_DIM)
            kc = kc_ref[s, kh]
            vc = vc_ref[s, kh]
            kn = knt_scr[hrows, :]
            vn = vnt_scr[hrows, :]
            kn_at_end = pltpu.roll(kn, new_shift, axis=1) if new_shift else kn
            vn_at_end = pltpu.roll(vn, new_shift, axis=1) if new_shift else vn
            ko_ref[s, kh] = jnp.where(keep_old, pltpu.roll(kc, WINDOW - dec_len, axis=1), kn_at_end)
            vo_ref[s, kh] = jnp.where(keep_old, pltpu.roll(vc, WINDOW - dec_len, axis=1), vn_at_end)

            kt = jnp.concatenate([kc, kn], axis=1).astype(BF16)
            vt = jnp.concatenate([vc, vn], axis=1).astype(BF16)
            qp = jnp.concatenate([q_scr[tok_rows, (2 * kh + c) * LANES:(2 * kh + c + 1) * LANES]
                                  for c in range(2)], axis=0).astype(BF16)
            sc = [jnp.where(mask, _dot(qp, select_head(kt, parity)), NEG_INF)
                  for parity in range(2)]
            out.append((sc, vt))
        return out

    def values(s, per_head):
        tok_rows = slice(s * dec_len, (s + 1) * dec_len)
        probs = []
        for kh, (scs, vt) in enumerate(per_head):
            for parity, sc in enumerate(scs):
                hd_lo, hd_hi = 4 * kh + parity, 4 * kh + 2 + parity
                sink = jnp.where(upper_row, sink_ref[hd_hi], sink_ref[hd_lo]) * LOG2_E
                m = jnp.maximum(jnp.max(sc, axis=-1, keepdims=True), sink)
                pr = jnp.exp2(sc - m)
                den = jnp.sum(pr, axis=-1, keepdims=True) + jnp.exp2(sink - m)
                probs.append((pr / den).astype(BF16))
        for kh, (scs, vt) in enumerate(per_head):
            o_pair = (_dot_nt(probs[2 * kh], select_head(vt, 0))
                      + _dot_nt(probs[2 * kh + 1], select_head(vt, 1)))
            for c in range(2):
                o_scr[tok_rows, (2 * kh + c) * LANES:(2 * kh + c + 1) * LANES] = (
                    o_pair[c * dec_len:(c + 1) * dec_len, :])

    group = min(SAMPLE_ATTN_PIPELINE_GROUP, n_seq)
    pending = [scores(s) for s in range(group)]
    for s0 in range(0, n_seq, group):
        upcoming = [scores(s) for s in range(s0 + group, min(s0 + 2 * group, n_seq))]
        for i, per_head in enumerate(pending):
            values(s0 + i, per_head)
        pending = upcoming

    g = _dot(h, wg_ref[...])
    emb = _dot(p_ref[...].astype(BF16), wple_ref[...])
    og = (o_scr[...] * _silu(g)).astype(BF16)
    _out_proj_and_residual(_row_splits(x.shape[0], 1), [og], x, emb, gpost_ref[...],
                           wout_ref, wgate_ref, xo_ref)


def _attn_sample_layer(layer, x, p_all, tables, gpre, gpost, sinks, w_q, w_kt, w_vt, w_g,
                       w_out, w_gate, w_ple, cache_kt, cache_vt, dec_len):
    n_tok, d = x.shape
    n_all = cache_kt.shape[0]
    n_seq = min(16, n_all)
    tq = n_seq * dec_len
    ple = p_all.shape[-1]
    cos, sin, cost, sint = tables
    kernel = functools.partial(_attn_sample_kernel, n_seq=n_seq, dec_len=dec_len)
    tok = lambda i: (i, 0)
    cache_shape = (n_seq, ATTN_KV_HEADS, ATTN_HEAD_DIM, WINDOW)
    cache_blk = lambda i: (i, 0, 0, 0)
    consts = (cos, sin, cost, sint, gpre, gpost, w_q, w_kt, w_vt, w_g, w_out, w_gate, w_ple)
    return pl.pallas_call(
        kernel,
        grid=(n_all // n_seq,),
        in_specs=[
            pl.BlockSpec(memory_space=pltpu.SMEM),
            pl.BlockSpec((tq, d), tok),
            pl.BlockSpec((None, tq, ple), lambda i: (layer, i, 0)),
        ] + [_const_spec(c.shape) for c in consts] + [
            pl.BlockSpec(cache_shape, cache_blk),
            pl.BlockSpec(cache_shape, cache_blk),
        ],
        out_specs=[
            pl.BlockSpec((tq, d), tok),
            pl.BlockSpec(cache_shape, cache_blk),
            pl.BlockSpec(cache_shape, cache_blk),
        ],
        out_shape=[
            jax.ShapeDtypeStruct(x.shape, F32),
            jax.ShapeDtypeStruct(cache_kt.shape, F32),
            jax.ShapeDtypeStruct(cache_vt.shape, F32),
        ],
        scratch_shapes=[
            pltpu.VMEM((tq, ATTN_Q_DIM), F32),
            pltpu.VMEM((ATTN_KV_DIM, tq), F32),
            pltpu.VMEM((ATTN_KV_DIM, tq), F32),
            pltpu.VMEM((tq, ATTN_Q_DIM), F32),
        ],
        compiler_params=pltpu.CompilerParams(
            dimension_semantics=("arbitrary",),
            vmem_limit_bytes=VMEM_LIMIT_BYTES),
        name="attn_sample_layer",
    )(sinks, x, p_all, *consts, cache_kt, cache_vt)


RET_GATE_COL0 = 2 * RET_QK_W + RET_VW


def _ret_project(h, cos, sin, win_ref, chunk_len):
    q0, k0, v0, g0 = 0, RET_QK_W, 2 * RET_QK_W, RET_GATE_COL0
    q = _rope_ret(_dot(h, win_ref[:, q0:k0]), cos, sin)
    k = _rope_ret(_dot(h, win_ref[:, k0:v0]), cos, sin) * (RET_QK_DIM ** -0.5)
    v = _dot(h, win_ref[:, v0:g0])
    n = h.shape[0]
    idx = (lax.broadcasted_iota(jnp.int32, (n, 1), 0) % chunk_len).astype(F32)
    kd = jnp.concatenate(
        [k[:, hd * RET_QK_DIM:(hd + 1) * RET_QK_DIM]
         * jnp.exp((chunk_len - 1.0 - idx) * RET_LOG_GAMMA[hd]) for hd in range(RET_HEADS)], axis=1)
    return q, k, kd, v


def _ret_decay_mask(n, chunk_len, hd):
    ri = lax.broadcasted_iota(jnp.int32, (n, n), 0)
    ci = lax.broadcasted_iota(jnp.int32, (n, n), 1)
    diff = ri - ci
    ok = (diff >= 0) & ((ri // chunk_len) == (ci // chunk_len))
    return jnp.where(ok, jnp.exp(jnp.maximum(diff, 0).astype(F32) * RET_LOG_GAMMA[hd]), 0.0)


def _ret_gated(o, g):
    outs = []
    for hd in range(RET_HEADS):
        oh = o[:, hd * RET_V_DIM:(hd + 1) * RET_V_DIM]
        mu = jnp.mean(oh, axis=-1, keepdims=True)
        var = jnp.mean(jnp.square(oh - mu), axis=-1, keepdims=True)
        outs.append((oh - mu) * lax.rsqrt(var + NORM_EPS))
    return (jnp.concatenate(outs, axis=1) * _silu(g)).astype(BF16)


def _ret_prompt_kernel(x_ref, p_ref, cos_ref, sin_ref, gpre_ref, gpost_ref,
                       win_ref, wout_ref, wgate_ref, wple_ref,
                       xo_ref, st_ref,
                       q_scr, k_scr, kd_scr, v_scr, o_scr, g_scr, dm_scr, *, tq):
    b = pl.program_id(0)
    t = pl.program_id(1)

    @pl.when((b == 0) & (t == 0))
    def _():
        for hd in range(RET_HEADS):
            dm_scr[hd] = _ret_decay_mask(RET_CHUNK, RET_CHUNK, hd)

    @pl.when(t == 0)
    def _():
        st_ref[...] = jnp.zeros(st_ref.shape, F32)

    emb = _dot(p_ref[...].astype(BF16), wple_ref[...])
    x = x_ref[...]
    h = _rms_norm(x, gpre_ref[...]).astype(BF16)
    q, k, kd, v = _ret_project(h, cos_ref[...], sin_ref[...], win_ref, RET_CHUNK)
    q_scr[...] = q.astype(BF16)
    k_scr[...] = k.astype(BF16)
    kd_scr[...] = kd.astype(BF16)
    v_scr[...] = v.astype(BF16)

    idx = lax.broadcasted_iota(jnp.int32, (RET_CHUNK, 1), 0).astype(F32)
    heads = range(RET_HEADS)
    qs = [slice(hd * RET_QK_DIM, (hd + 1) * RET_QK_DIM) for hd in heads]
    vs = [slice(hd * RET_V_DIM, (hd + 1) * RET_V_DIM) for hd in heads]
    chunks = _row_splits(tq, tq // RET_CHUNK)
    gate_cols = iter(_row_splits(RET_VW, 2 * len(chunks)))

    def gate_piece():
        c = next(gate_cols)
        g_scr[:, c] = _dot(h, win_ref[:, RET_GATE_COL0 + c.start:RET_GATE_COL0 + c.stop])

    for rows in chunks:
        scores = [_dot_nt(q_scr[rows, qs[hd]], k_scr[rows, qs[hd]]) for hd in heads]
        cross = [_dot(q_scr[rows, qs[hd]], st_ref[hd].astype(BF16)) for hd in heads]
        gate_piece()
        for hd in heads:
            st_ref[hd] = (math.exp(RET_CHUNK * RET_LOG_GAMMA[hd]) * st_ref[hd]
                          + _dot_tn(kd_scr[rows, qs[hd]], v_scr[rows, vs[hd]]))
        for hd in heads:
            inner = _dot((scores[hd] * dm_scr[hd]).astype(BF16), v_scr[rows, vs[hd]])
            o_scr[rows, vs[hd]] = inner + cross[hd] * jnp.exp((idx + 1.0) * RET_LOG_GAMMA[hd])
        gate_piece()

    ogs = [_ret_gated(o_scr[r, :], g_scr[r, :]) for r in chunks]
    _out_proj_and_residual(chunks, ogs, x, emb, gpost_ref[...], wout_ref, wgate_ref, xo_ref)


def _ret_prompt_layer(layer, x, p_all, cos, sin, gpre, gpost, w_in, w_out, w_gate, w_ple):
    nb, seq, d = x.shape
    tq = min(512, seq)
    ple = p_all.shape[-1]
    kernel = functools.partial(_ret_prompt_kernel, tq=tq)
    tok = lambda b, t: (b, t, 0)
    return pl.pallas_call(
        kernel,
        grid=(nb, seq // tq),
        in_specs=[
            pl.BlockSpec((None, tq, d), tok),
            pl.BlockSpec((None, None, tq, ple), lambda b, t: (layer, b, t, 0)),
            pl.BlockSpec((tq, LANES), lambda b, t: (t, 0)),
            pl.BlockSpec((tq, LANES), lambda b, t: (t, 0)),
            _const_spec(gpre.shape), _const_spec(gpost.shape),
            _const_spec(w_in.shape), _const_spec(w_out.shape),
            _const_spec(w_gate.shape), _const_spec(w_ple.shape),
        ],
        out_specs=[
            pl.BlockSpec((None, tq, d), tok),
            pl.BlockSpec((None, RET_HEADS, RET_QK_DIM, RET_V_DIM), lambda b, t: (b, 0, 0, 0)),
        ],
        out_shape=[
            jax.ShapeDtypeStruct(x.shape, F32),
            jax.ShapeDtypeStruct((nb, RET_HEADS, RET_QK_DIM, RET_V_DIM), F32),
        ],
        scratch_shapes=[
            pltpu.VMEM((tq, RET_QK_W), BF16),
            pltpu.VMEM((tq, RET_QK_W), BF16),
            pltpu.VMEM((tq, RET_QK_W), BF16),
            pltpu.VMEM((tq, RET_VW), BF16),
            pltpu.VMEM((tq, RET_VW), F32),
            pltpu.VMEM((tq, RET_VW), F32),
            pltpu.VMEM((RET_HEADS, RET_CHUNK, RET_CHUNK), F32),
        ],
        compiler_params=pltpu.CompilerParams(
            dimension_semantics=("arbitrary", "arbitrary"),
            vmem_limit_bytes=VMEM_LIMIT_BYTES),
        name="ret_prompt_layer",
    )(x, p_all, cos, sin, gpre, gpost, w_in, w_out, w_gate, w_ple)


def _ret_sample_proj_kernel(x_ref, cos_ref, sin_ref, gpre_ref, win_ref,
                            q_ref, k_ref, kd_ref, v_ref, g_ref, *, dec_len):
    h = _rms_norm(x_ref[...], gpre_ref[...]).astype(BF16)
    q, k, kd, v = _ret_project(h, cos_ref[...], sin_ref[...], win_ref, dec_len)
    q_ref[...] = q
    k_ref[...] = k
    kd_ref[...] = kd
    v_ref[...] = v
    g_ref[...] = _dot(h, win_ref[:, RET_GATE_COL0:])


def _ret_sample_proj(x, cos, sin, gpre, w_in, dec_len):
    n_tok, d = x.shape
    tq = min(512, n_tok)
    kernel = functools.partial(_ret_sample_proj_kernel, dec_len=dec_len)
    tok = lambda i: (i, 0)
    widths = (RET_QK_W, RET_QK_W, RET_QK_W, RET_VW, RET_VW)
    return pl.pallas_call(
        kernel,
        grid=(n_tok // tq,),
        in_specs=[
            pl.BlockSpec((tq, d), tok),
            _const_spec((tq, LANES)), _const_spec((tq, LANES)),
            _const_spec(gpre.shape), _const_spec(w_in.shape),
        ],
        out_specs=[pl.BlockSpec((tq, w), tok) for w in widths],
        out_shape=[jax.ShapeDtypeStruct((n_tok, w), F32) for w in widths],
        compiler_params=pltpu.CompilerParams(
            dimension_semantics=("arbitrary",),
            vmem_limit_bytes=VMEM_LIMIT_BYTES),
        name="ret_sample_proj",
    )(x, cos, sin, gpre, w_in)


def _ret_state_update_tile(q_ref, k_ref, kd_ref, v_ref, st_in, st_out, o_ref, *,
                           first_seq, n_seq, dec_len):
    pad_rows = 2 * dec_len
    idx = lax.broadcasted_iota(jnp.int32, (pad_rows, 1), 0).astype(F32)

    def tok_rows(s):
        return slice((first_seq + s) * dec_len, (first_seq + s + 1) * dec_len)

    def padded(ref, s, cols):
        val = ref[tok_rows(s), cols]
        return jnp.concatenate([val, jnp.zeros_like(val)], axis=0).astype(BF16)

    units = [(s, hd) for s in range(n_seq) for hd in range(RET_HEADS)]
    qs = [slice(hd * RET_QK_DIM, (hd + 1) * RET_QK_DIM) for hd in range(RET_HEADS)]
    vs = [slice(hd * RET_V_DIM, (hd + 1) * RET_V_DIM) for hd in range(RET_HEADS)]
    qh = {(s, hd): padded(q_ref, s, qs[hd]) for s, hd in units}
    vh = {(s, hd): padded(v_ref, s, vs[hd]) for s, hd in units}
    scores = {(s, hd): _dot_nt(qh[s, hd], padded(k_ref, s, qs[hd])) for s, hd in units}
    cross = {(s, hd): _dot(qh[s, hd], st_in[s, hd].astype(BF16)) for s, hd in units}
    for s, hd in units:
        st_out[s, hd] = (math.exp(dec_len * RET_LOG_GAMMA[hd]) * st_in[s, hd]
                         + _dot_tn(padded(kd_ref, s, qs[hd]), vh[s, hd]))
    masks = [_ret_decay_mask(pad_rows, dec_len, hd) for hd in range(RET_HEADS)]
    for s, hd in units:
        inner = _dot((scores[s, hd] * masks[hd]).astype(BF16), vh[s, hd])
        out = inner + cross[s, hd] * jnp.exp((idx + 1.0) * RET_LOG_GAMMA[hd])
        o_ref[tok_rows(s), vs[hd]] = out[0:dec_len, :]


def _ret_sample_finish_kernel(o_ref, g_ref, x_ref, p_ref, gpost_ref, wout_ref, wgate_ref, wple_ref,
                              xo_ref):
    emb = _dot(p_ref[...].astype(BF16), wple_ref[...])
    x = x_ref[...]
    rows = _row_splits(x.shape[0], 2)
    ogs = [_ret_gated(o_ref[r, :], g_ref[r, :]) for r in rows]
    _out_proj_and_residual(rows, ogs, x, emb, gpost_ref[...], wout_ref, wgate_ref, xo_ref)


def _ret_sample_finish(layer, o, g, x, p_all, gpost, w_out, w_gate, w_ple):
    n_tok, d = x.shape
    tq = min(512, n_tok)
    ple = p_all.shape[-1]
    tok = lambda i: (i, 0)
    return pl.pallas_call(
        _ret_sample_finish_kernel,
        grid=(n_tok // tq,),
        in_specs=[
            pl.BlockSpec((tq, RET_VW), tok), pl.BlockSpec((tq, RET_VW), tok),
            pl.BlockSpec((tq, d), tok),
            pl.BlockSpec((None, tq, ple), lambda i: (layer, i, 0)),
            _const_spec(gpost.shape), _const_spec(w_out.shape),
            _const_spec(w_gate.shape), _const_spec(w_ple.shape),
        ],
        out_specs=pl.BlockSpec((tq, d), tok),
        out_shape=jax.ShapeDtypeStruct(x.shape, F32),
        compiler_params=pltpu.CompilerParams(
            dimension_semantics=("arbitrary",),
            vmem_limit_bytes=VMEM_LIMIT_BYTES),
        name="ret_sample_finish",
    )(o, g, x, p_all, gpost, w_out, w_gate, w_ple)


def _rope_tables(pos, head_dim):
    half = head_dim // 2
    inv = ROPE_THETA ** (-np.arange(half, dtype=np.float64) / half)
    ang = np.asarray(pos, np.float64)[:, None] * inv[None, :]
    return np.cos(ang).astype(np.float32), np.sin(ang).astype(np.float32)


def _attn_tables(pos):
    cos, sin = _rope_tables(pos, ATTN_HEAD_DIM)
    reps = LANES // ATTN_HEAD_DIM
    return (np.tile(np.concatenate([cos, cos], axis=1), (1, reps)),
            np.tile(np.concatenate([-sin, sin], axis=1), (1, reps)))


def kernel(x_prompt, x_sample, cache_k_win, cache_v_win, state_ret, p_prompt, p_sample, pre_norm,
           post_norm, w_in_attn, attn_sinks, w_out_attn, w_in_ret, w_out_ret, w_ple, w_ple_gate):
    depth = p_prompt.shape[0]
    nb, seq, d = x_prompt.shape
    n_dec, dec_len, _ = x_sample.shape
    n_stok = n_dec * dec_len
    assert seq % 512 == 0 or seq in (128, 256), seq

    pos_p = np.arange(seq)
    pos_s = PAST_LEN + np.arange(dec_len)
    cos_ap, sin_ap = _attn_tables(pos_p)
    cos_apt, sin_apt = (np.ascontiguousarray(a.T) for a in _rope_tables(pos_p, ATTN_HEAD_DIM))
    cos_as, sin_as = _attn_tables(pos_s)
    cos_rp, sin_rp = _rope_tables(pos_p, RET_QK_DIM)
    cos_rs, sin_rs = _rope_tables(pos_s, RET_QK_DIM)
    attn_s_tile = min(16, n_dec)
    ret_s_tile = min(512, n_stok) // dec_len
    cos_as, sin_as = (np.tile(a, (attn_s_tile, 1)) for a in (cos_as, sin_as))
    cos_ast, sin_ast = (np.ascontiguousarray(np.tile(a, (attn_s_tile, 1)).T)
                        for a in _rope_tables(pos_s, ATTN_HEAD_DIM))
    cos_rs, sin_rs = (np.tile(a, (ret_s_tile, 1)) for a in (cos_rs, sin_rs))
    (cos_ap, sin_ap, cos_apt, sin_apt, cos_as, sin_as, cos_ast, sin_ast,
     cos_rp, sin_rp, cos_rs, sin_rs) = (
        jnp.asarray(a) for a in
        (cos_ap, sin_ap, cos_apt, sin_apt, cos_as, sin_as, cos_ast, sin_ast,
         cos_rp, sin_rp, cos_rs, sin_rs))

    assert depth == 2, depth
    xs = x_sample.reshape(n_stok, d)
    p_s = p_sample.reshape(depth, n_stok, p_sample.shape[-1])
    bf = lambda w: w.astype(BF16)
    gpre = [pre_norm[i][None, :] for i in range(depth)]
    gpost = [post_norm[i][None, :] for i in range(depth)]
    w_gate = [bf(w_ple_gate[i]) for i in range(depth)]
    w_pl = [bf(w_ple[i]) for i in range(depth)]

    w_in = bf(w_in_attn[0])
    w_out = bf(w_out_attn[0])
    sinks = attn_sinks[0]
    k0, v0, g0 = ATTN_Q_DIM, ATTN_Q_DIM + ATTN_KV_DIM, ATTN_Q_DIM + 2 * ATTN_KV_DIM
    w_q, w_k, w_v, w_g = w_in[:, :k0], w_in[:, k0:v0], w_in[:, v0:g0], w_in[:, g0:]
    to_t = lambda a: a.transpose(0, 2, 3, 1)
    from_t = lambda a: a.transpose(0, 3, 1, 2)
    xs, knt, vnt = _attn_sample_layer(
        0, xs, p_s, (cos_as, sin_as, cos_ast, sin_ast), gpre[0], gpost[0], sinks,
        w_q, w_k.T, w_v.T, w_g, w_out, w_gate[0], w_pl[0],
        to_t(cache_k_win[0]), to_t(cache_v_win[0]), dec_len)

    w_in_r = bf(w_in_ret[0])
    w_out_r = bf(w_out_ret[0])
    rq, rk, rkd, rv, rg = _ret_sample_proj(xs, cos_rs, sin_rs, gpre[1], w_in_r, dec_len)
    xp, kpt, vpt, ro, ret_state_sample = _attn_prompt_layer(
        0, x_prompt, p_prompt, (cos_ap, sin_ap, cos_apt, sin_apt), gpre[0], gpost[0], sinks,
        w_q.T, w_k, w_v.T, w_g, w_out, w_gate[0], w_pl[0],
        (rq, rk, rkd, rv), state_ret[0], dec_len)
    head_shape = (nb, ATTN_KV_HEADS, ATTN_HEAD_DIM, WINDOW)

    xp, ret_state_prompt = _ret_prompt_layer(1, xp, p_prompt, cos_rp, sin_rp, gpre[1], gpost[1],
                                             w_in_r, w_out_r, w_gate[1], w_pl[1])
    xs = _ret_sample_finish(1, ro, rg, xs, p_s, gpost[1], w_out_r, w_gate[1], w_pl[1])
    return (xp, xs.reshape(n_dec, dec_len, d),
            from_t(kpt.reshape(head_shape))[None], from_t(vpt.reshape(head_shape))[None],
            from_t(knt)[None], from_t(vnt)[None], ret_state_prompt[None], ret_state_sample[None])
```

```python
import functools
import math

import numpy as np
import jax
import jax.numpy as jnp
from jax import lax
from jax.experimental import pallas as pl
from jax.experimental.pallas import tpu as pltpu

F32 = jnp.float32
BF16 = jnp.bfloat16

PAST_LEN = 16384
ROPE_THETA = 10000.0
NORM_EPS = 1e-6
NEG_INF = -1e30
WINDOW = 128
ATTN_HEADS = 16
ATTN_KV_HEADS = 4
ATTN_GROUP = ATTN_HEADS // ATTN_KV_HEADS
ATTN_HEAD_DIM = 64
ATTN_Q_DIM = ATTN_HEADS * ATTN_HEAD_DIM
ATTN_KV_DIM = ATTN_KV_HEADS * ATTN_HEAD_DIM
RET_HEADS = 4
RET_QK_DIM = 256
RET_V_DIM = 512
RET_QK_W = RET_HEADS * RET_QK_DIM
RET_VW = RET_HEADS * RET_V_DIM
RET_CHUNK = 256
LANES = 128

LOG2_E = math.log2(math.e)
RET_LOG_GAMMA = tuple(math.log1p(-(2.0 ** (-5.0 - h))) for h in range(RET_HEADS))

VMEM_LIMIT_BYTES = 56 * 1024 * 1024
ATTN_PROMPT_VMEM_LIMIT_BYTES = 50 * 1024 * 1024
STATE_PHASES = 4
SAMPLE_ATTN_PIPELINE_GROUP = 8


def _dot(a, b):
    return jnp.dot(a, b, preferred_element_type=F32)


def _dot_nt(a, b):
    return lax.dot_general(a, b, (((1,), (1,)), ((), ())), preferred_element_type=F32)


def _dot_tn(a, b):
    return lax.dot_general(a, b, (((0,), (0,)), ((), ())), preferred_element_type=F32)


def _rms_norm(x, g):
    return x * lax.rsqrt(jnp.mean(x * x, axis=-1, keepdims=True) + NORM_EPS) * g


def _silu(g):
    return g * jax.nn.sigmoid(g)


def _row_splits(n, parts):
    step = n // parts
    return [slice(i * step, (i + 1) * step) for i in range(parts)]


def _out_proj_and_residual(rows, ogs, x, emb, gpost, wout_ref, wgate_ref, xo_ref):
    ys = [_dot(og, wout_ref[...]) for og in ogs]
    x1s = [x[r, :] + _rms_norm(y, gpost) for r, y in zip(rows, ys)]
    gates = [_dot(x1.astype(BF16), wgate_ref[...]) for x1 in x1s]
    for r, x1, gate in zip(rows, x1s, gates):
        xo_ref[r, :] = x1 + jax.nn.sigmoid(gate) * emb[r, :]


def _rope_attn(x, cos, sin_signed):
    lane = lax.broadcasted_iota(jnp.int32, (1, LANES), 1)
    first_half = (lane % ATTN_HEAD_DIM) < (ATTN_HEAD_DIM // 2)
    outs = []
    for j in range(x.shape[1] // LANES):
        xj = x[:, j * LANES:(j + 1) * LANES]
        fwd = pltpu.roll(xj, LANES - ATTN_HEAD_DIM // 2, axis=1)
        bwd = pltpu.roll(xj, ATTN_HEAD_DIM // 2, axis=1)
        outs.append(xj * cos + jnp.where(first_half, fwd, bwd) * sin_signed)
    return jnp.concatenate(outs, axis=1)


def _rope_ret(x, cos, sin):
    half = RET_QK_DIM // 2
    outs = []
    for h in range(x.shape[1] // RET_QK_DIM):
        x1 = x[:, h * RET_QK_DIM:h * RET_QK_DIM + half]
        x2 = x[:, h * RET_QK_DIM + half:(h + 1) * RET_QK_DIM]
        outs.append(x1 * cos - x2 * sin)
        outs.append(x2 * cos + x1 * sin)
    return jnp.concatenate(outs, axis=1)


def _attn_prompt_kernel(sink_ref, x_ref, p_ref, cos_ref, sin_ref, cost_ref, sint_ref,
                        gpre_ref, gpost_ref, wqt_ref, wk_ref, wvt_ref, wg_ref,
                        wout_ref, wgate_ref, wple_ref,
                        rq_ref, rk_ref, rkd_ref, rv_ref, rstate_hbm,
                        xo_ref, kw_ref, vw_ref, ro_ref, rstate_out_hbm,
                        qt_scr, kext_scr, vt_scr, st_scr, pt_scr, inv_scr, ot_scr,
                        rst_in, rst_out, sem_in, sem_out, *, tq, n_seq, dec_len):
    nblk = tq // WINDOW
    half = ATTN_HEAD_DIM // 2
    t = pl.program_id(1)
    nt = pl.num_programs(1)

    @pl.when(t == 0)
    def _():
        kext_scr[0:WINDOW, :] = jnp.zeros((WINDOW, ATTN_KV_DIM), BF16)
        vt_scr[0] = jnp.zeros((ATTN_KV_DIM, WINDOW), BF16)

    per_phase = n_seq // STATE_PHASES
    step = pl.program_id(0) * nt + t
    n_steps = pl.num_programs(0) * nt
    seq0 = step * n_seq
    last_seq = n_steps * n_seq - 1

    def load_copy(seq, slot, i):
        return pltpu.make_async_copy(rstate_hbm.at[seq], rst_in.at[slot, i], sem_in.at[slot, i])

    def store_copy(seq, slot, i):
        return pltpu.make_async_copy(rst_out.at[slot, i], rstate_out_hbm.at[seq],
                                     sem_out.at[slot, i])

    @pl.when(step == 0)
    def _():
        rst_out[...] = jnp.zeros(rst_out.shape, F32)
        for slot in range(2):
            for i in range(per_phase):
                load_copy(slot * per_phase + i, slot, i).start()
                store_copy(slot * per_phase + i, slot, i).start()

    def state_phase(phase):
        slot = phase % 2
        first = seq0 + phase * per_phase
        for i in range(per_phase):
            load_copy(first + i, slot, i).wait()
            store_copy(first + i, slot, i).wait()
        _ret_state_update_tile(rq_ref, rk_ref, rkd_ref, rv_ref, rst_in.at[slot], rst_out.at[slot],
                               ro_ref, first_seq=phase * per_phase, n_seq=per_phase,
                               dec_len=dec_len)
        for i in range(per_phase):
            store_copy(first + i, slot, i).start()
            load_copy(jnp.minimum(first + i + 2 * per_phase, last_seq), slot, i).start()

    emb = _dot(p_ref[...].astype(BF16), wple_ref[...])
    x = x_ref[...]
    h = _rms_norm(x, gpre_ref[...]).astype(BF16)

    k = _rope_attn(_dot(h, wk_ref[...]), cos_ref[...], sin_ref[...])
    kext_scr[WINDOW:, :] = k.astype(BF16)

    qt = _dot_nt(wqt_ref[...], h)
    cost = cost_ref[...]
    sint = sint_ref[...]
    pieces = []
    for hd in range(ATTN_HEADS):
        x1 = qt[hd * ATTN_HEAD_DIM:hd * ATTN_HEAD_DIM + half, :]
        x2 = qt[hd * ATTN_HEAD_DIM + half:(hd + 1) * ATTN_HEAD_DIM, :]
        pieces.append(x1 * cost - x2 * sint)
        pieces.append(x2 * cost + x1 * sint)
    qrot = (jnp.concatenate(pieces, axis=0) * (ATTN_HEAD_DIM ** -0.5 * LOG2_E)).astype(BF16)
    for jb in range(nblk):
        qt_scr[jb] = qrot[:, jb * WINDOW:(jb + 1) * WINDOW]

    vt = _dot_nt(wvt_ref[...], h)
    vtb = vt.astype(BF16)
    for jb in range(nblk):
        vt_scr[jb + 1] = vtb[:, jb * WINDOW:(jb + 1) * WINDOW]

    @pl.when(t == nt - 1)
    def _():
        kw_ref[...] = k[tq - WINDOW:, :].T
        vw_ref[...] = vt[:, tq - WINDOW:]

    state_phase(0)
    zero_rows = jnp.zeros((ATTN_HEAD_DIM, ATTN_GROUP * WINDOW), BF16)
    groups = [(jb, kh) for jb in range(nblk) for kh in range(ATTN_KV_HEADS)]

    for gidx, (jb, kh) in enumerate(groups):
        heads = [kh * ATTN_GROUP + gi for gi in range(ATTN_GROUP)]
        pair = kh // 2
        kb = kext_scr[jb * WINDOW:(jb + 2) * WINDOW, pair * LANES:(pair + 1) * LANES]
        qg = jnp.concatenate(
            [qt_scr[jb, hd * ATTN_HEAD_DIM:(hd + 1) * ATTN_HEAD_DIM, :] for hd in heads], axis=1)
        qg = jnp.concatenate([qg, zero_rows] if kh % 2 == 0 else [zero_rows, qg], axis=0)
        st_scr[gidx] = _dot(kb, qg)

    g = _dot(h, wg_ref[...])

    state_phase(1)

    key_r = lax.broadcasted_iota(jnp.int32, (WINDOW, WINDOW), 0)
    query_c = lax.broadcasted_iota(jnp.int32, (WINDOW, WINDOW), 1)
    from_prev = key_r > query_c
    prev_w = from_prev.astype(BF16)
    cur_w = 1.0 - prev_w
    for gidx, (jb, kh) in enumerate(groups):
        for gi in range(ATTN_GROUP):
            hd = kh * ATTN_GROUP + gi
            cols = slice(gi * WINDOW, (gi + 1) * WINDOW)
            s_prev = st_scr[gidx, 0:WINDOW, cols]
            if jb == 0:
                s_prev = jnp.where(t > 0, s_prev, NEG_INF)
            s = jnp.where(from_prev, s_prev, st_scr[gidx, WINDOW:, cols])
            sink = sink_ref[hd] * LOG2_E
            m = jnp.maximum(jnp.max(s, axis=0, keepdims=True), sink)
            pr = jnp.exp2(s - m)
            den = jnp.sum(pr, axis=0, keepdims=True) + jnp.exp2(sink - m)
            prb = pr.astype(BF16)
            pt_scr[gidx, 0:WINDOW, cols] = prb * prev_w
            pt_scr[gidx, WINDOW:, cols] = prb * cur_w
            inv_scr[jb, hd:hd + 1, :] = 1.0 / den

    state_phase(2)

    for gidx, (jb, kh) in enumerate(groups):
        vband = jnp.concatenate([vt_scr[jb, kh * ATTN_HEAD_DIM:(kh + 1) * ATTN_HEAD_DIM, :],
                                 vt_scr[jb + 1, kh * ATTN_HEAD_DIM:(kh + 1) * ATTN_HEAD_DIM, :]],
                                axis=1)
        otg = _dot(vband, pt_scr[gidx])
        for gi in range(ATTN_GROUP):
            hd = kh * ATTN_GROUP + gi
            ot_scr[jb, hd * ATTN_HEAD_DIM:(hd + 1) * ATTN_HEAD_DIM, :] = (
                otg[:, gi * WINDOW:(gi + 1) * WINDOW] * inv_scr[jb, hd:hd + 1, :])

    state_phase(3)

    parts = 2 if nblk % 2 == 0 else 1
    rows = _row_splits(tq, parts)
    ogs = []
    for r in rows:
        o = jnp.concatenate([ot_scr[jb].T for jb in range(r.start // WINDOW, r.stop // WINDOW)],
                            axis=0)
        ogs.append((o * _silu(g[r, :])).astype(BF16))
    _out_proj_and_residual(rows, ogs, x, emb, gpost_ref[...], wout_ref, wgate_ref, xo_ref)

    kext_scr[0:WINDOW, :] = kext_scr[tq:tq + WINDOW, :]
    vt_scr[0] = vt_scr[nblk]

    @pl.when(step == n_steps - 1)
    def _():
        for slot in range(2):
            for i in range(per_phase):
                load_copy(last_seq, slot, i).wait()
                store_copy(last_seq, slot, i).wait()


def _const_spec(shape):
    nd = len(shape)
    return pl.BlockSpec(shape, lambda *_: (0,) * nd)


def _attn_prompt_layer(layer, x, p_all, tables, gpre, gpost, sinks, w_qt, w_k, w_vt, w_g,
                       w_out, w_gate, w_ple, ret_qkv, ret_state, dec_len):
    nb, seq, d = x.shape
    tq = min(512, seq)
    nblk = tq // WINDOW
    nt = seq // tq
    n_seq = ret_state.shape[0] // (nb * nt)
    assert n_seq * nb * nt == ret_state.shape[0] and n_seq % STATE_PHASES == 0, (
        ret_state.shape, nb, nt)
    ple = p_all.shape[-1]
    cos, sin, cost, sint = tables
    kernel = functools.partial(_attn_prompt_kernel, tq=tq, n_seq=n_seq, dec_len=dec_len)
    tok = lambda b, t: (b, t, 0)
    win_blk = lambda b, t: (b, 0, 0)
    stok = lambda b, t: (b * nt + t, 0)
    consts = (gpre, gpost, w_qt, w_k, w_vt, w_g, w_out, w_gate, w_ple)
    rq, rk, rkd, rv = ret_qkv
    st_shape = (2, n_seq // STATE_PHASES) + ret_state.shape[1:]
    return pl.pallas_call(
        kernel,
        grid=(nb, nt),
        in_specs=[
            pl.BlockSpec(memory_space=pltpu.SMEM),
            pl.BlockSpec((None, tq, d), tok),
            pl.BlockSpec((None, None, tq, ple), lambda b, t: (layer, b, t, 0)),
            pl.BlockSpec((tq, LANES), lambda b, t: (t, 0)),
            pl.BlockSpec((tq, LANES), lambda b, t: (t, 0)),
            pl.BlockSpec((ATTN_HEAD_DIM // 2, tq), lambda b, t: (0, t)),
            pl.BlockSpec((ATTN_HEAD_DIM // 2, tq), lambda b, t: (0, t)),
        ] + [_const_spec(c.shape) for c in consts] + [
            pl.BlockSpec((n_seq * dec_len, RET_QK_W), stok),
            pl.BlockSpec((n_seq * dec_len, RET_QK_W), stok),
            pl.BlockSpec((n_seq * dec_len, RET_QK_W), stok),
            pl.BlockSpec((n_seq * dec_len, RET_VW), stok),
            pl.BlockSpec(memory_space=pl.ANY),
        ],
        out_specs=[
            pl.BlockSpec((None, tq, d), tok),
            pl.BlockSpec((None, ATTN_KV_DIM, WINDOW), win_blk),
            pl.BlockSpec((None, ATTN_KV_DIM, WINDOW), win_blk),
            pl.BlockSpec((n_seq * dec_len, RET_VW), stok),
            pl.BlockSpec(memory_space=pl.ANY),
        ],
        out_shape=[
            jax.ShapeDtypeStruct(x.shape, F32),
            jax.ShapeDtypeStruct((nb, ATTN_KV_DIM, WINDOW), F32),
            jax.ShapeDtypeStruct((nb, ATTN_KV_DIM, WINDOW), F32),
            jax.ShapeDtypeStruct((rq.shape[0], RET_VW), F32),
            jax.ShapeDtypeStruct(ret_state.shape, F32),
        ],
        scratch_shapes=[
            pltpu.VMEM((nblk, ATTN_Q_DIM, WINDOW), BF16),
            pltpu.VMEM((tq + WINDOW, ATTN_KV_DIM), BF16),
            pltpu.VMEM((nblk + 1, ATTN_KV_DIM, WINDOW), BF16),
            pltpu.VMEM((nblk * ATTN_KV_HEADS, 2 * WINDOW, ATTN_GROUP * WINDOW), F32),
            pltpu.VMEM((nblk * ATTN_KV_HEADS, 2 * WINDOW, ATTN_GROUP * WINDOW), BF16),
            pltpu.VMEM((nblk, ATTN_HEADS, WINDOW), F32),
            pltpu.VMEM((nblk, ATTN_Q_DIM, WINDOW), F32),
            pltpu.VMEM(st_shape, F32),
            pltpu.VMEM(st_shape, F32),
            pltpu.SemaphoreType.DMA((2, n_seq // STATE_PHASES)),
            pltpu.SemaphoreType.DMA((2, n_seq // STATE_PHASES)),
        ],
        compiler_params=pltpu.CompilerParams(
            dimension_semantics=("arbitrary", "arbitrary"),
            vmem_limit_bytes=ATTN_PROMPT_VMEM_LIMIT_BYTES),
        name="attn_prompt_layer",
    )(sinks, x, p_all, cos, sin, cost, sint, *consts, rq, rk, rkd, rv, ret_state)


def _attn_sample_kernel(sink_ref, x_ref, p_ref, cos_ref, sin_ref, cost_ref, sint_ref,
                        gpre_ref, gpost_ref, wq_ref, wkt_ref, wvt_ref, wg_ref,
                        wout_ref, wgate_ref, wple_ref, kc_ref, vc_ref,
                        xo_ref, ko_ref, vo_ref,
                        q_scr, knt_scr, vnt_scr, o_scr, *, n_seq, dec_len):
    half = ATTN_HEAD_DIM // 2
    x = x_ref[...]
    h = _rms_norm(x, gpre_ref[...]).astype(BF16)
    q_scr[...] = (_rope_attn(_dot(h, wq_ref[...]), cos_ref[...], sin_ref[...])
                  * (ATTN_HEAD_DIM ** -0.5 * LOG2_E))
    knt = _dot_nt(wkt_ref[...], h)
    cost = cost_ref[...]
    sint = sint_ref[...]
    pieces = []
    for kh in range(ATTN_KV_HEADS):
        x1 = knt[kh * ATTN_HEAD_DIM:kh * ATTN_HEAD_DIM + half, :]
        x2 = knt[kh * ATTN_HEAD_DIM + half:(kh + 1) * ATTN_HEAD_DIM, :]
        pieces.append(x1 * cost - x2 * sint)
        pieces.append(x2 * cost + x1 * sint)
    knt_scr[...] = jnp.concatenate(pieces, axis=0)
    vnt_scr[...] = _dot_nt(wvt_ref[...], h)

    rows = 2 * dec_len
    tok = lax.broadcasted_iota(jnp.int32, (rows, 2 * WINDOW), 0) % dec_len
    col = lax.broadcasted_iota(jnp.int32, (rows, 2 * WINDOW), 1)
    upper_row = lax.broadcasted_iota(jnp.int32, (rows, 1), 0) >= dec_len
    lane = lax.broadcasted_iota(jnp.int32, (ATTN_HEAD_DIM, WINDOW), 1)
    keep_old = lane < WINDOW - dec_len
    zeros = jnp.zeros((ATTN_HEAD_DIM, 2 * WINDOW), BF16)

    def select_head(t, parity):
        return jnp.concatenate([t, zeros] if parity == 0 else [zeros, t], axis=0)

    def scores(s):
        tok_rows = slice(s * dec_len, (s + 1) * dec_len)
        new_col = col - (WINDOW + s * dec_len)
        mask = ((col < WINDOW) & (col > tok)) | ((new_col >= 0) & (new_col <= tok))
        new_shift = (WINDOW - dec_len - s * dec_len) % WINDOW
        out = []
        for kh in range(ATTN_KV_HEADS):
            hrows = slice(kh * ATTN_HEAD_DIM, (kh + 1) * ATTN_HEAD_DIM)
            kc = kc_ref[s, kh]
            vc = vc_ref[s, kh]
            kn = knt_scr[hrows, :]
            vn = vnt_scr[hrows, :]
            kn_at_end = pltpu.roll(kn, new_shift, axis=1) if new_shift else kn
            vn_at_end = pltpu.roll(vn, new_shift, axis=1) if new_shift else vn
            ko_ref[s, kh] = jnp.where(keep_old, pltpu.roll(kc, WINDOW - dec_len, axis=1), kn_at_end)
            vo_ref[s, kh] = jnp.where(keep_old, pltpu.roll(vc, WINDOW - dec_len, axis=1), vn_at_end)

            kt = jnp.concatenate([kc, kn], axis=1).astype(BF16)
            vt = jnp.concatenate([vc, vn], axis=1).astype(BF16)
            qp = jnp.concatenate([q_scr[tok_rows, (2 * kh + c) * LANES:(2 * kh + c + 1) * LANES]
                                  for c in range(2)], axis=0).astype(BF16)
            sc = [jnp.where(mask, _dot(qp, select_head(kt, parity)), NEG_INF)
                  for parity in range(2)]
            out.append((sc, vt))
        return out

    def values(s, per_head):
        tok_rows = slice(s * dec_len, (s + 1) * dec_len)
        probs = []
        for kh, (scs, vt) in enumerate(per_head):
            for parity, sc in enumerate(scs):
                hd_lo, hd_hi = 4 * kh + parity, 4 * kh + 2 + parity
                sink = jnp.where(upper_row, sink_ref[hd_hi], sink_ref[hd_lo]) * LOG2_E
                m = jnp.maximum(jnp.max(sc, axis=-1, keepdims=True), sink)
                pr = jnp.exp2(sc - m)
                den = jnp.sum(pr, axis=-1, keepdims=True) + jnp.exp2(sink - m)
                probs.append((pr / den).astype(BF16))
        for kh, (scs, vt) in enumerate(per_head):
            o_pair = (_dot_nt(probs[2 * kh], select_head(vt, 0))
                      + _dot_nt(probs[2 * kh + 1], select_head(vt, 1)))
            for c in range(2):
                o_scr[tok_rows, (2 * kh + c) * LANES:(2 * kh + c + 1) * LANES] = (
                    o_pair[c * dec_len:(c + 1) * dec_len, :])

    group = min(SAMPLE_ATTN_PIPELINE_GROUP, n_seq)
    pending = [scores(s) for s in range(group)]
    for s0 in range(0, n_seq, group):
        upcoming = [scores(s) for s in range(s0 + group, min(s0 + 2 * group, n_seq))]
        for i, per_head in enumerate(pending):
            values(s0 + i, per_head)
        pending = upcoming

    g = _dot(h, wg_ref[...])
    emb = _dot(p_ref[...].astype(BF16), wple_ref[...])
    og = (o_scr[...] * _silu(g)).astype(BF16)
    _out_proj_and_residual(_row_splits(x.shape[0], 1), [og], x, emb, gpost_ref[...],
                           wout_ref, wgate_ref, xo_ref)


def _attn_sample_layer(layer, x, p_all, tables, gpre, gpost, sinks, w_q, w_kt, w_vt, w_g,
                       w_out, w_gate, w_ple, cache_kt, cache_vt, dec_len):
    n_tok, d = x.shape
    n_all = cache_kt.shape[0]
    n_seq = min(16, n_all)
    tq = n_seq * dec_len
    ple = p_all.shape[-1]
    cos, sin, cost, sint = tables
    kernel = functools.partial(_attn_sample_kernel, n_seq=n_seq, dec_len=dec_len)
    tok = lambda i: (i, 0)
    cache_shape = (n_seq, ATTN_KV_HEADS, ATTN_HEAD_DIM, WINDOW)
    cache_blk = lambda i: (i, 0, 0, 0)
    consts = (cos, sin, cost, sint, gpre, gpost, w_q, w_kt, w_vt, w_g, w_out, w_gate, w_ple)
    return pl.pallas_call(
        kernel,
        grid=(n_all // n_seq,),
        in_specs=[
            pl.BlockSpec(memory_space=pltpu.SMEM),
            pl.BlockSpec((tq, d), tok),
            pl.BlockSpec((None, tq, ple), lambda i: (layer, i, 0)),
        ] + [_const_spec(c.shape) for c in consts] + [
            pl.BlockSpec(cache_shape, cache_blk),
            pl.BlockSpec(cache_shape, cache_blk),
        ],
        out_specs=[
            pl.BlockSpec((tq, d), tok),
            pl.BlockSpec(cache_shape, cache_blk),
            pl.BlockSpec(cache_shape, cache_blk),
        ],
        out_shape=[
            jax.ShapeDtypeStruct(x.shape, F32),
            jax.ShapeDtypeStruct(cache_kt.shape, F32),
            jax.ShapeDtypeStruct(cache_vt.shape, F32),
        ],
        scratch_shapes=[
            pltpu.VMEM((tq, ATTN_Q_DIM), F32),
            pltpu.VMEM((ATTN_KV_DIM, tq), F32),
            pltpu.VMEM((ATTN_KV_DIM, tq), F32),
            pltpu.VMEM((tq, ATTN_Q_DIM), F32),
        ],
        compiler_params=pltpu.CompilerParams(
            dimension_semantics=("arbitrary",),
            vmem_limit_bytes=VMEM_LIMIT_BYTES),
        name="attn_sample_layer",
    )(sinks, x, p_all, *consts, cache_kt, cache_vt)


RET_GATE_COL0 = 2 * RET_QK_W + RET_VW


def _ret_project(h, cos, sin, win_ref, chunk_len):
    q0, k0, v0, g0 = 0, RET_QK_W, 2 * RET_QK_W, RET_GATE_COL0
    q = _rope_ret(_dot(h, win_ref[:, q0:k0]), cos, sin)
    k = _rope_ret(_dot(h, win_ref[:, k0:v0]), cos, sin) * (RET_QK_DIM ** -0.5)
    v = _dot(h, win_ref[:, v0:g0])
    n = h.shape[0]
    idx = (lax.broadcasted_iota(jnp.int32, (n, 1), 0) % chunk_len).astype(F32)
    kd = jnp.concatenate(
        [k[:, hd * RET_QK_DIM:(hd + 1) * RET_QK_DIM]
         * jnp.exp((chunk_len - 1.0 - idx) * RET_LOG_GAMMA[hd]) for hd in range(RET_HEADS)], axis=1)
    return q, k, kd, v


def _ret_decay_mask(n, chunk_len, hd):
    ri = lax.broadcasted_iota(jnp.int32, (n, n), 0)
    ci = lax.broadcasted_iota(jnp.int32, (n, n), 1)
    diff = ri - ci
    ok = (diff >= 0) & ((ri // chunk_len) == (ci // chunk_len))
    return jnp.where(ok, jnp.exp(jnp.maximum(diff, 0).astype(F32) * RET_LOG_GAMMA[hd]), 0.0)


def _ret_gated(o, g):
    outs = []
    for hd in range(RET_HEADS):
        oh = o[:, hd * RET_V_DIM:(hd + 1) * RET_V_DIM]
        mu = jnp.mean(oh, axis=-1, keepdims=True)
        var = jnp.mean(jnp.square(oh - mu), axis=-1, keepdims=True)
        outs.append((oh - mu) * lax.rsqrt(var + NORM_EPS))
    return (jnp.concatenate(outs, axis=1) * _silu(g)).astype(BF16)


def _ret_prompt_kernel(x_ref, p_ref, cos_ref, sin_ref, gpre_ref, gpost_ref,
                       win_ref, wout_ref, wgate_ref, wple_ref,
                       xo_ref, st_ref,
                       q_scr, k_scr, kd_scr, v_scr, o_scr, g_scr, dm_scr, *, tq):
    b = pl.program_id(0)
    t = pl.program_id(1)

    @pl.when((b == 0) & (t == 0))
    def _():
        for hd in range(RET_HEADS):
            dm_scr[hd] = _ret_decay_mask(RET_CHUNK, RET_CHUNK, hd)

    @pl.when(t == 0)
    def _():
        st_ref[...] = jnp.zeros(st_ref.shape, F32)

    emb = _dot(p_ref[...].astype(BF16), wple_ref[...])
    x = x_ref[...]
    h = _rms_norm(x, gpre_ref[...]).astype(BF16)
    q, k, kd, v = _ret_project(h, cos_ref[...], sin_ref[...], win_ref, RET_CHUNK)
    q_scr[...] = q.astype(BF16)
    k_scr[...] = k.astype(BF16)
    kd_scr[...] = kd.astype(BF16)
    v_scr[...] = v.astype(BF16)

    idx = lax.broadcasted_iota(jnp.int32, (RET_CHUNK, 1), 0).astype(F32)
    heads = range(RET_HEADS)
    qs = [slice(hd * RET_QK_DIM, (hd + 1) * RET_QK_DIM) for hd in heads]
    vs = [slice(hd * RET_V_DIM, (hd + 1) * RET_V_DIM) for hd in heads]
    chunks = _row_splits(tq, tq // RET_CHUNK)
    gate_cols = iter(_row_splits(RET_VW, 2 * len(chunks)))

    def gate_piece():
        c = next(gate_cols)
        g_scr[:, c] = _dot(h, win_ref[:, RET_GATE_COL0 + c.start:RET_GATE_COL0 + c.stop])

    for rows in chunks:
        scores = [_dot_nt(q_scr[rows, qs[hd]], k_scr[rows, qs[hd]]) for hd in heads]
        cross = [_dot(q_scr[rows, qs[hd]], st_ref[hd].astype(BF16)) for hd in heads]
        gate_piece()
        for hd in heads:
            st_ref[hd] = (math.exp(RET_CHUNK * RET_LOG_GAMMA[hd]) * st_ref[hd]
                          + _dot_tn(kd_scr[rows, qs[hd]], v_scr[rows, vs[hd]]))
        for hd in heads:
            inner = _dot((scores[hd] * dm_scr[hd]).astype(BF16), v_scr[rows, vs[hd]])
            o_scr[rows, vs[hd]] = inner + cross[hd] * jnp.exp((idx + 1.0) * RET_LOG_GAMMA[hd])
        gate_piece()

    ogs = [_ret_gated(o_scr[r, :], g_scr[r, :]) for r in chunks]
    _out_proj_and_residual(chunks, ogs, x, emb, gpost_ref[...], wout_ref, wgate_ref, xo_ref)


def _ret_prompt_layer(layer, x, p_all, cos, sin, gpre, gpost, w_in, w_out, w_gate, w_ple):
    nb, seq, d = x.shape
    tq = min(512, seq)
    ple = p_all.shape[-1]
    kernel = functools.partial(_ret_prompt_kernel, tq=tq)
    tok = lambda b, t: (b, t, 0)
    return pl.pallas_call(
        kernel,
        grid=(nb, seq // tq),
        in_specs=[
            pl.BlockSpec((None, tq, d), tok),
            pl.BlockSpec((None, None, tq, ple), lambda b, t: (layer, b, t, 0)),
            pl.BlockSpec((tq, LANES), lambda b, t: (t, 0)),
            pl.BlockSpec((tq, LANES), lambda b, t: (t, 0)),
            _const_spec(gpre.shape), _const_spec(gpost.shape),
            _const_spec(w_in.shape), _const_spec(w_out.shape),
            _const_spec(w_gate.shape), _const_spec(w_ple.shape),
        ],
        out_specs=[
            pl.BlockSpec((None, tq, d), tok),
            pl.BlockSpec((None, RET_HEADS, RET_QK_DIM, RET_V_DIM), lambda b, t: (b, 0, 0, 0)),
        ],
        out_shape=[
            jax.ShapeDtypeStruct(x.shape, F32),
            jax.ShapeDtypeStruct((nb, RET_HEADS, RET_QK_DIM, RET_V_DIM), F32),
        ],
        scratch_shapes=[
            pltpu.VMEM((tq, RET_QK_W), BF16),
            pltpu.VMEM((tq, RET_QK_W), BF16),
            pltpu.VMEM((tq, RET_QK_W), BF16),
            pltpu.VMEM((tq, RET_VW), BF16),
            pltpu.VMEM((tq, RET_VW), F32),
            pltpu.VMEM((tq, RET_VW), F32),
            pltpu.VMEM((RET_HEADS, RET_CHUNK, RET_CHUNK), F32),
        ],
        compiler_params=pltpu.CompilerParams(
            dimension_semantics=("arbitrary", "arbitrary"),
            vmem_limit_bytes=VMEM_LIMIT_BYTES),
        name="ret_prompt_layer",
    )(x, p_all, cos, sin, gpre, gpost, w_in, w_out, w_gate, w_ple)


def _ret_sample_proj_kernel(x_ref, cos_ref, sin_ref, gpre_ref, win_ref,
                            q_ref, k_ref, kd_ref, v_ref, g_ref, *, dec_len):
    h = _rms_norm(x_ref[...], gpre_ref[...]).astype(BF16)
    q, k, kd, v = _ret_project(h, cos_ref[...], sin_ref[...], win_ref, dec_len)
    q_ref[...] = q
    k_ref[...] = k
    kd_ref[...] = kd
    v_ref[...] = v
    g_ref[...] = _dot(h, win_ref[:, RET_GATE_COL0:])


def _ret_sample_proj(x, cos, sin, gpre, w_in, dec_len):
    n_tok, d = x.shape
    tq = min(512, n_tok)
    kernel = functools.partial(_ret_sample_proj_kernel, dec_len=dec_len)
    tok = lambda i: (i, 0)
    widths = (RET_QK_W, RET_QK_W, RET_QK_W, RET_VW, RET_VW)
    return pl.pallas_call(
        kernel,
        grid=(n_tok // tq,),
        in_specs=[
            pl.BlockSpec((tq, d), tok),
            _const_spec((tq, LANES)), _const_spec((tq, LANES)),
            _const_spec(gpre.shape), _const_spec(w_in.shape),
        ],
        out_specs=[pl.BlockSpec((tq, w), tok) for w in widths],
        out_shape=[jax.ShapeDtypeStruct((n_tok, w), F32) for w in widths],
        compiler_params=pltpu.CompilerParams(
            dimension_semantics=("arbitrary",),
            vmem_limit_bytes=VMEM_LIMIT_BYTES),
        name="ret_sample_proj",
    )(x, cos, sin, gpre, w_in)


def _ret_state_update_tile(q_ref, k_ref, kd_ref, v_ref, st_in, st_out, o_ref, *,
                           first_seq, n_seq, dec_len):
    pad_rows = 2 * dec_len
    idx = lax.broadcasted_iota(jnp.int32, (pad_rows, 1), 0).astype(F32)

    def tok_rows(s):
        return slice((first_seq + s) * dec_len, (first_seq + s + 1) * dec_len)

    def padded(ref, s, cols):
        val = ref[tok_rows(s), cols]
        return jnp.concatenate([val, jnp.zeros_like(val)], axis=0).astype(BF16)

    units = [(s, hd) for s in range(n_seq) for hd in range(RET_HEADS)]
    qs = [slice(hd * RET_QK_DIM, (hd + 1) * RET_QK_DIM) for hd in range(RET_HEADS)]
    vs = [slice(hd * RET_V_DIM, (hd + 1) * RET_V_DIM) for hd in range(RET_HEADS)]
    qh = {(s, hd): padded(q_ref, s, qs[hd]) for s, hd in units}
    vh = {(s, hd): padded(v_ref, s, vs[hd]) for s, hd in units}
    scores = {(s, hd): _dot_nt(qh[s, hd], padded(k_ref, s, qs[hd])) for s, hd in units}
    cross = {(s, hd): _dot(qh[s, hd], st_in[s, hd].astype(BF16)) for s, hd in units}
    for s, hd in units:
        st_out[s, hd] = (math.exp(dec_len * RET_LOG_GAMMA[hd]) * st_in[s, hd]
                         + _dot_tn(padded(kd_ref, s, qs[hd]), vh[s, hd]))
    masks = [_ret_decay_mask(pad_rows, dec_len, hd) for hd in range(RET_HEADS)]
    for s, hd in units:
        inner = _dot((scores[s, hd] * masks[hd]).astype(BF16), vh[s, hd])
        out = inner + cross[s, hd] * jnp.exp((idx + 1.0) * RET_LOG_GAMMA[hd])
        o_ref[tok_rows(s), vs[hd]] = out[0:dec_len, :]


def _ret_sample_finish_kernel(o_ref, g_ref, x_ref, p_ref, gpost_ref, wout_ref, wgate_ref, wple_ref,
                              xo_ref):
    emb = _dot(p_ref[...].astype(BF16), wple_ref[...])
    x = x_ref[...]
    rows = _row_splits(x.shape[0], 2)
    ogs = [_ret_gated(o_ref[r, :], g_ref[r, :]) for r in rows]
    _out_proj_and_residual(rows, ogs, x, emb, gpost_ref[...], wout_ref, wgate_ref, xo_ref)


def _ret_sample_finish(layer, o, g, x, p_all, gpost, w_out, w_gate, w_ple):
    n_tok, d = x.shape
    tq = min(512, n_tok)
    ple = p_all.shape[-1]
    tok = lambda i: (i, 0)
    return pl.pallas_call(
        _ret_sample_finish_kernel,
        grid=(n_tok // tq,),
        in_specs=[
            pl.BlockSpec((tq, RET_VW), tok), pl.BlockSpec((tq, RET_VW), tok),
            pl.BlockSpec((tq, d), tok),
            pl.BlockSpec((None, tq, ple), lambda i: (layer, i, 0)),
            _const_spec(gpost.shape), _const_spec(w_out.shape),
            _const_spec(w_gate.shape), _const_spec(w_ple.shape),
        ],
        out_specs=pl.BlockSpec((tq, d), tok),
        out_shape=jax.ShapeDtypeStruct(x.shape, F32),
        compiler_params=pltpu.CompilerParams(
            dimension_semantics=("arbitrary",),
            vmem_limit_bytes=VMEM_LIMIT_BYTES),
        name="ret_sample_finish",
    )(o, g, x, p_all, gpost, w_out, w_gate, w_ple)


def _rope_tables(pos, head_dim):
    half = head_dim // 2
    inv = ROPE_THETA ** (-np.arange(half, dtype=np.float64) / half)
    ang = np.asarray(pos, np.float64)[:, None] * inv[None, :]
    return np.cos(ang).astype(np.float32), np.sin(ang).astype(np.float32)


def _attn_tables(pos):
    cos, sin = _rope_tables(pos, ATTN_HEAD_DIM)
    reps = LANES // ATTN_HEAD_DIM
    return (np.tile(np.concatenate([cos, cos], axis=1), (1, reps)),
            np.tile(np.concatenate([-sin, sin], axis=1), (1, reps)))


def kernel(x_prompt, x_sample, cache_k_win, cache_v_win, state_ret, p_prompt, p_sample, pre_norm,
           post_norm, w_in_attn, attn_sinks, w_out_attn, w_in_ret, w_out_ret, w_ple, w_ple_gate):
    depth = p_prompt.shape[0]
    nb, seq, d = x_prompt.shape
    n_dec, dec_len, _ = x_sample.shape
    n_stok = n_dec * dec_len
    assert seq % 512 == 0 or seq in (128, 256), seq

    pos_p = np.arange(seq)
    pos_s = PAST_LEN + np.arange(dec_len)
    cos_ap, sin_ap = _attn_tables(pos_p)
    cos_apt, sin_apt = (np.ascontiguousarray(a.T) for a in _rope_tables(pos_p, ATTN_HEAD_DIM))
    cos_as, sin_as = _attn_tables(pos_s)
    cos_rp, sin_rp = _rope_tables(pos_p, RET_QK_DIM)
    cos_rs, sin_rs = _rope_tables(pos_s, RET_QK_DIM)
    attn_s_tile = min(16, n_dec)
    ret_s_tile = min(512, n_stok) // dec_len
    cos_as, sin_as = (np.tile(a, (attn_s_tile, 1)) for a in (cos_as, sin_as))
    cos_ast, sin_ast = (np.ascontiguousarray(np.tile(a, (attn_s_tile, 1)).T)
                        for a in _rope_tables(pos_s, ATTN_HEAD_DIM))
    cos_rs, sin_rs = (np.tile(a, (ret_s_tile, 1)) for a in (cos_rs, sin_rs))
    (cos_ap, sin_ap, cos_apt, sin_apt, cos_as, sin_as, cos_ast, sin_ast,
     cos_rp, sin_rp, cos_rs, sin_rs) = (
        jnp.asarray(a) for a in
        (cos_ap, sin_ap, cos_apt, sin_apt, cos_as, sin_as, cos_ast, sin_ast,
         cos_rp, sin_rp, cos_rs, sin_rs))

    assert depth == 2, depth
    xs = x_sample.reshape(n_stok, d)
    p_s = p_sample.reshape(depth, n_stok, p_sample.shape[-1])
    bf = lambda w: w.astype(BF16)
    gpre = [pre_norm[i][None, :] for i in range(depth)]
    gpost = [post_norm[i][None, :] for i in range(depth)]
    w_gate = [bf(w_ple_gate[i]) for i in range(depth)]
    w_pl = [bf(w_ple[i]) for i in range(depth)]

    w_in = bf(w_in_attn[0])
    w_out = bf(w_out_attn[0])
    sinks = attn_sinks[0]
    k0, v0, g0 = ATTN_Q_DIM, ATTN_Q_DIM + ATTN_KV_DIM, ATTN_Q_DIM + 2 * ATTN_KV_DIM
    w_q, w_k, w_v, w_g = w_in[:, :k0], w_in[:, k0:v0], w_in[:, v0:g0], w_in[:, g0:]
    to_t = lambda a: a.transpose(0, 2, 3, 1)
    from_t = lambda a: a.transpose(0, 3, 1, 2)
    xs, knt, vnt = _attn_sample_layer(
        0, xs, p_s, (cos_as, sin_as, cos_ast, sin_ast), gpre[0], gpost[0], sinks,
        w_q, w_k.T, w_v.T, w_g, w_out, w_gate[0], w_pl[0],
        to_t(cache_k_win[0]), to_t(cache_v_win[0]), dec_len)

    w_in_r = bf(w_in_ret[0])
    w_out_r = bf(w_out_ret[0])
    rq, rk, rkd, rv, rg = _ret_sample_proj(xs, cos_rs, sin_rs, gpre[1], w_in_r, dec_len)
    xp, kpt, vpt, ro, ret_state_sample = _attn_prompt_layer(
        0, x_prompt, p_prompt, (cos_ap, sin_ap, cos_apt, sin_apt), gpre[0], gpost[0], sinks,
        w_q.T, w_k, w_v.T, w_g, w_out, w_gate[0], w_pl[0],
        (rq, rk, rkd, rv), state_ret[0], dec_len)
    head_shape = (nb, ATTN_KV_HEADS, ATTN_HEAD_DIM, WINDOW)

    xp, ret_state_prompt = _ret_prompt_layer(1, xp, p_prompt, cos_rp, sin_rp, gpre[1], gpost[1],
                                             w_in_r, w_out_r, w_gate[1], w_pl[1])
    xs = _ret_sample_finish(1, ro, rg, xs, p_s, gpost[1], w_out_r, w_gate[1], w_pl[1])
    return (xp, xs.reshape(n_dec, dec_len, d),
            from_t(kpt.reshape(head_shape))[None], from_t(vpt.reshape(head_shape))[None],
            from_t(knt)[None], from_t(vnt)[None], ret_state_prompt[None], ret_state_sample[None])
```

```python
import functools
import math

import numpy as np
import jax
import jax.numpy as jnp
from jax import lax
from jax.experimental import pallas as pl
from jax.experimental.pallas import tpu as pltpu

F32 = jnp.float32
BF16 = jnp.bfloat16

PAST_LEN = 16384
ROPE_THETA = 10000.0
NORM_EPS = 1e-6
NEG_INF = -1e30
WINDOW = 128
ATTN_HEADS = 16
ATTN_KV_HEADS = 4
ATTN_GROUP = ATTN_HEADS // ATTN_KV_HEADS
ATTN_HEAD_DIM = 64
ATTN_Q_DIM = ATTN_HEADS * ATTN_HEAD_DIM
ATTN_KV_DIM = ATTN_KV_HEADS * ATTN_HEAD_DIM
RET_HEADS = 4
RET_QK_DIM = 256
RET_V_DIM = 512
RET_QK_W = RET_HEADS * RET_QK_DIM
RET_VW = RET_HEADS * RET_V_DIM
RET_CHUNK = 256
LANES = 128

LOG2_E = math.log2(math.e)
RET_LOG_GAMMA = tuple(math.log1p(-(2.0 ** (-5.0 - h))) for h in range(RET_HEADS))

VMEM_LIMIT_BYTES = 56 * 1024 * 1024
ATTN_PROMPT_VMEM_LIMIT_BYTES = 50 * 1024 * 1024
STATE_PHASES = 4
SAMPLE_ATTN_PIPELINE_GROUP = 8


def _dot(a, b):
    return jnp.dot(a, b, preferred_element_type=F32)


def _dot_nt(a, b):
    return lax.dot_general(a, b, (((1,), (1,)), ((), ())), preferred_element_type=F32)


def _dot_tn(a, b):
    return lax.dot_general(a, b, (((0,), (0,)), ((), ())), preferred_element_type=F32)


def _rms_norm(x, g):
    return x * lax.rsqrt(jnp.mean(x * x, axis=-1, keepdims=True) + NORM_EPS) * g


def _silu(g):
    return g * jax.nn.sigmoid(g)


def _row_splits(n, parts):
    step = n // parts
    return [slice(i * step, (i + 1) * step) for i in range(parts)]


def _out_proj_and_residual(rows, ogs, x, emb, gpost, wout_ref, wgate_ref, xo_ref):
    ys = [_dot(og, wout_ref[...]) for og in ogs]
    x1s = [x[r, :] + _rms_norm(y, gpost) for r, y in zip(rows, ys)]
    gates = [_dot(x1.astype(BF16), wgate_ref[...]) for x1 in x1s]
    for r, x1, gate in zip(rows, x1s, gates):
        xo_ref[r, :] = x1 + jax.nn.sigmoid(gate) * emb[r, :]


def _rope_attn(x, cos, sin_signed):
    lane = lax.broadcasted_iota(jnp.int32, (1, LANES), 1)
    first_half = (lane % ATTN_HEAD_DIM) < (ATTN_HEAD_DIM // 2)
    outs = []
    for j in range(x.shape[1] // LANES):
        xj = x[:, j * LANES:(j + 1) * LANES]
        fwd = pltpu.roll(xj, LANES - ATTN_HEAD_DIM // 2, axis=1)
        bwd = pltpu.roll(xj, ATTN_HEAD_DIM // 2, axis=1)
        outs.append(xj * cos + jnp.where(first_half, fwd, bwd) * sin_signed)
    return jnp.concatenate(outs, axis=1)


def _rope_ret(x, cos, sin):
    half = RET_QK_DIM // 2
    outs = []
    for h in range(x.shape[1] // RET_QK_DIM):
        x1 = x[:, h * RET_QK_DIM:h * RET_QK_DIM + half]
        x2 = x[:, h * RET_QK_DIM + half:(h + 1) * RET_QK_DIM]
        outs.append(x1 * cos - x2 * sin)
        outs.append(x2 * cos + x1 * sin)
    return jnp.concatenate(outs, axis=1)


def _attn_prompt_kernel(sink_ref, x_ref, p_ref, cos_ref, sin_ref, cost_ref, sint_ref,
                        gpre_ref, gpost_ref, wqt_ref, wk_ref, wvt_ref, wg_ref,
                        wout_ref, wgate_ref, wple_ref,
                        rq_ref, rk_ref, rkd_ref, rv_ref, rstate_hbm,
                        xo_ref, kw_ref, vw_ref, ro_ref, rstate_out_hbm,
                        qt_scr, kext_scr, vt_scr, st_scr, pt_scr, inv_scr, ot_scr,
                        rst_in, rst_out, sem_in, sem_out, *, tq, n_seq, dec_len):
    nblk = tq // WINDOW
    half = ATTN_HEAD_DIM // 2
    t = pl.program_id(1)
    nt = pl.num_programs(1)

    @pl.when(t == 0)
    def _():
        kext_scr[0:WINDOW, :] = jnp.zeros((WINDOW, ATTN_KV_DIM), BF16)
        vt_scr[0] = jnp.zeros((ATTN_KV_DIM, WINDOW), BF16)

    per_phase = n_seq // STATE_PHASES
    step = pl.program_id(0) * nt + t
    n_steps = pl.num_programs(0) * nt
    group0 = step * STATE_PHASES
    last_group = n_steps * STATE_PHASES - 1

    def load_copy(phase, i):
        slot = phase % 2
        seq = jnp.minimum(group0 + phase, last_group) * per_phase + i
        return pltpu.make_async_copy(rstate_hbm.at[seq], rst_in.at[slot, i], sem_in.at[slot, i])

    def store_copy(phase, i):
        slot = phase % 2
        seq = jnp.maximum(group0 + phase, 0) * per_phase + i
        return pltpu.make_async_copy(rst_out.at[slot, i], rstate_out_hbm.at[seq],
                                     sem_out.at[slot, i])

    @pl.when(step == 0)
    def _():
        rst_out[...] = jnp.zeros(rst_out.shape, F32)
        for i in range(per_phase):
            load_copy(0, i).start()
            store_copy(-2, i).start()

    def state_phase(phase):
        for i in range(per_phase):
            load_copy(phase, i).wait()
            store_copy(phase - 2, i).wait()
        for i in range(per_phase):
            store_copy(phase - 1, i).start()
            load_copy(phase + 1, i).start()
        slot = phase % 2
        _ret_state_update_tile(rq_ref, rk_ref, rkd_ref, rv_ref, rst_in.at[slot], rst_out.at[slot],
                               ro_ref, first_seq=phase * per_phase, n_seq=per_phase,
                               dec_len=dec_len)

    state_phase(0)

    emb = _dot(p_ref[...].astype(BF16), wple_ref[...])
    x = x_ref[...]
    h = _rms_norm(x, gpre_ref[...]).astype(BF16)

    k = _rope_attn(_dot(h, wk_ref[...]), cos_ref[...], sin_ref[...])
    kext_scr[WINDOW:, :] = k.astype(BF16)
    state_phase(1)

    qt = _dot_nt(wqt_ref[...], h)
    cost = cost_ref[...]
    sint = sint_ref[...]
    pieces = []
    for hd in range(ATTN_HEADS):
        x1 = qt[hd * ATTN_HEAD_DIM:hd * ATTN_HEAD_DIM + half, :]
        x2 = qt[hd * ATTN_HEAD_DIM + half:(hd + 1) * ATTN_HEAD_DIM, :]
        pieces.append(x1 * cost - x2 * sint)
        pieces.append(x2 * cost + x1 * sint)
    qrot = (jnp.concatenate(pieces, axis=0) * (ATTN_HEAD_DIM ** -0.5 * LOG2_E)).astype(BF16)
    for jb in range(nblk):
        qt_scr[jb] = qrot[:, jb * WINDOW:(jb + 1) * WINDOW]

    vt = _dot_nt(wvt_ref[...], h)
    vtb = vt.astype(BF16)
    for jb in range(nblk):
        vt_scr[jb + 1] = vtb[:, jb * WINDOW:(jb + 1) * WINDOW]

    @pl.when(t == nt - 1)
    def _():
        kw_ref[...] = k[tq - WINDOW:, :].T
        vw_ref[...] = vt[:, tq - WINDOW:]

    state_phase(2)
    zero_rows = jnp.zeros((ATTN_HEAD_DIM, ATTN_GROUP * WINDOW), BF16)
    groups = [(jb, kh) for jb in range(nblk) for kh in range(ATTN_KV_HEADS)]

    for gidx, (jb, kh) in enumerate(groups):
        heads = [kh * ATTN_GROUP + gi for gi in range(ATTN_GROUP)]
        pair = kh // 2
        kb = kext_scr[jb * WINDOW:(jb + 2) * WINDOW, pair * LANES:(pair + 1) * LANES]
        qg = jnp.concatenate(
            [qt_scr[jb, hd * ATTN_HEAD_DIM:(hd + 1) * ATTN_HEAD_DIM, :] for hd in heads], axis=1)
        qg = jnp.concatenate([qg, zero_rows] if kh % 2 == 0 else [zero_rows, qg], axis=0)
        st_scr[gidx] = _dot(kb, qg)

    g = _dot(h, wg_ref[...])

    key_r = lax.broadcasted_iota(jnp.int32, (WINDOW, WINDOW), 0)
    query_c = lax.broadcasted_iota(jnp.int32, (WINDOW, WINDOW), 1)
    from_prev = key_r > query_c
    prev_w = from_prev.astype(BF16)
    cur_w = 1.0 - prev_w
    for gidx, (jb, kh) in enumerate(groups):
        for gi in range(ATTN_GROUP):
            hd = kh * ATTN_GROUP + gi
            cols = slice(gi * WINDOW, (gi + 1) * WINDOW)
            s_prev = st_scr[gidx, 0:WINDOW, cols]
            if jb == 0:
                s_prev = jnp.where(t > 0, s_prev, NEG_INF)
            s = jnp.where(from_prev, s_prev, st_scr[gidx, WINDOW:, cols])
            sink = sink_ref[hd] * LOG2_E
            m = jnp.maximum(jnp.max(s, axis=0, keepdims=True), sink)
            pr = jnp.exp2(s - m)
            den = jnp.sum(pr, axis=0, keepdims=True) + jnp.exp2(sink - m)
            prb = pr.astype(BF16)
            pt_scr[gidx, 0:WINDOW, cols] = prb * prev_w
            pt_scr[gidx, WINDOW:, cols] = prb * cur_w
            inv_scr[jb, hd:hd + 1, :] = 1.0 / den

    for gidx, (jb, kh) in enumerate(groups):
        vband = jnp.concatenate([vt_scr[jb, kh * ATTN_HEAD_DIM:(kh + 1) * ATTN_HEAD_DIM, :],
                                 vt_scr[jb + 1, kh * ATTN_HEAD_DIM:(kh + 1) * ATTN_HEAD_DIM, :]],
                                axis=1)
        otg = _dot(vband, pt_scr[gidx])
        for gi in range(ATTN_GROUP):
            hd = kh * ATTN_GROUP + gi
            ot_scr[jb, hd * ATTN_HEAD_DIM:(hd + 1) * ATTN_HEAD_DIM, :] = (
                otg[:, gi * WINDOW:(gi + 1) * WINDOW] * inv_scr[jb, hd:hd + 1, :])

    state_phase(3)

    parts = 2 if nblk % 2 == 0 else 1
    rows = _row_splits(tq, parts)
    ogs = []
    for r in rows:
        o = jnp.concatenate([ot_scr[jb].T for jb in range(r.start // WINDOW, r.stop // WINDOW)],
                            axis=0)
        ogs.append((o * _silu(g[r, :])).astype(BF16))
    _out_proj_and_residual(rows, ogs, x, emb, gpost_ref[...], wout_ref, wgate_ref, xo_ref)

    kext_scr[0:WINDOW, :] = kext_scr[tq:tq + WINDOW, :]
    vt_scr[0] = vt_scr[nblk]

    @pl.when(step == n_steps - 1)
    def _():
        for i in range(per_phase):
            store_copy(STATE_PHASES - 1, i).start()
        for i in range(per_phase):
            load_copy(STATE_PHASES, i).wait()
            store_copy(STATE_PHASES - 2, i).wait()
            store_copy(STATE_PHASES - 1, i).wait()


def _const_spec(shape):
    nd = len(shape)
    return pl.BlockSpec(shape, lambda *_: (0,) * nd)


def _attn_prompt_layer(layer, x, p_all, tables, gpre, gpost, sinks, w_qt, w_k, w_vt, w_g,
                       w_out, w_gate, w_ple, ret_qkv, ret_state, dec_len):
    nb, seq, d = x.shape
    tq = min(512, seq)
    nblk = tq // WINDOW
    nt = seq // tq
    n_seq = ret_state.shape[0] // (nb * nt)
    assert n_seq * nb * nt == ret_state.shape[0] and n_seq % STATE_PHASES == 0, (
        ret_state.shape, nb, nt)
    ple = p_all.shape[-1]
    cos, sin, cost, sint = tables
    kernel = functools.partial(_attn_prompt_kernel, tq=tq, n_seq=n_seq, dec_len=dec_len)
    tok = lambda b, t: (b, t, 0)
    win_blk = lambda b, t: (b, 0, 0)
    stok = lambda b, t: (b * nt + t, 0)
    consts = (gpre, gpost, w_qt, w_k, w_vt, w_g, w_out, w_gate, w_ple)
    rq, rk, rkd, rv = ret_qkv
    st_shape = (2, n_seq // STATE_PHASES) + ret_state.shape[1:]
    return pl.pallas_call(
        kernel,
        grid=(nb, nt),
        in_specs=[
            pl.BlockSpec(memory_space=pltpu.SMEM),
            pl.BlockSpec((None, tq, d), tok),
            pl.BlockSpec((None, None, tq, ple), lambda b, t: (layer, b, t, 0)),
            pl.BlockSpec((tq, LANES), lambda b, t: (t, 0)),
            pl.BlockSpec((tq, LANES), lambda b, t: (t, 0)),
            pl.BlockSpec((ATTN_HEAD_DIM // 2, tq), lambda b, t: (0, t)),
            pl.BlockSpec((ATTN_HEAD_DIM // 2, tq), lambda b, t: (0, t)),
        ] + [_const_spec(c.shape) for c in consts] + [
            pl.BlockSpec((n_seq * dec_len, RET_QK_W), stok),
            pl.BlockSpec((n_seq * dec_len, RET_QK_W), stok),
            pl.BlockSpec((n_seq * dec_len, RET_QK_W), stok),
            pl.BlockSpec((n_seq * dec_len, RET_VW), stok),
            pl.BlockSpec(memory_space=pl.ANY),
        ],
        out_specs=[
            pl.BlockSpec((None, tq, d), tok),
            pl.BlockSpec((None, ATTN_KV_DIM, WINDOW), win_blk),
            pl.BlockSpec((None, ATTN_KV_DIM, WINDOW), win_blk),
            pl.BlockSpec((n_seq * dec_len, RET_VW), stok),
            pl.BlockSpec(memory_space=pl.ANY),
        ],
        out_shape=[
            jax.ShapeDtypeStruct(x.shape, F32),
            jax.ShapeDtypeStruct((nb, ATTN_KV_DIM, WINDOW), F32),
            jax.ShapeDtypeStruct((nb, ATTN_KV_DIM, WINDOW), F32),
            jax.ShapeDtypeStruct((rq.shape[0], RET_VW), F32),
            jax.ShapeDtypeStruct(ret_state.shape, F32),
        ],
        scratch_shapes=[
            pltpu.VMEM((nblk, ATTN_Q_DIM, WINDOW), BF16),
            pltpu.VMEM((tq + WINDOW, ATTN_KV_DIM), BF16),
            pltpu.VMEM((nblk + 1, ATTN_KV_DIM, WINDOW), BF16),
            pltpu.VMEM((nblk * ATTN_KV_HEADS, 2 * WINDOW, ATTN_GROUP * WINDOW), F32),
            pltpu.VMEM((nblk * ATTN_KV_HEADS, 2 * WINDOW, ATTN_GROUP * WINDOW), BF16),
            pltpu.VMEM((nblk, ATTN_HEADS, WINDOW), F32),
            pltpu.VMEM((nblk, ATTN_Q_DIM, WINDOW), F32),
            pltpu.VMEM(st_shape, F32),
            pltpu.VMEM(st_shape, F32),
            pltpu.SemaphoreType.DMA((2, n_seq // STATE_PHASES)),
            pltpu.SemaphoreType.DMA((2, n_seq // STATE_PHASES)),
        ],
        compiler_params=pltpu.CompilerParams(
            dimension_semantics=("arbitrary", "arbitrary"),
            vmem_limit_bytes=ATTN_PROMPT_VMEM_LIMIT_BYTES),
        name="attn_prompt_layer",
    )(sinks, x, p_all, cos, sin, cost, sint, *consts, rq, rk, rkd, rv, ret_state)


def _attn_sample_kernel(sink_ref, x_ref, p_ref, cos_ref, sin_ref, cost_ref, sint_ref,
                        gpre_ref, gpost_ref, wq_ref, wkt_ref, wvt_ref, wg_ref,
                        wout_ref, wgate_ref, wple_ref, kc_ref, vc_ref,
                        xo_ref, ko_ref, vo_ref,
                        q_scr, knt_scr, vnt_scr, o_scr, *, n_seq, dec_len):
    half = ATTN_HEAD_DIM // 2
    x = x_ref[...]
    h = _rms_norm(x, gpre_ref[...]).astype(BF16)
    q_scr[...] = (_rope_attn(_dot(h, wq_ref[...]), cos_ref[...], sin_ref[...])
                  * (ATTN_HEAD_DIM ** -0.5 * LOG2_E))
    knt = _dot_nt(wkt_ref[...], h)
    cost = cost_ref[...]
    sint = sint_ref[...]
    pieces = []
    for kh in range(ATTN_KV_HEADS):
        x1 = knt[kh * ATTN_HEAD_DIM:kh * ATTN_HEAD_DIM + half, :]
        x2 = knt[kh * ATTN_HEAD_DIM + half:(kh + 1) * ATTN_HEAD_DIM, :]
        pieces.append(x1 * cost - x2 * sint)
        pieces.append(x2 * cost + x1 * sint)
    knt_scr[...] = jnp.concatenate(pieces, axis=0)
    vnt_scr[...] = _dot_nt(wvt_ref[...], h)

    rows = 2 * dec_len
    tok = lax.broadcasted_iota(jnp.int32, (rows, 2 * WINDOW), 0) % dec_len
    col = lax.broadcasted_iota(jnp.int32, (rows, 2 * WINDOW), 1)
    upper_row = lax.broadcasted_iota(jnp.int32, (rows, 1), 0) >= dec_len
    lane = lax.broadcasted_iota(jnp.int32, (ATTN_HEAD_DIM, WINDOW), 1)
    keep_old = lane < WINDOW - dec_len
    zeros = jnp.zeros((ATTN_HEAD_DIM, 2 * WINDOW), BF16)

    def select_head(t, parity):
        return jnp.concatenate([t, zeros] if parity == 0 else [zeros, t], axis=0)

    def scores(s):
        tok_rows = slice(s * dec_len, (s + 1) * dec_len)
        new_col = col - (WINDOW + s * dec_len)
        mask = ((col < WINDOW) & (col > tok)) | ((new_col >= 0) & (new_col <= tok))
        new_shift = (WINDOW - dec_len - s * dec_len) % WINDOW
        out = []
        for kh in range(ATTN_KV_HEADS):
            hrows = slice(kh * ATTN_HEAD_DIM, (kh + 1) * ATTN_HEAD_DIM)
            kc = kc_ref[s, kh]
            vc = vc_ref[s, kh]
            kn = knt_scr[hrows, :]
            vn = vnt_scr[hrows, :]
            kn_at_end = pltpu.roll(kn, new_shift, axis=1) if new_shift else kn
            vn_at_end = pltpu.roll(vn, new_shift, axis=1) if new_shift else vn
            ko_ref[s, kh] = jnp.where(keep_old, pltpu.roll(kc, WINDOW - dec_len, axis=1), kn_at_end)
            vo_ref[s, kh] = jnp.where(keep_old, pltpu.roll(vc, WINDOW - dec_len, axis=1), vn_at_end)

            kt = jnp.concatenate([kc, kn], axis=1).astype(BF16)
            vt = jnp.concatenate([vc, vn], axis=1).astype(BF16)
            qp = jnp.concatenate([q_scr[tok_rows, (2 * kh + c) * LANES:(2 * kh + c + 1) * LANES]
                                  for c in range(2)], axis=0).astype(BF16)
            sc = [jnp.where(mask, _dot(qp, select_head(kt, parity)), NEG_INF)
                  for parity in range(2)]
            out.append((sc, vt))
        return out

    def values(s, per_head):
        tok_rows = slice(s * dec_len, (s + 1) * dec_len)
        probs = []
        for kh, (scs, vt) in enumerate(per_head):
            for parity, sc in enumerate(scs):
                hd_lo, hd_hi = 4 * kh + parity, 4 * kh + 2 + parity
                sink = jnp.where(upper_row, sink_ref[hd_hi], sink_ref[hd_lo]) * LOG2_E
                m = jnp.maximum(jnp.max(sc, axis=-1, keepdims=True), sink)
                pr = jnp.exp2(sc - m)
                den = jnp.sum(pr, axis=-1, keepdims=True) + jnp.exp2(sink - m)
                probs.append((pr / den).astype(BF16))
        for kh, (scs, vt) in enumerate(per_head):
            o_pair = (_dot_nt(probs[2 * kh], select_head(vt, 0))
                      + _dot_nt(probs[2 * kh + 1], select_head(vt, 1)))
            for c in range(2):
                o_scr[tok_rows, (2 * kh + c) * LANES:(2 * kh + c + 1) * LANES] = (
                    o_pair[c * dec_len:(c + 1) * dec_len, :])

    group = min(SAMPLE_ATTN_PIPELINE_GROUP, n_seq)
    pending = [scores(s) for s in range(group)]
    for s0 in range(0, n_seq, group):
        upcoming = [scores(s) for s in range(s0 + group, min(s0 + 2 * group, n_seq))]
        for i, per_head in enumerate(pending):
            values(s0 + i, per_head)
        pending = upcoming

    g = _dot(h, wg_ref[...])
    emb = _dot(p_ref[...].astype(BF16), wple_ref[...])
    og = (o_scr[...] * _silu(g)).astype(BF16)
    _out_proj_and_residual(_row_splits(x.shape[0], 1), [og], x, emb, gpost_ref[...],
                           wout_ref, wgate_ref, xo_ref)


def _attn_sample_layer(layer, x, p_all, tables, gpre, gpost, sinks, w_q, w_kt, w_vt, w_g,
                       w_out, w_gate, w_ple, cache_kt, cache_vt, dec_len):
    n_tok, d = x.shape
    n_all = cache_kt.shape[0]
    n_seq = min(16, n_all)
    tq = n_seq * dec_len
    ple = p_all.shape[-1]
    cos, sin, cost, sint = tables
    kernel = functools.partial(_attn_sample_kernel, n_seq=n_seq, dec_len=dec_len)
    tok = lambda i: (i, 0)
    cache_shape = (n_seq, ATTN_KV_HEADS, ATTN_HEAD_DIM, WINDOW)
    cache_blk = lambda i: (i, 0, 0, 0)
    consts = (cos, sin, cost, sint, gpre, gpost, w_q, w_kt, w_vt, w_g, w_out, w_gate, w_ple)
    return pl.pallas_call(
        kernel,
        grid=(n_all // n_seq,),
        in_specs=[
            pl.BlockSpec(memory_space=pltpu.SMEM),
            pl.BlockSpec((tq, d), tok),
            pl.BlockSpec((None, tq, ple), lambda i: (layer, i, 0)),
        ] + [_const_spec(c.shape) for c in consts] + [
            pl.BlockSpec(cache_shape, cache_blk),
            pl.BlockSpec(cache_shape, cache_blk),
        ],
        out_specs=[
            pl.BlockSpec((tq, d), tok),
            pl.BlockSpec(cache_shape, cache_blk),
            pl.BlockSpec(cache_shape, cache_blk),
        ],
        out_shape=[
            jax.ShapeDtypeStruct(x.shape, F32),
            jax.ShapeDtypeStruct(cache_kt.shape, F32),
            jax.ShapeDtypeStruct(cache_vt.shape, F32),
        ],
        scratch_shapes=[
            pltpu.VMEM((tq, ATTN_Q_DIM), F32),
            pltpu.VMEM((ATTN_KV_DIM, tq), F32),
            pltpu.VMEM((ATTN_KV_DIM, tq), F32),
            pltpu.VMEM((tq, ATTN_Q_DIM), F32),
        ],
        compiler_params=pltpu.CompilerParams(
            dimension_semantics=("arbitrary",),
            vmem_limit_bytes=VMEM_LIMIT_BYTES),
        name="attn_sample_layer",
    )(sinks, x, p_all, *consts, cache_kt, cache_vt)


RET_GATE_COL0 = 2 * RET_QK_W + RET_VW


def _ret_project(h, cos, sin, win_ref, chunk_len):
    q0, k0, v0, g0 = 0, RET_QK_W, 2 * RET_QK_W, RET_GATE_COL0
    q = _rope_ret(_dot(h, win_ref[:, q0:k0]), cos, sin)
    k = _rope_ret(_dot(h, win_ref[:, k0:v0]), cos, sin) * (RET_QK_DIM ** -0.5)
    v = _dot(h, win_ref[:, v0:g0])
    n = h.shape[0]
    idx = (lax.broadcasted_iota(jnp.int32, (n, 1), 0) % chunk_len).astype(F32)
    kd = jnp.concatenate(
        [k[:, hd * RET_QK_DIM:(hd + 1) * RET_QK_DIM]
         * jnp.exp((chunk_len - 1.0 - idx) * RET_LOG_GAMMA[hd]) for hd in range(RET_HEADS)], axis=1)
    return q, k, kd, v


def _ret_decay_mask(n, chunk_len, hd):
    ri = lax.broadcasted_iota(jnp.int32, (n, n), 0)
    ci = lax.broadcasted_iota(jnp.int32, (n, n), 1)
    diff = ri - ci
    ok = (diff >= 0) & ((ri // chunk_len) == (ci // chunk_len))
    return jnp.where(ok, jnp.exp(jnp.maximum(diff, 0).astype(F32) * RET_LOG_GAMMA[hd]), 0.0)


def _ret_gated(o, g):
    outs = []
    for hd in range(RET_HEADS):
        oh = o[:, hd * RET_V_DIM:(hd + 1) * RET_V_DIM]
        mu = jnp.mean(oh, axis=-1, keepdims=True)
        var = jnp.mean(jnp.square(oh - mu), axis=-1, keepdims=True)
        outs.append((oh - mu) * lax.rsqrt(var + NORM_EPS))
    return (jnp.concatenate(outs, axis=1) * _silu(g)).astype(BF16)


def _ret_prompt_kernel(x_ref, p_ref, cos_ref, sin_ref, gpre_ref, gpost_ref,
                       win_ref, wout_ref, wgate_ref, wple_ref,
                       xo_ref, st_ref,
                       q_scr, k_scr, kd_scr, v_scr, o_scr, g_scr, dm_scr, *, tq):
    b = pl.program_id(0)
    t = pl.program_id(1)

    @pl.when((b == 0) & (t == 0))
    def _():
        for hd in range(RET_HEADS):
            dm_scr[hd] = _ret_decay_mask(RET_CHUNK, RET_CHUNK, hd)

    @pl.when(t == 0)
    def _():
        st_ref[...] = jnp.zeros(st_ref.shape, F32)

    emb = _dot(p_ref[...].astype(BF16), wple_ref[...])
    x = x_ref[...]
    h = _rms_norm(x, gpre_ref[...]).astype(BF16)
    q, k, kd, v = _ret_project(h, cos_ref[...], sin_ref[...], win_ref, RET_CHUNK)
    q_scr[...] = q.astype(BF16)
    k_scr[...] = k.astype(BF16)
    kd_scr[...] = kd.astype(BF16)
    v_scr[...] = v.astype(BF16)

    idx = lax.broadcasted_iota(jnp.int32, (RET_CHUNK, 1), 0).astype(F32)
    heads = range(RET_HEADS)
    qs = [slice(hd * RET_QK_DIM, (hd + 1) * RET_QK_DIM) for hd in heads]
    vs = [slice(hd * RET_V_DIM, (hd + 1) * RET_V_DIM) for hd in heads]
    chunks = _row_splits(tq, tq // RET_CHUNK)
    gate_cols = iter(_row_splits(RET_VW, 2 * len(chunks)))

    def gate_piece():
        c = next(gate_cols)
        g_scr[:, c] = _dot(h, win_ref[:, RET_GATE_COL0 + c.start:RET_GATE_COL0 + c.stop])

    for rows in chunks:
        scores = [_dot_nt(q_scr[rows, qs[hd]], k_scr[rows, qs[hd]]) for hd in heads]
        cross = [_dot(q_scr[rows, qs[hd]], st_ref[hd].astype(BF16)) for hd in heads]
        gate_piece()
        for hd in heads:
            st_ref[hd] = (math.exp(RET_CHUNK * RET_LOG_GAMMA[hd]) * st_ref[hd]
                          + _dot_tn(kd_scr[rows, qs[hd]], v_scr[rows, vs[hd]]))
        for hd in heads:
            inner = _dot((scores[hd] * dm_scr[hd]).astype(BF16), v_scr[rows, vs[hd]])
            o_scr[rows, vs[hd]] = inner + cross[hd] * jnp.exp((idx + 1.0) * RET_LOG_GAMMA[hd])
        gate_piece()

    ogs = [_ret_gated(o_scr[r, :], g_scr[r, :]) for r in chunks]
    _out_proj_and_residual(chunks, ogs, x, emb, gpost_ref[...], wout_ref, wgate_ref, xo_ref)


def _ret_prompt_layer(layer, x, p_all, cos, sin, gpre, gpost, w_in, w_out, w_gate, w_ple):
    nb, seq, d = x.shape
    tq = min(512, seq)
    ple = p_all.shape[-1]
    kernel = functools.partial(_ret_prompt_kernel, tq=tq)
    tok = lambda b, t: (b, t, 0)
    return pl.pallas_call(
        kernel,
        grid=(nb, seq // tq),
        in_specs=[
            pl.BlockSpec((None, tq, d), tok),
            pl.BlockSpec((None, None, tq, ple), lambda b, t: (layer, b, t, 0)),
            pl.BlockSpec((tq, LANES), lambda b, t: (t, 0)),
            pl.BlockSpec((tq, LANES), lambda b, t: (t, 0)),
            _const_spec(gpre.shape), _const_spec(gpost.shape),
            _const_spec(w_in.shape), _const_spec(w_out.shape),
            _const_spec(w_gate.shape), _const_spec(w_ple.shape),
        ],
        out_specs=[
            pl.BlockSpec((None, tq, d), tok),
            pl.BlockSpec((None, RET_HEADS, RET_QK_DIM, RET_V_DIM), lambda b, t: (b, 0, 0, 0)),
        ],
        out_shape=[
            jax.ShapeDtypeStruct(x.shape, F32),
            jax.ShapeDtypeStruct((nb, RET_HEADS, RET_QK_DIM, RET_V_DIM), F32),
        ],
        scratch_shapes=[
            pltpu.VMEM((tq, RET_QK_W), BF16),
            pltpu.VMEM((tq, RET_QK_W), BF16),
            pltpu.VMEM((tq, RET_QK_W), BF16),
            pltpu.VMEM((tq, RET_VW), BF16),
            pltpu.VMEM((tq, RET_VW), F32),
            pltpu.VMEM((tq, RET_VW), F32),
            pltpu.VMEM((RET_HEADS, RET_CHUNK, RET_CHUNK), F32),
        ],
        compiler_params=pltpu.CompilerParams(
            dimension_semantics=("arbitrary", "arbitrary"),
            vmem_limit_bytes=VMEM_LIMIT_BYTES),
        name="ret_prompt_layer",
    )(x, p_all, cos, sin, gpre, gpost, w_in, w_out, w_gate, w_ple)


def _ret_sample_proj_kernel(x_ref, cos_ref, sin_ref, gpre_ref, win_ref,
                            q_ref, k_ref, kd_ref, v_ref, g_ref, *, dec_len):
    h = _rms_norm(x_ref[...], gpre_ref[...]).astype(BF16)
    q, k, kd, v = _ret_project(h, cos_ref[...], sin_ref[...], win_ref, dec_len)
    q_ref[...] = q
    k_ref[...] = k
    kd_ref[...] = kd
    v_ref[...] = v
    g_ref[...] = _dot(h, win_ref[:, RET_GATE_COL0:])


def _ret_sample_proj(x, cos, sin, gpre, w_in, dec_len):
    n_tok, d = x.shape
    tq = min(512, n_tok)
    kernel = functools.partial(_ret_sample_proj_kernel, dec_len=dec_len)
    tok = lambda i: (i, 0)
    widths = (RET_QK_W, RET_QK_W, RET_QK_W, RET_VW, RET_VW)
    return pl.pallas_call(
        kernel,
        grid=(n_tok // tq,),
        in_specs=[
            pl.BlockSpec((tq, d), tok),
            _const_spec((tq, LANES)), _const_spec((tq, LANES)),
            _const_spec(gpre.shape), _const_spec(w_in.shape),
        ],
        out_specs=[pl.BlockSpec((tq, w), tok) for w in widths],
        out_shape=[jax.ShapeDtypeStruct((n_tok, w), F32) for w in widths],
        compiler_params=pltpu.CompilerParams(
            dimension_semantics=("arbitrary",),
            vmem_limit_bytes=VMEM_LIMIT_BYTES),
        name="ret_sample_proj",
    )(x, cos, sin, gpre, w_in)


def _ret_state_update_tile(q_ref, k_ref, kd_ref, v_ref, st_in, st_out, o_ref, *,
                           first_seq, n_seq, dec_len):
    pad_rows = 2 * dec_len
    idx = lax.broadcasted_iota(jnp.int32, (pad_rows, 1), 0).astype(F32)

    def tok_rows(s):
        return slice((first_seq + s) * dec_len, (first_seq + s + 1) * dec_len)

    def padded(ref, s, cols):
        val = ref[tok_rows(s), cols]
        return jnp.concatenate([val, jnp.zeros_like(val)], axis=0).astype(BF16)

    units = [(s, hd) for s in range(n_seq) for hd in range(RET_HEADS)]
    qs = [slice(hd * RET_QK_DIM, (hd + 1) * RET_QK_DIM) for hd in range(RET_HEADS)]
    vs = [slice(hd * RET_V_DIM, (hd + 1) * RET_V_DIM) for hd in range(RET_HEADS)]
    qh = {(s, hd): padded(q_ref, s, qs[hd]) for s, hd in units}
    vh = {(s, hd): padded(v_ref, s, vs[hd]) for s, hd in units}
    scores = {(s, hd): _dot_nt(qh[s, hd], padded(k_ref, s, qs[hd])) for s, hd in units}
    cross = {(s, hd): _dot(qh[s, hd], st_in[s, hd].astype(BF16)) for s, hd in units}
    for s, hd in units:
        st_out[s, hd] = (math.exp(dec_len * RET_LOG_GAMMA[hd]) * st_in[s, hd]
                         + _dot_tn(padded(kd_ref, s, qs[hd]), vh[s, hd]))
    masks = [_ret_decay_mask(pad_rows, dec_len, hd) for hd in range(RET_HEADS)]
    for s, hd in units:
        inner = _dot((scores[s, hd] * masks[hd]).astype(BF16), vh[s, hd])
        out = inner + cross[s, hd] * jnp.exp((idx + 1.0) * RET_LOG_GAMMA[hd])
        o_ref[tok_rows(s), vs[hd]] = out[0:dec_len, :]


def _ret_sample_finish_kernel(o_ref, g_ref, x_ref, p_ref, gpost_ref, wout_ref, wgate_ref, wple_ref,
                              xo_ref):
    emb = _dot(p_ref[...].astype(BF16), wple_ref[...])
    x = x_ref[...]
    rows = _row_splits(x.shape[0], 2)
    ogs = [_ret_gated(o_ref[r, :], g_ref[r, :]) for r in rows]
    _out_proj_and_residual(rows, ogs, x, emb, gpost_ref[...], wout_ref, wgate_ref, xo_ref)


def _ret_sample_finish(layer, o, g, x, p_all, gpost, w_out, w_gate, w_ple):
    n_tok, d = x.shape
    tq = min(512, n_tok)
    ple = p_all.shape[-1]
    tok = lambda i: (i, 0)
    return pl.pallas_call(
        _ret_sample_finish_kernel,
        grid=(n_tok // tq,),
        in_specs=[
            pl.BlockSpec((tq, RET_VW), tok), pl.BlockSpec((tq, RET_VW), tok),
            pl.BlockSpec((tq, d), tok),
            pl.BlockSpec((None, tq, ple), lambda i: (layer, i, 0)),
            _const_spec(gpost.shape), _const_spec(w_out.shape),
            _const_spec(w_gate.shape), _const_spec(w_ple.shape),
        ],
        out_specs=pl.BlockSpec((tq, d), tok),
        out_shape=jax.ShapeDtypeStruct(x.shape, F32),
        compiler_params=pltpu.CompilerParams(
            dimension_semantics=("arbitrary",),
            vmem_limit_bytes=VMEM_LIMIT_BYTES),
        name="ret_sample_finish",
    )(o, g, x, p_all, gpost, w_out, w_gate, w_ple)


def _rope_tables(pos, head_dim):
    half = head_dim // 2
    inv = ROPE_THETA ** (-np.arange(half, dtype=np.float64) / half)
    ang = np.asarray(pos, np.float64)[:, None] * inv[None, :]
    return np.cos(ang).astype(np.float32), np.sin(ang).astype(np.float32)


def _attn_tables(pos):
    cos, sin = _rope_tables(pos, ATTN_HEAD_DIM)
    reps = LANES // ATTN_HEAD_DIM
    return (np.tile(np.concatenate([cos, cos], axis=1), (1, reps)),
            np.tile(np.concatenate([-sin, sin], axis=1), (1, reps)))


def kernel(x_prompt, x_sample, cache_k_win, cache_v_win, state_ret, p_prompt, p_sample, pre_norm,
           post_norm, w_in_attn, attn_sinks, w_out_attn, w_in_ret, w_out_ret, w_ple, w_ple_gate):
    depth = p_prompt.shape[0]
    nb, seq, d = x_prompt.shape
    n_dec, dec_len, _ = x_sample.shape
    n_stok = n_dec * dec_len
    assert seq % 512 == 0 or seq in (128, 256), seq

    pos_p = np.arange(seq)
    pos_s = PAST_LEN + np.arange(dec_len)
    cos_ap, sin_ap = _attn_tables(pos_p)
    cos_apt, sin_apt = (np.ascontiguousarray(a.T) for a in _rope_tables(pos_p, ATTN_HEAD_DIM))
    cos_as, sin_as = _attn_tables(pos_s)
    cos_rp, sin_rp = _rope_tables(pos_p, RET_QK_DIM)
    cos_rs, sin_rs = _rope_tables(pos_s, RET_QK_DIM)
    attn_s_tile = min(16, n_dec)
    ret_s_tile = min(512, n_stok) // dec_len
    cos_as, sin_as = (np.tile(a, (attn_s_tile, 1)) for a in (cos_as, sin_as))
    cos_ast, sin_ast = (np.ascontiguousarray(np.tile(a, (attn_s_tile, 1)).T)
                        for a in _rope_tables(pos_s, ATTN_HEAD_DIM))
    cos_rs, sin_rs = (np.tile(a, (ret_s_tile, 1)) for a in (cos_rs, sin_rs))
    (cos_ap, sin_ap, cos_apt, sin_apt, cos_as, sin_as, cos_ast, sin_ast,
     cos_rp, sin_rp, cos_rs, sin_rs) = (
        jnp.asarray(a) for a in
        (cos_ap, sin_ap, cos_apt, sin_apt, cos_as, sin_as, cos_ast, sin_ast,
         cos_rp, sin_rp, cos_rs, sin_rs))

    assert depth == 2, depth
    xs = x_sample.reshape(n_stok, d)
    p_s = p_sample.reshape(depth, n_stok, p_sample.shape[-1])
    bf = lambda w: w.astype(BF16)
    gpre = [pre_norm[i][None, :] for i in range(depth)]
    gpost = [post_norm[i][None, :] for i in range(depth)]
    w_gate = [bf(w_ple_gate[i]) for i in range(depth)]
    w_pl = [bf(w_ple[i]) for i in range(depth)]

    w_in = bf(w_in_attn[0])
    w_out = bf(w_out_attn[0])
    sinks = attn_sinks[0]
    k0, v0, g0 = ATTN_Q_DIM, ATTN_Q_DIM + ATTN_KV_DIM, ATTN_Q_DIM + 2 * ATTN_KV_DIM
    w_q, w_k, w_v, w_g = w_in[:, :k0], w_in[:, k0:v0], w_in[:, v0:g0], w_in[:, g0:]
    to_t = lambda a: a.transpose(0, 2, 3, 1)
    from_t = lambda a: a.transpose(0, 3, 1, 2)
    xs, knt, vnt = _attn_sample_layer(
        0, xs, p_s, (cos_as, sin_as, cos_ast, sin_ast), gpre[0], gpost[0], sinks,
        w_q, w_k.T, w_v.T, w_g, w_out, w_gate[0], w_pl[0],
        to_t(cache_k_win[0]), to_t(cache_v_win[0]), dec_len)

    w_in_r = bf(w_in_ret[0])
    w_out_r = bf(w_out_ret[0])
    rq, rk, rkd, rv, rg = _ret_sample_proj(xs, cos_rs, sin_rs, gpre[1], w_in_r, dec_len)
    xp, kpt, vpt, ro, ret_state_sample = _attn_prompt_layer(
        0, x_prompt, p_prompt, (cos_ap, sin_ap, cos_apt, sin_apt), gpre[0], gpost[0], sinks,
        w_q.T, w_k, w_v.T, w_g, w_out, w_gate[0], w_pl[0],
        (rq, rk, rkd, rv), state_ret[0], dec_len)
    head_shape = (nb, ATTN_KV_HEADS, ATTN_HEAD_DIM, WINDOW)

    xp, ret_state_prompt = _ret_prompt_layer(1, xp, p_prompt, cos_rp, sin_rp, gpre[1], gpost[1],
                                             w_in_r, w_out_r, w_gate[1], w_pl[1])
    xs = _ret_sample_finish(1, ro, rg, xs, p_s, gpost[1], w_out_r, w_gate[1], w_pl[1])
    return (xp, xs.reshape(n_dec, dec_len, d),
            from_t(kpt.reshape(head_shape))[None], from_t(vpt.reshape(head_shape))[None],
            from_t(knt)[None], from_t(vnt)[None], ret_state_prompt[None], ret_state_sample[None])
```

```python
import functools
import math

import numpy as np
import jax
import jax.numpy as jnp
from jax import lax
from jax.experimental import pallas as pl
from jax.experimental.pallas import tpu as pltpu

F32 = jnp.float32
BF16 = jnp.bfloat16

PAST_LEN = 16384
ROPE_THETA = 10000.0
NORM_EPS = 1e-6
NEG_INF = -1e30
WINDOW = 128
ATTN_HEADS = 16
ATTN_KV_HEADS = 4
ATTN_GROUP = ATTN_HEADS // ATTN_KV_HEADS
ATTN_HEAD_DIM = 64
ATTN_Q_DIM = ATTN_HEADS * ATTN_HEAD_DIM
ATTN_KV_DIM = ATTN_KV_HEADS * ATTN_HEAD_DIM
RET_HEADS = 4
RET_QK_DIM = 256
RET_V_DIM = 512
RET_QK_W = RET_HEADS * RET_QK_DIM
RET_VW = RET_HEADS * RET_V_DIM
RET_CHUNK = 256
LANES = 128

LOG2_E = math.log2(math.e)
RET_LOG_GAMMA = tuple(math.log1p(-(2.0 ** (-5.0 - h))) for h in range(RET_HEADS))

VMEM_LIMIT_BYTES = 56 * 1024 * 1024
ATTN_PROMPT_VMEM_LIMIT_BYTES = 50 * 1024 * 1024
STATE_PHASES = 4
GATE_PIECES = 4
SAMPLE_ATTN_PIPELINE_GROUP = 8


def _dot(a, b):
    return jnp.dot(a, b, preferred_element_type=F32)


def _dot_nt(a, b):
    return lax.dot_general(a, b, (((1,), (1,)), ((), ())), preferred_element_type=F32)


def _dot_tn(a, b):
    return lax.dot_general(a, b, (((0,), (0,)), ((), ())), preferred_element_type=F32)


def _rms_norm(x, g):
    return x * lax.rsqrt(jnp.mean(x * x, axis=-1, keepdims=True) + NORM_EPS) * g


def _silu(g):
    return g * jax.nn.sigmoid(g)


def _row_splits(n, parts):
    step = n // parts
    return [slice(i * step, (i + 1) * step) for i in range(parts)]


def _out_proj_and_residual(rows, ogs, x, emb, gpost, wout_ref, wgate_ref, xo_ref):
    ys = [_dot(og, wout_ref[...]) for og in ogs]
    x1s = [x[r, :] + _rms_norm(y, gpost) for r, y in zip(rows, ys)]
    gates = [_dot(x1.astype(BF16), wgate_ref[...]) for x1 in x1s]
    for r, x1, gate in zip(rows, x1s, gates):
        xo_ref[r, :] = x1 + jax.nn.sigmoid(gate) * emb[r, :]


def _rope_attn(x, cos, sin_signed):
    lane = lax.broadcasted_iota(jnp.int32, (1, LANES), 1)
    first_half = (lane % ATTN_HEAD_DIM) < (ATTN_HEAD_DIM // 2)
    outs = []
    for j in range(x.shape[1] // LANES):
        xj = x[:, j * LANES:(j + 1) * LANES]
        fwd = pltpu.roll(xj, LANES - ATTN_HEAD_DIM // 2, axis=1)
        bwd = pltpu.roll(xj, ATTN_HEAD_DIM // 2, axis=1)
        outs.append(xj * cos + jnp.where(first_half, fwd, bwd) * sin_signed)
    return jnp.concatenate(outs, axis=1)


def _rope_ret(x, cos, sin):
    half = RET_QK_DIM // 2
    outs = []
    for h in range(x.shape[1] // RET_QK_DIM):
        x1 = x[:, h * RET_QK_DIM:h * RET_QK_DIM + half]
        x2 = x[:, h * RET_QK_DIM + half:(h + 1) * RET_QK_DIM]
        outs.append(x1 * cos - x2 * sin)
        outs.append(x2 * cos + x1 * sin)
    return jnp.concatenate(outs, axis=1)


def _attn_prompt_kernel(sink_ref, x_ref, p_ref, cos_ref, sin_ref, cost_ref, sint_ref,
                        gpre_ref, gpost_ref, wqt_ref, wk_ref, wvt_ref, wg_ref,
                        wout_ref, wgate_ref, wple_ref,
                        rq_ref, rk_ref, rkd_ref, rv_ref, rstate_hbm,
                        xo_ref, kw_ref, vw_ref, ro_ref, rstate_out_hbm,
                        qt_scr, kext_scr, vt_scr, st_scr, pt_scr, inv_scr, ot_scr, g_scr,
                        rst_in, rst_out, sem_in, sem_out, *, tq, n_seq, dec_len):
    nblk = tq // WINDOW
    half = ATTN_HEAD_DIM // 2
    t = pl.program_id(1)
    nt = pl.num_programs(1)

    @pl.when(t == 0)
    def _():
        kext_scr[0:WINDOW, :] = jnp.zeros((WINDOW, ATTN_KV_DIM), BF16)
        vt_scr[0] = jnp.zeros((ATTN_KV_DIM, WINDOW), BF16)

    per_phase = n_seq // STATE_PHASES
    step = pl.program_id(0) * nt + t
    n_steps = pl.num_programs(0) * nt
    group0 = step * STATE_PHASES
    last_group = n_steps * STATE_PHASES - 1

    def load_copy(phase, i):
        slot = phase % 2
        seq = jnp.minimum(group0 + phase, last_group) * per_phase + i
        return pltpu.make_async_copy(rstate_hbm.at[seq], rst_in.at[slot, i], sem_in.at[slot, i])

    def store_copy(phase, i):
        slot = phase % 2
        seq = jnp.maximum(group0 + phase, 0) * per_phase + i
        return pltpu.make_async_copy(rst_out.at[slot, i], rstate_out_hbm.at[seq],
                                     sem_out.at[slot, i])

    @pl.when(step == 0)
    def _():
        rst_out[...] = jnp.zeros(rst_out.shape, F32)
        for i in range(per_phase):
            load_copy(0, i).start()
            store_copy(-2, i).start()

    def state_phase(phase):
        for i in range(per_phase):
            load_copy(phase, i).wait()
            store_copy(phase - 2, i).wait()
        for i in range(per_phase):
            store_copy(phase - 1, i).start()
            load_copy(phase + 1, i).start()
        slot = phase % 2
        _ret_state_update_tile(rq_ref, rk_ref, rkd_ref, rv_ref, rst_in.at[slot], rst_out.at[slot],
                               ro_ref, first_seq=phase * per_phase, n_seq=per_phase,
                               dec_len=dec_len)

    state_phase(0)

    emb = _dot(p_ref[...].astype(BF16), wple_ref[...])
    x = x_ref[...]
    h = _rms_norm(x, gpre_ref[...]).astype(BF16)

    k = _rope_attn(_dot(h, wk_ref[...]), cos_ref[...], sin_ref[...])
    kext_scr[WINDOW:, :] = k.astype(BF16)

    qt = _dot_nt(wqt_ref[...], h)
    cost = cost_ref[...]
    sint = sint_ref[...]
    pieces = []
    for hd in range(ATTN_HEADS):
        x1 = qt[hd * ATTN_HEAD_DIM:hd * ATTN_HEAD_DIM + half, :]
        x2 = qt[hd * ATTN_HEAD_DIM + half:(hd + 1) * ATTN_HEAD_DIM, :]
        pieces.append(x1 * cost - x2 * sint)
        pieces.append(x2 * cost + x1 * sint)
    qrot = (jnp.concatenate(pieces, axis=0) * (ATTN_HEAD_DIM ** -0.5 * LOG2_E)).astype(BF16)
    for jb in range(nblk):
        qt_scr[jb] = qrot[:, jb * WINDOW:(jb + 1) * WINDOW]

    vt = _dot_nt(wvt_ref[...], h)
    vtb = vt.astype(BF16)
    for jb in range(nblk):
        vt_scr[jb + 1] = vtb[:, jb * WINDOW:(jb + 1) * WINDOW]

    @pl.when(t == nt - 1)
    def _():
        kw_ref[...] = k[tq - WINDOW:, :].T
        vw_ref[...] = vt[:, tq - WINDOW:]

    state_phase(1)
    zero_rows = jnp.zeros((ATTN_HEAD_DIM, ATTN_GROUP * WINDOW), BF16)
    groups = [(jb, kh) for jb in range(nblk) for kh in range(ATTN_KV_HEADS)]

    for gidx, (jb, kh) in enumerate(groups):
        heads = [kh * ATTN_GROUP + gi for gi in range(ATTN_GROUP)]
        pair = kh // 2
        kb = kext_scr[jb * WINDOW:(jb + 2) * WINDOW, pair * LANES:(pair + 1) * LANES]
        qg = jnp.concatenate(
            [qt_scr[jb, hd * ATTN_HEAD_DIM:(hd + 1) * ATTN_HEAD_DIM, :] for hd in heads], axis=1)
        qg = jnp.concatenate([qg, zero_rows] if kh % 2 == 0 else [zero_rows, qg], axis=0)
        st_scr[gidx] = _dot(kb, qg)

    state_phase(2)

    key_r = lax.broadcasted_iota(jnp.int32, (WINDOW, WINDOW), 0)
    query_c = lax.broadcasted_iota(jnp.int32, (WINDOW, WINDOW), 1)
    from_prev = key_r > query_c
    prev_w = from_prev.astype(BF16)
    cur_w = 1.0 - prev_w
    gate_cols = _row_splits(ATTN_Q_DIM, GATE_PIECES)
    for gidx, (jb, kh) in enumerate(groups):
        if gidx % (len(groups) // GATE_PIECES) == 0:
            c = gate_cols[gidx // (len(groups) // GATE_PIECES)]
            g_scr[:, c] = _dot(h, wg_ref[:, c])
        for gi in range(ATTN_GROUP):
            hd = kh * ATTN_GROUP + gi
            cols = slice(gi * WINDOW, (gi + 1) * WINDOW)
            s_prev = st_scr[gidx, 0:WINDOW, cols]
            if jb == 0:
                s_prev = jnp.where(t > 0, s_prev, NEG_INF)
            s = jnp.where(from_prev, s_prev, st_scr[gidx, WINDOW:, cols])
            sink = sink_ref[hd] * LOG2_E
            m = jnp.maximum(jnp.max(s, axis=0, keepdims=True), sink)
            pr = jnp.exp2(s - m)
            den = jnp.sum(pr, axis=0, keepdims=True) + jnp.exp2(sink - m)
            prb = pr.astype(BF16)
            pt_scr[gidx, 0:WINDOW, cols] = prb * prev_w
            pt_scr[gidx, WINDOW:, cols] = prb * cur_w
            inv_scr[jb, hd:hd + 1, :] = 1.0 / den

    for gidx, (jb, kh) in enumerate(groups):
        vband = jnp.concatenate([vt_scr[jb, kh * ATTN_HEAD_DIM:(kh + 1) * ATTN_HEAD_DIM, :],
                                 vt_scr[jb + 1, kh * ATTN_HEAD_DIM:(kh + 1) * ATTN_HEAD_DIM, :]],
                                axis=1)
        otg = _dot(vband, pt_scr[gidx])
        for gi in range(ATTN_GROUP):
            hd = kh * ATTN_GROUP + gi
            ot_scr[jb, hd * ATTN_HEAD_DIM:(hd + 1) * ATTN_HEAD_DIM, :] = (
                otg[:, gi * WINDOW:(gi + 1) * WINDOW] * inv_scr[jb, hd:hd + 1, :])

    state_phase(3)

    parts = 2 if nblk % 2 == 0 else 1
    rows = _row_splits(tq, parts)
    ogs = []
    for r in rows:
        o = jnp.concatenate([ot_scr[jb].T for jb in range(r.start // WINDOW, r.stop // WINDOW)],
                            axis=0)
        ogs.append((o * _silu(g_scr[r, :])).astype(BF16))
    _out_proj_and_residual(rows, ogs, x, emb, gpost_ref[...], wout_ref, wgate_ref, xo_ref)

    kext_scr[0:WINDOW, :] = kext_scr[tq:tq + WINDOW, :]
    vt_scr[0] = vt_scr[nblk]

    @pl.when(step == n_steps - 1)
    def _():
        for i in range(per_phase):
            store_copy(STATE_PHASES - 1, i).start()
        for i in range(per_phase):
            load_copy(STATE_PHASES, i).wait()
            store_copy(STATE_PHASES - 2, i).wait()
            store_copy(STATE_PHASES - 1, i).wait()


def _const_spec(shape):
    nd = len(shape)
    return pl.BlockSpec(shape, lambda *_: (0,) * nd)


def _attn_prompt_layer(layer, x, p_all, tables, gpre, gpost, sinks, w_qt, w_k, w_vt, w_g,
                       w_out, w_gate, w_ple, ret_qkv, ret_state, dec_len):
    nb, seq, d = x.shape
    tq = min(512, seq)
    nblk = tq // WINDOW
    nt = seq // tq
    n_seq = ret_state.shape[0] // (nb * nt)
    assert n_seq * nb * nt == ret_state.shape[0] and n_seq % STATE_PHASES == 0, (
        ret_state.shape, nb, nt)
    ple = p_all.shape[-1]
    cos, sin, cost, sint = tables
    kernel = functools.partial(_attn_prompt_kernel, tq=tq, n_seq=n_seq, dec_len=dec_len)
    tok = lambda b, t: (b, t, 0)
    win_blk = lambda b, t: (b, 0, 0)
    stok = lambda b, t: (b * nt + t, 0)
    consts = (gpre, gpost, w_qt, w_k, w_vt, w_g, w_out, w_gate, w_ple)
    rq, rk, rkd, rv = ret_qkv
    st_shape = (2, n_seq // STATE_PHASES) + ret_state.shape[1:]
    return pl.pallas_call(
        kernel,
        grid=(nb, nt),
        in_specs=[
            pl.BlockSpec(memory_space=pltpu.SMEM),
            pl.BlockSpec((None, tq, d), tok),
            pl.BlockSpec((None, None, tq, ple), lambda b, t: (layer, b, t, 0)),
            pl.BlockSpec((tq, LANES), lambda b, t: (t, 0)),
            pl.BlockSpec((tq, LANES), lambda b, t: (t, 0)),
            pl.BlockSpec((ATTN_HEAD_DIM // 2, tq), lambda b, t: (0, t)),
            pl.BlockSpec((ATTN_HEAD_DIM // 2, tq), lambda b, t: (0, t)),
        ] + [_const_spec(c.shape) for c in consts] + [
            pl.BlockSpec((n_seq * dec_len, RET_QK_W), stok),
            pl.BlockSpec((n_seq * dec_len, RET_QK_W), stok),
            pl.BlockSpec((n_seq * dec_len, RET_QK_W), stok),
            pl.BlockSpec((n_seq * dec_len, RET_VW), stok),
            pl.BlockSpec(memory_space=pl.ANY),
        ],
        out_specs=[
            pl.BlockSpec((None, tq, d), tok),
            pl.BlockSpec((None, ATTN_KV_DIM, WINDOW), win_blk),
            pl.BlockSpec((None, ATTN_KV_DIM, WINDOW), win_blk),
            pl.BlockSpec((n_seq * dec_len, RET_VW), stok),
            pl.BlockSpec(memory_space=pl.ANY),
        ],
        out_shape=[
            jax.ShapeDtypeStruct(x.shape, F32),
            jax.ShapeDtypeStruct((nb, ATTN_KV_DIM, WINDOW), F32),
            jax.ShapeDtypeStruct((nb, ATTN_KV_DIM, WINDOW), F32),
            jax.ShapeDtypeStruct((rq.shape[0], RET_VW), F32),
            jax.ShapeDtypeStruct(ret_state.shape, F32),
        ],
        scratch_shapes=[
            pltpu.VMEM((nblk, ATTN_Q_DIM, WINDOW), BF16),
            pltpu.VMEM((tq + WINDOW, ATTN_KV_DIM), BF16),
            pltpu.VMEM((nblk + 1, ATTN_KV_DIM, WINDOW), BF16),
            pltpu.VMEM((nblk * ATTN_KV_HEADS, 2 * WINDOW, ATTN_GROUP * WINDOW), F32),
            pltpu.VMEM((nblk * ATTN_KV_HEADS, 2 * WINDOW, ATTN_GROUP * WINDOW), BF16),
            pltpu.VMEM((nblk, ATTN_HEADS, WINDOW), F32),
            pltpu.VMEM((nblk, ATTN_Q_DIM, WINDOW), F32),
            pltpu.VMEM((tq, ATTN_Q_DIM), F32),
            pltpu.VMEM(st_shape, F32),
            pltpu.VMEM(st_shape, F32),
            pltpu.SemaphoreType.DMA((2, n_seq // STATE_PHASES)),
            pltpu.SemaphoreType.DMA((2, n_seq // STATE_PHASES)),
        ],
        compiler_params=pltpu.CompilerParams(
            dimension_semantics=("arbitrary", "arbitrary"),
            vmem_limit_bytes=ATTN_PROMPT_VMEM_LIMIT_BYTES),
        name="attn_prompt_layer",
    )(sinks, x, p_all, cos, sin, cost, sint, *consts, rq, rk, rkd, rv, ret_state)


def _attn_sample_kernel(sink_ref, x_ref, p_ref, cos_ref, sin_ref, cost_ref, sint_ref,
                        gpre_ref, gpost_ref, wq_ref, wkt_ref, wvt_ref, wg_ref,
                        wout_ref, wgate_ref, wple_ref, kc_ref, vc_ref,
                        xo_ref, ko_ref, vo_ref,
                        q_scr, knt_scr, vnt_scr, o_scr, *, n_seq, dec_len):
    half = ATTN_HEAD_DIM // 2
    x = x_ref[...]
    h = _rms_norm(x, gpre_ref[...]).astype(BF16)
    q_scr[...] = (_rope_attn(_dot(h, wq_ref[...]), cos_ref[...], sin_ref[...])
                  * (ATTN_HEAD_DIM ** -0.5 * LOG2_E))
    knt = _dot_nt(wkt_ref[...], h)
    cost = cost_ref[...]
    sint = sint_ref[...]
    pieces = []
    for kh in range(ATTN_KV_HEADS):
        x1 = knt[kh * ATTN_HEAD_DIM:kh * ATTN_HEAD_DIM + half, :]
        x2 = knt[kh * ATTN_HEAD_DIM + half:(kh + 1) * ATTN_HEAD_DIM, :]
        pieces.append(x1 * cost - x2 * sint)
        pieces.append(x2 * cost + x1 * sint)
    knt_scr[...] = jnp.concatenate(pieces, axis=0)
    vnt_scr[...] = _dot_nt(wvt_ref[...], h)

    rows = 2 * dec_len
    tok = lax.broadcasted_iota(jnp.int32, (rows, 2 * WINDOW), 0) % dec_len
    col = lax.broadcasted_iota(jnp.int32, (rows, 2 * WINDOW), 1)
    upper_row = lax.broadcasted_iota(jnp.int32, (rows, 1), 0) >= dec_len
    lane = lax.broadcasted_iota(jnp.int32, (ATTN_HEAD_DIM, WINDOW), 1)
    keep_old = lane < WINDOW - dec_len
    zeros = jnp.zeros((ATTN_HEAD_DIM, 2 * WINDOW), BF16)

    def select_head(t, parity):
        return jnp.concatenate([t, zeros] if parity == 0 else [zeros, t], axis=0)

    def scores(s):
        tok_rows = slice(s * dec_len, (s + 1) * dec_len)
        new_col = col - (WINDOW + s * dec_len)
        mask = ((col < WINDOW) & (col > tok)) | ((new_col >= 0) & (new_col <= tok))
        new_shift = (WINDOW - dec_len - s * dec_len) % WINDOW
        out = []
        for kh in range(ATTN_KV_HEADS):
            hrows = slice(kh * ATTN_HEAD_DIM, (kh + 1) * ATTN_HEAD_DIM)
            kc = kc_ref[s, kh]
            vc = vc_ref[s, kh]
            kn = knt_scr[hrows, :]
            vn = vnt_scr[hrows, :]
            kn_at_end = pltpu.roll(kn, new_shift, axis=1) if new_shift else kn
            vn_at_end = pltpu.roll(vn, new_shift, axis=1) if new_shift else vn
            ko_ref[s, kh] = jnp.where(keep_old, pltpu.roll(kc, WINDOW - dec_len, axis=1), kn_at_end)
            vo_ref[s, kh] = jnp.where(keep_old, pltpu.roll(vc, WINDOW - dec_len, axis=1), vn_at_end)

            kt = jnp.concatenate([kc, kn], axis=1).astype(BF16)
            vt = jnp.concatenate([vc, vn], axis=1).astype(BF16)
            qp = jnp.concatenate([q_scr[tok_rows, (2 * kh + c) * LANES:(2 * kh + c + 1) * LANES]
                                  for c in range(2)], axis=0).astype(BF16)
            sc = [jnp.where(mask, _dot(qp, select_head(kt, parity)), NEG_INF)
                  for parity in range(2)]
            out.append((sc, vt))
        return out

    def values(s, per_head):
        tok_rows = slice(s * dec_len, (s + 1) * dec_len)
        probs = []
        for kh, (scs, vt) in enumerate(per_head):
            for parity, sc in enumerate(scs):
                hd_lo, hd_hi = 4 * kh + parity, 4 * kh + 2 + parity
                sink = jnp.where(upper_row, sink_ref[hd_hi], sink_ref[hd_lo]) * LOG2_E
                m = jnp.maximum(jnp.max(sc, axis=-1, keepdims=True), sink)
                pr = jnp.exp2(sc - m)
                den = jnp.sum(pr, axis=-1, keepdims=True) + jnp.exp2(sink - m)
                probs.append((pr / den).astype(BF16))
        for kh, (scs, vt) in enumerate(per_head):
            o_pair = (_dot_nt(probs[2 * kh], select_head(vt, 0))
                      + _dot_nt(probs[2 * kh + 1], select_head(vt, 1)))
            for c in range(2):
                o_scr[tok_rows, (2 * kh + c) * LANES:(2 * kh + c + 1) * LANES] = (
                    o_pair[c * dec_len:(c + 1) * dec_len, :])

    group = min(SAMPLE_ATTN_PIPELINE_GROUP, n_seq)
    pending = [scores(s) for s in range(group)]
    for s0 in range(0, n_seq, group):
        upcoming = [scores(s) for s in range(s0 + group, min(s0 + 2 * group, n_seq))]
        for i, per_head in enumerate(pending):
            values(s0 + i, per_head)
        pending = upcoming

    g = _dot(h, wg_ref[...])
    emb = _dot(p_ref[...].astype(BF16), wple_ref[...])
    og = (o_scr[...] * _silu(g)).astype(BF16)
    _out_proj_and_residual(_row_splits(x.shape[0], 1), [og], x, emb, gpost_ref[...],
                           wout_ref, wgate_ref, xo_ref)


def _attn_sample_layer(layer, x, p_all, tables, gpre, gpost, sinks, w_q, w_kt, w_vt, w_g,
                       w_out, w_gate, w_ple, cache_kt, cache_vt, dec_len):
    n_tok, d = x.shape
    n_all = cache_kt.shape[0]
    n_seq = min(16, n_all)
    tq = n_seq * dec_len
    ple = p_all.shape[-1]
    cos, sin, cost, sint = tables
    kernel = functools.partial(_attn_sample_kernel, n_seq=n_seq, dec_len=dec_len)
    tok = lambda i: (i, 0)
    cache_shape = (n_seq, ATTN_KV_HEADS, ATTN_HEAD_DIM, WINDOW)
    cache_blk = lambda i: (i, 0, 0, 0)
    consts = (cos, sin, cost, sint, gpre, gpost, w_q, w_kt, w_vt, w_g, w_out, w_gate, w_ple)
    return pl.pallas_call(
        kernel,
        grid=(n_all // n_seq,),
        in_specs=[
            pl.BlockSpec(memory_space=pltpu.SMEM),
            pl.BlockSpec((tq, d), tok),
            pl.BlockSpec((None, tq, ple), lambda i: (layer, i, 0)),
        ] + [_const_spec(c.shape) for c in consts] + [
            pl.BlockSpec(cache_shape, cache_blk),
            pl.BlockSpec(cache_shape, cache_blk),
        ],
        out_specs=[
            pl.BlockSpec((tq, d), tok),
            pl.BlockSpec(cache_shape, cache_blk),
            pl.BlockSpec(cache_shape, cache_blk),
        ],
        out_shape=[
            jax.ShapeDtypeStruct(x.shape, F32),
            jax.ShapeDtypeStruct(cache_kt.shape, F32),
            jax.ShapeDtypeStruct(cache_vt.shape, F32),
        ],
        scratch_shapes=[
            pltpu.VMEM((tq, ATTN_Q_DIM), F32),
            pltpu.VMEM((ATTN_KV_DIM, tq), F32),
            pltpu.VMEM((ATTN_KV_DIM, tq), F32),
            pltpu.VMEM((tq, ATTN_Q_DIM), F32),
        ],
        compiler_params=pltpu.CompilerParams(
            dimension_semantics=("arbitrary",),
            vmem_limit_bytes=VMEM_LIMIT_BYTES),
        name="attn_sample_layer",
    )(sinks, x, p_all, *consts, cache_kt, cache_vt)


RET_GATE_COL0 = 2 * RET_QK_W + RET_VW


def _ret_project(h, cos, sin, win_ref, chunk_len):
    q0, k0, v0, g0 = 0, RET_QK_W, 2 * RET_QK_W, RET_GATE_COL0
    q = _rope_ret(_dot(h, win_ref[:, q0:k0]), cos, sin)
    k = _rope_ret(_dot(h, win_ref[:, k0:v0]), cos, sin) * (RET_QK_DIM ** -0.5)
    v = _dot(h, win_ref[:, v0:g0])
    n = h.shape[0]
    idx = (lax.broadcasted_iota(jnp.int32, (n, 1), 0) % chunk_len).astype(F32)
    kd = jnp.concatenate(
        [k[:, hd * RET_QK_DIM:(hd + 1) * RET_QK_DIM]
         * jnp.exp((chunk_len - 1.0 - idx) * RET_LOG_GAMMA[hd]) for hd in range(RET_HEADS)], axis=1)
    return q, k, kd, v


def _ret_decay_mask(n, chunk_len, hd):
    ri = lax.broadcasted_iota(jnp.int32, (n, n), 0)
    ci = lax.broadcasted_iota(jnp.int32, (n, n), 1)
    diff = ri - ci
    ok = (diff >= 0) & ((ri // chunk_len) == (ci // chunk_len))
    return jnp.where(ok, jnp.exp(jnp.maximum(diff, 0).astype(F32) * RET_LOG_GAMMA[hd]), 0.0)


def _ret_gated(o, g):
    outs = []
    for hd in range(RET_HEADS):
        oh = o[:, hd * RET_V_DIM:(hd + 1) * RET_V_DIM]
        mu = jnp.mean(oh, axis=-1, keepdims=True)
        var = jnp.mean(jnp.square(oh - mu), axis=-1, keepdims=True)
        outs.append((oh - mu) * lax.rsqrt(var + NORM_EPS))
    return (jnp.concatenate(outs, axis=1) * _silu(g)).astype(BF16)


def _ret_prompt_kernel(x_ref, p_ref, cos_ref, sin_ref, gpre_ref, gpost_ref,
                       win_ref, wout_ref, wgate_ref, wple_ref,
                       xo_ref, st_ref,
                       q_scr, k_scr, kd_scr, v_scr, o_scr, g_scr, dm_scr, *, tq):
    b = pl.program_id(0)
    t = pl.program_id(1)

    @pl.when((b == 0) & (t == 0))
    def _():
        for hd in range(RET_HEADS):
            dm_scr[hd] = _ret_decay_mask(RET_CHUNK, RET_CHUNK, hd)

    @pl.when(t == 0)
    def _():
        st_ref[...] = jnp.zeros(st_ref.shape, F32)

    emb = _dot(p_ref[...].astype(BF16), wple_ref[...])
    x = x_ref[...]
    h = _rms_norm(x, gpre_ref[...]).astype(BF16)
    q, k, kd, v = _ret_project(h, cos_ref[...], sin_ref[...], win_ref, RET_CHUNK)
    q_scr[...] = q.astype(BF16)
    k_scr[...] = k.astype(BF16)
    kd_scr[...] = kd.astype(BF16)
    v_scr[...] = v.astype(BF16)

    idx = lax.broadcasted_iota(jnp.int32, (RET_CHUNK, 1), 0).astype(F32)
    heads = range(RET_HEADS)
    qs = [slice(hd * RET_QK_DIM, (hd + 1) * RET_QK_DIM) for hd in heads]
    vs = [slice(hd * RET_V_DIM, (hd + 1) * RET_V_DIM) for hd in heads]
    chunks = _row_splits(tq, tq // RET_CHUNK)
    gate_cols = iter(_row_splits(RET_VW, 2 * len(chunks)))

    def gate_piece():
        c = next(gate_cols)
        g_scr[:, c] = _dot(h, win_ref[:, RET_GATE_COL0 + c.start:RET_GATE_COL0 + c.stop])

    for rows in chunks:
        scores = [_dot_nt(q_scr[rows, qs[hd]], k_scr[rows, qs[hd]]) for hd in heads]
        cross = [_dot(q_scr[rows, qs[hd]], st_ref[hd].astype(BF16)) for hd in heads]
        gate_piece()
        for hd in heads:
            st_ref[hd] = (math.exp(RET_CHUNK * RET_LOG_GAMMA[hd]) * st_ref[hd]
                          + _dot_tn(kd_scr[rows, qs[hd]], v_scr[rows, vs[hd]]))
        for hd in heads:
            inner = _dot((scores[hd] * dm_scr[hd]).astype(BF16), v_scr[rows, vs[hd]])
            o_scr[rows, vs[hd]] = inner + cross[hd] * jnp.exp((idx + 1.0) * RET_LOG_GAMMA[hd])
        gate_piece()

    ogs = [_ret_gated(o_scr[r, :], g_scr[r, :]) for r in chunks]
    _out_proj_and_residual(chunks, ogs, x, emb, gpost_ref[...], wout_ref, wgate_ref, xo_ref)


def _ret_prompt_layer(layer, x, p_all, cos, sin, gpre, gpost, w_in, w_out, w_gate, w_ple):
    nb, seq, d = x.shape
    tq = min(512, seq)
    ple = p_all.shape[-1]
    kernel = functools.partial(_ret_prompt_kernel, tq=tq)
    tok = lambda b, t: (b, t, 0)
    return pl.pallas_call(
        kernel,
        grid=(nb, seq // tq),
        in_specs=[
            pl.BlockSpec((None, tq, d), tok),
            pl.BlockSpec((None, None, tq, ple), lambda b, t: (layer, b, t, 0)),
            pl.BlockSpec((tq, LANES), lambda b, t: (t, 0)),
            pl.BlockSpec((tq, LANES), lambda b, t: (t, 0)),
            _const_spec(gpre.shape), _const_spec(gpost.shape),
            _const_spec(w_in.shape), _const_spec(w_out.shape),
            _const_spec(w_gate.shape), _const_spec(w_ple.shape),
        ],
        out_specs=[
            pl.BlockSpec((None, tq, d), tok),
            pl.BlockSpec((None, RET_HEADS, RET_QK_DIM, RET_V_DIM), lambda b, t: (b, 0, 0, 0)),
        ],
        out_shape=[
            jax.ShapeDtypeStruct(x.shape, F32),
            jax.ShapeDtypeStruct((nb, RET_HEADS, RET_QK_DIM, RET_V_DIM), F32),
        ],
        scratch_shapes=[
            pltpu.VMEM((tq, RET_QK_W), BF16),
            pltpu.VMEM((tq, RET_QK_W), BF16),
            pltpu.VMEM((tq, RET_QK_W), BF16),
            pltpu.VMEM((tq, RET_VW), BF16),
            pltpu.VMEM((tq, RET_VW), F32),
            pltpu.VMEM((tq, RET_VW), F32),
            pltpu.VMEM((RET_HEADS, RET_CHUNK, RET_CHUNK), F32),
        ],
        compiler_params=pltpu.CompilerParams(
            dimension_semantics=("arbitrary", "arbitrary"),
            vmem_limit_bytes=VMEM_LIMIT_BYTES),
        name="ret_prompt_layer",
    )(x, p_all, cos, sin, gpre, gpost, w_in, w_out, w_gate, w_ple)


def _ret_sample_proj_kernel(x_ref, cos_ref, sin_ref, gpre_ref, win_ref,
                            q_ref, k_ref, kd_ref, v_ref, g_ref, *, dec_len):
    h = _rms_norm(x_ref[...], gpre_ref[...]).astype(BF16)
    q, k, kd, v = _ret_project(h, cos_ref[...], sin_ref[...], win_ref, dec_len)
    q_ref[...] = q
    k_ref[...] = k
    kd_ref[...] = kd
    v_ref[...] = v
    g_ref[...] = _dot(h, win_ref[:, RET_GATE_COL0:])


def _ret_sample_proj(x, cos, sin, gpre, w_in, dec_len):
    n_tok, d = x.shape
    tq = min(512, n_tok)
    kernel = functools.partial(_ret_sample_proj_kernel, dec_len=dec_len)
    tok = lambda i: (i, 0)
    widths = (RET_QK_W, RET_QK_W, RET_QK_W, RET_VW, RET_VW)
    return pl.pallas_call(
        kernel,
        grid=(n_tok // tq,),
        in_specs=[
            pl.BlockSpec((tq, d), tok),
            _const_spec((tq, LANES)), _const_spec((tq, LANES)),
            _const_spec(gpre.shape), _const_spec(w_in.shape),
        ],
        out_specs=[pl.BlockSpec((tq, w), tok) for w in widths],
        out_shape=[jax.ShapeDtypeStruct((n_tok, w), F32) for w in widths],
        compiler_params=pltpu.CompilerParams(
            dimension_semantics=("arbitrary",),
            vmem_limit_bytes=VMEM_LIMIT_BYTES),
        name="ret_sample_proj",
    )(x, cos, sin, gpre, w_in)


def _ret_state_update_tile(q_ref, k_ref, kd_ref, v_ref, st_in, st_out, o_ref, *,
                           first_seq, n_seq, dec_len):
    pad_rows = 2 * dec_len
    idx = lax.broadcasted_iota(jnp.int32, (pad_rows, 1), 0).astype(F32)

    def tok_rows(s):
        return slice((first_seq + s) * dec_len, (first_seq + s + 1) * dec_len)

    def padded(ref, s, cols):
        val = ref[tok_rows(s), cols]
        return jnp.concatenate([val, jnp.zeros_like(val)], axis=0).astype(BF16)

    units = [(s, hd) for s in range(n_seq) for hd in range(RET_HEADS)]
    qs = [slice(hd * RET_QK_DIM, (hd + 1) * RET_QK_DIM) for hd in range(RET_HEADS)]
    vs = [slice(hd * RET_V_DIM, (hd + 1) * RET_V_DIM) for hd in range(RET_HEADS)]
    qh = {(s, hd): padded(q_ref, s, qs[hd]) for s, hd in units}
    vh = {(s, hd): padded(v_ref, s, vs[hd]) for s, hd in units}
    scores = {(s, hd): _dot_nt(qh[s, hd], padded(k_ref, s, qs[hd])) for s, hd in units}
    cross = {(s, hd): _dot(qh[s, hd], st_in[s, hd].astype(BF16)) for s, hd in units}
    for s, hd in units:
        st_out[s, hd] = (math.exp(dec_len * RET_LOG_GAMMA[hd]) * st_in[s, hd]
                         + _dot_tn(padded(kd_ref, s, qs[hd]), vh[s, hd]))
    masks = [_ret_decay_mask(pad_rows, dec_len, hd) for hd in range(RET_HEADS)]
    for s, hd in units:
        inner = _dot((scores[s, hd] * masks[hd]).astype(BF16), vh[s, hd])
        out = inner + cross[s, hd] * jnp.exp((idx + 1.0) * RET_LOG_GAMMA[hd])
        o_ref[tok_rows(s), vs[hd]] = out[0:dec_len, :]


def _ret_sample_finish_kernel(o_ref, g_ref, x_ref, p_ref, gpost_ref, wout_ref, wgate_ref, wple_ref,
                              xo_ref):
    emb = _dot(p_ref[...].astype(BF16), wple_ref[...])
    x = x_ref[...]
    rows = _row_splits(x.shape[0], 2)
    ogs = [_ret_gated(o_ref[r, :], g_ref[r, :]) for r in rows]
    _out_proj_and_residual(rows, ogs, x, emb, gpost_ref[...], wout_ref, wgate_ref, xo_ref)


def _ret_sample_finish(layer, o, g, x, p_all, gpost, w_out, w_gate, w_ple):
    n_tok, d = x.shape
    tq = min(512, n_tok)
    ple = p_all.shape[-1]
    tok = lambda i: (i, 0)
    return pl.pallas_call(
        _ret_sample_finish_kernel,
        grid=(n_tok // tq,),
        in_specs=[
            pl.BlockSpec((tq, RET_VW), tok), pl.BlockSpec((tq, RET_VW), tok),
            pl.BlockSpec((tq, d), tok),
            pl.BlockSpec((None, tq, ple), lambda i: (layer, i, 0)),
            _const_spec(gpost.shape), _const_spec(w_out.shape),
            _const_spec(w_gate.shape), _const_spec(w_ple.shape),
        ],
        out_specs=pl.BlockSpec((tq, d), tok),
        out_shape=jax.ShapeDtypeStruct(x.shape, F32),
        compiler_params=pltpu.CompilerParams(
            dimension_semantics=("arbitrary",),
            vmem_limit_bytes=VMEM_LIMIT_BYTES),
        name="ret_sample_finish",
    )(o, g, x, p_all, gpost, w_out, w_gate, w_ple)


def _rope_tables(pos, head_dim):
    half = head_dim // 2
    inv = ROPE_THETA ** (-np.arange(half, dtype=np.float64) / half)
    ang = np.asarray(pos, np.float64)[:, None] * inv[None, :]
    return np.cos(ang).astype(np.float32), np.sin(ang).astype(np.float32)


def _attn_tables(pos):
    cos, sin = _rope_tables(pos, ATTN_HEAD_DIM)
    reps = LANES // ATTN_HEAD_DIM
    return (np.tile(np.concatenate([cos, cos], axis=1), (1, reps)),
            np.tile(np.concatenate([-sin, sin], axis=1), (1, reps)))


def kernel(x_prompt, x_sample, cache_k_win, cache_v_win, state_ret, p_prompt, p_sample, pre_norm,
           post_norm, w_in_attn, attn_sinks, w_out_attn, w_in_ret, w_out_ret, w_ple, w_ple_gate):
    depth = p_prompt.shape[0]
    nb, seq, d = x_prompt.shape
    n_dec, dec_len, _ = x_sample.shape
    n_stok = n_dec * dec_len
    assert seq % 512 == 0 or seq in (128, 256), seq

    pos_p = np.arange(seq)
    pos_s = PAST_LEN + np.arange(dec_len)
    cos_ap, sin_ap = _attn_tables(pos_p)
    cos_apt, sin_apt = (np.ascontiguousarray(a.T) for a in _rope_tables(pos_p, ATTN_HEAD_DIM))
    cos_as, sin_as = _attn_tables(pos_s)
    cos_rp, sin_rp = _rope_tables(pos_p, RET_QK_DIM)
    cos_rs, sin_rs = _rope_tables(pos_s, RET_QK_DIM)
    attn_s_tile = min(16, n_dec)
    ret_s_tile = min(512, n_stok) // dec_len
    cos_as, sin_as = (np.tile(a, (attn_s_tile, 1)) for a in (cos_as, sin_as))
    cos_ast, sin_ast = (np.ascontiguousarray(np.tile(a, (attn_s_tile, 1)).T)
                        for a in _rope_tables(pos_s, ATTN_HEAD_DIM))
    cos_rs, sin_rs = (np.tile(a, (ret_s_tile, 1)) for a in (cos_rs, sin_rs))
    (cos_ap, sin_ap, cos_apt, sin_apt, cos_as, sin_as, cos_ast, sin_ast,
     cos_rp, sin_rp, cos_rs, sin_rs) = (
        jnp.asarray(a) for a in
        (cos_ap, sin_ap, cos_apt, sin_apt, cos_as, sin_as, cos_ast, sin_ast,
         cos_rp, sin_rp, cos_rs, sin_rs))

    assert depth == 2, depth
    xs = x_sample.reshape(n_stok, d)
    p_s = p_sample.reshape(depth, n_stok, p_sample.shape[-1])
    bf = lambda w: w.astype(BF16)
    gpre = [pre_norm[i][None, :] for i in range(depth)]
    gpost = [post_norm[i][None, :] for i in range(depth)]
    w_gate = [bf(w_ple_gate[i]) for i in range(depth)]
    w_pl = [bf(w_ple[i]) for i in range(depth)]

    w_in = bf(w_in_attn[0])
    w_out = bf(w_out_attn[0])
    sinks = attn_sinks[0]
    k0, v0, g0 = ATTN_Q_DIM, ATTN_Q_DIM + ATTN_KV_DIM, ATTN_Q_DIM + 2 * ATTN_KV_DIM
    w_q, w_k, w_v, w_g = w_in[:, :k0], w_in[:, k0:v0], w_in[:, v0:g0], w_in[:, g0:]
    to_t = lambda a: a.transpose(0, 2, 3, 1)
    from_t = lambda a: a.transpose(0, 3, 1, 2)
    xs, knt, vnt = _attn_sample_layer(
        0, xs, p_s, (cos_as, sin_as, cos_ast, sin_ast), gpre[0], gpost[0], sinks,
        w_q, w_k.T, w_v.T, w_g, w_out, w_gate[0], w_pl[0],
        to_t(cache_k_win[0]), to_t(cache_v_win[0]), dec_len)

    w_in_r = bf(w_in_ret[0])
    w_out_r = bf(w_out_ret[0])
    rq, rk, rkd, rv, rg = _ret_sample_proj(xs, cos_rs, sin_rs, gpre[1], w_in_r, dec_len)
    xp, kpt, vpt, ro, ret_state_sample = _attn_prompt_layer(
        0, x_prompt, p_prompt, (cos_ap, sin_ap, cos_apt, sin_apt), gpre[0], gpost[0], sinks,
        w_q.T, w_k, w_v.T, w_g, w_out, w_gate[0], w_pl[0],
        (rq, rk, rkd, rv), state_ret[0], dec_len)
    head_shape = (nb, ATTN_KV_HEADS, ATTN_HEAD_DIM, WINDOW)

    xp, ret_state_prompt = _ret_prompt_layer(1, xp, p_prompt, cos_rp, sin_rp, gpre[1], gpost[1],
                                             w_in_r, w_out_r, w_gate[1], w_pl[1])
    xs = _ret_sample_finish(1, ro, rg, xs, p_s, gpost[1], w_out_r, w_gate[1], w_pl[1])
    return (xp, xs.reshape(n_dec, dec_len, d),
            from_t(kpt.reshape(head_shape))[None], from_t(vpt.reshape(head_shape))[None],
            from_t(knt)[None], from_t(vnt)[None], ret_state_prompt[None], ret_state_sample[None])
```

```python
import functools
import math

import numpy as np
import jax
import jax.numpy as jnp
from jax import lax
from jax.experimental import pallas as pl
from jax.experimental.pallas import tpu as pltpu

F32 = jnp.float32
BF16 = jnp.bfloat16

PAST_LEN = 16384
ROPE_THETA = 10000.0
NORM_EPS = 1e-6
NEG_INF = -1e30
WINDOW = 128
ATTN_HEADS = 16
ATTN_KV_HEADS = 4
ATTN_GROUP = ATTN_HEADS // ATTN_KV_HEADS
ATTN_HEAD_DIM = 64
ATTN_Q_DIM = ATTN_HEADS * ATTN_HEAD_DIM
ATTN_KV_DIM = ATTN_KV_HEADS * ATTN_HEAD_DIM
RET_HEADS = 4
RET_QK_DIM = 256
RET_V_DIM = 512
RET_QK_W = RET_HEADS * RET_QK_DIM
RET_VW = RET_HEADS * RET_V_DIM
RET_CHUNK = 256
LANES = 128

LOG2_E = math.log2(math.e)
RET_LOG_GAMMA = tuple(math.log1p(-(2.0 ** (-5.0 - h))) for h in range(RET_HEADS))

VMEM_LIMIT_BYTES = 56 * 1024 * 1024
ATTN_PROMPT_VMEM_LIMIT_BYTES = 50 * 1024 * 1024
STATE_PHASES = 4
SAMPLE_ATTN_PIPELINE_GROUP = 8


def _dot(a, b):
    return jnp.dot(a, b, preferred_element_type=F32)


def _dot_nt(a, b):
    return lax.dot_general(a, b, (((1,), (1,)), ((), ())), preferred_element_type=F32)


def _dot_tn(a, b):
    return lax.dot_general(a, b, (((0,), (0,)), ((), ())), preferred_element_type=F32)


def _rms_norm(x, g):
    return x * lax.rsqrt(jnp.mean(x * x, axis=-1, keepdims=True) + NORM_EPS) * g


def _silu(g):
    return g * jax.nn.sigmoid(g)


def _row_splits(n, parts):
    step = n // parts
    return [slice(i * step, (i + 1) * step) for i in range(parts)]


def _out_proj_and_residual(rows, ogs, x, emb, gpost, wout_ref, wgate_ref, xo_ref):
    ys = [_dot(og, wout_ref[...]) for og in ogs]
    x1s = [x[r, :] + _rms_norm(y, gpost) for r, y in zip(rows, ys)]
    gates = [_dot(x1.astype(BF16), wgate_ref[...]) for x1 in x1s]
    for r, x1, gate in zip(rows, x1s, gates):
        xo_ref[r, :] = x1 + jax.nn.sigmoid(gate) * emb[r, :]


def _rope_attn(x, cos, sin_signed):
    lane = lax.broadcasted_iota(jnp.int32, (1, LANES), 1)
    first_half = (lane % ATTN_HEAD_DIM) < (ATTN_HEAD_DIM // 2)
    outs = []
    for j in range(x.shape[1] // LANES):
        xj = x[:, j * LANES:(j + 1) * LANES]
        fwd = pltpu.roll(xj, LANES - ATTN_HEAD_DIM // 2, axis=1)
        bwd = pltpu.roll(xj, ATTN_HEAD_DIM // 2, axis=1)
        outs.append(xj * cos + jnp.where(first_half, fwd, bwd) * sin_signed)
    return jnp.concatenate(outs, axis=1)


def _rope_ret(x, cos, sin):
    half = RET_QK_DIM // 2
    outs = []
    for h in range(x.shape[1] // RET_QK_DIM):
        x1 = x[:, h * RET_QK_DIM:h * RET_QK_DIM + half]
        x2 = x[:, h * RET_QK_DIM + half:(h + 1) * RET_QK_DIM]
        outs.append(x1 * cos - x2 * sin)
        outs.append(x2 * cos + x1 * sin)
    return jnp.concatenate(outs, axis=1)


def _attn_prompt_kernel(sink_ref, x_ref, p_ref, cos_ref, sin_ref, cost_ref, sint_ref,
                        gpre_ref, gpost_ref, wqt_ref, wk_ref, wvt_ref, wg_ref,
                        wout_ref, wgate_ref, wple_ref,
                        rq_ref, rk_ref, rkd_ref, rv_ref, rstate_hbm,
                        xo_ref, kw_ref, vw_ref, ro_ref, rstate_out_hbm,
                        qt_scr, kext_scr, vt_scr, st_scr, pt_scr, inv_scr, ot_scr,
                        rst_in, rst_out, sem_in, sem_out, *, tq, n_seq, dec_len):
    nblk = tq // WINDOW
    half = ATTN_HEAD_DIM // 2
    t = pl.program_id(1)
    nt = pl.num_programs(1)

    @pl.when(t == 0)
    def _():
        kext_scr[0:WINDOW, :] = jnp.zeros((WINDOW, ATTN_KV_DIM), BF16)
        vt_scr[0] = jnp.zeros((ATTN_KV_DIM, WINDOW), BF16)

    per_phase = n_seq // STATE_PHASES
    step = pl.program_id(0) * nt + t
    n_steps = pl.num_programs(0) * nt
    seq0 = step * n_seq
    last_seq = n_steps * n_seq - 1

    def load_copy(seq, slot, i):
        return pltpu.make_async_copy(rstate_hbm.at[seq], rst_in.at[slot, i], sem_in.at[slot, i])

    def store_copy(seq, slot, i):
        return pltpu.make_async_copy(rst_out.at[slot, i], rstate_out_hbm.at[seq],
                                     sem_out.at[slot, i])

    @pl.when(step == 0)
    def _():
        rst_out[...] = jnp.zeros(rst_out.shape, F32)
        for slot in range(2):
            for i in range(per_phase):
                load_copy(slot * per_phase + i, slot, i).start()
                store_copy(slot * per_phase + i, slot, i).start()

    def state_phase(phase):
        slot = phase % 2
        first = seq0 + phase * per_phase
        for i in range(per_phase):
            load_copy(first + i, slot, i).wait()
            store_copy(first + i, slot, i).wait()
        _ret_state_update_tile(rq_ref, rk_ref, rkd_ref, rv_ref, rst_in.at[slot], rst_out.at[slot],
                               ro_ref, first_seq=phase * per_phase, n_seq=per_phase,
                               dec_len=dec_len)
        for i in range(per_phase):
            store_copy(first + i, slot, i).start()
            load_copy(jnp.minimum(first + i + 2 * per_phase, last_seq), slot, i).start()

    state_phase(0)

    emb = _dot(p_ref[...].astype(BF16), wple_ref[...])
    x = x_ref[...]
    h = _rms_norm(x, gpre_ref[...]).astype(BF16)

    k = _rope_attn(_dot(h, wk_ref[...]), cos_ref[...], sin_ref[...])
    kext_scr[WINDOW:, :] = k.astype(BF16)

    qt = _dot_nt(wqt_ref[...], h)
    cost = cost_ref[...]
    sint = sint_ref[...]
    pieces = []
    for hd in range(ATTN_HEADS):
        x1 = qt[hd * ATTN_HEAD_DIM:hd * ATTN_HEAD_DIM + half, :]
        x2 = qt[hd * ATTN_HEAD_DIM + half:(hd + 1) * ATTN_HEAD_DIM, :]
        pieces.append(x1 * cost - x2 * sint)
        pieces.append(x2 * cost + x1 * sint)
    qrot = (jnp.concatenate(pieces, axis=0) * (ATTN_HEAD_DIM ** -0.5 * LOG2_E)).astype(BF16)
    for jb in range(nblk):
        qt_scr[jb] = qrot[:, jb * WINDOW:(jb + 1) * WINDOW]

    vt = _dot_nt(wvt_ref[...], h)
    vtb = vt.astype(BF16)
    for jb in range(nblk):
        vt_scr[jb + 1] = vtb[:, jb * WINDOW:(jb + 1) * WINDOW]

    @pl.when(t == nt - 1)
    def _():
        kw_ref[...] = k[tq - WINDOW:, :].T
        vw_ref[...] = vt[:, tq - WINDOW:]

    state_phase(1)
    zero_rows = jnp.zeros((ATTN_HEAD_DIM, ATTN_GROUP * WINDOW), BF16)
    groups = [(jb, kh) for jb in range(nblk) for kh in range(ATTN_KV_HEADS)]

    for gidx, (jb, kh) in enumerate(groups):
        heads = [kh * ATTN_GROUP + gi for gi in range(ATTN_GROUP)]
        pair = kh // 2
        kb = kext_scr[jb * WINDOW:(jb + 2) * WINDOW, pair * LANES:(pair + 1) * LANES]
        qg = jnp.concatenate(
            [qt_scr[jb, hd * ATTN_HEAD_DIM:(hd + 1) * ATTN_HEAD_DIM, :] for hd in heads], axis=1)
        qg = jnp.concatenate([qg, zero_rows] if kh % 2 == 0 else [zero_rows, qg], axis=0)
        st_scr[gidx] = _dot(kb, qg)

    state_phase(2)

    g = _dot(h, wg_ref[...])

    key_r = lax.broadcasted_iota(jnp.int32, (WINDOW, WINDOW), 0)
    query_c = lax.broadcasted_iota(jnp.int32, (WINDOW, WINDOW), 1)
    from_prev = key_r > query_c
    prev_w = from_prev.astype(BF16)
    cur_w = 1.0 - prev_w
    for gidx, (jb, kh) in enumerate(groups):
        for gi in range(ATTN_GROUP):
            hd = kh * ATTN_GROUP + gi
            cols = slice(gi * WINDOW, (gi + 1) * WINDOW)
            s_prev = st_scr[gidx, 0:WINDOW, cols]
            if jb == 0:
                s_prev = jnp.where(t > 0, s_prev, NEG_INF)
            s = jnp.where(from_prev, s_prev, st_scr[gidx, WINDOW:, cols])
            sink = sink_ref[hd] * LOG2_E
            m = jnp.maximum(jnp.max(s, axis=0, keepdims=True), sink)
            pr = jnp.exp2(s - m)
            den = jnp.sum(pr, axis=0, keepdims=True) + jnp.exp2(sink - m)
            prb = pr.astype(BF16)
            pt_scr[gidx, 0:WINDOW, cols] = prb * prev_w
            pt_scr[gidx, WINDOW:, cols] = prb * cur_w
            inv_scr[jb, hd:hd + 1, :] = 1.0 / den

    for gidx, (jb, kh) in enumerate(groups):
        vband = jnp.concatenate([vt_scr[jb, kh * ATTN_HEAD_DIM:(kh + 1) * ATTN_HEAD_DIM, :],
                                 vt_scr[jb + 1, kh * ATTN_HEAD_DIM:(kh + 1) * ATTN_HEAD_DIM, :]],
                                axis=1)
        otg = _dot(vband, pt_scr[gidx])
        for gi in range(ATTN_GROUP):
            hd = kh * ATTN_GROUP + gi
            ot_scr[jb, hd * ATTN_HEAD_DIM:(hd + 1) * ATTN_HEAD_DIM, :] = (
                otg[:, gi * WINDOW:(gi + 1) * WINDOW] * inv_scr[jb, hd:hd + 1, :])

    state_phase(3)

    parts = 2 if nblk % 2 == 0 else 1
    rows = _row_splits(tq, parts)
    ogs = []
    for r in rows:
        o = jnp.concatenate([ot_scr[jb].T for jb in range(r.start // WINDOW, r.stop // WINDOW)],
                            axis=0)
        ogs.append((o * _silu(g[r, :])).astype(BF16))
    _out_proj_and_residual(rows, ogs, x, emb, gpost_ref[...], wout_ref, wgate_ref, xo_ref)

    kext_scr[0:WINDOW, :] = kext_scr[tq:tq + WINDOW, :]
    vt_scr[0] = vt_scr[nblk]

    @pl.when(step == n_steps - 1)
    def _():
        for slot in range(2):
            for i in range(per_phase):
                load_copy(last_seq, slot, i).wait()
                store_copy(last_seq, slot, i).wait()


def _const_spec(shape):
    nd = len(shape)
    return pl.BlockSpec(shape, lambda *_: (0,) * nd)


def _attn_prompt_layer(layer, x, p_all, tables, gpre, gpost, sinks, w_qt, w_k, w_vt, w_g,
                       w_out, w_gate, w_ple, ret_qkv, ret_state, dec_len):
    nb, seq, d = x.shape
    tq = min(512, seq)
    nblk = tq // WINDOW
    nt = seq // tq
    n_seq = ret_state.shape[0] // (nb * nt)
    assert n_seq * nb * nt == ret_state.shape[0] and n_seq % STATE_PHASES == 0, (
        ret_state.shape, nb, nt)
    ple = p_all.shape[-1]
    cos, sin, cost, sint = tables
    kernel = functools.partial(_attn_prompt_kernel, tq=tq, n_seq=n_seq, dec_len=dec_len)
    tok = lambda b, t: (b, t, 0)
    win_blk = lambda b, t: (b, 0, 0)
    stok = lambda b, t: (b * nt + t, 0)
    consts = (gpre, gpost, w_qt, w_k, w_vt, w_g, w_out, w_gate, w_ple)
    rq, rk, rkd, rv = ret_qkv
    st_shape = (2, n_seq // STATE_PHASES) + ret_state.shape[1:]
    return pl.pallas_call(
        kernel,
        grid=(nb, nt),
        in_specs=[
            pl.BlockSpec(memory_space=pltpu.SMEM),
            pl.BlockSpec((None, tq, d), tok),
            pl.BlockSpec((None, None, tq, ple), lambda b, t: (layer, b, t, 0)),
            pl.BlockSpec((tq, LANES), lambda b, t: (t, 0)),
            pl.BlockSpec((tq, LANES), lambda b, t: (t, 0)),
            pl.BlockSpec((ATTN_HEAD_DIM // 2, tq), lambda b, t: (0, t)),
            pl.BlockSpec((ATTN_HEAD_DIM // 2, tq), lambda b, t: (0, t)),
        ] + [_const_spec(c.shape) for c in consts] + [
            pl.BlockSpec((n_seq * dec_len, RET_QK_W), stok),
            pl.BlockSpec((n_seq * dec_len, RET_QK_W), stok),
            pl.BlockSpec((n_seq * dec_len, RET_QK_W), stok),
            pl.BlockSpec((n_seq * dec_len, RET_VW), stok),
            pl.BlockSpec(memory_space=pl.ANY),
        ],
        out_specs=[
            pl.BlockSpec((None, tq, d), tok),
            pl.BlockSpec((None, ATTN_KV_DIM, WINDOW), win_blk),
            pl.BlockSpec((None, ATTN_KV_DIM, WINDOW), win_blk),
            pl.BlockSpec((n_seq * dec_len, RET_VW), stok),
            pl.BlockSpec(memory_space=pl.ANY),
        ],
        out_shape=[
            jax.ShapeDtypeStruct(x.shape, F32),
            jax.ShapeDtypeStruct((nb, ATTN_KV_DIM, WINDOW), F32),
            jax.ShapeDtypeStruct((nb, ATTN_KV_DIM, WINDOW), F32),
            jax.ShapeDtypeStruct((rq.shape[0], RET_VW), F32),
            jax.ShapeDtypeStruct(ret_state.shape, F32),
        ],
        scratch_shapes=[
            pltpu.VMEM((nblk, ATTN_Q_DIM, WINDOW), BF16),
            pltpu.VMEM((tq + WINDOW, ATTN_KV_DIM), BF16),
            pltpu.VMEM((nblk + 1, ATTN_KV_DIM, WINDOW), BF16),
            pltpu.VMEM((nblk * ATTN_KV_HEADS, 2 * WINDOW, ATTN_GROUP * WINDOW), F32),
            pltpu.VMEM((nblk * ATTN_KV_HEADS, 2 * WINDOW, ATTN_GROUP * WINDOW), BF16),
            pltpu.VMEM((nblk, ATTN_HEADS, WINDOW), F32),
            pltpu.VMEM((nblk, ATTN_Q_DIM, WINDOW), F32),
            pltpu.VMEM(st_shape, F32),
            pltpu.VMEM(st_shape, F32),
            pltpu.SemaphoreType.DMA((2, n_seq // STATE_PHASES)),
            pltpu.SemaphoreType.DMA((2, n_seq // STATE_PHASES)),
        ],
        compiler_params=pltpu.CompilerParams(
            dimension_semantics=("arbitrary", "arbitrary"),
            vmem_limit_bytes=ATTN_PROMPT_VMEM_LIMIT_BYTES),
        name="attn_prompt_layer",
    )(sinks, x, p_all, cos, sin, cost, sint, *consts, rq, rk, rkd, rv, ret_state)


def _attn_sample_kernel(sink_ref, x_ref, p_ref, cos_ref, sin_ref, cost_ref, sint_ref,
                        gpre_ref, gpost_ref, wq_ref, wkt_ref, wvt_ref, wg_ref,
                        wout_ref, wgate_ref, wple_ref, kc_ref, vc_ref,
                        xo_ref, ko_ref, vo_ref,
                        q_scr, knt_scr, vnt_scr, o_scr, *, n_seq, dec_len):
    half = ATTN_HEAD_DIM // 2
    x = x_ref[...]
    h = _rms_norm(x, gpre_ref[...]).astype(BF16)
    q_scr[...] = (_rope_attn(_dot(h, wq_ref[...]), cos_ref[...], sin_ref[...])
                  * (ATTN_HEAD_DIM ** -0.5 * LOG2_E))
    knt = _dot_nt(wkt_ref[...], h)
    cost = cost_ref[...]
    sint = sint_ref[...]
    pieces = []
    for kh in range(ATTN_KV_HEADS):
        x1 = knt[kh * ATTN_HEAD_DIM:kh * ATTN_HEAD_DIM + half, :]
        x2 = knt[kh * ATTN_HEAD_DIM + half:(kh + 1) * ATTN_HEAD_DIM, :]
        pieces.append(x1 * cost - x2 * sint)
        pieces.append(x2 * cost + x1 * sint)
    knt_scr[...] = jnp.concatenate(pieces, axis=0)
    vnt_scr[...] = _dot_nt(wvt_ref[...], h)

    rows = 2 * dec_len
    tok = lax.broadcasted_iota(jnp.int32, (rows, 2 * WINDOW), 0) % dec_len
    col = lax.broadcasted_iota(jnp.int32, (rows, 2 * WINDOW), 1)
    upper_row = lax.broadcasted_iota(jnp.int32, (rows, 1), 0) >= dec_len
    lane = lax.broadcasted_iota(jnp.int32, (ATTN_HEAD_DIM, WINDOW), 1)
    keep_old = lane < WINDOW - dec_len
    zeros = jnp.zeros((ATTN_HEAD_DIM, 2 * WINDOW), BF16)

    def select_head(t, parity):
        return jnp.concatenate([t, zeros] if parity == 0 else [zeros, t], axis=0)

    def scores(s):
        tok_rows = slice(s * dec_len, (s + 1) * dec_len)
        new_col = col - (WINDOW + s * dec_len)
        mask = ((col < WINDOW) & (col > tok)) | ((new_col >= 0) & (new_col <= tok))
        new_shift = (WINDOW - dec_len - s * dec_len) % WINDOW
        out = []
        for kh in range(ATTN_KV_HEADS):
            hrows = slice(kh * ATTN_HEAD_DIM, (kh + 1) * ATTN_HEAD_DIM)
            kc = kc_ref[s, kh]
            vc = vc_ref[s, kh]
            kn = knt_scr[hrows, :]
            vn = vnt_scr[hrows, :]
            kn_at_end = pltpu.roll(kn, new_shift, axis=1) if new_shift else kn
            vn_at_end = pltpu.roll(vn, new_shift, axis=1) if new_shift else vn
            ko_ref[s, kh] = jnp.where(keep_old, pltpu.roll(kc, WINDOW - dec_len, axis=1), kn_at_end)
            vo_ref[s, kh] = jnp.where(keep_old, pltpu.roll(vc, WINDOW - dec_len, axis=1), vn_at_end)

            kt = jnp.concatenate([kc, kn], axis=1).astype(BF16)
            vt = jnp.concatenate([vc, vn], axis=1).astype(BF16)
            qp = jnp.concatenate([q_scr[tok_rows, (2 * kh + c) * LANES:(2 * kh + c + 1) * LANES]
                                  for c in range(2)], axis=0).astype(BF16)
            sc = [jnp.where(mask, _dot(qp, select_head(kt, parity)), NEG_INF)
                  for parity in range(2)]
            out.append((sc, vt))
        return out

    def values(s, per_head):
        tok_rows = slice(s * dec_len, (s + 1) * dec_len)
        probs = []
        for kh, (scs, vt) in enumerate(per_head):
            for parity, sc in enumerate(scs):
                hd_lo, hd_hi = 4 * kh + parity, 4 * kh + 2 + parity
                sink = jnp.where(upper_row, sink_ref[hd_hi], sink_ref[hd_lo]) * LOG2_E
                m = jnp.maximum(jnp.max(sc, axis=-1, keepdims=True), sink)
                pr = jnp.exp2(sc - m)
                den = jnp.sum(pr, axis=-1, keepdims=True) + jnp.exp2(sink - m)
                probs.append((pr / den).astype(BF16))
        for kh, (scs, vt) in enumerate(per_head):
            o_pair = (_dot_nt(probs[2 * kh], select_head(vt, 0))
                      + _dot_nt(probs[2 * kh + 1], select_head(vt, 1)))
            for c in range(2):
                o_scr[tok_rows, (2 * kh + c) * LANES:(2 * kh + c + 1) * LANES] = (
                    o_pair[c * dec_len:(c + 1) * dec_len, :])

    group = min(SAMPLE_ATTN_PIPELINE_GROUP, n_seq)
    pending = [scores(s) for s in range(group)]
    for s0 in range(0, n_seq, group):
        upcoming = [scores(s) for s in range(s0 + group, min(s0 + 2 * group, n_seq))]
        for i, per_head in enumerate(pending):
            values(s0 + i, per_head)
        pending = upcoming

    g = _dot(h, wg_ref[...])
    emb = _dot(p_ref[...].astype(BF16), wple_ref[...])
    og = (o_scr[...] * _silu(g)).astype(BF16)
    _out_proj_and_residual(_row_splits(x.shape[0], 1), [og], x, emb, gpost_ref[...],
                           wout_ref, wgate_ref, xo_ref)


def _attn_sample_layer(layer, x, p_all, tables, gpre, gpost, sinks, w_q, w_kt, w_vt, w_g,
                       w_out, w_gate, w_ple, cache_kt, cache_vt, dec_len):
    n_tok, d = x.shape
    n_all = cache_kt.shape[0]
    n_seq = min(16, n_all)
    tq = n_seq * dec_len
    ple = p_all.shape[-1]
    cos, sin, cost, sint = tables
    kernel = functools.partial(_attn_sample_kernel, n_seq=n_seq, dec_len=dec_len)
    tok = lambda i: (i, 0)
    cache_shape = (n_seq, ATTN_KV_HEADS, ATTN_HEAD_DIM, WINDOW)
    cache_blk = lambda i: (i, 0, 0, 0)
    consts = (cos, sin, cost, sint, gpre, gpost, w_q, w_kt, w_vt, w_g, w_out, w_gate, w_ple)
    return pl.pallas_call(
        kernel,
        grid=(n_all // n_seq,),
        in_specs=[
            pl.BlockSpec(memory_space=pltpu.SMEM),
            pl.BlockSpec((tq, d), tok),
            pl.BlockSpec((None, tq, ple), lambda i: (layer, i, 0)),
        ] + [_const_spec(c.shape) for c in consts] + [
            pl.BlockSpec(cache_shape, cache_blk),
            pl.BlockSpec(cache_shape, cache_blk),
        ],
        out_specs=[
            pl.BlockSpec((tq, d), tok),
            pl.BlockSpec(cache_shape, cache_blk),
            pl.BlockSpec(cache_shape, cache_blk),
        ],
        out_shape=[
            jax.ShapeDtypeStruct(x.shape, F32),
            jax.ShapeDtypeStruct(cache_kt.shape, F32),
            jax.ShapeDtypeStruct(cache_vt.shape, F32),
        ],
        scratch_shapes=[
            pltpu.VMEM((tq, ATTN_Q_DIM), F32),
            pltpu.VMEM((ATTN_KV_DIM, tq), F32),
            pltpu.VMEM((ATTN_KV_DIM, tq), F32),
            pltpu.VMEM((tq, ATTN_Q_DIM), F32),
        ],
        compiler_params=pltpu.CompilerParams(
            dimension_semantics=("arbitrary",),
            vmem_limit_bytes=VMEM_LIMIT_BYTES),
        name="attn_sample_layer",
    )(sinks, x, p_all, *consts, cache_kt, cache_vt)


RET_GATE_COL0 = 2 * RET_QK_W + RET_VW


def _ret_project(h, cos, sin, win_ref, chunk_len):
    q0, k0, v0, g0 = 0, RET_QK_W, 2 * RET_QK_W, RET_GATE_COL0
    q = _rope_ret(_dot(h, win_ref[:, q0:k0]), cos, sin)
    k = _rope_ret(_dot(h, win_ref[:, k0:v0]), cos, sin) * (RET_QK_DIM ** -0.5)
    v = _dot(h, win_ref[:, v0:g0])
    n = h.shape[0]
    idx = (lax.broadcasted_iota(jnp.int32, (n, 1), 0) % chunk_len).astype(F32)
    kd = jnp.concatenate(
        [k[:, hd * RET_QK_DIM:(hd + 1) * RET_QK_DIM]
         * jnp.exp((chunk_len - 1.0 - idx) * RET_LOG_GAMMA[hd]) for hd in range(RET_HEADS)], axis=1)
    return q, k, kd, v


def _ret_decay_mask(n, chunk_len, hd):
    ri = lax.broadcasted_iota(jnp.int32, (n, n), 0)
    ci = lax.broadcasted_iota(jnp.int32, (n, n), 1)
    diff = ri - ci
    ok = (diff >= 0) & ((ri // chunk_len) == (ci // chunk_len))
    return jnp.where(ok, jnp.exp(jnp.maximum(diff, 0).astype(F32) * RET_LOG_GAMMA[hd]), 0.0)


def _ret_gated(o, g):
    outs = []
    for hd in range(RET_HEADS):
        oh = o[:, hd * RET_V_DIM:(hd + 1) * RET_V_DIM]
        mu = jnp.mean(oh, axis=-1, keepdims=True)
        var = jnp.mean(jnp.square(oh - mu), axis=-1, keepdims=True)
        outs.append((oh - mu) * lax.rsqrt(var + NORM_EPS))
    return (jnp.concatenate(outs, axis=1) * _silu(g)).astype(BF16)


def _ret_prompt_kernel(x_ref, p_ref, cos_ref, sin_ref, gpre_ref, gpost_ref,
                       win_ref, wout_ref, wgate_ref, wple_ref,
                       xo_ref, st_ref,
                       q_scr, k_scr, kd_scr, v_scr, o_scr, g_scr, dm_scr, *, tq):
    b = pl.program_id(0)
    t = pl.program_id(1)

    @pl.when((b == 0) & (t == 0))
    def _():
        for hd in range(RET_HEADS):
            dm_scr[hd] = _ret_decay_mask(RET_CHUNK, RET_CHUNK, hd)

    @pl.when(t == 0)
    def _():
        st_ref[...] = jnp.zeros(st_ref.shape, F32)

    emb = _dot(p_ref[...].astype(BF16), wple_ref[...])
    x = x_ref[...]
    h = _rms_norm(x, gpre_ref[...]).astype(BF16)
    q, k, kd, v = _ret_project(h, cos_ref[...], sin_ref[...], win_ref, RET_CHUNK)
    q_scr[...] = q.astype(BF16)
    k_scr[...] = k.astype(BF16)
    kd_scr[...] = kd.astype(BF16)
    v_scr[...] = v.astype(BF16)

    idx = lax.broadcasted_iota(jnp.int32, (RET_CHUNK, 1), 0).astype(F32)
    heads = range(RET_HEADS)
    qs = [slice(hd * RET_QK_DIM, (hd + 1) * RET_QK_DIM) for hd in heads]
    vs = [slice(hd * RET_V_DIM, (hd + 1) * RET_V_DIM) for hd in heads]
    chunks = _row_splits(tq, tq // RET_CHUNK)
    gate_cols = iter(_row_splits(RET_VW, 2 * len(chunks)))

    def gate_piece():
        c = next(gate_cols)
        g_scr[:, c] = _dot(h, win_ref[:, RET_GATE_COL0 + c.start:RET_GATE_COL0 + c.stop])

    for rows in chunks:
        scores = [_dot_nt(q_scr[rows, qs[hd]], k_scr[rows, qs[hd]]) for hd in heads]
        cross = [_dot(q_scr[rows, qs[hd]], st_ref[hd].astype(BF16)) for hd in heads]
        gate_piece()
        for hd in heads:
            st_ref[hd] = (math.exp(RET_CHUNK * RET_LOG_GAMMA[hd]) * st_ref[hd]
                          + _dot_tn(kd_scr[rows, qs[hd]], v_scr[rows, vs[hd]]))
        for hd in heads:
            inner = _dot((scores[hd] * dm_scr[hd]).astype(BF16), v_scr[rows, vs[hd]])
            o_scr[rows, vs[hd]] = inner + cross[hd] * jnp.exp((idx + 1.0) * RET_LOG_GAMMA[hd])
        gate_piece()

    ogs = [_ret_gated(o_scr[r, :], g_scr[r, :]) for r in chunks]
    _out_proj_and_residual(chunks, ogs, x, emb, gpost_ref[...], wout_ref, wgate_ref, xo_ref)


def _ret_prompt_layer(layer, x, p_all, cos, sin, gpre, gpost, w_in, w_out, w_gate, w_ple):
    nb, seq, d = x.shape
    tq = min(512, seq)
    ple = p_all.shape[-1]
    kernel = functools.partial(_ret_prompt_kernel, tq=tq)
    tok = lambda b, t: (b, t, 0)
    return pl.pallas_call(
        kernel,
        grid=(nb, seq // tq),
        in_specs=[
            pl.BlockSpec((None, tq, d), tok),
            pl.BlockSpec((None, None, tq, ple), lambda b, t: (layer, b, t, 0)),
            pl.BlockSpec((tq, LANES), lambda b, t: (t, 0)),
            pl.BlockSpec((tq, LANES), lambda b, t: (t, 0)),
            _const_spec(gpre.shape), _const_spec(gpost.shape),
            _const_spec(w_in.shape), _const_spec(w_out.shape),
            _const_spec(w_gate.shape), _const_spec(w_ple.shape),
        ],
        out_specs=[
            pl.BlockSpec((None, tq, d), tok),
            pl.BlockSpec((None, RET_HEADS, RET_QK_DIM, RET_V_DIM), lambda b, t: (b, 0, 0, 0)),
        ],
        out_shape=[
            jax.ShapeDtypeStruct(x.shape, F32),
            jax.ShapeDtypeStruct((nb, RET_HEADS, RET_QK_DIM, RET_V_DIM), F32),
        ],
        scratch_shapes=[
            pltpu.VMEM((tq, RET_QK_W), BF16),
            pltpu.VMEM((tq, RET_QK_W), BF16),
            pltpu.VMEM((tq, RET_QK_W), BF16),
            pltpu.VMEM((tq, RET_VW), BF16),
            pltpu.VMEM((tq, RET_VW), F32),
            pltpu.VMEM((tq, RET_VW), F32),
            pltpu.VMEM((RET_HEADS, RET_CHUNK, RET_CHUNK), F32),
        ],
        compiler_params=pltpu.CompilerParams(
            dimension_semantics=("arbitrary", "arbitrary"),
            vmem_limit_bytes=VMEM_LIMIT_BYTES),
        name="ret_prompt_layer",
    )(x, p_all, cos, sin, gpre, gpost, w_in, w_out, w_gate, w_ple)


def _ret_sample_proj_kernel(x_ref, cos_ref, sin_ref, gpre_ref, win_ref,
                            q_ref, k_ref, kd_ref, v_ref, g_ref, *, dec_len):
    h = _rms_norm(x_ref[...], gpre_ref[...]).astype(BF16)
    q, k, kd, v = _ret_project(h, cos_ref[...], sin_ref[...], win_ref, dec_len)
    q_ref[...] = q
    k_ref[...] = k
    kd_ref[...] = kd
    v_ref[...] = v
    g_ref[...] = _dot(h, win_ref[:, RET_GATE_COL0:])


def _ret_sample_proj(x, cos, sin, gpre, w_in, dec_len):
    n_tok, d = x.shape
    tq = min(512, n_tok)
    kernel = functools.partial(_ret_sample_proj_kernel, dec_len=dec_len)
    tok = lambda i: (i, 0)
    widths = (RET_QK_W, RET_QK_W, RET_QK_W, RET_VW, RET_VW)
    return pl.pallas_call(
        kernel,
        grid=(n_tok // tq,),
        in_specs=[
            pl.BlockSpec((tq, d), tok),
            _const_spec((tq, LANES)), _const_spec((tq, LANES)),
            _const_spec(gpre.shape), _const_spec(w_in.shape),
        ],
        out_specs=[pl.BlockSpec((tq, w), tok) for w in widths],
        out_shape=[jax.ShapeDtypeStruct((n_tok, w), F32) for w in widths],
        compiler_params=pltpu.CompilerParams(
            dimension_semantics=("arbitrary",),
            vmem_limit_bytes=VMEM_LIMIT_BYTES),
        name="ret_sample_proj",
    )(x, cos, sin, gpre, w_in)


def _ret_state_update_tile(q_ref, k_ref, kd_ref, v_ref, st_in, st_out, o_ref, *,
                           first_seq, n_seq, dec_len):
    pad_rows = 2 * dec_len
    idx = lax.broadcasted_iota(jnp.int32, (pad_rows, 1), 0).astype(F32)

    def tok_rows(s):
        return slice((first_seq + s) * dec_len, (first_seq + s + 1) * dec_len)

    def padded(ref, s, cols):
        val = ref[tok_rows(s), cols]
        return jnp.concatenate([val, jnp.zeros_like(val)], axis=0).astype(BF16)

    units = [(s, hd) for s in range(n_seq) for hd in range(RET_HEADS)]
    qs = [slice(hd * RET_QK_DIM, (hd + 1) * RET_QK_DIM) for hd in range(RET_HEADS)]
    vs = [slice(hd * RET_V_DIM, (hd + 1) * RET_V_DIM) for hd in range(RET_HEADS)]
    qh = {(s, hd): padded(q_ref, s, qs[hd]) for s, hd in units}
    vh = {(s, hd): padded(v_ref, s, vs[hd]) for s, hd in units}
    scores = {(s, hd): _dot_nt(qh[s, hd], padded(k_ref, s, qs[hd])) for s, hd in units}
    cross = {(s, hd): _dot(qh[s, hd], st_in[s, hd].astype(BF16)) for s, hd in units}
    for s, hd in units:
        st_out[s, hd] = (math.exp(dec_len * RET_LOG_GAMMA[hd]) * st_in[s, hd]
                         + _dot_tn(padded(kd_ref, s, qs[hd]), vh[s, hd]))
    masks = [_ret_decay_mask(pad_rows, dec_len, hd) for hd in range(RET_HEADS)]
    for s, hd in units:
        inner = _dot((scores[s, hd] * masks[hd]).astype(BF16), vh[s, hd])
        out = inner + cross[s, hd] * jnp.exp((idx + 1.0) * RET_LOG_GAMMA[hd])
        o_ref[tok_rows(s), vs[hd]] = out[0:dec_len, :]


def _ret_sample_finish_kernel(o_ref, g_ref, x_ref, p_ref, gpost_ref, wout_ref, wgate_ref, wple_ref,
                              xo_ref):
    emb = _dot(p_ref[...].astype(BF16), wple_ref[...])
    x = x_ref[...]
    rows = _row_splits(x.shape[0], 2)
    ogs = [_ret_gated(o_ref[r, :], g_ref[r, :]) for r in rows]
    _out_proj_and_residual(rows, ogs, x, emb, gpost_ref[...], wout_ref, wgate_ref, xo_ref)


def _ret_sample_finish(layer, o, g, x, p_all, gpost, w_out, w_gate, w_ple):
    n_tok, d = x.shape
    tq = min(512, n_tok)
    ple = p_all.shape[-1]
    tok = lambda i: (i, 0)
    return pl.pallas_call(
        _ret_sample_finish_kernel,
        grid=(n_tok // tq,),
        in_specs=[
            pl.BlockSpec((tq, RET_VW), tok), pl.BlockSpec((tq, RET_VW), tok),
            pl.BlockSpec((tq, d), tok),
            pl.BlockSpec((None, tq, ple), lambda i: (layer, i, 0)),
            _const_spec(gpost.shape), _const_spec(w_out.shape),
            _const_spec(w_gate.shape), _const_spec(w_ple.shape),
        ],
        out_specs=pl.BlockSpec((tq, d), tok),
        out_shape=jax.ShapeDtypeStruct(x.shape, F32),
        compiler_params=pltpu.CompilerParams(
            dimension_semantics=("arbitrary",),
            vmem_limit_bytes=VMEM_LIMIT_BYTES),
        name="ret_sample_finish",
    )(o, g, x, p_all, gpost, w_out, w_gate, w_ple)


def _rope_tables(pos, head_dim):
    half = head_dim // 2
    inv = ROPE_THETA ** (-np.arange(half, dtype=np.float64) / half)
    ang = np.asarray(pos, np.float64)[:, None] * inv[None, :]
    return np.cos(ang).astype(np.float32), np.sin(ang).astype(np.float32)


def _attn_tables(pos):
    cos, sin = _rope_tables(pos, ATTN_HEAD_DIM)
    reps = LANES // ATTN_HEAD_DIM
    return (np.tile(np.concatenate([cos, cos], axis=1), (1, reps)),
            np.tile(np.concatenate([-sin, sin], axis=1), (1, reps)))


def kernel(x_prompt, x_sample, cache_k_win, cache_v_win, state_ret, p_prompt, p_sample, pre_norm,
           post_norm, w_in_attn, attn_sinks, w_out_attn, w_in_ret, w_out_ret, w_ple, w_ple_gate):
    depth = p_prompt.shape[0]
    nb, seq, d = x_prompt.shape
    n_dec, dec_len, _ = x_sample.shape
    n_stok = n_dec * dec_len
    assert seq % 512 == 0 or seq in (128, 256), seq

    pos_p = np.arange(seq)
    pos_s = PAST_LEN + np.arange(dec_len)
    cos_ap, sin_ap = _attn_tables(pos_p)
    cos_apt, sin_apt = (np.ascontiguousarray(a.T) for a in _rope_tables(pos_p, ATTN_HEAD_DIM))
    cos_as, sin_as = _attn_tables(pos_s)
    cos_rp, sin_rp = _rope_tables(pos_p, RET_QK_DIM)
    cos_rs, sin_rs = _rope_tables(pos_s, RET_QK_DIM)
    attn_s_tile = min(16, n_dec)
    ret_s_tile = min(512, n_stok) // dec_len
    cos_as, sin_as = (np.tile(a, (attn_s_tile, 1)) for a in (cos_as, sin_as))
    cos_ast, sin_ast = (np.ascontiguousarray(np.tile(a, (attn_s_tile, 1)).T)
                        for a in _rope_tables(pos_s, ATTN_HEAD_DIM))
    cos_rs, sin_rs = (np.tile(a, (ret_s_tile, 1)) for a in (cos_rs, sin_rs))
    (cos_ap, sin_ap, cos_apt, sin_apt, cos_as, sin_as, cos_ast, sin_ast,
     cos_rp, sin_rp, cos_rs, sin_rs) = (
        jnp.asarray(a) for a in
        (cos_ap, sin_ap, cos_apt, sin_apt, cos_as, sin_as, cos_ast, sin_ast,
         cos_rp, sin_rp, cos_rs, sin_rs))

    assert depth == 2, depth
    xs = x_sample.reshape(n_stok, d)
    p_s = p_sample.reshape(depth, n_stok, p_sample.shape[-1])
    bf = lambda w: w.astype(BF16)
    gpre = [pre_norm[i][None, :] for i in range(depth)]
    gpost = [post_norm[i][None, :] for i in range(depth)]
    w_gate = [bf(w_ple_gate[i]) for i in range(depth)]
    w_pl = [bf(w_ple[i]) for i in range(depth)]

    w_in = bf(w_in_attn[0])
    w_out = bf(w_out_attn[0])
    sinks = attn_sinks[0]
    k0, v0, g0 = ATTN_Q_DIM, ATTN_Q_DIM + ATTN_KV_DIM, ATTN_Q_DIM + 2 * ATTN_KV_DIM
    w_q, w_k, w_v, w_g = w_in[:, :k0], w_in[:, k0:v0], w_in[:, v0:g0], w_in[:, g0:]
    to_t = lambda a: a.transpose(0, 2, 3, 1)
    from_t = lambda a: a.transpose(0, 3, 1, 2)
    xs, knt, vnt = _attn_sample_layer(
        0, xs, p_s, (cos_as, sin_as, cos_ast, sin_ast), gpre[0], gpost[0], sinks,
        w_q, w_k.T, w_v.T, w_g, w_out, w_gate[0], w_pl[0],
        to_t(cache_k_win[0]), to_t(cache_v_win[0]), dec_len)

    w_in_r = bf(w_in_ret[0])
    w_out_r = bf(w_out_ret[0])
    rq, rk, rkd, rv, rg = _ret_sample_proj(xs, cos_rs, sin_rs, gpre[1], w_in_r, dec_len)
    xp, kpt, vpt, ro, ret_state_sample = _attn_prompt_layer(
        0, x_prompt, p_prompt, (cos_ap, sin_ap, cos_apt, sin_apt), gpre[0], gpost[0], sinks,
        w_q.T, w_k, w_v.T, w_g, w_out, w_gate[0], w_pl[0],
        (rq, rk, rkd, rv), state_ret[0], dec_len)
    head_shape = (nb, ATTN_KV_HEADS, ATTN_HEAD_DIM, WINDOW)

    xp, ret_state_prompt = _ret_prompt_layer(1, xp, p_prompt, cos_rp, sin_rp, gpre[1], gpost[1],
                                             w_in_r, w_out_r, w_gate[1], w_pl[1])
    xs = _ret_sample_finish(1, ro, rg, xs, p_s, gpost[1], w_out_r, w_gate[1], w_pl[1])
    return (xp, xs.reshape(n_dec, dec_len, d),
            from_t(kpt.reshape(head_shape))[None], from_t(vpt.reshape(head_shape))[None],
            from_t(knt)[None], from_t(vnt)[None], ret_state_prompt[None], ret_state_sample[None])
```

```python
import functools
import math

import numpy as np
import jax
import jax.numpy as jnp
from jax import lax
from jax.experimental import pallas as pl
from jax.experimental.pallas import tpu as pltpu

F32 = jnp.float32
BF16 = jnp.bfloat16

PAST_LEN = 16384
ROPE_THETA = 10000.0
NORM_EPS = 1e-6
NEG_INF = -1e30
WINDOW = 128
ATTN_HEADS = 16
ATTN_KV_HEADS = 4
ATTN_GROUP = ATTN_HEADS // ATTN_KV_HEADS
ATTN_HEAD_DIM = 64
ATTN_Q_DIM = ATTN_HEADS * ATTN_HEAD_DIM
ATTN_KV_DIM = ATTN_KV_HEADS * ATTN_HEAD_DIM
RET_HEADS = 4
RET_QK_DIM = 256
RET_V_DIM = 512
RET_QK_W = RET_HEADS * RET_QK_DIM
RET_VW = RET_HEADS * RET_V_DIM
RET_CHUNK = 256
LANES = 128

LOG2_E = math.log2(math.e)
RET_LOG_GAMMA = tuple(math.log1p(-(2.0 ** (-5.0 - h))) for h in range(RET_HEADS))

VMEM_LIMIT_BYTES = 56 * 1024 * 1024
ATTN_PROMPT_VMEM_LIMIT_BYTES = 50 * 1024 * 1024
STATE_PHASES = 4
SAMPLE_ATTN_PIPELINE_GROUP = 8


def _dot(a, b):
    return jnp.dot(a, b, preferred_element_type=F32)


def _dot_nt(a, b):
    return lax.dot_general(a, b, (((1,), (1,)), ((), ())), preferred_element_type=F32)


def _dot_tn(a, b):
    return lax.dot_general(a, b, (((0,), (0,)), ((), ())), preferred_element_type=F32)


def _rms_norm(x, g):
    return x * lax.rsqrt(jnp.mean(x * x, axis=-1, keepdims=True) + NORM_EPS) * g


def _silu(g):
    return g * jax.nn.sigmoid(g)


def _row_splits(n, parts):
    step = n // parts
    return [slice(i * step, (i + 1) * step) for i in range(parts)]


def _out_proj_and_residual(rows, ogs, x, emb, gpost, wout_ref, wgate_ref, xo_ref,
                           after_out_proj=None):
    ys = [_dot(og, wout_ref[...]) for og in ogs]
    if after_out_proj is not None:
        after_out_proj()
    x1s = [x[r, :] + _rms_norm(y, gpost) for r, y in zip(rows, ys)]
    gates = [_dot(x1.astype(BF16), wgate_ref[...]) for x1 in x1s]
    for r, x1, gate in zip(rows, x1s, gates):
        xo_ref[r, :] = x1 + jax.nn.sigmoid(gate) * emb[r, :]


def _rope_attn(x, cos, sin_signed):
    lane = lax.broadcasted_iota(jnp.int32, (1, LANES), 1)
    first_half = (lane % ATTN_HEAD_DIM) < (ATTN_HEAD_DIM // 2)
    outs = []
    for j in range(x.shape[1] // LANES):
        xj = x[:, j * LANES:(j + 1) * LANES]
        fwd = pltpu.roll(xj, LANES - ATTN_HEAD_DIM // 2, axis=1)
        bwd = pltpu.roll(xj, ATTN_HEAD_DIM // 2, axis=1)
        outs.append(xj * cos + jnp.where(first_half, fwd, bwd) * sin_signed)
    return jnp.concatenate(outs, axis=1)


def _rope_ret(x, cos, sin):
    half = RET_QK_DIM // 2
    outs = []
    for h in range(x.shape[1] // RET_QK_DIM):
        x1 = x[:, h * RET_QK_DIM:h * RET_QK_DIM + half]
        x2 = x[:, h * RET_QK_DIM + half:(h + 1) * RET_QK_DIM]
        outs.append(x1 * cos - x2 * sin)
        outs.append(x2 * cos + x1 * sin)
    return jnp.concatenate(outs, axis=1)


def _attn_prompt_kernel(sink_ref, x_ref, p_ref, cos_ref, sin_ref, cost_ref, sint_ref,
                        gpre_ref, gpost_ref, wqt_ref, wk_ref, wvt_ref, wg_ref,
                        wout_ref, wgate_ref, wple_ref,
                        rq_ref, rk_ref, rkd_ref, rv_ref, rstate_hbm,
                        xo_ref, kw_ref, vw_ref, ro_ref, rstate_out_hbm,
                        qt_scr, kext_scr, vt_scr, st_scr, pt_scr, inv_scr, ot_scr,
                        rst_in, rst_out, sem_in, sem_out, *, tq, n_seq, dec_len):
    nblk = tq // WINDOW
    half = ATTN_HEAD_DIM // 2
    t = pl.program_id(1)
    nt = pl.num_programs(1)

    @pl.when(t == 0)
    def _():
        kext_scr[0:WINDOW, :] = jnp.zeros((WINDOW, ATTN_KV_DIM), BF16)
        vt_scr[0] = jnp.zeros((ATTN_KV_DIM, WINDOW), BF16)

    per_phase = n_seq // STATE_PHASES
    step = pl.program_id(0) * nt + t
    n_steps = pl.num_programs(0) * nt
    seq0 = step * n_seq
    last_seq = n_steps * n_seq - 1

    def load_copy(seq, slot, i):
        return pltpu.make_async_copy(rstate_hbm.at[seq], rst_in.at[slot, i], sem_in.at[slot, i])

    def store_copy(seq, slot, i):
        return pltpu.make_async_copy(rst_out.at[slot, i], rstate_out_hbm.at[seq],
                                     sem_out.at[slot, i])

    @pl.when(step == 0)
    def _():
        rst_out[...] = jnp.zeros(rst_out.shape, F32)
        for slot in range(2):
            for i in range(per_phase):
                load_copy(slot * per_phase + i, slot, i).start()
                store_copy(slot * per_phase + i, slot, i).start()

    def state_phase(phase):
        slot = phase % 2
        first = seq0 + phase * per_phase
        for i in range(per_phase):
            load_copy(first + i, slot, i).wait()
            store_copy(first + i, slot, i).wait()
        _ret_state_update_tile(rq_ref, rk_ref, rkd_ref, rv_ref, rst_in.at[slot], rst_out.at[slot],
                               ro_ref, first_seq=phase * per_phase, n_seq=per_phase,
                               dec_len=dec_len)
        for i in range(per_phase):
            store_copy(first + i, slot, i).start()
            load_copy(jnp.minimum(first + i + 2 * per_phase, last_seq), slot, i).start()

    state_phase(0)

    emb = _dot(p_ref[...].astype(BF16), wple_ref[...])
    x = x_ref[...]
    h = _rms_norm(x, gpre_ref[...]).astype(BF16)

    k = _rope_attn(_dot(h, wk_ref[...]), cos_ref[...], sin_ref[...])
    kext_scr[WINDOW:, :] = k.astype(BF16)

    qt = _dot_nt(wqt_ref[...], h)
    cost = cost_ref[...]
    sint = sint_ref[...]
    pieces = []
    for hd in range(ATTN_HEADS):
        x1 = qt[hd * ATTN_HEAD_DIM:hd * ATTN_HEAD_DIM + half, :]
        x2 = qt[hd * ATTN_HEAD_DIM + half:(hd + 1) * ATTN_HEAD_DIM, :]
        pieces.append(x1 * cost - x2 * sint)
        pieces.append(x2 * cost + x1 * sint)
    qrot = (jnp.concatenate(pieces, axis=0) * (ATTN_HEAD_DIM ** -0.5 * LOG2_E)).astype(BF16)
    for jb in range(nblk):
        qt_scr[jb] = qrot[:, jb * WINDOW:(jb + 1) * WINDOW]

    vt = _dot_nt(wvt_ref[...], h)
    vtb = vt.astype(BF16)
    for jb in range(nblk):
        vt_scr[jb + 1] = vtb[:, jb * WINDOW:(jb + 1) * WINDOW]

    @pl.when(t == nt - 1)
    def _():
        kw_ref[...] = k[tq - WINDOW:, :].T
        vw_ref[...] = vt[:, tq - WINDOW:]

    state_phase(1)
    zero_rows = jnp.zeros((ATTN_HEAD_DIM, ATTN_GROUP * WINDOW), BF16)
    groups = [(jb, kh) for jb in range(nblk) for kh in range(ATTN_KV_HEADS)]

    for gidx, (jb, kh) in enumerate(groups):
        heads = [kh * ATTN_GROUP + gi for gi in range(ATTN_GROUP)]
        pair = kh // 2
        kb = kext_scr[jb * WINDOW:(jb + 2) * WINDOW, pair * LANES:(pair + 1) * LANES]
        qg = jnp.concatenate(
            [qt_scr[jb, hd * ATTN_HEAD_DIM:(hd + 1) * ATTN_HEAD_DIM, :] for hd in heads], axis=1)
        qg = jnp.concatenate([qg, zero_rows] if kh % 2 == 0 else [zero_rows, qg], axis=0)
        st_scr[gidx] = _dot(kb, qg)

    g = _dot(h, wg_ref[...])

    key_r = lax.broadcasted_iota(jnp.int32, (WINDOW, WINDOW), 0)
    query_c = lax.broadcasted_iota(jnp.int32, (WINDOW, WINDOW), 1)
    from_prev = key_r > query_c
    prev_w = from_prev.astype(BF16)
    cur_w = 1.0 - prev_w
    for gidx, (jb, kh) in enumerate(groups):
        for gi in range(ATTN_GROUP):
            hd = kh * ATTN_GROUP + gi
            cols = slice(gi * WINDOW, (gi + 1) * WINDOW)
            s_prev = st_scr[gidx, 0:WINDOW, cols]
            if jb == 0:
                s_prev = jnp.where(t > 0, s_prev, NEG_INF)
            s = jnp.where(from_prev, s_prev, st_scr[gidx, WINDOW:, cols])
            sink = sink_ref[hd] * LOG2_E
            m = jnp.maximum(jnp.max(s, axis=0, keepdims=True), sink)
            pr = jnp.exp2(s - m)
            den = jnp.sum(pr, axis=0, keepdims=True) + jnp.exp2(sink - m)
            prb = pr.astype(BF16)
            pt_scr[gidx, 0:WINDOW, cols] = prb * prev_w
            pt_scr[gidx, WINDOW:, cols] = prb * cur_w
            inv_scr[jb, hd:hd + 1, :] = 1.0 / den

    state_phase(2)

    for gidx, (jb, kh) in enumerate(groups):
        vband = jnp.concatenate([vt_scr[jb, kh * ATTN_HEAD_DIM:(kh + 1) * ATTN_HEAD_DIM, :],
                                 vt_scr[jb + 1, kh * ATTN_HEAD_DIM:(kh + 1) * ATTN_HEAD_DIM, :]],
                                axis=1)
        otg = _dot(vband, pt_scr[gidx])
        for gi in range(ATTN_GROUP):
            hd = kh * ATTN_GROUP + gi
            ot_scr[jb, hd * ATTN_HEAD_DIM:(hd + 1) * ATTN_HEAD_DIM, :] = (
                otg[:, gi * WINDOW:(gi + 1) * WINDOW] * inv_scr[jb, hd:hd + 1, :])

    parts = 2 if nblk % 2 == 0 else 1
    rows = _row_splits(tq, parts)
    ogs = []
    for r in rows:
        o = jnp.concatenate([ot_scr[jb].T for jb in range(r.start // WINDOW, r.stop // WINDOW)],
                            axis=0)
        ogs.append((o * _silu(g[r, :])).astype(BF16))
    _out_proj_and_residual(rows, ogs, x, emb, gpost_ref[...], wout_ref, wgate_ref, xo_ref,
                           after_out_proj=lambda: state_phase(3))

    kext_scr[0:WINDOW, :] = kext_scr[tq:tq + WINDOW, :]
    vt_scr[0] = vt_scr[nblk]

    @pl.when(step == n_steps - 1)
    def _():
        for slot in range(2):
            for i in range(per_phase):
                load_copy(last_seq, slot, i).wait()
                store_copy(last_seq, slot, i).wait()


def _const_spec(shape):
    nd = len(shape)
    return pl.BlockSpec(shape, lambda *_: (0,) * nd)


def _attn_prompt_layer(layer, x, p_all, tables, gpre, gpost, sinks, w_qt, w_k, w_vt, w_g,
                       w_out, w_gate, w_ple, ret_qkv, ret_state, dec_len):
    nb, seq, d = x.shape
    tq = min(512, seq)
    nblk = tq // WINDOW
    nt = seq // tq
    n_seq = ret_state.shape[0] // (nb * nt)
    assert n_seq * nb * nt == ret_state.shape[0] and n_seq % STATE_PHASES == 0, (
        ret_state.shape, nb, nt)
    ple = p_all.shape[-1]
    cos, sin, cost, sint = tables
    kernel = functools.partial(_attn_prompt_kernel, tq=tq, n_seq=n_seq, dec_len=dec_len)
    tok = lambda b, t: (b, t, 0)
    win_blk = lambda b, t: (b, 0, 0)
    stok = lambda b, t: (b * nt + t, 0)
    consts = (gpre, gpost, w_qt, w_k, w_vt, w_g, w_out, w_gate, w_ple)
    rq, rk, rkd, rv = ret_qkv
    st_shape = (2, n_seq // STATE_PHASES) + ret_state.shape[1:]
    return pl.pallas_call(
        kernel,
        grid=(nb, nt),
        in_specs=[
            pl.BlockSpec(memory_space=pltpu.SMEM),
            pl.BlockSpec((None, tq, d), tok),
            pl.BlockSpec((None, None, tq, ple), lambda b, t: (layer, b, t, 0)),
            pl.BlockSpec((tq, LANES), lambda b, t: (t, 0)),
            pl.BlockSpec((tq, LANES), lambda b, t: (t, 0)),
            pl.BlockSpec((ATTN_HEAD_DIM // 2, tq), lambda b, t: (0, t)),
            pl.BlockSpec((ATTN_HEAD_DIM // 2, tq), lambda b, t: (0, t)),
        ] + [_const_spec(c.shape) for c in consts] + [
            pl.BlockSpec((n_seq * dec_len, RET_QK_W), stok),
            pl.BlockSpec((n_seq * dec_len, RET_QK_W), stok),
            pl.BlockSpec((n_seq * dec_len, RET_QK_W), stok),
            pl.BlockSpec((n_seq * dec_len, RET_VW), stok),
            pl.BlockSpec(memory_space=pl.ANY),
        ],
        out_specs=[
            pl.BlockSpec((None, tq, d), tok),
            pl.BlockSpec((None, ATTN_KV_DIM, WINDOW), win_blk),
            pl.BlockSpec((None, ATTN_KV_DIM, WINDOW), win_blk),
            pl.BlockSpec((n_seq * dec_len, RET_VW), stok),
            pl.BlockSpec(memory_space=pl.ANY),
        ],
        out_shape=[
            jax.ShapeDtypeStruct(x.shape, F32),
            jax.ShapeDtypeStruct((nb, ATTN_KV_DIM, WINDOW), F32),
            jax.ShapeDtypeStruct((nb, ATTN_KV_DIM, WINDOW), F32),
            jax.ShapeDtypeStruct((rq.shape[0], RET_VW), F32),
            jax.ShapeDtypeStruct(ret_state.shape, F32),
        ],
        scratch_shapes=[
            pltpu.VMEM((nblk, ATTN_Q_DIM, WINDOW), BF16),
            pltpu.VMEM((tq + WINDOW, ATTN_KV_DIM), BF16),
            pltpu.VMEM((nblk + 1, ATTN_KV_DIM, WINDOW), BF16),
            pltpu.VMEM((nblk * ATTN_KV_HEADS, 2 * WINDOW, ATTN_GROUP * WINDOW), F32),
            pltpu.VMEM((nblk * ATTN_KV_HEADS, 2 * WINDOW, ATTN_GROUP * WINDOW), BF16),
            pltpu.VMEM((nblk, ATTN_HEADS, WINDOW), F32),
            pltpu.VMEM((nblk, ATTN_Q_DIM, WINDOW), F32),
            pltpu.VMEM(st_shape, F32),
            pltpu.VMEM(st_shape, F32),
            pltpu.SemaphoreType.DMA((2, n_seq // STATE_PHASES)),
            pltpu.SemaphoreType.DMA((2, n_seq // STATE_PHASES)),
        ],
        compiler_params=pltpu.CompilerParams(
            dimension_semantics=("arbitrary", "arbitrary"),
            vmem_limit_bytes=ATTN_PROMPT_VMEM_LIMIT_BYTES),
        name="attn_prompt_layer",
    )(sinks, x, p_all, cos, sin, cost, sint, *consts, rq, rk, rkd, rv, ret_state)


def _attn_sample_kernel(sink_ref, x_ref, p_ref, cos_ref, sin_ref, cost_ref, sint_ref,
                        gpre_ref, gpost_ref, wq_ref, wkt_ref, wvt_ref, wg_ref,
                        wout_ref, wgate_ref, wple_ref, kc_ref, vc_ref,
                        xo_ref, ko_ref, vo_ref,
                        q_scr, knt_scr, vnt_scr, o_scr, *, n_seq, dec_len):
    half = ATTN_HEAD_DIM // 2
    x = x_ref[...]
    h = _rms_norm(x, gpre_ref[...]).astype(BF16)
    q_scr[...] = (_rope_attn(_dot(h, wq_ref[...]), cos_ref[...], sin_ref[...])
                  * (ATTN_HEAD_DIM ** -0.5 * LOG2_E))
    knt = _dot_nt(wkt_ref[...], h)
    cost = cost_ref[...]
    sint = sint_ref[...]
    pieces = []
    for kh in range(ATTN_KV_HEADS):
        x1 = knt[kh * ATTN_HEAD_DIM:kh * ATTN_HEAD_DIM + half, :]
        x2 = knt[kh * ATTN_HEAD_DIM + half:(kh + 1) * ATTN_HEAD_DIM, :]
        pieces.append(x1 * cost - x2 * sint)
        pieces.append(x2 * cost + x1 * sint)
    knt_scr[...] = jnp.concatenate(pieces, axis=0)
    vnt_scr[...] = _dot_nt(wvt_ref[...], h)

    rows = 2 * dec_len
    tok = lax.broadcasted_iota(jnp.int32, (rows, 2 * WINDOW), 0) % dec_len
    col = lax.broadcasted_iota(jnp.int32, (rows, 2 * WINDOW), 1)
    upper_row = lax.broadcasted_iota(jnp.int32, (rows, 1), 0) >= dec_len
    lane = lax.broadcasted_iota(jnp.int32, (ATTN_HEAD_DIM, WINDOW), 1)
    keep_old = lane < WINDOW - dec_len
    zeros = jnp.zeros((ATTN_HEAD_DIM, 2 * WINDOW), BF16)

    def select_head(t, parity):
        return jnp.concatenate([t, zeros] if parity == 0 else [zeros, t], axis=0)

    def scores(s):
        tok_rows = slice(s * dec_len, (s + 1) * dec_len)
        new_col = col - (WINDOW + s * dec_len)
        mask = ((col < WINDOW) & (col > tok)) | ((new_col >= 0) & (new_col <= tok))
        new_shift = (WINDOW - dec_len - s * dec_len) % WINDOW
        out = []
        for kh in range(ATTN_KV_HEADS):
            hrows = slice(kh * ATTN_HEAD_DIM, (kh + 1) * ATTN_HEAD_DIM)
            kc = kc_ref[s, kh]
            vc = vc_ref[s, kh]
            kn = knt_scr[hrows, :]
            vn = vnt_scr[hrows, :]
            kn_at_end = pltpu.roll(kn, new_shift, axis=1) if new_shift else kn
            vn_at_end = pltpu.roll(vn, new_shift, axis=1) if new_shift else vn
            ko_ref[s, kh] = jnp.where(keep_old, pltpu.roll(kc, WINDOW - dec_len, axis=1), kn_at_end)
            vo_ref[s, kh] = jnp.where(keep_old, pltpu.roll(vc, WINDOW - dec_len, axis=1), vn_at_end)

            kt = jnp.concatenate([kc, kn], axis=1).astype(BF16)
            vt = jnp.concatenate([vc, vn], axis=1).astype(BF16)
            qp = jnp.concatenate([q_scr[tok_rows, (2 * kh + c) * LANES:(2 * kh + c + 1) * LANES]
                                  for c in range(2)], axis=0).astype(BF16)
            sc = [jnp.where(mask, _dot(qp, select_head(kt, parity)), NEG_INF)
                  for parity in range(2)]
            out.append((sc, vt))
        return out

    def values(s, per_head):
        tok_rows = slice(s * dec_len, (s + 1) * dec_len)
        probs = []
        for kh, (scs, vt) in enumerate(per_head):
            for parity, sc in enumerate(scs):
                hd_lo, hd_hi = 4 * kh + parity, 4 * kh + 2 + parity
                sink = jnp.where(upper_row, sink_ref[hd_hi], sink_ref[hd_lo]) * LOG2_E
                m = jnp.maximum(jnp.max(sc, axis=-1, keepdims=True), sink)
                pr = jnp.exp2(sc - m)
                den = jnp.sum(pr, axis=-1, keepdims=True) + jnp.exp2(sink - m)
                probs.append((pr / den).astype(BF16))
        for kh, (scs, vt) in enumerate(per_head):
            o_pair = (_dot_nt(probs[2 * kh], select_head(vt, 0))
                      + _dot_nt(probs[2 * kh + 1], select_head(vt, 1)))
            for c in range(2):
                o_scr[tok_rows, (2 * kh + c) * LANES:(2 * kh + c + 1) * LANES] = (
                    o_pair[c * dec_len:(c + 1) * dec_len, :])

    group = min(SAMPLE_ATTN_PIPELINE_GROUP, n_seq)
    pending = [scores(s) for s in range(group)]
    for s0 in range(0, n_seq, group):
        upcoming = [scores(s) for s in range(s0 + group, min(s0 + 2 * group, n_seq))]
        for i, per_head in enumerate(pending):
            values(s0 + i, per_head)
        pending = upcoming

    g = _dot(h, wg_ref[...])
    emb = _dot(p_ref[...].astype(BF16), wple_ref[...])
    og = (o_scr[...] * _silu(g)).astype(BF16)
    _out_proj_and_residual(_row_splits(x.shape[0], 1), [og], x, emb, gpost_ref[...],
                           wout_ref, wgate_ref, xo_ref)


def _attn_sample_layer(layer, x, p_all, tables, gpre, gpost, sinks, w_q, w_kt, w_vt, w_g,
                       w_out, w_gate, w_ple, cache_kt, cache_vt, dec_len):
    n_tok, d = x.shape
    n_all = cache_kt.shape[0]
    n_seq = min(16, n_all)
    tq = n_seq * dec_len
    ple = p_all.shape[-1]
    cos, sin, cost, sint = tables
    kernel = functools.partial(_attn_sample_kernel, n_seq=n_seq, dec_len=dec_len)
    tok = lambda i: (i, 0)
    cache_shape = (n_seq, ATTN_KV_HEADS, ATTN_HEAD_DIM, WINDOW)
    cache_blk = lambda i: (i, 0, 0, 0)
    consts = (cos, sin, cost, sint, gpre, gpost, w_q, w_kt, w_vt, w_g, w_out, w_gate, w_ple)
    return pl.pallas_call(
        kernel,
        grid=(n_all // n_seq,),
        in_specs=[
            pl.BlockSpec(memory_space=pltpu.SMEM),
            pl.BlockSpec((tq, d), tok),
            pl.BlockSpec((None, tq, ple), lambda i: (layer, i, 0)),
        ] + [_const_spec(c.shape) for c in consts] + [
            pl.BlockSpec(cache_shape, cache_blk),
            pl.BlockSpec(cache_shape, cache_blk),
        ],
        out_specs=[
            pl.BlockSpec((tq, d), tok),
            pl.BlockSpec(cache_shape, cache_blk),
            pl.BlockSpec(cache_shape, cache_blk),
        ],
        out_shape=[
            jax.ShapeDtypeStruct(x.shape, F32),
            jax.ShapeDtypeStruct(cache_kt.shape, F32),
            jax.ShapeDtypeStruct(cache_vt.shape, F32),
        ],
        scratch_shapes=[
            pltpu.VMEM((tq, ATTN_Q_DIM), F32),
            pltpu.VMEM((ATTN_KV_DIM, tq), F32),
            pltpu.VMEM((ATTN_KV_DIM, tq), F32),
            pltpu.VMEM((tq, ATTN_Q_DIM), F32),
        ],
        compiler_params=pltpu.CompilerParams(
            dimension_semantics=("arbitrary",),
            vmem_limit_bytes=VMEM_LIMIT_BYTES),
        name="attn_sample_layer",
    )(sinks, x, p_all, *consts, cache_kt, cache_vt)


RET_GATE_COL0 = 2 * RET_QK_W + RET_VW


def _ret_project(h, cos, sin, win_ref, chunk_len):
    q0, k0, v0, g0 = 0, RET_QK_W, 2 * RET_QK_W, RET_GATE_COL0
    q = _rope_ret(_dot(h, win_ref[:, q0:k0]), cos, sin)
    k = _rope_ret(_dot(h, win_ref[:, k0:v0]), cos, sin) * (RET_QK_DIM ** -0.5)
    v = _dot(h, win_ref[:, v0:g0])
    n = h.shape[0]
    idx = (lax.broadcasted_iota(jnp.int32, (n, 1), 0) % chunk_len).astype(F32)
    kd = jnp.concatenate(
        [k[:, hd * RET_QK_DIM:(hd + 1) * RET_QK_DIM]
         * jnp.exp((chunk_len - 1.0 - idx) * RET_LOG_GAMMA[hd]) for hd in range(RET_HEADS)], axis=1)
    return q, k, kd, v


def _ret_decay_mask(n, chunk_len, hd):
    ri = lax.broadcasted_iota(jnp.int32, (n, n), 0)
    ci = lax.broadcasted_iota(jnp.int32, (n, n), 1)
    diff = ri - ci
    ok = (diff >= 0) & ((ri // chunk_len) == (ci // chunk_len))
    return jnp.where(ok, jnp.exp(jnp.maximum(diff, 0).astype(F32) * RET_LOG_GAMMA[hd]), 0.0)


def _ret_gated(o, g):
    outs = []
    for hd in range(RET_HEADS):
        oh = o[:, hd * RET_V_DIM:(hd + 1) * RET_V_DIM]
        mu = jnp.mean(oh, axis=-1, keepdims=True)
        var = jnp.mean(jnp.square(oh - mu), axis=-1, keepdims=True)
        outs.append((oh - mu) * lax.rsqrt(var + NORM_EPS))
    return (jnp.concatenate(outs, axis=1) * _silu(g)).astype(BF16)


def _ret_prompt_kernel(x_ref, p_ref, cos_ref, sin_ref, gpre_ref, gpost_ref,
                       win_ref, wout_ref, wgate_ref, wple_ref,
                       xo_ref, st_ref,
                       q_scr, k_scr, kd_scr, v_scr, o_scr, g_scr, dm_scr, *, tq):
    b = pl.program_id(0)
    t = pl.program_id(1)

    @pl.when((b == 0) & (t == 0))
    def _():
        for hd in range(RET_HEADS):
            dm_scr[hd] = _ret_decay_mask(RET_CHUNK, RET_CHUNK, hd)

    @pl.when(t == 0)
    def _():
        st_ref[...] = jnp.zeros(st_ref.shape, F32)

    emb = _dot(p_ref[...].astype(BF16), wple_ref[...])
    x = x_ref[...]
    h = _rms_norm(x, gpre_ref[...]).astype(BF16)
    q, k, kd, v = _ret_project(h, cos_ref[...], sin_ref[...], win_ref, RET_CHUNK)
    q_scr[...] = q.astype(BF16)
    k_scr[...] = k.astype(BF16)
    kd_scr[...] = kd.astype(BF16)
    v_scr[...] = v.astype(BF16)

    idx = lax.broadcasted_iota(jnp.int32, (RET_CHUNK, 1), 0).astype(F32)
    heads = range(RET_HEADS)
    qs = [slice(hd * RET_QK_DIM, (hd + 1) * RET_QK_DIM) for hd in heads]
    vs = [slice(hd * RET_V_DIM, (hd + 1) * RET_V_DIM) for hd in heads]
    chunks = _row_splits(tq, tq // RET_CHUNK)
    gate_cols = iter(_row_splits(RET_VW, 2 * len(chunks)))

    def gate_piece():
        c = next(gate_cols)
        g_scr[:, c] = _dot(h, win_ref[:, RET_GATE_COL0 + c.start:RET_GATE_COL0 + c.stop])

    for rows in chunks:
        scores = [_dot_nt(q_scr[rows, qs[hd]], k_scr[rows, qs[hd]]) for hd in heads]
        cross = [_dot(q_scr[rows, qs[hd]], st_ref[hd].astype(BF16)) for hd in heads]
        gate_piece()
        for hd in heads:
            st_ref[hd] = (math.exp(RET_CHUNK * RET_LOG_GAMMA[hd]) * st_ref[hd]
                          + _dot_tn(kd_scr[rows, qs[hd]], v_scr[rows, vs[hd]]))
        for hd in heads:
            inner = _dot((scores[hd] * dm_scr[hd]).astype(BF16), v_scr[rows, vs[hd]])
            o_scr[rows, vs[hd]] = inner + cross[hd] * jnp.exp((idx + 1.0) * RET_LOG_GAMMA[hd])
        gate_piece()

    ogs = [_ret_gated(o_scr[r, :], g_scr[r, :]) for r in chunks]
    _out_proj_and_residual(chunks, ogs, x, emb, gpost_ref[...], wout_ref, wgate_ref, xo_ref)


def _ret_prompt_layer(layer, x, p_all, cos, sin, gpre, gpost, w_in, w_out, w_gate, w_ple):
    nb, seq, d = x.shape
    tq = min(512, seq)
    ple = p_all.shape[-1]
    kernel = functools.partial(_ret_prompt_kernel, tq=tq)
    tok = lambda b, t: (b, t, 0)
    return pl.pallas_call(
        kernel,
        grid=(nb, seq // tq),
        in_specs=[
            pl.BlockSpec((None, tq, d), tok),
            pl.BlockSpec((None, None, tq, ple), lambda b, t: (layer, b, t, 0)),
            pl.BlockSpec((tq, LANES), lambda b, t: (t, 0)),
            pl.BlockSpec((tq, LANES), lambda b, t: (t, 0)),
            _const_spec(gpre.shape), _const_spec(gpost.shape),
            _const_spec(w_in.shape), _const_spec(w_out.shape),
            _const_spec(w_gate.shape), _const_spec(w_ple.shape),
        ],
        out_specs=[
            pl.BlockSpec((None, tq, d), tok),
            pl.BlockSpec((None, RET_HEADS, RET_QK_DIM, RET_V_DIM), lambda b, t: (b, 0, 0, 0)),
        ],
        out_shape=[
            jax.ShapeDtypeStruct(x.shape, F32),
            jax.ShapeDtypeStruct((nb, RET_HEADS, RET_QK_DIM, RET_V_DIM), F32),
        ],
        scratch_shapes=[
            pltpu.VMEM((tq, RET_QK_W), BF16),
            pltpu.VMEM((tq, RET_QK_W), BF16),
            pltpu.VMEM((tq, RET_QK_W), BF16),
            pltpu.VMEM((tq, RET_VW), BF16),
            pltpu.VMEM((tq, RET_VW), F32),
            pltpu.VMEM((tq, RET_VW), F32),
            pltpu.VMEM((RET_HEADS, RET_CHUNK, RET_CHUNK), F32),
        ],
        compiler_params=pltpu.CompilerParams(
            dimension_semantics=("arbitrary", "arbitrary"),
            vmem_limit_bytes=VMEM_LIMIT_BYTES),
        name="ret_prompt_layer",
    )(x, p_all, cos, sin, gpre, gpost, w_in, w_out, w_gate, w_ple)


def _ret_sample_proj_kernel(x_ref, cos_ref, sin_ref, gpre_ref, win_ref,
                            q_ref, k_ref, kd_ref, v_ref, g_ref, *, dec_len):
    h = _rms_norm(x_ref[...], gpre_ref[...]).astype(BF16)
    q, k, kd, v = _ret_project(h, cos_ref[...], sin_ref[...], win_ref, dec_len)
    q_ref[...] = q
    k_ref[...] = k
    kd_ref[...] = kd
    v_ref[...] = v
    g_ref[...] = _dot(h, win_ref[:, RET_GATE_COL0:])


def _ret_sample_proj(x, cos, sin, gpre, w_in, dec_len):
    n_tok, d = x.shape
    tq = min(512, n_tok)
    kernel = functools.partial(_ret_sample_proj_kernel, dec_len=dec_len)
    tok = lambda i: (i, 0)
    widths = (RET_QK_W, RET_QK_W, RET_QK_W, RET_VW, RET_VW)
    return pl.pallas_call(
        kernel,
        grid=(n_tok // tq,),
        in_specs=[
            pl.BlockSpec((tq, d), tok),
            _const_spec((tq, LANES)), _const_spec((tq, LANES)),
            _const_spec(gpre.shape), _const_spec(w_in.shape),
        ],
        out_specs=[pl.BlockSpec((tq, w), tok) for w in widths],
        out_shape=[jax.ShapeDtypeStruct((n_tok, w), F32) for w in widths],
        compiler_params=pltpu.CompilerParams(
            dimension_semantics=("arbitrary",),
            vmem_limit_bytes=VMEM_LIMIT_BYTES),
        name="ret_sample_proj",
    )(x, cos, sin, gpre, w_in)


def _ret_state_update_tile(q_ref, k_ref, kd_ref, v_ref, st_in, st_out, o_ref, *,
                           first_seq, n_seq, dec_len):
    pad_rows = 2 * dec_len
    idx = lax.broadcasted_iota(jnp.int32, (pad_rows, 1), 0).astype(F32)

    def tok_rows(s):
        return slice((first_seq + s) * dec_len, (first_seq + s + 1) * dec_len)

    def padded(ref, s, cols):
        val = ref[tok_rows(s), cols]
        return jnp.concatenate([val, jnp.zeros_like(val)], axis=0).astype(BF16)

    units = [(s, hd) for s in range(n_seq) for hd in range(RET_HEADS)]
    qs = [slice(hd * RET_QK_DIM, (hd + 1) * RET_QK_DIM) for hd in range(RET_HEADS)]
    vs = [slice(hd * RET_V_DIM, (hd + 1) * RET_V_DIM) for hd in range(RET_HEADS)]
    qh = {(s, hd): padded(q_ref, s, qs[hd]) for s, hd in units}
    vh = {(s, hd): padded(v_ref, s, vs[hd]) for s, hd in units}
    scores = {(s, hd): _dot_nt(qh[s, hd], padded(k_ref, s, qs[hd])) for s, hd in units}
    cross = {(s, hd): _dot(qh[s, hd], st_in[s, hd].astype(BF16)) for s, hd in units}
    for s, hd in units:
        st_out[s, hd] = (math.exp(dec_len * RET_LOG_GAMMA[hd]) * st_in[s, hd]
                         + _dot_tn(padded(kd_ref, s, qs[hd]), vh[s, hd]))
    masks = [_ret_decay_mask(pad_rows, dec_len, hd) for hd in range(RET_HEADS)]
    for s, hd in units:
        inner = _dot((scores[s, hd] * masks[hd]).astype(BF16), vh[s, hd])
        out = inner + cross[s, hd] * jnp.exp((idx + 1.0) * RET_LOG_GAMMA[hd])
        o_ref[tok_rows(s), vs[hd]] = out[0:dec_len, :]


def _ret_sample_finish_kernel(o_ref, g_ref, x_ref, p_ref, gpost_ref, wout_ref, wgate_ref, wple_ref,
                              xo_ref):
    emb = _dot(p_ref[...].astype(BF16), wple_ref[...])
    x = x_ref[...]
    rows = _row_splits(x.shape[0], 2)
    ogs = [_ret_gated(o_ref[r, :], g_ref[r, :]) for r in rows]
    _out_proj_and_residual(rows, ogs, x, emb, gpost_ref[...], wout_ref, wgate_ref, xo_ref)


def _ret_sample_finish(layer, o, g, x, p_all, gpost, w_out, w_gate, w_ple):
    n_tok, d = x.shape
    tq = min(512, n_tok)
    ple = p_all.shape[-1]
    tok = lambda i: (i, 0)
    return pl.pallas_call(
        _ret_sample_finish_kernel,
        grid=(n_tok // tq,),
        in_specs=[
            pl.BlockSpec((tq, RET_VW), tok), pl.BlockSpec((tq, RET_VW), tok),
            pl.BlockSpec((tq, d), tok),
            pl.BlockSpec((None, tq, ple), lambda i: (layer, i, 0)),
            _const_spec(gpost.shape), _const_spec(w_out.shape),
            _const_spec(w_gate.shape), _const_spec(w_ple.shape),
        ],
        out_specs=pl.BlockSpec((tq, d), tok),
        out_shape=jax.ShapeDtypeStruct(x.shape, F32),
        compiler_params=pltpu.CompilerParams(
            dimension_semantics=("arbitrary",),
            vmem_limit_bytes=VMEM_LIMIT_BYTES),
        name="ret_sample_finish",
    )(o, g, x, p_all, gpost, w_out, w_gate, w_ple)


def _rope_tables(pos, head_dim):
    half = head_dim // 2
    inv = ROPE_THETA ** (-np.arange(half, dtype=np.float64) / half)
    ang = np.asarray(pos, np.float64)[:, None] * inv[None, :]
    return np.cos(ang).astype(np.float32), np.sin(ang).astype(np.float32)


def _attn_tables(pos):
    cos, sin = _rope_tables(pos, ATTN_HEAD_DIM)
    reps = LANES // ATTN_HEAD_DIM
    return (np.tile(np.concatenate([cos, cos], axis=1), (1, reps)),
            np.tile(np.concatenate([-sin, sin], axis=1), (1, reps)))


def kernel(x_prompt, x_sample, cache_k_win, cache_v_win, state_ret, p_prompt, p_sample, pre_norm,
           post_norm, w_in_attn, attn_sinks, w_out_attn, w_in_ret, w_out_ret, w_ple, w_ple_gate):
    depth = p_prompt.shape[0]
    nb, seq, d = x_prompt.shape
    n_dec, dec_len, _ = x_sample.shape
    n_stok = n_dec * dec_len
    assert seq % 512 == 0 or seq in (128, 256), seq

    pos_p = np.arange(seq)
    pos_s = PAST_LEN + np.arange(dec_len)
    cos_ap, sin_ap = _attn_tables(pos_p)
    cos_apt, sin_apt = (np.ascontiguousarray(a.T) for a in _rope_tables(pos_p, ATTN_HEAD_DIM))
    cos_as, sin_as = _attn_tables(pos_s)
    cos_rp, sin_rp = _rope_tables(pos_p, RET_QK_DIM)
    cos_rs, sin_rs = _rope_tables(pos_s, RET_QK_DIM)
    attn_s_tile = min(16, n_dec)
    ret_s_tile = min(512, n_stok) // dec_len
    cos_as, sin_as = (np.tile(a, (attn_s_tile, 1)) for a in (cos_as, sin_as))
    cos_ast, sin_ast = (np.ascontiguousarray(np.tile(a, (attn_s_tile, 1)).T)
                        for a in _rope_tables(pos_s, ATTN_HEAD_DIM))
    cos_rs, sin_rs = (np.tile(a, (ret_s_tile, 1)) for a in (cos_rs, sin_rs))
    (cos_ap, sin_ap, cos_apt, sin_apt, cos_as, sin_as, cos_ast, sin_ast,
     cos_rp, sin_rp, cos_rs, sin_rs) = (
        jnp.asarray(a) for a in
        (cos_ap, sin_ap, cos_apt, sin_apt, cos_as, sin_as, cos_ast, sin_ast,
         cos_rp, sin_rp, cos_rs, sin_rs))

    assert depth == 2, depth
    xs = x_sample.reshape(n_stok, d)
    p_s = p_sample.reshape(depth, n_stok, p_sample.shape[-1])
    bf = lambda w: w.astype(BF16)
    gpre = [pre_norm[i][None, :] for i in range(depth)]
    gpost = [post_norm[i][None, :] for i in range(depth)]
    w_gate = [bf(w_ple_gate[i]) for i in range(depth)]
    w_pl = [bf(w_ple[i]) for i in range(depth)]

    w_in = bf(w_in_attn[0])
    w_out = bf(w_out_attn[0])
    sinks = attn_sinks[0]
    k0, v0, g0 = ATTN_Q_DIM, ATTN_Q_DIM + ATTN_KV_DIM, ATTN_Q_DIM + 2 * ATTN_KV_DIM
    w_q, w_k, w_v, w_g = w_in[:, :k0], w_in[:, k0:v0], w_in[:, v0:g0], w_in[:, g0:]
    to_t = lambda a: a.transpose(0, 2, 3, 1)
    from_t = lambda a: a.transpose(0, 3, 1, 2)
    xs, knt, vnt = _attn_sample_layer(
        0, xs, p_s, (cos_as, sin_as, cos_ast, sin_ast), gpre[0], gpost[0], sinks,
        w_q, w_k.T, w_v.T, w_g, w_out, w_gate[0], w_pl[0],
        to_t(cache_k_win[0]), to_t(cache_v_win[0]), dec_len)

    w_in_r = bf(w_in_ret[0])
    w_out_r = bf(w_out_ret[0])
    rq, rk, rkd, rv, rg = _ret_sample_proj(xs, cos_rs, sin_rs, gpre[1], w_in_r, dec_len)
    xp, kpt, vpt, ro, ret_state_sample = _attn_prompt_layer(
        0, x_prompt, p_prompt, (cos_ap, sin_ap, cos_apt, sin_apt), gpre[0], gpost[0], sinks,
        w_q.T, w_k, w_v.T, w_g, w_out, w_gate[0], w_pl[0],
        (rq, rk, rkd, rv), state_ret[0], dec_len)
    head_shape = (nb, ATTN_KV_HEADS, ATTN_HEAD_DIM, WINDOW)

    xp, ret_state_prompt = _ret_prompt_layer(1, xp, p_prompt, cos_rp, sin_rp, gpre[1], gpost[1],
                                             w_in_r, w_out_r, w_gate[1], w_pl[1])
    xs = _ret_sample_finish(1, ro, rg, xs, p_s, gpost[1], w_out_r, w_gate[1], w_pl[1])
    return (xp, xs.reshape(n_dec, dec_len, d),
            from_t(kpt.reshape(head_shape))[None], from_t(vpt.reshape(head_shape))[None],
            from_t(knt)[None], from_t(vnt)[None], ret_state_prompt[None], ret_state_sample[None])
```

```python
import functools
import math

import numpy as np
import jax
import jax.numpy as jnp
from jax import lax
from jax.experimental import pallas as pl
from jax.experimental.pallas import tpu as pltpu

F32 = jnp.float32
BF16 = jnp.bfloat16

PAST_LEN = 16384
ROPE_THETA = 10000.0
NORM_EPS = 1e-6
NEG_INF = -1e30
WINDOW = 128
ATTN_HEADS = 16
ATTN_KV_HEADS = 4
ATTN_GROUP = ATTN_HEADS // ATTN_KV_HEADS
ATTN_HEAD_DIM = 64
ATTN_Q_DIM = ATTN_HEADS * ATTN_HEAD_DIM
ATTN_KV_DIM = ATTN_KV_HEADS * ATTN_HEAD_DIM
RET_HEADS = 4
RET_QK_DIM = 256
RET_V_DIM = 512
RET_QK_W = RET_HEADS * RET_QK_DIM
RET_VW = RET_HEADS * RET_V_DIM
RET_CHUNK = 256
LANES = 128

LOG2_E = math.log2(math.e)
RET_LOG_GAMMA = tuple(math.log1p(-(2.0 ** (-5.0 - h))) for h in range(RET_HEADS))

VMEM_LIMIT_BYTES = 56 * 1024 * 1024
ATTN_PROMPT_VMEM_LIMIT_BYTES = 50 * 1024 * 1024
STATE_PHASES = 4
STATE_DMA_PRIORITY = 1
SAMPLE_ATTN_PIPELINE_GROUP = 8


def _dot(a, b):
    return jnp.dot(a, b, preferred_element_type=F32)


def _dot_nt(a, b):
    return lax.dot_general(a, b, (((1,), (1,)), ((), ())), preferred_element_type=F32)


def _dot_tn(a, b):
    return lax.dot_general(a, b, (((0,), (0,)), ((), ())), preferred_element_type=F32)


def _rms_norm(x, g):
    return x * lax.rsqrt(jnp.mean(x * x, axis=-1, keepdims=True) + NORM_EPS) * g


def _silu(g):
    return g * jax.nn.sigmoid(g)


def _row_splits(n, parts):
    step = n // parts
    return [slice(i * step, (i + 1) * step) for i in range(parts)]


def _out_proj_and_residual(rows, ogs, x, emb, gpost, wout_ref, wgate_ref, xo_ref,
                           after_out_proj=None):
    ys = [_dot(og, wout_ref[...]) for og in ogs]
    if after_out_proj is not None:
        after_out_proj()
    x1s = [x[r, :] + _rms_norm(y, gpost) for r, y in zip(rows, ys)]
    gates = [_dot(x1.astype(BF16), wgate_ref[...]) for x1 in x1s]
    for r, x1, gate in zip(rows, x1s, gates):
        xo_ref[r, :] = x1 + jax.nn.sigmoid(gate) * emb[r, :]


def _rope_attn(x, cos, sin_signed):
    lane = lax.broadcasted_iota(jnp.int32, (1, LANES), 1)
    first_half = (lane % ATTN_HEAD_DIM) < (ATTN_HEAD_DIM // 2)
    outs = []
    for j in range(x.shape[1] // LANES):
        xj = x[:, j * LANES:(j + 1) * LANES]
        fwd = pltpu.roll(xj, LANES - ATTN_HEAD_DIM // 2, axis=1)
        bwd = pltpu.roll(xj, ATTN_HEAD_DIM // 2, axis=1)
        outs.append(xj * cos + jnp.where(first_half, fwd, bwd) * sin_signed)
    return jnp.concatenate(outs, axis=1)


def _rope_ret(x, cos, sin):
    half = RET_QK_DIM // 2
    outs = []
    for h in range(x.shape[1] // RET_QK_DIM):
        x1 = x[:, h * RET_QK_DIM:h * RET_QK_DIM + half]
        x2 = x[:, h * RET_QK_DIM + half:(h + 1) * RET_QK_DIM]
        outs.append(x1 * cos - x2 * sin)
        outs.append(x2 * cos + x1 * sin)
    return jnp.concatenate(outs, axis=1)


def _attn_prompt_kernel(sink_ref, x_ref, p_ref, cos_ref, sin_ref, cost_ref, sint_ref,
                        gpre_ref, gpost_ref, wqt_ref, wk_ref, wvt_ref, wg_ref,
                        wout_ref, wgate_ref, wple_ref,
                        rq_ref, rk_ref, rkd_ref, rv_ref, rstate_hbm,
                        xo_ref, kw_ref, vw_ref, ro_ref, rstate_out_hbm,
                        qt_scr, kext_scr, vt_scr, st_scr, pt_scr, inv_scr, ot_scr,
                        rst_in, rst_out, sem_in, sem_out, *, tq, n_seq, dec_len):
    nblk = tq // WINDOW
    half = ATTN_HEAD_DIM // 2
    t = pl.program_id(1)
    nt = pl.num_programs(1)

    @pl.when(t == 0)
    def _():
        kext_scr[0:WINDOW, :] = jnp.zeros((WINDOW, ATTN_KV_DIM), BF16)
        vt_scr[0] = jnp.zeros((ATTN_KV_DIM, WINDOW), BF16)

    per_phase = n_seq // STATE_PHASES
    step = pl.program_id(0) * nt + t
    n_steps = pl.num_programs(0) * nt
    seq0 = step * n_seq
    last_seq = n_steps * n_seq - 1

    def load_copy(seq, slot, i):
        return pltpu.make_async_copy(rstate_hbm.at[seq], rst_in.at[slot, i], sem_in.at[slot, i])

    def store_copy(seq, slot, i):
        return pltpu.make_async_copy(rst_out.at[slot, i], rstate_out_hbm.at[seq],
                                     sem_out.at[slot, i])

    @pl.when(step == 0)
    def _():
        rst_out[...] = jnp.zeros(rst_out.shape, F32)
        for slot in range(2):
            for i in range(per_phase):
                load_copy(slot * per_phase + i, slot, i).start(priority=STATE_DMA_PRIORITY)
                store_copy(slot * per_phase + i, slot, i).start(priority=STATE_DMA_PRIORITY)

    def state_phase(phase):
        slot = phase % 2
        first = seq0 + phase * per_phase
        for i in range(per_phase):
            load_copy(first + i, slot, i).wait()
            store_copy(first + i, slot, i).wait()
        _ret_state_update_tile(rq_ref, rk_ref, rkd_ref, rv_ref, rst_in.at[slot], rst_out.at[slot],
                               ro_ref, first_seq=phase * per_phase, n_seq=per_phase,
                               dec_len=dec_len)
        for i in range(per_phase):
            store_copy(first + i, slot, i).start(priority=STATE_DMA_PRIORITY)
            load_copy(jnp.minimum(first + i + 2 * per_phase, last_seq), slot, i).start(
                priority=STATE_DMA_PRIORITY)

    state_phase(0)

    emb = _dot(p_ref[...].astype(BF16), wple_ref[...])
    x = x_ref[...]
    h = _rms_norm(x, gpre_ref[...]).astype(BF16)

    k = _rope_attn(_dot(h, wk_ref[...]), cos_ref[...], sin_ref[...])
    kext_scr[WINDOW:, :] = k.astype(BF16)

    qt = _dot_nt(wqt_ref[...], h)
    cost = cost_ref[...]
    sint = sint_ref[...]
    pieces = []
    for hd in range(ATTN_HEADS):
        x1 = qt[hd * ATTN_HEAD_DIM:hd * ATTN_HEAD_DIM + half, :]
        x2 = qt[hd * ATTN_HEAD_DIM + half:(hd + 1) * ATTN_HEAD_DIM, :]
        pieces.append(x1 * cost - x2 * sint)
        pieces.append(x2 * cost + x1 * sint)
    qrot = (jnp.concatenate(pieces, axis=0) * (ATTN_HEAD_DIM ** -0.5 * LOG2_E)).astype(BF16)
    for jb in range(nblk):
        qt_scr[jb] = qrot[:, jb * WINDOW:(jb + 1) * WINDOW]

    vt = _dot_nt(wvt_ref[...], h)
    vtb = vt.astype(BF16)
    for jb in range(nblk):
        vt_scr[jb + 1] = vtb[:, jb * WINDOW:(jb + 1) * WINDOW]

    @pl.when(t == nt - 1)
    def _():
        kw_ref[...] = k[tq - WINDOW:, :].T
        vw_ref[...] = vt[:, tq - WINDOW:]

    state_phase(1)
    zero_rows = jnp.zeros((ATTN_HEAD_DIM, ATTN_GROUP * WINDOW), BF16)
    groups = [(jb, kh) for jb in range(nblk) for kh in range(ATTN_KV_HEADS)]

    for gidx, (jb, kh) in enumerate(groups):
        heads = [kh * ATTN_GROUP + gi for gi in range(ATTN_GROUP)]
        pair = kh // 2
        kb = kext_scr[jb * WINDOW:(jb + 2) * WINDOW, pair * LANES:(pair + 1) * LANES]
        qg = jnp.concatenate(
            [qt_scr[jb, hd * ATTN_HEAD_DIM:(hd + 1) * ATTN_HEAD_DIM, :] for hd in heads], axis=1)
        qg = jnp.concatenate([qg, zero_rows] if kh % 2 == 0 else [zero_rows, qg], axis=0)
        st_scr[gidx] = _dot(kb, qg)

    g = _dot(h, wg_ref[...])

    key_r = lax.broadcasted_iota(jnp.int32, (WINDOW, WINDOW), 0)
    query_c = lax.broadcasted_iota(jnp.int32, (WINDOW, WINDOW), 1)
    from_prev = key_r > query_c
    prev_w = from_prev.astype(BF16)
    cur_w = 1.0 - prev_w
    for gidx, (jb, kh) in enumerate(groups):
        for gi in range(ATTN_GROUP):
            hd = kh * ATTN_GROUP + gi
            cols = slice(gi * WINDOW, (gi + 1) * WINDOW)
            s_prev = st_scr[gidx, 0:WINDOW, cols]
            if jb == 0:
                s_prev = jnp.where(t > 0, s_prev, NEG_INF)
            s = jnp.where(from_prev, s_prev, st_scr[gidx, WINDOW:, cols])
            sink = sink_ref[hd] * LOG2_E
            m = jnp.maximum(jnp.max(s, axis=0, keepdims=True), sink)
            pr = jnp.exp2(s - m)
            den = jnp.sum(pr, axis=0, keepdims=True) + jnp.exp2(sink - m)
            prb = pr.astype(BF16)
            pt_scr[gidx, 0:WINDOW, cols] = prb * prev_w
            pt_scr[gidx, WINDOW:, cols] = prb * cur_w
            inv_scr[jb, hd:hd + 1, :] = 1.0 / den

    state_phase(2)

    for gidx, (jb, kh) in enumerate(groups):
        vband = jnp.concatenate([vt_scr[jb, kh * ATTN_HEAD_DIM:(kh + 1) * ATTN_HEAD_DIM, :],
                                 vt_scr[jb + 1, kh * ATTN_HEAD_DIM:(kh + 1) * ATTN_HEAD_DIM, :]],
                                axis=1)
        otg = _dot(vband, pt_scr[gidx])
        for gi in range(ATTN_GROUP):
            hd = kh * ATTN_GROUP + gi
            ot_scr[jb, hd * ATTN_HEAD_DIM:(hd + 1) * ATTN_HEAD_DIM, :] = (
                otg[:, gi * WINDOW:(gi + 1) * WINDOW] * inv_scr[jb, hd:hd + 1, :])

    parts = 2 if nblk % 2 == 0 else 1
    rows = _row_splits(tq, parts)
    ogs = []
    for r in rows:
        o = jnp.concatenate([ot_scr[jb].T for jb in range(r.start // WINDOW, r.stop // WINDOW)],
                            axis=0)
        ogs.append((o * _silu(g[r, :])).astype(BF16))
    _out_proj_and_residual(rows, ogs, x, emb, gpost_ref[...], wout_ref, wgate_ref, xo_ref,
                           after_out_proj=lambda: state_phase(3))

    kext_scr[0:WINDOW, :] = kext_scr[tq:tq + WINDOW, :]
    vt_scr[0] = vt_scr[nblk]

    @pl.when(step == n_steps - 1)
    def _():
        for slot in range(2):
            for i in range(per_phase):
                load_copy(last_seq, slot, i).wait()
                store_copy(last_seq, slot, i).wait()


def _const_spec(shape):
    nd = len(shape)
    return pl.BlockSpec(shape, lambda *_: (0,) * nd)


def _attn_prompt_layer(layer, x, p_all, tables, gpre, gpost, sinks, w_qt, w_k, w_vt, w_g,
                       w_out, w_gate, w_ple, ret_qkv, ret_state, dec_len):
    nb, seq, d = x.shape
    tq = min(512, seq)
    nblk = tq // WINDOW
    nt = seq // tq
    n_seq = ret_state.shape[0] // (nb * nt)
    assert n_seq * nb * nt == ret_state.shape[0] and n_seq % STATE_PHASES == 0, (
        ret_state.shape, nb, nt)
    ple = p_all.shape[-1]
    cos, sin, cost, sint = tables
    kernel = functools.partial(_attn_prompt_kernel, tq=tq, n_seq=n_seq, dec_len=dec_len)
    tok = lambda b, t: (b, t, 0)
    win_blk = lambda b, t: (b, 0, 0)
    stok = lambda b, t: (b * nt + t, 0)
    consts = (gpre, gpost, w_qt, w_k, w_vt, w_g, w_out, w_gate, w_ple)
    rq, rk, rkd, rv = ret_qkv
    st_shape = (2, n_seq // STATE_PHASES) + ret_state.shape[1:]
    return pl.pallas_call(
        kernel,
        grid=(nb, nt),
        in_specs=[
            pl.BlockSpec(memory_space=pltpu.SMEM),
            pl.BlockSpec((None, tq, d), tok),
            pl.BlockSpec((None, None, tq, ple), lambda b, t: (layer, b, t, 0)),
            pl.BlockSpec((tq, LANES), lambda b, t: (t, 0)),
            pl.BlockSpec((tq, LANES), lambda b, t: (t, 0)),
            pl.BlockSpec((ATTN_HEAD_DIM // 2, tq), lambda b, t: (0, t)),
            pl.BlockSpec((ATTN_HEAD_DIM // 2, tq), lambda b, t: (0, t)),
        ] + [_const_spec(c.shape) for c in consts] + [
            pl.BlockSpec((n_seq * dec_len, RET_QK_W), stok),
            pl.BlockSpec((n_seq * dec_len, RET_QK_W), stok),
            pl.BlockSpec((n_seq * dec_len, RET_QK_W), stok),
            pl.BlockSpec((n_seq * dec_len, RET_VW), stok),
            pl.BlockSpec(memory_space=pl.ANY),
        ],
        out_specs=[
            pl.BlockSpec((None, tq, d), tok),
            pl.BlockSpec((None, ATTN_KV_DIM, WINDOW), win_blk),
            pl.BlockSpec((None, ATTN_KV_DIM, WINDOW), win_blk),
            pl.BlockSpec((n_seq * dec_len, RET_VW), stok),
            pl.BlockSpec(memory_space=pl.ANY),
        ],
        out_shape=[
            jax.ShapeDtypeStruct(x.shape, F32),
            jax.ShapeDtypeStruct((nb, ATTN_KV_DIM, WINDOW), F32),
            jax.ShapeDtypeStruct((nb, ATTN_KV_DIM, WINDOW), F32),
            jax.ShapeDtypeStruct((rq.shape[0], RET_VW), F32),
            jax.ShapeDtypeStruct(ret_state.shape, F32),
        ],
        scratch_shapes=[
            pltpu.VMEM((nblk, ATTN_Q_DIM, WINDOW), BF16),
            pltpu.VMEM((tq + WINDOW, ATTN_KV_DIM), BF16),
            pltpu.VMEM((nblk + 1, ATTN_KV_DIM, WINDOW), BF16),
            pltpu.VMEM((nblk * ATTN_KV_HEADS, 2 * WINDOW, ATTN_GROUP * WINDOW), F32),
            pltpu.VMEM((nblk * ATTN_KV_HEADS, 2 * WINDOW, ATTN_GROUP * WINDOW), BF16),
            pltpu.VMEM((nblk, ATTN_HEADS, WINDOW), F32),
            pltpu.VMEM((nblk, ATTN_Q_DIM, WINDOW), F32),
            pltpu.VMEM(st_shape, F32),
            pltpu.VMEM(st_shape, F32),
            pltpu.SemaphoreType.DMA((2, n_seq // STATE_PHASES)),
            pltpu.SemaphoreType.DMA((2, n_seq // STATE_PHASES)),
        ],
        compiler_params=pltpu.CompilerParams(
            dimension_semantics=("arbitrary", "arbitrary"),
            vmem_limit_bytes=ATTN_PROMPT_VMEM_LIMIT_BYTES),
        name="attn_prompt_layer",
    )(sinks, x, p_all, cos, sin, cost, sint, *consts, rq, rk, rkd, rv, ret_state)


def _attn_sample_kernel(sink_ref, x_ref, p_ref, cos_ref, sin_ref, cost_ref, sint_ref,
                        gpre_ref, gpost_ref, wq_ref, wkt_ref, wvt_ref, wg_ref,
                        wout_ref, wgate_ref, wple_ref, kc_ref, vc_ref,
                        xo_ref, ko_ref, vo_ref,
                        q_scr, knt_scr, vnt_scr, o_scr, *, n_seq, dec_len):
    half = ATTN_HEAD_DIM // 2
    x = x_ref[...]
    h = _rms_norm(x, gpre_ref[...]).astype(BF16)
    q_scr[...] = (_rope_attn(_dot(h, wq_ref[...]), cos_ref[...], sin_ref[...])
                  * (ATTN_HEAD_DIM ** -0.5 * LOG2_E))
    knt = _dot_nt(wkt_ref[...], h)
    cost = cost_ref[...]
    sint = sint_ref[...]
    pieces = []
    for kh in range(ATTN_KV_HEADS):
        x1 = knt[kh * ATTN_HEAD_DIM:kh * ATTN_HEAD_DIM + half, :]
        x2 = knt[kh * ATTN_HEAD_DIM + half:(kh + 1) * ATTN_HEAD_DIM, :]
        pieces.append(x1 * cost - x2 * sint)
        pieces.append(x2 * cost + x1 * sint)
    knt_scr[...] = jnp.concatenate(pieces, axis=0)
    vnt_scr[...] = _dot_nt(wvt_ref[...], h)

    rows = 2 * dec_len
    tok = lax.broadcasted_iota(jnp.int32, (rows, 2 * WINDOW), 0) % dec_len
    col = lax.broadcasted_iota(jnp.int32, (rows, 2 * WINDOW), 1)
    upper_row = lax.broadcasted_iota(jnp.int32, (rows, 1), 0) >= dec_len
    lane = lax.broadcasted_iota(jnp.int32, (ATTN_HEAD_DIM, WINDOW), 1)
    keep_old = lane < WINDOW - dec_len
    zeros = jnp.zeros((ATTN_HEAD_DIM, 2 * WINDOW), BF16)

    def select_head(t, parity):
        return jnp.concatenate([t, zeros] if parity == 0 else [zeros, t], axis=0)

    def scores(s):
        tok_rows = slice(s * dec_len, (s + 1) * dec_len)
        new_col = col - (WINDOW + s * dec_len)
        mask = ((col < WINDOW) & (col > tok)) | ((new_col >= 0) & (new_col <= tok))
        new_shift = (WINDOW - dec_len - s * dec_len) % WINDOW
        out = []
        for kh in range(ATTN_KV_HEADS):
            hrows = slice(kh * ATTN_HEAD_DIM, (kh + 1) * ATTN_HEAD_DIM)
            kc = kc_ref[s, kh]
            vc = vc_ref[s, kh]
            kn = knt_scr[hrows, :]
            vn = vnt_scr[hrows, :]
            kn_at_end = pltpu.roll(kn, new_shift, axis=1) if new_shift else kn
            vn_at_end = pltpu.roll(vn, new_shift, axis=1) if new_shift else vn
            ko_ref[s, kh] = jnp.where(keep_old, pltpu.roll(kc, WINDOW - dec_len, axis=1), kn_at_end)
            vo_ref[s, kh] = jnp.where(keep_old, pltpu.roll(vc, WINDOW - dec_len, axis=1), vn_at_end)

            kt = jnp.concatenate([kc, kn], axis=1).astype(BF16)
            vt = jnp.concatenate([vc, vn], axis=1).astype(BF16)
            qp = jnp.concatenate([q_scr[tok_rows, (2 * kh + c) * LANES:(2 * kh + c + 1) * LANES]
                                  for c in range(2)], axis=0).astype(BF16)
            sc = [jnp.where(mask, _dot(qp, select_head(kt, parity)), NEG_INF)
                  for parity in range(2)]
            out.append((sc, vt))
        return out

    def values(s, per_head):
        tok_rows = slice(s * dec_len, (s + 1) * dec_len)
        probs = []
        for kh, (scs, vt) in enumerate(per_head):
            for parity, sc in enumerate(scs):
                hd_lo, hd_hi = 4 * kh + parity, 4 * kh + 2 + parity
                sink = jnp.where(upper_row, sink_ref[hd_hi], sink_ref[hd_lo]) * LOG2_E
                m = jnp.maximum(jnp.max(sc, axis=-1, keepdims=True), sink)
                pr = jnp.exp2(sc - m)
                den = jnp.sum(pr, axis=-1, keepdims=True) + jnp.exp2(sink - m)
                probs.append((pr / den).astype(BF16))
        for kh, (scs, vt) in enumerate(per_head):
            o_pair = (_dot_nt(probs[2 * kh], select_head(vt, 0))
                      + _dot_nt(probs[2 * kh + 1], select_head(vt, 1)))
            for c in range(2):
                o_scr[tok_rows, (2 * kh + c) * LANES:(2 * kh + c + 1) * LANES] = (
                    o_pair[c * dec_len:(c + 1) * dec_len, :])

    group = min(SAMPLE_ATTN_PIPELINE_GROUP, n_seq)
    pending = [scores(s) for s in range(group)]
    for s0 in range(0, n_seq, group):
        upcoming = [scores(s) for s in range(s0 + group, min(s0 + 2 * group, n_seq))]
        for i, per_head in enumerate(pending):
            values(s0 + i, per_head)
        pending = upcoming

    g = _dot(h, wg_ref[...])
    emb = _dot(p_ref[...].astype(BF16), wple_ref[...])
    og = (o_scr[...] * _silu(g)).astype(BF16)
    _out_proj_and_residual(_row_splits(x.shape[0], 1), [og], x, emb, gpost_ref[...],
                           wout_ref, wgate_ref, xo_ref)


def _attn_sample_layer(layer, x, p_all, tables, gpre, gpost, sinks, w_q, w_kt, w_vt, w_g,
                       w_out, w_gate, w_ple, cache_kt, cache_vt, dec_len):
    n_tok, d = x.shape
    n_all = cache_kt.shape[0]
    n_seq = WINDOW // dec_len
    tq = n_seq * dec_len
    assert tq == WINDOW and n_all % n_seq == 0, (dec_len, n_all)
    ple = p_all.shape[-1]
    cos, sin, cost, sint = tables
    kernel = functools.partial(_attn_sample_kernel, n_seq=n_seq, dec_len=dec_len)
    tok = lambda i: (i, 0)
    cache_shape = (n_seq, ATTN_KV_HEADS, ATTN_HEAD_DIM, WINDOW)
    cache_blk = lambda i: (i, 0, 0, 0)
    consts = (cos, sin, cost, sint, gpre, gpost, w_q, w_kt, w_vt, w_g, w_out, w_gate, w_ple)
    return pl.pallas_call(
        kernel,
        grid=(n_all // n_seq,),
        in_specs=[
            pl.BlockSpec(memory_space=pltpu.SMEM),
            pl.BlockSpec((tq, d), tok),
            pl.BlockSpec((None, tq, ple), lambda i: (layer, i, 0)),
        ] + [_const_spec(c.shape) for c in consts] + [
            pl.BlockSpec(cache_shape, cache_blk),
            pl.BlockSpec(cache_shape, cache_blk),
        ],
        out_specs=[
            pl.BlockSpec((tq, d), tok),
            pl.BlockSpec(cache_shape, cache_blk),
            pl.BlockSpec(cache_shape, cache_blk),
        ],
        out_shape=[
            jax.ShapeDtypeStruct(x.shape, F32),
            jax.ShapeDtypeStruct(cache_kt.shape, F32),
            jax.ShapeDtypeStruct(cache_vt.shape, F32),
        ],
        scratch_shapes=[
            pltpu.VMEM((tq, ATTN_Q_DIM), F32),
            pltpu.VMEM((ATTN_KV_DIM, tq), F32),
            pltpu.VMEM((ATTN_KV_DIM, tq), F32),
            pltpu.VMEM((tq, ATTN_Q_DIM), F32),
        ],
        compiler_params=pltpu.CompilerParams(
            dimension_semantics=("arbitrary",),
            vmem_limit_bytes=VMEM_LIMIT_BYTES),
        name="attn_sample_layer",
    )(sinks, x, p_all, *consts, cache_kt, cache_vt)


RET_GATE_COL0 = 2 * RET_QK_W + RET_VW


def _ret_project(h, cos, sin, win_ref, chunk_len):
    q0, k0, v0, g0 = 0, RET_QK_W, 2 * RET_QK_W, RET_GATE_COL0
    q = _rope_ret(_dot(h, win_ref[:, q0:k0]), cos, sin)
    k = _rope_ret(_dot(h, win_ref[:, k0:v0]), cos, sin) * (RET_QK_DIM ** -0.5)
    v = _dot(h, win_ref[:, v0:g0])
    n = h.shape[0]
    idx = (lax.broadcasted_iota(jnp.int32, (n, 1), 0) % chunk_len).astype(F32)
    kd = jnp.concatenate(
        [k[:, hd * RET_QK_DIM:(hd + 1) * RET_QK_DIM]
         * jnp.exp((chunk_len - 1.0 - idx) * RET_LOG_GAMMA[hd]) for hd in range(RET_HEADS)], axis=1)
    return q, k, kd, v


def _ret_decay_mask(n, chunk_len, hd):
    ri = lax.broadcasted_iota(jnp.int32, (n, n), 0)
    ci = lax.broadcasted_iota(jnp.int32, (n, n), 1)
    diff = ri - ci
    ok = (diff >= 0) & ((ri // chunk_len) == (ci // chunk_len))
    return jnp.where(ok, jnp.exp(jnp.maximum(diff, 0).astype(F32) * RET_LOG_GAMMA[hd]), 0.0)


def _ret_gated(o, g):
    outs = []
    for hd in range(RET_HEADS):
        oh = o[:, hd * RET_V_DIM:(hd + 1) * RET_V_DIM]
        mu = jnp.mean(oh, axis=-1, keepdims=True)
        var = jnp.mean(jnp.square(oh - mu), axis=-1, keepdims=True)
        outs.append((oh - mu) * lax.rsqrt(var + NORM_EPS))
    return (jnp.concatenate(outs, axis=1) * _silu(g)).astype(BF16)


def _ret_prompt_kernel(x_ref, p_ref, cos_ref, sin_ref, gpre_ref, gpost_ref,
                       win_ref, wout_ref, wgate_ref, wple_ref,
                       xo_ref, st_ref,
                       q_scr, k_scr, kd_scr, v_scr, o_scr, g_scr, dm_scr, *, tq):
    b = pl.program_id(0)
    t = pl.program_id(1)

    @pl.when((b == 0) & (t == 0))
    def _():
        for hd in range(RET_HEADS):
            dm_scr[hd] = _ret_decay_mask(RET_CHUNK, RET_CHUNK, hd)

    @pl.when(t == 0)
    def _():
        st_ref[...] = jnp.zeros(st_ref.shape, F32)

    emb = _dot(p_ref[...].astype(BF16), wple_ref[...])
    x = x_ref[...]
    h = _rms_norm(x, gpre_ref[...]).astype(BF16)
    q, k, kd, v = _ret_project(h, cos_ref[...], sin_ref[...], win_ref, RET_CHUNK)
    q_scr[...] = q.astype(BF16)
    k_scr[...] = k.astype(BF16)
    kd_scr[...] = kd.astype(BF16)
    v_scr[...] = v.astype(BF16)

    idx = lax.broadcasted_iota(jnp.int32, (RET_CHUNK, 1), 0).astype(F32)
    heads = range(RET_HEADS)
    qs = [slice(hd * RET_QK_DIM, (hd + 1) * RET_QK_DIM) for hd in heads]
    vs = [slice(hd * RET_V_DIM, (hd + 1) * RET_V_DIM) for hd in heads]
    chunks = _row_splits(tq, tq // RET_CHUNK)
    gate_cols = iter(_row_splits(RET_VW, 2 * len(chunks)))

    def gate_piece():
        c = next(gate_cols)
        g_scr[:, c] = _dot(h, win_ref[:, RET_GATE_COL0 + c.start:RET_GATE_COL0 + c.stop])

    for rows in chunks:
        scores = [_dot_nt(q_scr[rows, qs[hd]], k_scr[rows, qs[hd]]) for hd in heads]
        cross = [_dot(q_scr[rows, qs[hd]], st_ref[hd].astype(BF16)) for hd in heads]
        gate_piece()
        for hd in heads:
            st_ref[hd] = (math.exp(RET_CHUNK * RET_LOG_GAMMA[hd]) * st_ref[hd]
                          + _dot_tn(kd_scr[rows, qs[hd]], v_scr[rows, vs[hd]]))
        for hd in heads:
            inner = _dot((scores[hd] * dm_scr[hd]).astype(BF16), v_scr[rows, vs[hd]])
            o_scr[rows, vs[hd]] = inner + cross[hd] * jnp.exp((idx + 1.0) * RET_LOG_GAMMA[hd])
        gate_piece()

    ogs = [_ret_gated(o_scr[r, :], g_scr[r, :]) for r in chunks]
    _out_proj_and_residual(chunks, ogs, x, emb, gpost_ref[...], wout_ref, wgate_ref, xo_ref)


def _ret_prompt_layer(layer, x, p_all, cos, sin, gpre, gpost, w_in, w_out, w_gate, w_ple):
    nb, seq, d = x.shape
    tq = min(512, seq)
    ple = p_all.shape[-1]
    kernel = functools.partial(_ret_prompt_kernel, tq=tq)
    tok = lambda b, t: (b, t, 0)
    return pl.pallas_call(
        kernel,
        grid=(nb, seq // tq),
        in_specs=[
            pl.BlockSpec((None, tq, d), tok),
            pl.BlockSpec((None, None, tq, ple), lambda b, t: (layer, b, t, 0)),
            pl.BlockSpec((tq, LANES), lambda b, t: (t, 0)),
            pl.BlockSpec((tq, LANES), lambda b, t: (t, 0)),
            _const_spec(gpre.shape), _const_spec(gpost.shape),
            _const_spec(w_in.shape), _const_spec(w_out.shape),
            _const_spec(w_gate.shape), _const_spec(w_ple.shape),
        ],
        out_specs=[
            pl.BlockSpec((None, tq, d), tok),
            pl.BlockSpec((None, RET_HEADS, RET_QK_DIM, RET_V_DIM), lambda b, t: (b, 0, 0, 0)),
        ],
        out_shape=[
            jax.ShapeDtypeStruct(x.shape, F32),
            jax.ShapeDtypeStruct((nb, RET_HEADS, RET_QK_DIM, RET_V_DIM), F32),
        ],
        scratch_shapes=[
            pltpu.VMEM((tq, RET_QK_W), BF16),
            pltpu.VMEM((tq, RET_QK_W), BF16),
            pltpu.VMEM((tq, RET_QK_W), BF16),
            pltpu.VMEM((tq, RET_VW), BF16),
            pltpu.VMEM((tq, RET_VW), F32),
            pltpu.VMEM((tq, RET_VW), F32),
            pltpu.VMEM((RET_HEADS, RET_CHUNK, RET_CHUNK), F32),
        ],
        compiler_params=pltpu.CompilerParams(
            dimension_semantics=("arbitrary", "arbitrary"),
            vmem_limit_bytes=VMEM_LIMIT_BYTES),
        name="ret_prompt_layer",
    )(x, p_all, cos, sin, gpre, gpost, w_in, w_out, w_gate, w_ple)


def _ret_sample_proj_kernel(x_ref, cos_ref, sin_ref, gpre_ref, win_ref,
                            q_ref, k_ref, kd_ref, v_ref, g_ref, *, dec_len):
    h = _rms_norm(x_ref[...], gpre_ref[...]).astype(BF16)
    q, k, kd, v = _ret_project(h, cos_ref[...], sin_ref[...], win_ref, dec_len)
    q_ref[...] = q
    k_ref[...] = k
    kd_ref[...] = kd
    v_ref[...] = v
    g_ref[...] = _dot(h, win_ref[:, RET_GATE_COL0:])


def _ret_sample_proj(x, cos, sin, gpre, w_in, dec_len):
    n_tok, d = x.shape
    tq = min(512, n_tok)
    kernel = functools.partial(_ret_sample_proj_kernel, dec_len=dec_len)
    tok = lambda i: (i, 0)
    widths = (RET_QK_W, RET_QK_W, RET_QK_W, RET_VW, RET_VW)
    return pl.pallas_call(
        kernel,
        grid=(n_tok // tq,),
        in_specs=[
            pl.BlockSpec((tq, d), tok),
            _const_spec((tq, LANES)), _const_spec((tq, LANES)),
            _const_spec(gpre.shape), _const_spec(w_in.shape),
        ],
        out_specs=[pl.BlockSpec((tq, w), tok) for w in widths],
        out_shape=[jax.ShapeDtypeStruct((n_tok, w), F32) for w in widths],
        compiler_params=pltpu.CompilerParams(
            dimension_semantics=("arbitrary",),
            vmem_limit_bytes=VMEM_LIMIT_BYTES),
        name="ret_sample_proj",
    )(x, cos, sin, gpre, w_in)


def _ret_state_update_tile(q_ref, k_ref, kd_ref, v_ref, st_in, st_out, o_ref, *,
                           first_seq, n_seq, dec_len):
    pad_rows = 2 * dec_len
    idx = lax.broadcasted_iota(jnp.int32, (pad_rows, 1), 0).astype(F32)

    def tok_rows(s):
        return slice((first_seq + s) * dec_len, (first_seq + s + 1) * dec_len)

    def padded(ref, s, cols):
        val = ref[tok_rows(s), cols]
        return jnp.concatenate([val, jnp.zeros_like(val)], axis=0).astype(BF16)

    units = [(s, hd) for s in range(n_seq) for hd in range(RET_HEADS)]
    qs = [slice(hd * RET_QK_DIM, (hd + 1) * RET_QK_DIM) for hd in range(RET_HEADS)]
    vs = [slice(hd * RET_V_DIM, (hd + 1) * RET_V_DIM) for hd in range(RET_HEADS)]
    qh = {(s, hd): padded(q_ref, s, qs[hd]) for s, hd in units}
    vh = {(s, hd): padded(v_ref, s, vs[hd]) for s, hd in units}
    scores = {(s, hd): _dot_nt(qh[s, hd], padded(k_ref, s, qs[hd])) for s, hd in units}
    cross = {(s, hd): _dot(qh[s, hd], st_in[s, hd].astype(BF16)) for s, hd in units}
    for s, hd in units:
        st_out[s, hd] = (math.exp(dec_len * RET_LOG_GAMMA[hd]) * st_in[s, hd]
                         + _dot_tn(padded(kd_ref, s, qs[hd]), vh[s, hd]))
    masks = [_ret_decay_mask(pad_rows, dec_len, hd) for hd in range(RET_HEADS)]
    for s, hd in units:
        inner = _dot((scores[s, hd] * masks[hd]).astype(BF16), vh[s, hd])
        out = inner + cross[s, hd] * jnp.exp((idx + 1.0) * RET_LOG_GAMMA[hd])
        o_ref[tok_rows(s), vs[hd]] = out[0:dec_len, :]


def _ret_sample_finish_kernel(o_ref, g_ref, x_ref, p_ref, gpost_ref, wout_ref, wgate_ref, wple_ref,
                              xo_ref):
    emb = _dot(p_ref[...].astype(BF16), wple_ref[...])
    x = x_ref[...]
    rows = _row_splits(x.shape[0], 2)
    ogs = [_ret_gated(o_ref[r, :], g_ref[r, :]) for r in rows]
    _out_proj_and_residual(rows, ogs, x, emb, gpost_ref[...], wout_ref, wgate_ref, xo_ref)


def _ret_sample_finish(layer, o, g, x, p_all, gpost, w_out, w_gate, w_ple):
    n_tok, d = x.shape
    tq = min(512, n_tok)
    ple = p_all.shape[-1]
    tok = lambda i: (i, 0)
    return pl.pallas_call(
        _ret_sample_finish_kernel,
        grid=(n_tok // tq,),
        in_specs=[
            pl.BlockSpec((tq, RET_VW), tok), pl.BlockSpec((tq, RET_VW), tok),
            pl.BlockSpec((tq, d), tok),
            pl.BlockSpec((None, tq, ple), lambda i: (layer, i, 0)),
            _const_spec(gpost.shape), _const_spec(w_out.shape),
            _const_spec(w_gate.shape), _const_spec(w_ple.shape),
        ],
        out_specs=pl.BlockSpec((tq, d), tok),
        out_shape=jax.ShapeDtypeStruct(x.shape, F32),
        compiler_params=pltpu.CompilerParams(
            dimension_semantics=("arbitrary",),
            vmem_limit_bytes=VMEM_LIMIT_BYTES),
        name="ret_sample_finish",
    )(o, g, x, p_all, gpost, w_out, w_gate, w_ple)


def _rope_tables(pos, head_dim):
    half = head_dim // 2
    inv = ROPE_THETA ** (-np.arange(half, dtype=np.float64) / half)
    ang = np.asarray(pos, np.float64)[:, None] * inv[None, :]
    return np.cos(ang).astype(np.float32), np.sin(ang).astype(np.float32)


def _attn_tables(pos):
    cos, sin = _rope_tables(pos, ATTN_HEAD_DIM)
    reps = LANES // ATTN_HEAD_DIM
    return (np.tile(np.concatenate([cos, cos], axis=1), (1, reps)),
            np.tile(np.concatenate([-sin, sin], axis=1), (1, reps)))


def kernel(x_prompt, x_sample, cache_k_win, cache_v_win, state_ret, p_prompt, p_sample, pre_norm,
           post_norm, w_in_attn, attn_sinks, w_out_attn, w_in_ret, w_out_ret, w_ple, w_ple_gate):
    depth = p_prompt.shape[0]
    nb, seq, d = x_prompt.shape
    n_dec, dec_len, _ = x_sample.shape
    n_stok = n_dec * dec_len
    assert seq % 512 == 0 or seq in (128, 256), seq

    pos_p = np.arange(seq)
    pos_s = PAST_LEN + np.arange(dec_len)
    cos_ap, sin_ap = _attn_tables(pos_p)
    cos_apt, sin_apt = (np.ascontiguousarray(a.T) for a in _rope_tables(pos_p, ATTN_HEAD_DIM))
    cos_as, sin_as = _attn_tables(pos_s)
    cos_rp, sin_rp = _rope_tables(pos_p, RET_QK_DIM)
    cos_rs, sin_rs = _rope_tables(pos_s, RET_QK_DIM)
    attn_s_tile = WINDOW // dec_len
    ret_s_tile = min(512, n_stok) // dec_len
    cos_as, sin_as = (np.tile(a, (attn_s_tile, 1)) for a in (cos_as, sin_as))
    cos_ast, sin_ast = (np.ascontiguousarray(np.tile(a, (attn_s_tile, 1)).T)
                        for a in _rope_tables(pos_s, ATTN_HEAD_DIM))
    cos_rs, sin_rs = (np.tile(a, (ret_s_tile, 1)) for a in (cos_rs, sin_rs))
    (cos_ap, sin_ap, cos_apt, sin_apt, cos_as, sin_as, cos_ast, sin_ast,
     cos_rp, sin_rp, cos_rs, sin_rs) = (
        jnp.asarray(a) for a in
        (cos_ap, sin_ap, cos_apt, sin_apt, cos_as, sin_as, cos_ast, sin_ast,
         cos_rp, sin_rp, cos_rs, sin_rs))

    assert depth == 2, depth
    xs = x_sample.reshape(n_stok, d)
    p_s = p_sample.reshape(depth, n_stok, p_sample.shape[-1])
    bf = lambda w: w.astype(BF16)
    gpre = [pre_norm[i][None, :] for i in range(depth)]
    gpost = [post_norm[i][None, :] for i in range(depth)]
    w_gate = [bf(w_ple_gate[i]) for i in range(depth)]
    w_pl = [bf(w_ple[i]) for i in range(depth)]

    w_in = bf(w_in_attn[0])
    w_out = bf(w_out_attn[0])
    sinks = attn_sinks[0]
    k0, v0, g0 = ATTN_Q_DIM, ATTN_Q_DIM + ATTN_KV_DIM, ATTN_Q_DIM + 2 * ATTN_KV_DIM
    w_q, w_k, w_v, w_g = w_in[:, :k0], w_in[:, k0:v0], w_in[:, v0:g0], w_in[:, g0:]
    to_t = lambda a: a.transpose(0, 2, 3, 1)
    from_t = lambda a: a.transpose(0, 3, 1, 2)
    xs, knt, vnt = _attn_sample_layer(
        0, xs, p_s, (cos_as, sin_as, cos_ast, sin_ast), gpre[0], gpost[0], sinks,
        w_q, w_k.T, w_v.T, w_g, w_out, w_gate[0], w_pl[0],
        to_t(cache_k_win[0]), to_t(cache_v_win[0]), dec_len)

    w_in_r = bf(w_in_ret[0])
    w_out_r = bf(w_out_ret[0])
    rq, rk, rkd, rv, rg = _ret_sample_proj(xs, cos_rs, sin_rs, gpre[1], w_in_r, dec_len)
    xp, kpt, vpt, ro, ret_state_sample = _attn_prompt_layer(
        0, x_prompt, p_prompt, (cos_ap, sin_ap, cos_apt, sin_apt), gpre[0], gpost[0], sinks,
        w_q.T, w_k, w_v.T, w_g, w_out, w_gate[0], w_pl[0],
        (rq, rk, rkd, rv), state_ret[0], dec_len)
    head_shape = (nb, ATTN_KV_HEADS, ATTN_HEAD_DIM, WINDOW)

    xp, ret_state_prompt = _ret_prompt_layer(1, xp, p_prompt, cos_rp, sin_rp, gpre[1], gpost[1],
                                             w_in_r, w_out_r, w_gate[1], w_pl[1])
    xs = _ret_sample_finish(1, ro, rg, xs, p_s, gpost[1], w_out_r, w_gate[1], w_pl[1])
    return (xp, xs.reshape(n_dec, dec_len, d),
            from_t(kpt.reshape(head_shape))[None], from_t(vpt.reshape(head_shape))[None],
            from_t(knt)[None], from_t(vnt)[None], ret_state_prompt[None], ret_state_sample[None])
```

```python
import functools
import math

import numpy as np
import jax
import jax.numpy as jnp
from jax import lax
from jax.experimental import pallas as pl
from jax.experimental.pallas import tpu as pltpu

F32 = jnp.float32
BF16 = jnp.bfloat16

PAST_LEN = 16384
ROPE_THETA = 10000.0
NORM_EPS = 1e-6
NEG_INF = -1e30
WINDOW = 128
ATTN_HEADS = 16
ATTN_KV_HEADS = 4
ATTN_GROUP = ATTN_HEADS // ATTN_KV_HEADS
ATTN_HEAD_DIM = 64
ATTN_Q_DIM = ATTN_HEADS * ATTN_HEAD_DIM
ATTN_KV_DIM = ATTN_KV_HEADS * ATTN_HEAD_DIM
RET_HEADS = 4
RET_QK_DIM = 256
RET_V_DIM = 512
RET_QK_W = RET_HEADS * RET_QK_DIM
RET_VW = RET_HEADS * RET_V_DIM
RET_CHUNK = 256
LANES = 128

LOG2_E = math.log2(math.e)
RET_LOG_GAMMA = tuple(math.log1p(-(2.0 ** (-5.0 - h))) for h in range(RET_HEADS))

VMEM_LIMIT_BYTES = 56 * 1024 * 1024
ATTN_PROMPT_VMEM_LIMIT_BYTES = 50 * 1024 * 1024
STATE_PHASES = 4
RET_SAMPLE_TILE = 256
SAMPLE_ATTN_PIPELINE_GROUP = 8


def _dot(a, b):
    return jnp.dot(a, b, preferred_element_type=F32)


def _dot_nt(a, b):
    return lax.dot_general(a, b, (((1,), (1,)), ((), ())), preferred_element_type=F32)


def _dot_tn(a, b):
    return lax.dot_general(a, b, (((0,), (0,)), ((), ())), preferred_element_type=F32)


def _rms_norm(x, g):
    return x * lax.rsqrt(jnp.mean(x * x, axis=-1, keepdims=True) + NORM_EPS) * g


def _silu(g):
    return g * jax.nn.sigmoid(g)


def _row_splits(n, parts):
    step = n // parts
    return [slice(i * step, (i + 1) * step) for i in range(parts)]


def _out_proj_and_residual(rows, ogs, x, emb, gpost, wout_ref, wgate_ref, xo_ref,
                           after_out_proj=None):
    ys = [_dot(og, wout_ref[...]) for og in ogs]
    if after_out_proj is not None:
        after_out_proj()
    x1s = [x[r, :] + _rms_norm(y, gpost) for r, y in zip(rows, ys)]
    gates = [_dot(x1.astype(BF16), wgate_ref[...]) for x1 in x1s]
    for r, x1, gate in zip(rows, x1s, gates):
        xo_ref[r, :] = x1 + jax.nn.sigmoid(gate) * emb[r, :]


def _rope_attn(x, cos, sin_signed):
    lane = lax.broadcasted_iota(jnp.int32, (1, LANES), 1)
    first_half = (lane % ATTN_HEAD_DIM) < (ATTN_HEAD_DIM // 2)
    outs = []
    for j in range(x.shape[1] // LANES):
        xj = x[:, j * LANES:(j + 1) * LANES]
        fwd = pltpu.roll(xj, LANES - ATTN_HEAD_DIM // 2, axis=1)
        bwd = pltpu.roll(xj, ATTN_HEAD_DIM // 2, axis=1)
        outs.append(xj * cos + jnp.where(first_half, fwd, bwd) * sin_signed)
    return jnp.concatenate(outs, axis=1)


def _rope_ret(x, cos, sin):
    half = RET_QK_DIM // 2
    outs = []
    for h in range(x.shape[1] // RET_QK_DIM):
        x1 = x[:, h * RET_QK_DIM:h * RET_QK_DIM + half]
        x2 = x[:, h * RET_QK_DIM + half:(h + 1) * RET_QK_DIM]
        outs.append(x1 * cos - x2 * sin)
        outs.append(x2 * cos + x1 * sin)
    return jnp.concatenate(outs, axis=1)


def _attn_prompt_kernel(sink_ref, x_ref, p_ref, cos_ref, sin_ref, cost_ref, sint_ref,
                        gpre_ref, gpost_ref, wqt_ref, wk_ref, wvt_ref, wg_ref,
                        wout_ref, wgate_ref, wple_ref,
                        rq_ref, rk_ref, rkd_ref, rv_ref, rstate_hbm,
                        xo_ref, kw_ref, vw_ref, ro_ref, rstate_out_hbm,
                        qt_scr, kext_scr, vt_scr, st_scr, pt_scr, inv_scr, ot_scr,
                        rst_in, rst_out, sem_in, sem_out, *, tq, n_seq, dec_len):
    nblk = tq // WINDOW
    half = ATTN_HEAD_DIM // 2
    t = pl.program_id(1)
    nt = pl.num_programs(1)

    @pl.when(t == 0)
    def _():
        kext_scr[0:WINDOW, :] = jnp.zeros((WINDOW, ATTN_KV_DIM), BF16)
        vt_scr[0] = jnp.zeros((ATTN_KV_DIM, WINDOW), BF16)

    per_phase = n_seq // STATE_PHASES
    step = pl.program_id(0) * nt + t
    n_steps = pl.num_programs(0) * nt
    seq0 = step * n_seq
    last_seq = n_steps * n_seq - 1

    def load_copy(seq, slot, i):
        return pltpu.make_async_copy(rstate_hbm.at[seq], rst_in.at[slot, i], sem_in.at[slot, i])

    def store_copy(seq, slot, i):
        return pltpu.make_async_copy(rst_out.at[slot, i], rstate_out_hbm.at[seq],
                                     sem_out.at[slot, i])

    @pl.when(step == 0)
    def _():
        rst_out[...] = jnp.zeros(rst_out.shape, F32)
        for slot in range(2):
            for i in range(per_phase):
                load_copy(slot * per_phase + i, slot, i).start()
                store_copy(slot * per_phase + i, slot, i).start()

    def state_phase(phase):
        slot = phase % 2
        first = seq0 + phase * per_phase
        for i in range(per_phase):
            load_copy(first + i, slot, i).wait()
            store_copy(first + i, slot, i).wait()
        _ret_state_update_tile(rq_ref, rk_ref, rkd_ref, rv_ref, rst_in.at[slot], rst_out.at[slot],
                               ro_ref, first_seq=phase * per_phase, n_seq=per_phase,
                               dec_len=dec_len)
        for i in range(per_phase):
            store_copy(first + i, slot, i).start()
            load_copy(jnp.minimum(first + i + 2 * per_phase, last_seq), slot, i).start()

    state_phase(0)

    emb = _dot(p_ref[...].astype(BF16), wple_ref[...])
    x = x_ref[...]
    h = _rms_norm(x, gpre_ref[...]).astype(BF16)

    k = _rope_attn(_dot(h, wk_ref[...]), cos_ref[...], sin_ref[...])
    kext_scr[WINDOW:, :] = k.astype(BF16)

    qt = _dot_nt(wqt_ref[...], h)
    cost = cost_ref[...]
    sint = sint_ref[...]
    pieces = []
    for hd in range(ATTN_HEADS):
        x1 = qt[hd * ATTN_HEAD_DIM:hd * ATTN_HEAD_DIM + half, :]
        x2 = qt[hd * ATTN_HEAD_DIM + half:(hd + 1) * ATTN_HEAD_DIM, :]
        pieces.append(x1 * cost - x2 * sint)
        pieces.append(x2 * cost + x1 * sint)
    qrot = (jnp.concatenate(pieces, axis=0) * (ATTN_HEAD_DIM ** -0.5 * LOG2_E)).astype(BF16)
    for jb in range(nblk):
        qt_scr[jb] = qrot[:, jb * WINDOW:(jb + 1) * WINDOW]

    vt = _dot_nt(wvt_ref[...], h)
    vtb = vt.astype(BF16)
    for jb in range(nblk):
        vt_scr[jb + 1] = vtb[:, jb * WINDOW:(jb + 1) * WINDOW]

    @pl.when(t == nt - 1)
    def _():
        kw_ref[...] = k[tq - WINDOW:, :].T
        vw_ref[...] = vt[:, tq - WINDOW:]

    state_phase(1)
    zero_rows = jnp.zeros((ATTN_HEAD_DIM, ATTN_GROUP * WINDOW), BF16)
    groups = [(jb, kh) for jb in range(nblk) for kh in range(ATTN_KV_HEADS)]

    for gidx, (jb, kh) in enumerate(groups):
        heads = [kh * ATTN_GROUP + gi for gi in range(ATTN_GROUP)]
        pair = kh // 2
        kb = kext_scr[jb * WINDOW:(jb + 2) * WINDOW, pair * LANES:(pair + 1) * LANES]
        qg = jnp.concatenate(
            [qt_scr[jb, hd * ATTN_HEAD_DIM:(hd + 1) * ATTN_HEAD_DIM, :] for hd in heads], axis=1)
        qg = jnp.concatenate([qg, zero_rows] if kh % 2 == 0 else [zero_rows, qg], axis=0)
        st_scr[gidx] = _dot(kb, qg)

    g = _dot(h, wg_ref[...])

    key_r = lax.broadcasted_iota(jnp.int32, (WINDOW, WINDOW), 0)
    query_c = lax.broadcasted_iota(jnp.int32, (WINDOW, WINDOW), 1)
    from_prev = key_r > query_c
    prev_w = from_prev.astype(BF16)
    cur_w = 1.0 - prev_w
    for gidx, (jb, kh) in enumerate(groups):
        for gi in range(ATTN_GROUP):
            hd = kh * ATTN_GROUP + gi
            cols = slice(gi * WINDOW, (gi + 1) * WINDOW)
            s_prev = st_scr[gidx, 0:WINDOW, cols]
            if jb == 0:
                s_prev = jnp.where(t > 0, s_prev, NEG_INF)
            s = jnp.where(from_prev, s_prev, st_scr[gidx, WINDOW:, cols])
            sink = sink_ref[hd] * LOG2_E
            m = jnp.maximum(jnp.max(s, axis=0, keepdims=True), sink)
            pr = jnp.exp2(s - m)
            den = jnp.sum(pr, axis=0, keepdims=True) + jnp.exp2(sink - m)
            prb = pr.astype(BF16)
            pt_scr[gidx, 0:WINDOW, cols] = prb * prev_w
            pt_scr[gidx, WINDOW:, cols] = prb * cur_w
            inv_scr[jb, hd:hd + 1, :] = 1.0 / den

    for gidx, (jb, kh) in enumerate(groups):
        vband = jnp.concatenate([vt_scr[jb, kh * ATTN_HEAD_DIM:(kh + 1) * ATTN_HEAD_DIM, :],
                                 vt_scr[jb + 1, kh * ATTN_HEAD_DIM:(kh + 1) * ATTN_HEAD_DIM, :]],
                                axis=1)
        otg = _dot(vband, pt_scr[gidx])
        for gi in range(ATTN_GROUP):
            hd = kh * ATTN_GROUP + gi
            ot_scr[jb, hd * ATTN_HEAD_DIM:(hd + 1) * ATTN_HEAD_DIM, :] = (
                otg[:, gi * WINDOW:(gi + 1) * WINDOW] * inv_scr[jb, hd:hd + 1, :])

    state_phase(2)

    parts = 2 if nblk % 2 == 0 else 1
    rows = _row_splits(tq, parts)
    ogs = []
    for r in rows:
        o = jnp.concatenate([ot_scr[jb].T for jb in range(r.start // WINDOW, r.stop // WINDOW)],
                            axis=0)
        ogs.append((o * _silu(g[r, :])).astype(BF16))
    _out_proj_and_residual(rows, ogs, x, emb, gpost_ref[...], wout_ref, wgate_ref, xo_ref,
                           after_out_proj=lambda: state_phase(3))

    kext_scr[0:WINDOW, :] = kext_scr[tq:tq + WINDOW, :]
    vt_scr[0] = vt_scr[nblk]

    @pl.when(step == n_steps - 1)
    def _():
        for slot in range(2):
            for i in range(per_phase):
                load_copy(last_seq, slot, i).wait()
                store_copy(last_seq, slot, i).wait()


def _const_spec(shape):
    nd = len(shape)
    return pl.BlockSpec(shape, lambda *_: (0,) * nd)


def _attn_prompt_layer(layer, x, p_all, tables, gpre, gpost, sinks, w_qt, w_k, w_vt, w_g,
                       w_out, w_gate, w_ple, ret_qkv, ret_state, dec_len):
    nb, seq, d = x.shape
    tq = min(512, seq)
    nblk = tq // WINDOW
    nt = seq // tq
    n_seq = ret_state.shape[0] // (nb * nt)
    assert n_seq * nb * nt == ret_state.shape[0] and n_seq % STATE_PHASES == 0, (
        ret_state.shape, nb, nt)
    ple = p_all.shape[-1]
    cos, sin, cost, sint = tables
    kernel = functools.partial(_attn_prompt_kernel, tq=tq, n_seq=n_seq, dec_len=dec_len)
    tok = lambda b, t: (b, t, 0)
    win_blk = lambda b, t: (b, 0, 0)
    stok = lambda b, t: (b * nt + t, 0)
    consts = (gpre, gpost, w_qt, w_k, w_vt, w_g, w_out, w_gate, w_ple)
    rq, rk, rkd, rv = ret_qkv
    st_shape = (2, n_seq // STATE_PHASES) + ret_state.shape[1:]
    return pl.pallas_call(
        kernel,
        grid=(nb, nt),
        in_specs=[
            pl.BlockSpec(memory_space=pltpu.SMEM),
            pl.BlockSpec((None, tq, d), tok),
            pl.BlockSpec((None, None, tq, ple), lambda b, t: (layer, b, t, 0)),
            pl.BlockSpec((tq, LANES), lambda b, t: (t, 0)),
            pl.BlockSpec((tq, LANES), lambda b, t: (t, 0)),
            pl.BlockSpec((ATTN_HEAD_DIM // 2, tq), lambda b, t: (0, t)),
            pl.BlockSpec((ATTN_HEAD_DIM // 2, tq), lambda b, t: (0, t)),
        ] + [_const_spec(c.shape) for c in consts] + [
            pl.BlockSpec((n_seq * dec_len, RET_QK_W), stok),
            pl.BlockSpec((n_seq * dec_len, RET_QK_W), stok),
            pl.BlockSpec((n_seq * dec_len, RET_QK_W), stok),
            pl.BlockSpec((n_seq * dec_len, RET_VW), stok),
            pl.BlockSpec(memory_space=pl.ANY),
        ],
        out_specs=[
            pl.BlockSpec((None, tq, d), tok),
            pl.BlockSpec((None, ATTN_KV_DIM, WINDOW), win_blk),
            pl.BlockSpec((None, ATTN_KV_DIM, WINDOW), win_blk),
            pl.BlockSpec((n_seq * dec_len, RET_VW), stok),
            pl.BlockSpec(memory_space=pl.ANY),
        ],
        out_shape=[
            jax.ShapeDtypeStruct(x.shape, F32),
            jax.ShapeDtypeStruct((nb, ATTN_KV_DIM, WINDOW), F32),
            jax.ShapeDtypeStruct((nb, ATTN_KV_DIM, WINDOW), F32),
            jax.ShapeDtypeStruct((rq.shape[0], RET_VW), F32),
            jax.ShapeDtypeStruct(ret_state.shape, F32),
        ],
        scratch_shapes=[
            pltpu.VMEM((nblk, ATTN_Q_DIM, WINDOW), BF16),
            pltpu.VMEM((tq + WINDOW, ATTN_KV_DIM), BF16),
            pltpu.VMEM((nblk + 1, ATTN_KV_DIM, WINDOW), BF16),
            pltpu.VMEM((nblk * ATTN_KV_HEADS, 2 * WINDOW, ATTN_GROUP * WINDOW), F32),
            pltpu.VMEM((nblk * ATTN_KV_HEADS, 2 * WINDOW, ATTN_GROUP * WINDOW), BF16),
            pltpu.VMEM((nblk, ATTN_HEADS, WINDOW), F32),
            pltpu.VMEM((nblk, ATTN_Q_DIM, WINDOW), F32),
            pltpu.VMEM(st_shape, F32),
            pltpu.VMEM(st_shape, F32),
            pltpu.SemaphoreType.DMA((2, n_seq // STATE_PHASES)),
            pltpu.SemaphoreType.DMA((2, n_seq // STATE_PHASES)),
        ],
        compiler_params=pltpu.CompilerParams(
            dimension_semantics=("arbitrary", "arbitrary"),
            vmem_limit_bytes=ATTN_PROMPT_VMEM_LIMIT_BYTES),
        name="attn_prompt_layer",
    )(sinks, x, p_all, cos, sin, cost, sint, *consts, rq, rk, rkd, rv, ret_state)


def _attn_sample_kernel(sink_ref, x_ref, p_ref, cos_ref, sin_ref, cost_ref, sint_ref,
                        gpre_ref, gpost_ref, wq_ref, wkt_ref, wvt_ref, wg_ref,
                        wout_ref, wgate_ref, wple_ref, kc_ref, vc_ref,
                        xo_ref, ko_ref, vo_ref,
                        q_scr, knt_scr, vnt_scr, o_scr, *, n_seq, dec_len):
    half = ATTN_HEAD_DIM // 2
    x = x_ref[...]
    h = _rms_norm(x, gpre_ref[...]).astype(BF16)
    q_scr[...] = (_rope_attn(_dot(h, wq_ref[...]), cos_ref[...], sin_ref[...])
                  * (ATTN_HEAD_DIM ** -0.5 * LOG2_E))
    knt = _dot_nt(wkt_ref[...], h)
    cost = cost_ref[...]
    sint = sint_ref[...]
    pieces = []
    for kh in range(ATTN_KV_HEADS):
        x1 = knt[kh * ATTN_HEAD_DIM:kh * ATTN_HEAD_DIM + half, :]
        x2 = knt[kh * ATTN_HEAD_DIM + half:(kh + 1) * ATTN_HEAD_DIM, :]
        pieces.append(x1 * cost - x2 * sint)
        pieces.append(x2 * cost + x1 * sint)
    knt_scr[...] = jnp.concatenate(pieces, axis=0)
    vnt_scr[...] = _dot_nt(wvt_ref[...], h)

    rows = 2 * dec_len
    tok = lax.broadcasted_iota(jnp.int32, (rows, 2 * WINDOW), 0) % dec_len
    col = lax.broadcasted_iota(jnp.int32, (rows, 2 * WINDOW), 1)
    upper_row = lax.broadcasted_iota(jnp.int32, (rows, 1), 0) >= dec_len
    lane = lax.broadcasted_iota(jnp.int32, (ATTN_HEAD_DIM, WINDOW), 1)
    keep_old = lane < WINDOW - dec_len
    zeros = jnp.zeros((ATTN_HEAD_DIM, 2 * WINDOW), BF16)

    def select_head(t, parity):
        return jnp.concatenate([t, zeros] if parity == 0 else [zeros, t], axis=0)

    def scores(s):
        tok_rows = slice(s * dec_len, (s + 1) * dec_len)
        new_col = col - (WINDOW + s * dec_len)
        mask = ((col < WINDOW) & (col > tok)) | ((new_col >= 0) & (new_col <= tok))
        new_shift = (WINDOW - dec_len - s * dec_len) % WINDOW
        out = []
        for kh in range(ATTN_KV_HEADS):
            hrows = slice(kh * ATTN_HEAD_DIM, (kh + 1) * ATTN_HEAD_DIM)
            kc = kc_ref[s, kh]
            vc = vc_ref[s, kh]
            kn = knt_scr[hrows, :]
            vn = vnt_scr[hrows, :]
            kn_at_end = pltpu.roll(kn, new_shift, axis=1) if new_shift else kn
            vn_at_end = pltpu.roll(vn, new_shift, axis=1) if new_shift else vn
            ko_ref[s, kh] = jnp.where(keep_old, pltpu.roll(kc, WINDOW - dec_len, axis=1), kn_at_end)
            vo_ref[s, kh] = jnp.where(keep_old, pltpu.roll(vc, WINDOW - dec_len, axis=1), vn_at_end)

            kt = jnp.concatenate([kc, kn], axis=1).astype(BF16)
            vt = jnp.concatenate([vc, vn], axis=1).astype(BF16)
            qp = jnp.concatenate([q_scr[tok_rows, (2 * kh + c) * LANES:(2 * kh + c + 1) * LANES]
                                  for c in range(2)], axis=0).astype(BF16)
            sc = [jnp.where(mask, _dot(qp, select_head(kt, parity)), NEG_INF)
                  for parity in range(2)]
            out.append((sc, vt))
        return out

    def values(s, per_head):
        tok_rows = slice(s * dec_len, (s + 1) * dec_len)
        probs = []
        for kh, (scs, vt) in enumerate(per_head):
            for parity, sc in enumerate(scs):
                hd_lo, hd_hi = 4 * kh + parity, 4 * kh + 2 + parity
                sink = jnp.where(upper_row, sink_ref[hd_hi], sink_ref[hd_lo]) * LOG2_E
                m = jnp.maximum(jnp.max(sc, axis=-1, keepdims=True), sink)
                pr = jnp.exp2(sc - m)
                den = jnp.sum(pr, axis=-1, keepdims=True) + jnp.exp2(sink - m)
                probs.append((pr / den).astype(BF16))
        for kh, (scs, vt) in enumerate(per_head):
            o_pair = (_dot_nt(probs[2 * kh], select_head(vt, 0))
                      + _dot_nt(probs[2 * kh + 1], select_head(vt, 1)))
            for c in range(2):
                o_scr[tok_rows, (2 * kh + c) * LANES:(2 * kh + c + 1) * LANES] = (
                    o_pair[c * dec_len:(c + 1) * dec_len, :])

    group = min(SAMPLE_ATTN_PIPELINE_GROUP, n_seq)
    pending = [scores(s) for s in range(group)]
    for s0 in range(0, n_seq, group):
        upcoming = [scores(s) for s in range(s0 + group, min(s0 + 2 * group, n_seq))]
        for i, per_head in enumerate(pending):
            values(s0 + i, per_head)
        pending = upcoming

    g = _dot(h, wg_ref[...])
    emb = _dot(p_ref[...].astype(BF16), wple_ref[...])
    og = (o_scr[...] * _silu(g)).astype(BF16)
    _out_proj_and_residual(_row_splits(x.shape[0], 1), [og], x, emb, gpost_ref[...],
                           wout_ref, wgate_ref, xo_ref)


def _attn_sample_layer(layer, x, p_all, tables, gpre, gpost, sinks, w_q, w_kt, w_vt, w_g,
                       w_out, w_gate, w_ple, cache_kt, cache_vt, dec_len):
    n_tok, d = x.shape
    n_all = cache_kt.shape[0]
    n_seq = WINDOW // dec_len
    tq = n_seq * dec_len
    assert tq == WINDOW and n_all % n_seq == 0, (dec_len, n_all)
    ple = p_all.shape[-1]
    cos, sin, cost, sint = tables
    kernel = functools.partial(_attn_sample_kernel, n_seq=n_seq, dec_len=dec_len)
    tok = lambda i: (i, 0)
    cache_shape = (n_seq, ATTN_KV_HEADS, ATTN_HEAD_DIM, WINDOW)
    cache_blk = lambda i: (i, 0, 0, 0)
    consts = (cos, sin, cost, sint, gpre, gpost, w_q, w_kt, w_vt, w_g, w_out, w_gate, w_ple)
    return pl.pallas_call(
        kernel,
        grid=(n_all // n_seq,),
        in_specs=[
            pl.BlockSpec(memory_space=pltpu.SMEM),
            pl.BlockSpec((tq, d), tok),
            pl.BlockSpec((None, tq, ple), lambda i: (layer, i, 0)),
        ] + [_const_spec(c.shape) for c in consts] + [
            pl.BlockSpec(cache_shape, cache_blk),
            pl.BlockSpec(cache_shape, cache_blk),
        ],
        out_specs=[
            pl.BlockSpec((tq, d), tok),
            pl.BlockSpec(cache_shape, cache_blk),
            pl.BlockSpec(cache_shape, cache_blk),
        ],
        out_shape=[
            jax.ShapeDtypeStruct(x.shape, F32),
            jax.ShapeDtypeStruct(cache_kt.shape, F32),
            jax.ShapeDtypeStruct(cache_vt.shape, F32),
        ],
        scratch_shapes=[
            pltpu.VMEM((tq, ATTN_Q_DIM), F32),
            pltpu.VMEM((ATTN_KV_DIM, tq), F32),
            pltpu.VMEM((ATTN_KV_DIM, tq), F32),
            pltpu.VMEM((tq, ATTN_Q_DIM), F32),
        ],
        compiler_params=pltpu.CompilerParams(
            dimension_semantics=("arbitrary",),
            vmem_limit_bytes=VMEM_LIMIT_BYTES),
        name="attn_sample_layer",
    )(sinks, x, p_all, *consts, cache_kt, cache_vt)


RET_GATE_COL0 = 2 * RET_QK_W + RET_VW


def _ret_project(h, cos, sin, win_ref, chunk_len):
    q0, k0, v0, g0 = 0, RET_QK_W, 2 * RET_QK_W, RET_GATE_COL0
    q = _rope_ret(_dot(h, win_ref[:, q0:k0]), cos, sin)
    k = _rope_ret(_dot(h, win_ref[:, k0:v0]), cos, sin) * (RET_QK_DIM ** -0.5)
    v = _dot(h, win_ref[:, v0:g0])
    n = h.shape[0]
    idx = (lax.broadcasted_iota(jnp.int32, (n, 1), 0) % chunk_len).astype(F32)
    kd = jnp.concatenate(
        [k[:, hd * RET_QK_DIM:(hd + 1) * RET_QK_DIM]
         * jnp.exp((chunk_len - 1.0 - idx) * RET_LOG_GAMMA[hd]) for hd in range(RET_HEADS)], axis=1)
    return q, k, kd, v


def _ret_decay_mask(n, chunk_len, hd):
    ri = lax.broadcasted_iota(jnp.int32, (n, n), 0)
    ci = lax.broadcasted_iota(jnp.int32, (n, n), 1)
    diff = ri - ci
    ok = (diff >= 0) & ((ri // chunk_len) == (ci // chunk_len))
    return jnp.where(ok, jnp.exp(jnp.maximum(diff, 0).astype(F32) * RET_LOG_GAMMA[hd]), 0.0)


def _ret_gated(o, g):
    outs = []
    for hd in range(RET_HEADS):
        oh = o[:, hd * RET_V_DIM:(hd + 1) * RET_V_DIM]
        mu = jnp.mean(oh, axis=-1, keepdims=True)
        var = jnp.mean(jnp.square(oh - mu), axis=-1, keepdims=True)
        outs.append((oh - mu) * lax.rsqrt(var + NORM_EPS))
    return (jnp.concatenate(outs, axis=1) * _silu(g)).astype(BF16)


def _ret_prompt_kernel(x_ref, p_ref, cos_ref, sin_ref, gpre_ref, gpost_ref,
                       win_ref, wout_ref, wgate_ref, wple_ref,
                       xo_ref, st_ref,
                       q_scr, k_scr, kd_scr, v_scr, o_scr, g_scr, dm_scr, *, tq):
    b = pl.program_id(0)
    t = pl.program_id(1)

    @pl.when((b == 0) & (t == 0))
    def _():
        for hd in range(RET_HEADS):
            dm_scr[hd] = _ret_decay_mask(RET_CHUNK, RET_CHUNK, hd)

    @pl.when(t == 0)
    def _():
        st_ref[...] = jnp.zeros(st_ref.shape, F32)

    emb = _dot(p_ref[...].astype(BF16), wple_ref[...])
    x = x_ref[...]
    h = _rms_norm(x, gpre_ref[...]).astype(BF16)
    q, k, kd, v = _ret_project(h, cos_ref[...], sin_ref[...], win_ref, RET_CHUNK)
    q_scr[...] = q.astype(BF16)
    k_scr[...] = k.astype(BF16)
    kd_scr[...] = kd.astype(BF16)
    v_scr[...] = v.astype(BF16)

    idx = lax.broadcasted_iota(jnp.int32, (RET_CHUNK, 1), 0).astype(F32)
    heads = range(RET_HEADS)
    qs = [slice(hd * RET_QK_DIM, (hd + 1) * RET_QK_DIM) for hd in heads]
    vs = [slice(hd * RET_V_DIM, (hd + 1) * RET_V_DIM) for hd in heads]
    chunks = _row_splits(tq, tq // RET_CHUNK)
    gate_cols = iter(_row_splits(RET_VW, 2 * len(chunks)))

    def gate_piece():
        c = next(gate_cols)
        g_scr[:, c] = _dot(h, win_ref[:, RET_GATE_COL0 + c.start:RET_GATE_COL0 + c.stop])

    for rows in chunks:
        scores = [_dot_nt(q_scr[rows, qs[hd]], k_scr[rows, qs[hd]]) for hd in heads]
        cross = [_dot(q_scr[rows, qs[hd]], st_ref[hd].astype(BF16)) for hd in heads]
        gate_piece()
        for hd in heads:
            st_ref[hd] = (math.exp(RET_CHUNK * RET_LOG_GAMMA[hd]) * st_ref[hd]
                          + _dot_tn(kd_scr[rows, qs[hd]], v_scr[rows, vs[hd]]))
        for hd in heads:
            inner = _dot((scores[hd] * dm_scr[hd]).astype(BF16), v_scr[rows, vs[hd]])
            o_scr[rows, vs[hd]] = inner + cross[hd] * jnp.exp((idx + 1.0) * RET_LOG_GAMMA[hd])
        gate_piece()

    ogs = [_ret_gated(o_scr[r, :], g_scr[r, :]) for r in chunks]
    _out_proj_and_residual(chunks, ogs, x, emb, gpost_ref[...], wout_ref, wgate_ref, xo_ref)


def _ret_prompt_layer(layer, x, p_all, cos, sin, gpre, gpost, w_in, w_out, w_gate, w_ple):
    nb, seq, d = x.shape
    tq = min(512, seq)
    ple = p_all.shape[-1]
    kernel = functools.partial(_ret_prompt_kernel, tq=tq)
    tok = lambda b, t: (b, t, 0)
    return pl.pallas_call(
        kernel,
        grid=(nb, seq // tq),
        in_specs=[
            pl.BlockSpec((None, tq, d), tok),
            pl.BlockSpec((None, None, tq, ple), lambda b, t: (layer, b, t, 0)),
            pl.BlockSpec((tq, LANES), lambda b, t: (t, 0)),
            pl.BlockSpec((tq, LANES), lambda b, t: (t, 0)),
            _const_spec(gpre.shape), _const_spec(gpost.shape),
            _const_spec(w_in.shape), _const_spec(w_out.shape),
            _const_spec(w_gate.shape), _const_spec(w_ple.shape),
        ],
        out_specs=[
            pl.BlockSpec((None, tq, d), tok),
            pl.BlockSpec((None, RET_HEADS, RET_QK_DIM, RET_V_DIM), lambda b, t: (b, 0, 0, 0)),
        ],
        out_shape=[
            jax.ShapeDtypeStruct(x.shape, F32),
            jax.ShapeDtypeStruct((nb, RET_HEADS, RET_QK_DIM, RET_V_DIM), F32),
        ],
        scratch_shapes=[
            pltpu.VMEM((tq, RET_QK_W), BF16),
            pltpu.VMEM((tq, RET_QK_W), BF16),
            pltpu.VMEM((tq, RET_QK_W), BF16),
            pltpu.VMEM((tq, RET_VW), BF16),
            pltpu.VMEM((tq, RET_VW), F32),
            pltpu.VMEM((tq, RET_VW), F32),
            pltpu.VMEM((RET_HEADS, RET_CHUNK, RET_CHUNK), F32),
        ],
        compiler_params=pltpu.CompilerParams(
            dimension_semantics=("arbitrary", "arbitrary"),
            vmem_limit_bytes=VMEM_LIMIT_BYTES),
        name="ret_prompt_layer",
    )(x, p_all, cos, sin, gpre, gpost, w_in, w_out, w_gate, w_ple)


def _ret_sample_proj_kernel(x_ref, cos_ref, sin_ref, gpre_ref, win_ref,
                            q_ref, k_ref, kd_ref, v_ref, g_ref, *, dec_len):
    h = _rms_norm(x_ref[...], gpre_ref[...]).astype(BF16)
    q, k, kd, v = _ret_project(h, cos_ref[...], sin_ref[...], win_ref, dec_len)
    q_ref[...] = q
    k_ref[...] = k
    kd_ref[...] = kd
    v_ref[...] = v
    g_ref[...] = _dot(h, win_ref[:, RET_GATE_COL0:])


def _ret_sample_proj(x, cos, sin, gpre, w_in, dec_len):
    n_tok, d = x.shape
    tq = min(RET_SAMPLE_TILE, n_tok)
    kernel = functools.partial(_ret_sample_proj_kernel, dec_len=dec_len)
    tok = lambda i: (i, 0)
    widths = (RET_QK_W, RET_QK_W, RET_QK_W, RET_VW, RET_VW)
    return pl.pallas_call(
        kernel,
        grid=(n_tok // tq,),
        in_specs=[
            pl.BlockSpec((tq, d), tok),
            _const_spec((tq, LANES)), _const_spec((tq, LANES)),
            _const_spec(gpre.shape), _const_spec(w_in.shape),
        ],
        out_specs=[pl.BlockSpec((tq, w), tok) for w in widths],
        out_shape=[jax.ShapeDtypeStruct((n_tok, w), F32) for w in widths],
        compiler_params=pltpu.CompilerParams(
            dimension_semantics=("arbitrary",),
            vmem_limit_bytes=VMEM_LIMIT_BYTES),
        name="ret_sample_proj",
    )(x, cos, sin, gpre, w_in)


def _ret_state_update_tile(q_ref, k_ref, kd_ref, v_ref, st_in, st_out, o_ref, *,
                           first_seq, n_seq, dec_len):
    pad_rows = 2 * dec_len
    idx = lax.broadcasted_iota(jnp.int32, (pad_rows, 1), 0).astype(F32)

    def tok_rows(s):
        return slice((first_seq + s) * dec_len, (first_seq + s + 1) * dec_len)

    def padded(ref, s, cols):
        val = ref[tok_rows(s), cols]
        return jnp.concatenate([val, jnp.zeros_like(val)], axis=0).astype(BF16)

    units = [(s, hd) for s in range(n_seq) for hd in range(RET_HEADS)]
    qs = [slice(hd * RET_QK_DIM, (hd + 1) * RET_QK_DIM) for hd in range(RET_HEADS)]
    vs = [slice(hd * RET_V_DIM, (hd + 1) * RET_V_DIM) for hd in range(RET_HEADS)]
    qh = {(s, hd): padded(q_ref, s, qs[hd]) for s, hd in units}
    vh = {(s, hd): padded(v_ref, s, vs[hd]) for s, hd in units}
    scores = {(s, hd): _dot_nt(qh[s, hd], padded(k_ref, s, qs[hd])) for s, hd in units}
    cross = {(s, hd): _dot(qh[s, hd], st_in[s, hd].astype(BF16)) for s, hd in units}
    for s, hd in units:
        st_out[s, hd] = (math.exp(dec_len * RET_LOG_GAMMA[hd]) * st_in[s, hd]
                         + _dot_tn(padded(kd_ref, s, qs[hd]), vh[s, hd]))
    masks = [_ret_decay_mask(pad_rows, dec_len, hd) for hd in range(RET_HEADS)]
    for s, hd in units:
        inner = _dot((scores[s, hd] * masks[hd]).astype(BF16), vh[s, hd])
        out = inner + cross[s, hd] * jnp.exp((idx + 1.0) * RET_LOG_GAMMA[hd])
        o_ref[tok_rows(s), vs[hd]] = out[0:dec_len, :]


def _ret_sample_finish_kernel(o_ref, g_ref, x_ref, p_ref, gpost_ref, wout_ref, wgate_ref, wple_ref,
                              xo_ref):
    emb = _dot(p_ref[...].astype(BF16), wple_ref[...])
    x = x_ref[...]
    rows = _row_splits(x.shape[0], 2)
    ogs = [_ret_gated(o_ref[r, :], g_ref[r, :]) for r in rows]
    _out_proj_and_residual(rows, ogs, x, emb, gpost_ref[...], wout_ref, wgate_ref, xo_ref)


def _ret_sample_finish(layer, o, g, x, p_all, gpost, w_out, w_gate, w_ple):
    n_tok, d = x.shape
    tq = min(RET_SAMPLE_TILE, n_tok)
    ple = p_all.shape[-1]
    tok = lambda i: (i, 0)
    return pl.pallas_call(
        _ret_sample_finish_kernel,
        grid=(n_tok // tq,),
        in_specs=[
            pl.BlockSpec((tq, RET_VW), tok), pl.BlockSpec((tq, RET_VW), tok),
            pl.BlockSpec((tq, d), tok),
            pl.BlockSpec((None, tq, ple), lambda i: (layer, i, 0)),
            _const_spec(gpost.shape), _const_spec(w_out.shape),
            _const_spec(w_gate.shape), _const_spec(w_ple.shape),
        ],
        out_specs=pl.BlockSpec((tq, d), tok),
        out_shape=jax.ShapeDtypeStruct(x.shape, F32),
        compiler_params=pltpu.CompilerParams(
            dimension_semantics=("arbitrary",),
            vmem_limit_bytes=VMEM_LIMIT_BYTES),
        name="ret_sample_finish",
    )(o, g, x, p_all, gpost, w_out, w_gate, w_ple)


def _rope_tables(pos, head_dim):
    half = head_dim // 2
    inv = ROPE_THETA ** (-np.arange(half, dtype=np.float64) / half)
    ang = np.asarray(pos, np.float64)[:, None] * inv[None, :]
    return np.cos(ang).astype(np.float32), np.sin(ang).astype(np.float32)


def _attn_tables(pos):
    cos, sin = _rope_tables(pos, ATTN_HEAD_DIM)
    reps = LANES // ATTN_HEAD_DIM
    return (np.tile(np.concatenate([cos, cos], axis=1), (1, reps)),
            np.tile(np.concatenate([-sin, sin], axis=1), (1, reps)))


def kernel(x_prompt, x_sample, cache_k_win, cache_v_win, state_ret, p_prompt, p_sample, pre_norm,
           post_norm, w_in_attn, attn_sinks, w_out_attn, w_in_ret, w_out_ret, w_ple, w_ple_gate):
    depth = p_prompt.shape[0]
    nb, seq, d = x_prompt.shape
    n_dec, dec_len, _ = x_sample.shape
    n_stok = n_dec * dec_len
    assert seq % 512 == 0 or seq in (128, 256), seq

    pos_p = np.arange(seq)
    pos_s = PAST_LEN + np.arange(dec_len)
    cos_ap, sin_ap = _attn_tables(pos_p)
    cos_apt, sin_apt = (np.ascontiguousarray(a.T) for a in _rope_tables(pos_p, ATTN_HEAD_DIM))
    cos_as, sin_as = _attn_tables(pos_s)
    cos_rp, sin_rp = _rope_tables(pos_p, RET_QK_DIM)
    cos_rs, sin_rs = _rope_tables(pos_s, RET_QK_DIM)
    attn_s_tile = WINDOW // dec_len
    ret_s_tile = min(RET_SAMPLE_TILE, n_stok) // dec_len
    cos_as, sin_as = (np.tile(a, (attn_s_tile, 1)) for a in (cos_as, sin_as))
    cos_ast, sin_ast = (np.ascontiguousarray(np.tile(a, (attn_s_tile, 1)).T)
                        for a in _rope_tables(pos_s, ATTN_HEAD_DIM))
    cos_rs, sin_rs = (np.tile(a, (ret_s_tile, 1)) for a in (cos_rs, sin_rs))
    (cos_ap, sin_ap, cos_apt, sin_apt, cos_as, sin_as, cos_ast, sin_ast,
     cos_rp, sin_rp, cos_rs, sin_rs) = (
        jnp.asarray(a) for a in
        (cos_ap, sin_ap, cos_apt, sin_apt, cos_as, sin_as, cos_ast, sin_ast,
         cos_rp, sin_rp, cos_rs, sin_rs))

    assert depth == 2, depth
    xs = x_sample.reshape(n_stok, d)
    p_s = p_sample.reshape(depth, n_stok, p_sample.shape[-1])
    bf = lambda w: w.astype(BF16)
    gpre = [pre_norm[i][None, :] for i in range(depth)]
    gpost = [post_norm[i][None, :] for i in range(depth)]
    w_gate = [bf(w_ple_gate[i]) for i in range(depth)]
    w_pl = [bf(w_ple[i]) for i in range(depth)]

    w_in = bf(w_in_attn[0])
    w_out = bf(w_out_attn[0])
    sinks = attn_sinks[0]
    k0, v0, g0 = ATTN_Q_DIM, ATTN_Q_DIM + ATTN_KV_DIM, ATTN_Q_DIM + 2 * ATTN_KV_DIM
    w_q, w_k, w_v, w_g = w_in[:, :k0], w_in[:, k0:v0], w_in[:, v0:g0], w_in[:, g0:]
    to_t = lambda a: a.transpose(0, 2, 3, 1)
    from_t = lambda a: a.transpose(0, 3, 1, 2)
    xs, knt, vnt = _attn_sample_layer(
        0, xs, p_s, (cos_as, sin_as, cos_ast, sin_ast), gpre[0], gpost[0], sinks,
        w_q, w_k.T, w_v.T, w_g, w_out, w_gate[0], w_pl[0],
        to_t(cache_k_win[0]), to_t(cache_v_win[0]), dec_len)

    w_in_r = bf(w_in_ret[0])
    w_out_r = bf(w_out_ret[0])
    rq, rk, rkd, rv, rg = _ret_sample_proj(xs, cos_rs, sin_rs, gpre[1], w_in_r, dec_len)
    xp, kpt, vpt, ro, ret_state_sample = _attn_prompt_layer(
        0, x_prompt, p_prompt, (cos_ap, sin_ap, cos_apt, sin_apt), gpre[0], gpost[0], sinks,
        w_q.T, w_k, w_v.T, w_g, w_out, w_gate[0], w_pl[0],
        (rq, rk, rkd, rv), state_ret[0], dec_len)
    head_shape = (nb, ATTN_KV_HEADS, ATTN_HEAD_DIM, WINDOW)

    xp, ret_state_prompt = _ret_prompt_layer(1, xp, p_prompt, cos_rp, sin_rp, gpre[1], gpost[1],
                                             w_in_r, w_out_r, w_gate[1], w_pl[1])
    xs = _ret_sample_finish(1, ro, rg, xs, p_s, gpost[1], w_out_r, w_gate[1], w_pl[1])
    return (xp, xs.reshape(n_dec, dec_len, d),
            from_t(kpt.reshape(head_shape))[None], from_t(vpt.reshape(head_shape))[None],
            from_t(knt)[None], from_t(vnt)[None], ret_state_prompt[None], ret_state_sample[None])
```

```python
import functools
import math

import numpy as np
import jax
import jax.numpy as jnp
from jax import lax
from jax.experimental import pallas as pl
from jax.experimental.pallas import tpu as pltpu

F32 = jnp.float32
BF16 = jnp.bfloat16

PAST_LEN = 16384
ROPE_THETA = 10000.0
NORM_EPS = 1e-6
NEG_INF = -1e30
WINDOW = 128
ATTN_HEADS = 16
ATTN_KV_HEADS = 4
ATTN_GROUP = ATTN_HEADS // ATTN_KV_HEADS
ATTN_HEAD_DIM = 64
ATTN_Q_DIM = ATTN_HEADS * ATTN_HEAD_DIM
ATTN_KV_DIM = ATTN_KV_HEADS * ATTN_HEAD_DIM
RET_HEADS = 4
RET_QK_DIM = 256
RET_V_DIM = 512
RET_QK_W = RET_HEADS * RET_QK_DIM
RET_VW = RET_HEADS * RET_V_DIM
RET_CHUNK = 256
LANES = 128

LOG2_E = math.log2(math.e)
RET_LOG_GAMMA = tuple(math.log1p(-(2.0 ** (-5.0 - h))) for h in range(RET_HEADS))

VMEM_LIMIT_BYTES = 56 * 1024 * 1024
ATTN_PROMPT_VMEM_LIMIT_BYTES = 50 * 1024 * 1024
STATE_PHASES = 4
RET_SAMPLE_TILE = 256
SAMPLE_ATTN_PIPELINE_GROUP = 8


def _dot(a, b):
    return jnp.dot(a, b, preferred_element_type=F32)


def _dot_nt(a, b):
    return lax.dot_general(a, b, (((1,), (1,)), ((), ())), preferred_element_type=F32)


def _dot_tn(a, b):
    return lax.dot_general(a, b, (((0,), (0,)), ((), ())), preferred_element_type=F32)


def _rms_norm(x, g):
    return x * lax.rsqrt(jnp.mean(x * x, axis=-1, keepdims=True) + NORM_EPS) * g


def _silu(g):
    return g * jax.nn.sigmoid(g)


def _row_splits(n, parts):
    step = n // parts
    return [slice(i * step, (i + 1) * step) for i in range(parts)]


def _out_proj_and_residual(rows, ogs, x, emb, gpost, wout_ref, wgate_ref, xo_ref):
    ys = [_dot(og, wout_ref[...]) for og in ogs]
    x1s = [x[r, :] + _rms_norm(y, gpost) for r, y in zip(rows, ys)]
    gates = [_dot(x1.astype(BF16), wgate_ref[...]) for x1 in x1s]
    for r, x1, gate in zip(rows, x1s, gates):
        xo_ref[r, :] = x1 + jax.nn.sigmoid(gate) * emb[r, :]


def _rope_attn(x, cos, sin_signed):
    lane = lax.broadcasted_iota(jnp.int32, (1, LANES), 1)
    first_half = (lane % ATTN_HEAD_DIM) < (ATTN_HEAD_DIM // 2)
    outs = []
    for j in range(x.shape[1] // LANES):
        xj = x[:, j * LANES:(j + 1) * LANES]
        fwd = pltpu.roll(xj, LANES - ATTN_HEAD_DIM // 2, axis=1)
        bwd = pltpu.roll(xj, ATTN_HEAD_DIM // 2, axis=1)
        outs.append(xj * cos + jnp.where(first_half, fwd, bwd) * sin_signed)
    return jnp.concatenate(outs, axis=1)


def _rope_ret(x, cos, sin):
    half = RET_QK_DIM // 2
    outs = []
    for h in range(x.shape[1] // RET_QK_DIM):
        x1 = x[:, h * RET_QK_DIM:h * RET_QK_DIM + half]
        x2 = x[:, h * RET_QK_DIM + half:(h + 1) * RET_QK_DIM]
        outs.append(x1 * cos - x2 * sin)
        outs.append(x2 * cos + x1 * sin)
    return jnp.concatenate(outs, axis=1)


def _attn_prompt_kernel(sink_ref, x_ref, p_ref, cos_ref, sin_ref, cost_ref, sint_ref,
                        gpre_ref, gpost_ref, wqt_ref, wk_ref, wvt_ref, wg_ref,
                        wout_ref, wgate_ref, wple_ref,
                        rq_ref, rk_ref, rkd_ref, rv_ref, rstate_hbm,
                        xo_ref, kw_ref, vw_ref, ro_ref, rstate_out_hbm,
                        qt_scr, kext_scr, vt_scr, st_scr, pt_scr, inv_scr, ot_scr,
                        rst_in, rst_out, sem_in, sem_out, *, tq, n_seq, dec_len):
    nblk = tq // WINDOW
    half = ATTN_HEAD_DIM // 2
    t = pl.program_id(1)
    nt = pl.num_programs(1)

    @pl.when(t == 0)
    def _():
        kext_scr[0:WINDOW, :] = jnp.zeros((WINDOW, ATTN_KV_DIM), BF16)
        vt_scr[0] = jnp.zeros((ATTN_KV_DIM, WINDOW), BF16)

    per_phase = n_seq // STATE_PHASES
    step = pl.program_id(0) * nt + t
    n_steps = pl.num_programs(0) * nt
    seq0 = step * n_seq
    last_seq = n_steps * n_seq - 1

    def load_copy(seq, slot, i):
        return pltpu.make_async_copy(rstate_hbm.at[seq], rst_in.at[slot, i], sem_in.at[slot, i])

    def store_copy(seq, slot, i):
        return pltpu.make_async_copy(rst_out.at[slot, i], rstate_out_hbm.at[seq],
                                     sem_out.at[slot, i])

    @pl.when(step == 0)
    def _():
        rst_out[...] = jnp.zeros(rst_out.shape, F32)
        for slot in range(2):
            for i in range(per_phase):
                load_copy(slot * per_phase + i, slot, i).start()
                store_copy(slot * per_phase + i, slot, i).start()

    def state_phase(phase):
        slot = phase % 2
        first = seq0 + phase * per_phase
        for i in range(per_phase):
            load_copy(first + i, slot, i).wait()
            store_copy(first + i, slot, i).wait()
        _ret_state_update_tile(rq_ref, rk_ref, rkd_ref, rv_ref, rst_in.at[slot], rst_out.at[slot],
                               ro_ref, first_seq=phase * per_phase, n_seq=per_phase,
                               dec_len=dec_len)
        for i in range(per_phase):
            store_copy(first + i, slot, i).start()
            load_copy(jnp.minimum(first + i + 2 * per_phase, last_seq), slot, i).start()

    state_phase(0)

    emb = _dot(p_ref[...].astype(BF16), wple_ref[...])
    x = x_ref[...]
    h = _rms_norm(x, gpre_ref[...]).astype(BF16)

    k = _rope_attn(_dot(h, wk_ref[...]), cos_ref[...], sin_ref[...])
    kext_scr[WINDOW:, :] = k.astype(BF16)

    qt = _dot_nt(wqt_ref[...], h)
    cost = cost_ref[...]
    sint = sint_ref[...]
    pieces = []
    for hd in range(ATTN_HEADS):
        x1 = qt[hd * ATTN_HEAD_DIM:hd * ATTN_HEAD_DIM + half, :]
        x2 = qt[hd * ATTN_HEAD_DIM + half:(hd + 1) * ATTN_HEAD_DIM, :]
        pieces.append(x1 * cost - x2 * sint)
        pieces.append(x2 * cost + x1 * sint)
    qrot = (jnp.concatenate(pieces, axis=0) * (ATTN_HEAD_DIM ** -0.5 * LOG2_E)).astype(BF16)
    for jb in range(nblk):
        qt_scr[jb] = qrot[:, jb * WINDOW:(jb + 1) * WINDOW]

    vt = _dot_nt(wvt_ref[...], h)
    vtb = vt.astype(BF16)
    for jb in range(nblk):
        vt_scr[jb + 1] = vtb[:, jb * WINDOW:(jb + 1) * WINDOW]

    @pl.when(t == nt - 1)
    def _():
        kw_ref[...] = k[tq - WINDOW:, :].T
        vw_ref[...] = vt[:, tq - WINDOW:]

    state_phase(1)
    zero_rows = jnp.zeros((ATTN_HEAD_DIM, ATTN_GROUP * WINDOW), BF16)
    groups = [(jb, kh) for jb in range(nblk) for kh in range(ATTN_KV_HEADS)]

    for gidx, (jb, kh) in enumerate(groups):
        heads = [kh * ATTN_GROUP + gi for gi in range(ATTN_GROUP)]
        pair = kh // 2
        kb = kext_scr[jb * WINDOW:(jb + 2) * WINDOW, pair * LANES:(pair + 1) * LANES]
        qg = jnp.concatenate(
            [qt_scr[jb, hd * ATTN_HEAD_DIM:(hd + 1) * ATTN_HEAD_DIM, :] for hd in heads], axis=1)
        qg = jnp.concatenate([qg, zero_rows] if kh % 2 == 0 else [zero_rows, qg], axis=0)
        st_scr[gidx] = _dot(kb, qg)

    g = _dot(h, wg_ref[...])

    key_r = lax.broadcasted_iota(jnp.int32, (WINDOW, WINDOW), 0)
    query_c = lax.broadcasted_iota(jnp.int32, (WINDOW, WINDOW), 1)
    from_prev = key_r > query_c
    prev_w = from_prev.astype(BF16)
    cur_w = 1.0 - prev_w
    for gidx, (jb, kh) in enumerate(groups):
        for gi in range(ATTN_GROUP):
            hd = kh * ATTN_GROUP + gi
            cols = slice(gi * WINDOW, (gi + 1) * WINDOW)
            s_prev = st_scr[gidx, 0:WINDOW, cols]
            if jb == 0:
                s_prev = jnp.where(t > 0, s_prev, NEG_INF)
            s = jnp.where(from_prev, s_prev, st_scr[gidx, WINDOW:, cols])
            sink = sink_ref[hd] * LOG2_E
            m = jnp.maximum(jnp.max(s, axis=0, keepdims=True), sink)
            pr = jnp.exp2(s - m)
            den = jnp.sum(pr, axis=0, keepdims=True) + jnp.exp2(sink - m)
            prb = pr.astype(BF16)
            pt_scr[gidx, 0:WINDOW, cols] = prb * prev_w
            pt_scr[gidx, WINDOW:, cols] = prb * cur_w
            inv_scr[jb, hd:hd + 1, :] = 1.0 / den

    for gidx, (jb, kh) in enumerate(groups):
        vband = jnp.concatenate([vt_scr[jb, kh * ATTN_HEAD_DIM:(kh + 1) * ATTN_HEAD_DIM, :],
                                 vt_scr[jb + 1, kh * ATTN_HEAD_DIM:(kh + 1) * ATTN_HEAD_DIM, :]],
                                axis=1)
        otg = _dot(vband, pt_scr[gidx])
        for gi in range(ATTN_GROUP):
            hd = kh * ATTN_GROUP + gi
            ot_scr[jb, hd * ATTN_HEAD_DIM:(hd + 1) * ATTN_HEAD_DIM, :] = (
                otg[:, gi * WINDOW:(gi + 1) * WINDOW] * inv_scr[jb, hd:hd + 1, :])

    state_phase(2)

    parts = 2 if nblk % 2 == 0 else 1
    rows = _row_splits(tq, parts)
    ogs = []
    for r in rows:
        o = jnp.concatenate([ot_scr[jb].T for jb in range(r.start // WINDOW, r.stop // WINDOW)],
                            axis=0)
        ogs.append((o * _silu(g[r, :])).astype(BF16))
    _out_proj_and_residual(rows, ogs, x, emb, gpost_ref[...], wout_ref, wgate_ref, xo_ref)
    state_phase(3)

    kext_scr[0:WINDOW, :] = kext_scr[tq:tq + WINDOW, :]
    vt_scr[0] = vt_scr[nblk]

    @pl.when(step == n_steps - 1)
    def _():
        for slot in range(2):
            for i in range(per_phase):
                load_copy(last_seq, slot, i).wait()
                store_copy(last_seq, slot, i).wait()


def _const_spec(shape):
    nd = len(shape)
    return pl.BlockSpec(shape, lambda *_: (0,) * nd)


def _attn_prompt_layer(layer, x, p_all, tables, gpre, gpost, sinks, w_qt, w_k, w_vt, w_g,
                       w_out, w_gate, w_ple, ret_qkv, ret_state, dec_len):
    nb, seq, d = x.shape
    tq = min(512, seq)
    nblk = tq // WINDOW
    nt = seq // tq
    n_seq = ret_state.shape[0] // (nb * nt)
    assert n_seq * nb * nt == ret_state.shape[0] and n_seq % STATE_PHASES == 0, (
        ret_state.shape, nb, nt)
    ple = p_all.shape[-1]
    cos, sin, cost, sint = tables
    kernel = functools.partial(_attn_prompt_kernel, tq=tq, n_seq=n_seq, dec_len=dec_len)
    tok = lambda b, t: (b, t, 0)
    win_blk = lambda b, t: (b, 0, 0)
    stok = lambda b, t: (b * nt + t, 0)
    consts = (gpre, gpost, w_qt, w_k, w_vt, w_g, w_out, w_gate, w_ple)
    rq, rk, rkd, rv = ret_qkv
    st_shape = (2, n_seq // STATE_PHASES) + ret_state.shape[1:]
    return pl.pallas_call(
        kernel,
        grid=(nb, nt),
        in_specs=[
            pl.BlockSpec(memory_space=pltpu.SMEM),
            pl.BlockSpec((None, tq, d), tok),
            pl.BlockSpec((None, None, tq, ple), lambda b, t: (layer, b, t, 0)),
            pl.BlockSpec((tq, LANES), lambda b, t: (t, 0)),
            pl.BlockSpec((tq, LANES), lambda b, t: (t, 0)),
            pl.BlockSpec((ATTN_HEAD_DIM // 2, tq), lambda b, t: (0, t)),
            pl.BlockSpec((ATTN_HEAD_DIM // 2, tq), lambda b, t: (0, t)),
        ] + [_const_spec(c.shape) for c in consts] + [
            pl.BlockSpec((n_seq * dec_len, RET_QK_W), stok),
            pl.BlockSpec((n_seq * dec_len, RET_QK_W), stok),
            pl.BlockSpec((n_seq * dec_len, RET_QK_W), stok),
            pl.BlockSpec((n_seq * dec_len, RET_VW), stok),
            pl.BlockSpec(memory_space=pl.ANY),
        ],
        out_specs=[
            pl.BlockSpec((None, tq, d), tok),
            pl.BlockSpec((None, ATTN_KV_DIM, WINDOW), win_blk),
            pl.BlockSpec((None, ATTN_KV_DIM, WINDOW), win_blk),
            pl.BlockSpec((n_seq * dec_len, RET_VW), stok),
            pl.BlockSpec(memory_space=pl.ANY),
        ],
        out_shape=[
            jax.ShapeDtypeStruct(x.shape, F32),
            jax.ShapeDtypeStruct((nb, ATTN_KV_DIM, WINDOW), F32),
            jax.ShapeDtypeStruct((nb, ATTN_KV_DIM, WINDOW), F32),
            jax.ShapeDtypeStruct((rq.shape[0], RET_VW), F32),
            jax.ShapeDtypeStruct(ret_state.shape, F32),
        ],
        scratch_shapes=[
            pltpu.VMEM((nblk, ATTN_Q_DIM, WINDOW), BF16),
            pltpu.VMEM((tq + WINDOW, ATTN_KV_DIM), BF16),
            pltpu.VMEM((nblk + 1, ATTN_KV_DIM, WINDOW), BF16),
            pltpu.VMEM((nblk * ATTN_KV_HEADS, 2 * WINDOW, ATTN_GROUP * WINDOW), F32),
            pltpu.VMEM((nblk * ATTN_KV_HEADS, 2 * WINDOW, ATTN_GROUP * WINDOW), BF16),
            pltpu.VMEM((nblk, ATTN_HEADS, WINDOW), F32),
            pltpu.VMEM((nblk, ATTN_Q_DIM, WINDOW), F32),
            pltpu.VMEM(st_shape, F32),
            pltpu.VMEM(st_shape, F32),
            pltpu.SemaphoreType.DMA((2, n_seq // STATE_PHASES)),
            pltpu.SemaphoreType.DMA((2, n_seq // STATE_PHASES)),
        ],
        compiler_params=pltpu.CompilerParams(
            dimension_semantics=("arbitrary", "arbitrary"),
            vmem_limit_bytes=ATTN_PROMPT_VMEM_LIMIT_BYTES),
        name="attn_prompt_layer",
    )(sinks, x, p_all, cos, sin, cost, sint, *consts, rq, rk, rkd, rv, ret_state)


def _attn_sample_kernel(sink_ref, x_ref, p_ref, cos_ref, sin_ref, cost_ref, sint_ref,
                        gpre_ref, gpost_ref, wq_ref, wkt_ref, wvt_ref, wg_ref,
                        wout_ref, wgate_ref, wple_ref, kc_ref, vc_ref,
                        xo_ref, ko_ref, vo_ref,
                        q_scr, knt_scr, vnt_scr, o_scr, *, n_seq, dec_len):
    half = ATTN_HEAD_DIM // 2
    x = x_ref[...]
    h = _rms_norm(x, gpre_ref[...]).astype(BF16)
    q_scr[...] = (_rope_attn(_dot(h, wq_ref[...]), cos_ref[...], sin_ref[...])
                  * (ATTN_HEAD_DIM ** -0.5 * LOG2_E))
    knt = _dot_nt(wkt_ref[...], h)
    cost = cost_ref[...]
    sint = sint_ref[...]
    pieces = []
    for kh in range(ATTN_KV_HEADS):
        x1 = knt[kh * ATTN_HEAD_DIM:kh * ATTN_HEAD_DIM + half, :]
        x2 = knt[kh * ATTN_HEAD_DIM + half:(kh + 1) * ATTN_HEAD_DIM, :]
        pieces.append(x1 * cost - x2 * sint)
        pieces.append(x2 * cost + x1 * sint)
    knt_scr[...] = jnp.concatenate(pieces, axis=0)
    vnt_scr[...] = _dot_nt(wvt_ref[...], h)

    rows = 2 * dec_len
    tok = lax.broadcasted_iota(jnp.int32, (rows, 2 * WINDOW), 0) % dec_len
    col = lax.broadcasted_iota(jnp.int32, (rows, 2 * WINDOW), 1)
    upper_row = lax.broadcasted_iota(jnp.int32, (rows, 1), 0) >= dec_len
    lane = lax.broadcasted_iota(jnp.int32, (ATTN_HEAD_DIM, WINDOW), 1)
    keep_old = lane < WINDOW - dec_len
    zeros = jnp.zeros((ATTN_HEAD_DIM, 2 * WINDOW), BF16)

    def select_head(t, parity):
        return jnp.concatenate([t, zeros] if parity == 0 else [zeros, t], axis=0)

    def scores(s):
        tok_rows = slice(s * dec_len, (s + 1) * dec_len)
        new_col = col - (WINDOW + s * dec_len)
        mask = ((col < WINDOW) & (col > tok)) | ((new_col >= 0) & (new_col <= tok))
        new_shift = (WINDOW - dec_len - s * dec_len) % WINDOW
        out = []
        for kh in range(ATTN_KV_HEADS):
            hrows = slice(kh * ATTN_HEAD_DIM, (kh + 1) * ATTN_HEAD_DIM)
            kc = kc_ref[s, kh]
            vc = vc_ref[s, kh]
            kn = knt_scr[hrows, :]
            vn = vnt_scr[hrows, :]
            kn_at_end = pltpu.roll(kn, new_shift, axis=1) if new_shift else kn
            vn_at_end = pltpu.roll(vn, new_shift, axis=1) if new_shift else vn
            ko_ref[s, kh] = jnp.where(keep_old, pltpu.roll(kc, WINDOW - dec_len, axis=1), kn_at_end)
            vo_ref[s, kh] = jnp.where(keep_old, pltpu.roll(vc, WINDOW - dec_len, axis=1), vn_at_end)

            kt = jnp.concatenate([kc, kn], axis=1).astype(BF16)
            vt = jnp.concatenate([vc, vn], axis=1).astype(BF16)
            qp = jnp.concatenate([q_scr[tok_rows, (2 * kh + c) * LANES:(2 * kh + c + 1) * LANES]
                                  for c in range(2)], axis=0).astype(BF16)
            sc = [jnp.where(mask, _dot(qp, select_head(kt, parity)), NEG_INF)
                  for parity in range(2)]
            out.append((sc, vt))
        return out

    def values(s, per_head):
        tok_rows = slice(s * dec_len, (s + 1) * dec_len)
        probs = []
        for kh, (scs, vt) in enumerate(per_head):
            for parity, sc in enumerate(scs):
                hd_lo, hd_hi = 4 * kh + parity, 4 * kh + 2 + parity
                sink = jnp.where(upper_row, sink_ref[hd_hi], sink_ref[hd_lo]) * LOG2_E
                m = jnp.maximum(jnp.max(sc, axis=-1, keepdims=True), sink)
                pr = jnp.exp2(sc - m)
                den = jnp.sum(pr, axis=-1, keepdims=True) + jnp.exp2(sink - m)
                probs.append((pr / den).astype(BF16))
        for kh, (scs, vt) in enumerate(per_head):
            o_pair = (_dot_nt(probs[2 * kh], select_head(vt, 0))
                      + _dot_nt(probs[2 * kh + 1], select_head(vt, 1)))
            for c in range(2):
                o_scr[tok_rows, (2 * kh + c) * LANES:(2 * kh + c + 1) * LANES] = (
                    o_pair[c * dec_len:(c + 1) * dec_len, :])

    group = min(SAMPLE_ATTN_PIPELINE_GROUP, n_seq)
    pending = [scores(s) for s in range(group)]
    for s0 in range(0, n_seq, group):
        upcoming = [scores(s) for s in range(s0 + group, min(s0 + 2 * group, n_seq))]
        for i, per_head in enumerate(pending):
            values(s0 + i, per_head)
        pending = upcoming

    g = _dot(h, wg_ref[...])
    emb = _dot(p_ref[...].astype(BF16), wple_ref[...])
    og = (o_scr[...] * _silu(g)).astype(BF16)
    _out_proj_and_residual(_row_splits(x.shape[0], 1), [og], x, emb, gpost_ref[...],
                           wout_ref, wgate_ref, xo_ref)


def _attn_sample_layer(layer, x, p_all, tables, gpre, gpost, sinks, w_q, w_kt, w_vt, w_g,
                       w_out, w_gate, w_ple, cache_kt, cache_vt, dec_len):
    n_tok, d = x.shape
    n_all = cache_kt.shape[0]
    n_seq = WINDOW // dec_len
    tq = n_seq * dec_len
    assert tq == WINDOW and n_all % n_seq == 0, (dec_len, n_all)
    ple = p_all.shape[-1]
    cos, sin, cost, sint = tables
    kernel = functools.partial(_attn_sample_kernel, n_seq=n_seq, dec_len=dec_len)
    tok = lambda i: (i, 0)
    cache_shape = (n_seq, ATTN_KV_HEADS, ATTN_HEAD_DIM, WINDOW)
    cache_blk = lambda i: (i, 0, 0, 0)
    consts = (cos, sin, cost, sint, gpre, gpost, w_q, w_kt, w_vt, w_g, w_out, w_gate, w_ple)
    return pl.pallas_call(
        kernel,
        grid=(n_all // n_seq,),
        in_specs=[
            pl.BlockSpec(memory_space=pltpu.SMEM),
            pl.BlockSpec((tq, d), tok),
            pl.BlockSpec((None, tq, ple), lambda i: (layer, i, 0)),
        ] + [_const_spec(c.shape) for c in consts] + [
            pl.BlockSpec(cache_shape, cache_blk),
            pl.BlockSpec(cache_shape, cache_blk),
        ],
        out_specs=[
            pl.BlockSpec((tq, d), tok),
            pl.BlockSpec(cache_shape, cache_blk),
            pl.BlockSpec(cache_shape, cache_blk),
        ],
        out_shape=[
            jax.ShapeDtypeStruct(x.shape, F32),
            jax.ShapeDtypeStruct(cache_kt.shape, F32),
            jax.ShapeDtypeStruct(cache_vt.shape, F32),
        ],
        scratch_shapes=[
            pltpu.VMEM((tq, ATTN_Q_DIM), F32),
            pltpu.VMEM((ATTN_KV_DIM, tq), F32),
            pltpu.VMEM((ATTN_KV_DIM, tq), F32),
            pltpu.VMEM((tq, ATTN_Q_DIM), F32),
        ],
        compiler_params=pltpu.CompilerParams(
            dimension_semantics=("arbitrary",),
            vmem_limit_bytes=VMEM_LIMIT_BYTES),
        name="attn_sample_layer",
    )(sinks, x, p_all, *consts, cache_kt, cache_vt)


RET_GATE_COL0 = 2 * RET_QK_W + RET_VW


def _ret_project(h, cos, sin, win_ref, chunk_len):
    q0, k0, v0, g0 = 0, RET_QK_W, 2 * RET_QK_W, RET_GATE_COL0
    q = _rope_ret(_dot(h, win_ref[:, q0:k0]), cos, sin)
    k = _rope_ret(_dot(h, win_ref[:, k0:v0]), cos, sin) * (RET_QK_DIM ** -0.5)
    v = _dot(h, win_ref[:, v0:g0])
    n = h.shape[0]
    idx = (lax.broadcasted_iota(jnp.int32, (n, 1), 0) % chunk_len).astype(F32)
    kd = jnp.concatenate(
        [k[:, hd * RET_QK_DIM:(hd + 1) * RET_QK_DIM]
         * jnp.exp((chunk_len - 1.0 - idx) * RET_LOG_GAMMA[hd]) for hd in range(RET_HEADS)], axis=1)
    return q, k, kd, v


def _ret_decay_mask(n, chunk_len, hd):
    ri = lax.broadcasted_iota(jnp.int32, (n, n), 0)
    ci = lax.broadcasted_iota(jnp.int32, (n, n), 1)
    diff = ri - ci
    ok = (diff >= 0) & ((ri // chunk_len) == (ci // chunk_len))
    return jnp.where(ok, jnp.exp(jnp.maximum(diff, 0).astype(F32) * RET_LOG_GAMMA[hd]), 0.0)


def _ret_gated(o, g):
    outs = []
    for hd in range(RET_HEADS):
        oh = o[:, hd * RET_V_DIM:(hd + 1) * RET_V_DIM]
        mu = jnp.mean(oh, axis=-1, keepdims=True)
        var = jnp.mean(jnp.square(oh - mu), axis=-1, keepdims=True)
        outs.append((oh - mu) * lax.rsqrt(var + NORM_EPS))
    return (jnp.concatenate(outs, axis=1) * _silu(g)).astype(BF16)


def _ret_prompt_kernel(x_ref, p_ref, cos_ref, sin_ref, gpre_ref, gpost_ref,
                       win_ref, wout_ref, wgate_ref, wple_ref,
                       xo_ref, st_ref,
                       q_scr, k_scr, kd_scr, v_scr, o_scr, g_scr, dm_scr, *, tq):
    b = pl.program_id(0)
    t = pl.program_id(1)

    @pl.when((b == 0) & (t == 0))
    def _():
        for hd in range(RET_HEADS):
            dm_scr[hd] = _ret_decay_mask(RET_CHUNK, RET_CHUNK, hd)

    @pl.when(t == 0)
    def _():
        st_ref[...] = jnp.zeros(st_ref.shape, F32)

    emb = _dot(p_ref[...].astype(BF16), wple_ref[...])
    x = x_ref[...]
    h = _rms_norm(x, gpre_ref[...]).astype(BF16)
    q, k, kd, v = _ret_project(h, cos_ref[...], sin_ref[...], win_ref, RET_CHUNK)
    q_scr[...] = q.astype(BF16)
    k_scr[...] = k.astype(BF16)
    kd_scr[...] = kd.astype(BF16)
    v_scr[...] = v.astype(BF16)

    idx = lax.broadcasted_iota(jnp.int32, (RET_CHUNK, 1), 0).astype(F32)
    heads = range(RET_HEADS)
    qs = [slice(hd * RET_QK_DIM, (hd + 1) * RET_QK_DIM) for hd in heads]
    vs = [slice(hd * RET_V_DIM, (hd + 1) * RET_V_DIM) for hd in heads]
    chunks = _row_splits(tq, tq // RET_CHUNK)
    gate_cols = iter(_row_splits(RET_VW, 2 * len(chunks)))

    def gate_piece():
        c = next(gate_cols)
        g_scr[:, c] = _dot(h, win_ref[:, RET_GATE_COL0 + c.start:RET_GATE_COL0 + c.stop])

    for rows in chunks:
        scores = [_dot_nt(q_scr[rows, qs[hd]], k_scr[rows, qs[hd]]) for hd in heads]
        cross = [_dot(q_scr[rows, qs[hd]], st_ref[hd].astype(BF16)) for hd in heads]
        gate_piece()
        for hd in heads:
            st_ref[hd] = (math.exp(RET_CHUNK * RET_LOG_GAMMA[hd]) * st_ref[hd]
                          + _dot_tn(kd_scr[rows, qs[hd]], v_scr[rows, vs[hd]]))
        for hd in heads:
            inner = _dot((scores[hd] * dm_scr[hd]).astype(BF16), v_scr[rows, vs[hd]])
            o_scr[rows, vs[hd]] = inner + cross[hd] * jnp.exp((idx + 1.0) * RET_LOG_GAMMA[hd])
        gate_piece()

    ogs = [_ret_gated(o_scr[r, :], g_scr[r, :]) for r in chunks]
    _out_proj_and_residual(chunks, ogs, x, emb, gpost_ref[...], wout_ref, wgate_ref, xo_ref)


def _ret_prompt_layer(layer, x, p_all, cos, sin, gpre, gpost, w_in, w_out, w_gate, w_ple):
    nb, seq, d = x.shape
    tq = min(512, seq)
    ple = p_all.shape[-1]
    kernel = functools.partial(_ret_prompt_kernel, tq=tq)
    tok = lambda b, t: (b, t, 0)
    return pl.pallas_call(
        kernel,
        grid=(nb, seq // tq),
        in_specs=[
            pl.BlockSpec((None, tq, d), tok),
            pl.BlockSpec((None, None, tq, ple), lambda b, t: (layer, b, t, 0)),
            pl.BlockSpec((tq, LANES), lambda b, t: (t, 0)),
            pl.BlockSpec((tq, LANES), lambda b, t: (t, 0)),
            _const_spec(gpre.shape), _const_spec(gpost.shape),
            _const_spec(w_in.shape), _const_spec(w_out.shape),
            _const_spec(w_gate.shape), _const_spec(w_ple.shape),
        ],
        out_specs=[
            pl.BlockSpec((None, tq, d), tok),
            pl.BlockSpec((None, RET_HEADS, RET_QK_DIM, RET_V_DIM), lambda b, t: (b, 0, 0, 0)),
        ],
        out_shape=[
            jax.ShapeDtypeStruct(x.shape, F32),
            jax.ShapeDtypeStruct((nb, RET_HEADS, RET_QK_DIM, RET_V_DIM), F32),
        ],
        scratch_shapes=[
            pltpu.VMEM((tq, RET_QK_W), BF16),
            pltpu.VMEM((tq, RET_QK_W), BF16),
            pltpu.VMEM((tq, RET_QK_W), BF16),
            pltpu.VMEM((tq, RET_VW), BF16),
            pltpu.VMEM((tq, RET_VW), F32),
            pltpu.VMEM((tq, RET_VW), F32),
            pltpu.VMEM((RET_HEADS, RET_CHUNK, RET_CHUNK), F32),
        ],
        compiler_params=pltpu.CompilerParams(
            dimension_semantics=("arbitrary", "arbitrary"),
            vmem_limit_bytes=VMEM_LIMIT_BYTES),
        name="ret_prompt_layer",
    )(x, p_all, cos, sin, gpre, gpost, w_in, w_out, w_gate, w_ple)


def _ret_sample_proj_kernel(x_ref, cos_ref, sin_ref, gpre_ref, win_ref,
                            q_ref, k_ref, kd_ref, v_ref, g_ref, *, dec_len):
    h = _rms_norm(x_ref[...], gpre_ref[...]).astype(BF16)
    q, k, kd, v = _ret_project(h, cos_ref[...], sin_ref[...], win_ref, dec_len)
    q_ref[...] = q
    k_ref[...] = k
    kd_ref[...] = kd
    v_ref[...] = v
    g_ref[...] = _dot(h, win_ref[:, RET_GATE_COL0:])


def _ret_sample_proj(x, cos, sin, gpre, w_in, dec_len):
    n_tok, d = x.shape
    tq = min(RET_SAMPLE_TILE, n_tok)
    kernel = functools.partial(_ret_sample_proj_kernel, dec_len=dec_len)
    tok = lambda i: (i, 0)
    widths = (RET_QK_W, RET_QK_W, RET_QK_W, RET_VW, RET_VW)
    return pl.pallas_call(
        kernel,
        grid=(n_tok // tq,),
        in_specs=[
            pl.BlockSpec((tq, d), tok),
            _const_spec((tq, LANES)), _const_spec((tq, LANES)),
            _const_spec(gpre.shape), _const_spec(w_in.shape),
        ],
        out_specs=[pl.BlockSpec((tq, w), tok) for w in widths],
        out_shape=[jax.ShapeDtypeStruct((n_tok, w), F32) for w in widths],
        compiler_params=pltpu.CompilerParams(
            dimension_semantics=("arbitrary",),
            vmem_limit_bytes=VMEM_LIMIT_BYTES),
        name="ret_sample_proj",
    )(x, cos, sin, gpre, w_in)


def _ret_state_update_tile(q_ref, k_ref, kd_ref, v_ref, st_in, st_out, o_ref, *,
                           first_seq, n_seq, dec_len):
    pad_rows = 2 * dec_len
    idx = lax.broadcasted_iota(jnp.int32, (pad_rows, 1), 0).astype(F32)

    def tok_rows(s):
        return slice((first_seq + s) * dec_len, (first_seq + s + 1) * dec_len)

    def padded(ref, s, cols):
        val = ref[tok_rows(s), cols]
        return jnp.concatenate([val, jnp.zeros_like(val)], axis=0).astype(BF16)

    units = [(s, hd) for s in range(n_seq) for hd in range(RET_HEADS)]
    qs = [slice(hd * RET_QK_DIM, (hd + 1) * RET_QK_DIM) for hd in range(RET_HEADS)]
    vs = [slice(hd * RET_V_DIM, (hd + 1) * RET_V_DIM) for hd in range(RET_HEADS)]
    qh = {(s, hd): padded(q_ref, s, qs[hd]) for s, hd in units}
    vh = {(s, hd): padded(v_ref, s, vs[hd]) for s, hd in units}
    scores = {(s, hd): _dot_nt(qh[s, hd], padded(k_ref, s, qs[hd])) for s, hd in units}
    cross = {(s, hd): _dot(qh[s, hd], st_in[s, hd].astype(BF16)) for s, hd in units}
    for s, hd in units:
        st_out[s, hd] = (math.exp(dec_len * RET_LOG_GAMMA[hd]) * st_in[s, hd]
                         + _dot_tn(padded(kd_ref, s, qs[hd]), vh[s, hd]))
    masks = [_ret_decay_mask(pad_rows, dec_len, hd) for hd in range(RET_HEADS)]
    for s, hd in units:
        inner = _dot((scores[s, hd] * masks[hd]).astype(BF16), vh[s, hd])
        out = inner + cross[s, hd] * jnp.exp((idx + 1.0) * RET_LOG_GAMMA[hd])
        o_ref[tok_rows(s), vs[hd]] = out[0:dec_len, :]


def _ret_sample_finish_kernel(o_ref, g_ref, x_ref, p_ref, gpost_ref, wout_ref, wgate_ref, wple_ref,
                              xo_ref):
    emb = _dot(p_ref[...].astype(BF16), wple_ref[...])
    x = x_ref[...]
    rows = _row_splits(x.shape[0], 2)
    ogs = [_ret_gated(o_ref[r, :], g_ref[r, :]) for r in rows]
    _out_proj_and_residual(rows, ogs, x, emb, gpost_ref[...], wout_ref, wgate_ref, xo_ref)


def _ret_sample_finish(layer, o, g, x, p_all, gpost, w_out, w_gate, w_ple):
    n_tok, d = x.shape
    tq = min(RET_SAMPLE_TILE, n_tok)
    ple = p_all.shape[-1]
    tok = lambda i: (i, 0)
    return pl.pallas_call(
        _ret_sample_finish_kernel,
        grid=(n_tok // tq,),
        in_specs=[
            pl.BlockSpec((tq, RET_VW), tok), pl.BlockSpec((tq, RET_VW), tok),
            pl.BlockSpec((tq, d), tok),
            pl.BlockSpec((None, tq, ple), lambda i: (layer, i, 0)),
            _const_spec(gpost.shape), _const_spec(w_out.shape),
            _const_spec(w_gate.shape), _const_spec(w_ple.shape),
        ],
        out_specs=pl.BlockSpec((tq, d), tok),
        out_shape=jax.ShapeDtypeStruct(x.shape, F32),
        compiler_params=pltpu.CompilerParams(
            dimension_semantics=("arbitrary",),
            vmem_limit_bytes=VMEM_LIMIT_BYTES),
        name="ret_sample_finish",
    )(o, g, x, p_all, gpost, w_out, w_gate, w_ple)


def _rope_tables(pos, head_dim):
    half = head_dim // 2
    inv = ROPE_THETA ** (-np.arange(half, dtype=np.float64) / half)
    ang = np.asarray(pos, np.float64)[:, None] * inv[None, :]
    return np.cos(ang).astype(np.float32), np.sin(ang).astype(np.float32)


def _attn_tables(pos):
    cos, sin = _rope_tables(pos, ATTN_HEAD_DIM)
    reps = LANES // ATTN_HEAD_DIM
    return (np.tile(np.concatenate([cos, cos], axis=1), (1, reps)),
            np.tile(np.concatenate([-sin, sin], axis=1), (1, reps)))


def kernel(x_prompt, x_sample, cache_k_win, cache_v_win, state_ret, p_prompt, p_sample, pre_norm,
           post_norm, w_in_attn, attn_sinks, w_out_attn, w_in_ret, w_out_ret, w_ple, w_ple_gate):
    depth = p_prompt.shape[0]
    nb, seq, d = x_prompt.shape
    n_dec, dec_len, _ = x_sample.shape
    n_stok = n_dec * dec_len
    assert seq % 512 == 0 or seq in (128, 256), seq

    pos_p = np.arange(seq)
    pos_s = PAST_LEN + np.arange(dec_len)
    cos_ap, sin_ap = _attn_tables(pos_p)
    cos_apt, sin_apt = (np.ascontiguousarray(a.T) for a in _rope_tables(pos_p, ATTN_HEAD_DIM))
    cos_as, sin_as = _attn_tables(pos_s)
    cos_rp, sin_rp = _rope_tables(pos_p, RET_QK_DIM)
    cos_rs, sin_rs = _rope_tables(pos_s, RET_QK_DIM)
    attn_s_tile = WINDOW // dec_len
    ret_s_tile = min(RET_SAMPLE_TILE, n_stok) // dec_len
    cos_as, sin_as = (np.tile(a, (attn_s_tile, 1)) for a in (cos_as, sin_as))
    cos_ast, sin_ast = (np.ascontiguousarray(np.tile(a, (attn_s_tile, 1)).T)
                        for a in _rope_tables(pos_s, ATTN_HEAD_DIM))
    cos_rs, sin_rs = (np.tile(a, (ret_s_tile, 1)) for a in (cos_rs, sin_rs))
    (cos_ap, sin_ap, cos_apt, sin_apt, cos_as, sin_as, cos_ast, sin_ast,
     cos_rp, sin_rp, cos_rs, sin_rs) = (
        jnp.asarray(a) for a in
        (cos_ap, sin_ap, cos_apt, sin_apt, cos_as, sin_as, cos_ast, sin_ast,
         cos_rp, sin_rp, cos_rs, sin_rs))

    assert depth == 2, depth
    xs = x_sample.reshape(n_stok, d)
    p_s = p_sample.reshape(depth, n_stok, p_sample.shape[-1])
    bf = lambda w: w.astype(BF16)
    gpre = [pre_norm[i][None, :] for i in range(depth)]
    gpost = [post_norm[i][None, :] for i in range(depth)]
    w_gate = [bf(w_ple_gate[i]) for i in range(depth)]
    w_pl = [bf(w_ple[i]) for i in range(depth)]

    w_in = bf(w_in_attn[0])
    w_out = bf(w_out_attn[0])
    sinks = attn_sinks[0]
    k0, v0, g0 = ATTN_Q_DIM, ATTN_Q_DIM + ATTN_KV_DIM, ATTN_Q_DIM + 2 * ATTN_KV_DIM
    w_q, w_k, w_v, w_g = w_in[:, :k0], w_in[:, k0:v0], w_in[:, v0:g0], w_in[:, g0:]
    to_t = lambda a: a.transpose(0, 2, 3, 1)
    from_t = lambda a: a.transpose(0, 3, 1, 2)
    xs, knt, vnt = _attn_sample_layer(
        0, xs, p_s, (cos_as, sin_as, cos_ast, sin_ast), gpre[0], gpost[0], sinks,
        w_q, w_k.T, w_v.T, w_g, w_out, w_gate[0], w_pl[0],
        to_t(cache_k_win[0]), to_t(cache_v_win[0]), dec_len)

    w_in_r = bf(w_in_ret[0])
    w_out_r = bf(w_out_ret[0])
    rq, rk, rkd, rv, rg = _ret_sample_proj(xs, cos_rs, sin_rs, gpre[1], w_in_r, dec_len)
    xp, kpt, vpt, ro, ret_state_sample = _attn_prompt_layer(
        0, x_prompt, p_prompt, (cos_ap, sin_ap, cos_apt, sin_apt), gpre[0], gpost[0], sinks,
        w_q.T, w_k, w_v.T, w_g, w_out, w_gate[0], w_pl[0],
        (rq, rk, rkd, rv), state_ret[0], dec_len)
    head_shape = (nb, ATTN_KV_HEADS, ATTN_HEAD_DIM, WINDOW)

    xp, ret_state_prompt = _ret_prompt_layer(1, xp, p_prompt, cos_rp, sin_rp, gpre[1], gpost[1],
                                             w_in_r, w_out_r, w_gate[1], w_pl[1])
    xs = _ret_sample_finish(1, ro, rg, xs, p_s, gpost[1], w_out_r, w_gate[1], w_pl[1])
    return (xp, xs.reshape(n_dec, dec_len, d),
            from_t(kpt.reshape(head_shape))[None], from_t(vpt.reshape(head_shape))[None],
            from_t(knt)[None], from_t(vnt)[None], ret_state_prompt[None], ret_state_sample[None])
```

```python
import functools
import math

import numpy as np
import jax
import jax.numpy as jnp
from jax import lax
from jax.experimental import pallas as pl
from jax.experimental.pallas import tpu as pltpu

F32 = jnp.float32
BF16 = jnp.bfloat16

PAST_LEN = 16384
ROPE_THETA = 10000.0
NORM_EPS = 1e-6
NEG_INF = -1e30
WINDOW = 128
ATTN_HEADS = 16
ATTN_KV_HEADS = 4
ATTN_GROUP = ATTN_HEADS // ATTN_KV_HEADS
ATTN_HEAD_DIM = 64
ATTN_Q_DIM = ATTN_HEADS * ATTN_HEAD_DIM
ATTN_KV_DIM = ATTN_KV_HEADS * ATTN_HEAD_DIM
RET_HEADS = 4
RET_QK_DIM = 256
RET_V_DIM = 512
RET_QK_W = RET_HEADS * RET_QK_DIM
RET_VW = RET_HEADS * RET_V_DIM
RET_CHUNK = 256
LANES = 128

LOG2_E = math.log2(math.e)
RET_LOG_GAMMA = tuple(math.log1p(-(2.0 ** (-5.0 - h))) for h in range(RET_HEADS))

VMEM_LIMIT_BYTES = 56 * 1024 * 1024
ATTN_PROMPT_VMEM_LIMIT_BYTES = 50 * 1024 * 1024
STATE_PHASES = 4
RET_SAMPLE_TILE = 256
SAMPLE_ATTN_PIPELINE_GROUP = 8


def _dot(a, b):
    return jnp.dot(a, b, preferred_element_type=F32)


def _dot_nt(a, b):
    return lax.dot_general(a, b, (((1,), (1,)), ((), ())), preferred_element_type=F32)


def _dot_tn(a, b):
    return lax.dot_general(a, b, (((0,), (0,)), ((), ())), preferred_element_type=F32)


def _rms_norm(x, g):
    return x * lax.rsqrt(jnp.mean(x * x, axis=-1, keepdims=True) + NORM_EPS) * g


def _silu(g):
    return g * jax.nn.sigmoid(g)


def _row_splits(n, parts):
    step = n // parts
    return [slice(i * step, (i + 1) * step) for i in range(parts)]


def _out_proj_and_residual(rows, ogs, x, emb, gpost, wout_ref, wgate_ref, xo_ref):
    ys = [_dot(og, wout_ref[...]) for og in ogs]
    x1s = [x[r, :] + _rms_norm(y, gpost) for r, y in zip(rows, ys)]
    gates = [_dot(x1.astype(BF16), wgate_ref[...]) for x1 in x1s]
    for r, x1, gate in zip(rows, x1s, gates):
        xo_ref[r, :] = x1 + jax.nn.sigmoid(gate) * emb[r, :]


def _rope_attn(x, cos, sin_signed):
    lane = lax.broadcasted_iota(jnp.int32, (1, LANES), 1)
    first_half = (lane % ATTN_HEAD_DIM) < (ATTN_HEAD_DIM // 2)
    outs = []
    for j in range(x.shape[1] // LANES):
        xj = x[:, j * LANES:(j + 1) * LANES]
        fwd = pltpu.roll(xj, LANES - ATTN_HEAD_DIM // 2, axis=1)
        bwd = pltpu.roll(xj, ATTN_HEAD_DIM // 2, axis=1)
        outs.append(xj * cos + jnp.where(first_half, fwd, bwd) * sin_signed)
    return jnp.concatenate(outs, axis=1)


def _rope_ret(x, cos, sin):
    half = RET_QK_DIM // 2
    outs = []
    for h in range(x.shape[1] // RET_QK_DIM):
        x1 = x[:, h * RET_QK_DIM:h * RET_QK_DIM + half]
        x2 = x[:, h * RET_QK_DIM + half:(h + 1) * RET_QK_DIM]
        outs.append(x1 * cos - x2 * sin)
        outs.append(x2 * cos + x1 * sin)
    return jnp.concatenate(outs, axis=1)


def _attn_prompt_kernel(sink_ref, x_ref, p_ref, cos_ref, sin_ref, cost_ref, sint_ref,
                        gpre_ref, gpost_ref, wqt_ref, wk_ref, wvt_ref, wg_ref,
                        wout_ref, wgate_ref, wple_ref,
                        rq_ref, rk_ref, rkd_ref, rv_ref, rstate_hbm,
                        xo_ref, kw_ref, vw_ref, ro_ref, rstate_out_hbm,
                        qt_scr, kext_scr, vt_scr, st_scr, pt_scr, inv_scr, ot_scr,
                        rst_in, rst_out, sem_in, sem_out, *, tq, n_seq, dec_len):
    nblk = tq // WINDOW
    half = ATTN_HEAD_DIM // 2
    t = pl.program_id(1)
    nt = pl.num_programs(1)

    @pl.when(t == 0)
    def _():
        kext_scr[0:WINDOW, :] = jnp.zeros((WINDOW, ATTN_KV_DIM), BF16)
        vt_scr[0] = jnp.zeros((ATTN_KV_DIM, WINDOW), BF16)

    per_phase = n_seq // STATE_PHASES
    step = pl.program_id(0) * nt + t
    n_steps = pl.num_programs(0) * nt
    seq0 = step * n_seq
    last_seq = n_steps * n_seq - 1

    def load_copy(seq, slot, i):
        return pltpu.make_async_copy(rstate_hbm.at[seq], rst_in.at[slot, i], sem_in.at[slot, i])

    def store_copy(seq, slot, i):
        return pltpu.make_async_copy(rst_out.at[slot, i], rstate_out_hbm.at[seq],
                                     sem_out.at[slot, i])

    @pl.when(step == 0)
    def _():
        rst_out[...] = jnp.zeros(rst_out.shape, F32)
        for slot in range(2):
            for i in range(per_phase):
                load_copy(slot * per_phase + i, slot, i).start()
                store_copy(slot * per_phase + i, slot, i).start()

    def state_phase(phase):
        slot = phase % 2
        first = seq0 + phase * per_phase
        for i in range(per_phase):
            load_copy(first + i, slot, i).wait()
            store_copy(first + i, slot, i).wait()
        _ret_state_update_tile(rq_ref, rk_ref, rkd_ref, rv_ref, rst_in.at[slot], rst_out.at[slot],
                               ro_ref, first_seq=phase * per_phase, n_seq=per_phase,
                               dec_len=dec_len)
        for i in range(per_phase):
            store_copy(first + i, slot, i).start()
            load_copy(jnp.minimum(first + i + 2 * per_phase, last_seq), slot, i).start()

    state_phase(0)

    emb = _dot(p_ref[...].astype(BF16), wple_ref[...])
    x = x_ref[...]
    h = _rms_norm(x, gpre_ref[...]).astype(BF16)

    k = _rope_attn(_dot(h, wk_ref[...]), cos_ref[...], sin_ref[...])
    kext_scr[WINDOW:, :] = k.astype(BF16)

    qt = _dot_nt(wqt_ref[...], h)
    cost = cost_ref[...]
    sint = sint_ref[...]
    pieces = []
    for hd in range(ATTN_HEADS):
        x1 = qt[hd * ATTN_HEAD_DIM:hd * ATTN_HEAD_DIM + half, :]
        x2 = qt[hd * ATTN_HEAD_DIM + half:(hd + 1) * ATTN_HEAD_DIM, :]
        pieces.append(x1 * cost - x2 * sint)
        pieces.append(x2 * cost + x1 * sint)
    qrot = (jnp.concatenate(pieces, axis=0) * (ATTN_HEAD_DIM ** -0.5 * LOG2_E)).astype(BF16)
    for jb in range(nblk):
        qt_scr[jb] = qrot[:, jb * WINDOW:(jb + 1) * WINDOW]

    vt = _dot_nt(wvt_ref[...], h)
    vtb = vt.astype(BF16)
    for jb in range(nblk):
        vt_scr[jb + 1] = vtb[:, jb * WINDOW:(jb + 1) * WINDOW]

    @pl.when(t == nt - 1)
    def _():
        kw_ref[...] = k[tq - WINDOW:, :].T
        vw_ref[...] = vt[:, tq - WINDOW:]

    state_phase(1)
    zero_rows = jnp.zeros((ATTN_HEAD_DIM, ATTN_GROUP * WINDOW), BF16)
    groups = [(jb, kh) for jb in range(nblk) for kh in range(ATTN_KV_HEADS)]

    for gidx, (jb, kh) in enumerate(groups):
        heads = [kh * ATTN_GROUP + gi for gi in range(ATTN_GROUP)]
        pair = kh // 2
        kb = kext_scr[jb * WINDOW:(jb + 2) * WINDOW, pair * LANES:(pair + 1) * LANES]
        qg = jnp.concatenate(
            [qt_scr[jb, hd * ATTN_HEAD_DIM:(hd + 1) * ATTN_HEAD_DIM, :] for hd in heads], axis=1)
        qg = jnp.concatenate([qg, zero_rows] if kh % 2 == 0 else [zero_rows, qg], axis=0)
        st_scr[gidx] = _dot(kb, qg)

    g = _dot(h, wg_ref[...])

    key_r = lax.broadcasted_iota(jnp.int32, (WINDOW, WINDOW), 0)
    query_c = lax.broadcasted_iota(jnp.int32, (WINDOW, WINDOW), 1)
    from_prev = key_r > query_c
    prev_w = from_prev.astype(BF16)
    cur_w = 1.0 - prev_w
    for gidx, (jb, kh) in enumerate(groups):
        for gi in range(ATTN_GROUP):
            hd = kh * ATTN_GROUP + gi
            cols = slice(gi * WINDOW, (gi + 1) * WINDOW)
            s_prev = st_scr[gidx, 0:WINDOW, cols]
            if jb == 0:
                s_prev = jnp.where(t > 0, s_prev, NEG_INF)
            s = jnp.where(from_prev, s_prev, st_scr[gidx, WINDOW:, cols])
            sink = sink_ref[hd] * LOG2_E
            m = jnp.maximum(jnp.max(s, axis=0, keepdims=True), sink)
            pr = jnp.exp2(s - m)
            den = jnp.sum(pr, axis=0, keepdims=True) + jnp.exp2(sink - m)
            prb = pr.astype(BF16)
            pt_scr[gidx, 0:WINDOW, cols] = prb * prev_w
            pt_scr[gidx, WINDOW:, cols] = prb * cur_w
            inv_scr[jb, hd:hd + 1, :] = 1.0 / den

    for gidx, (jb, kh) in enumerate(groups):
        vband = jnp.concatenate([vt_scr[jb, kh * ATTN_HEAD_DIM:(kh + 1) * ATTN_HEAD_DIM, :],
                                 vt_scr[jb + 1, kh * ATTN_HEAD_DIM:(kh + 1) * ATTN_HEAD_DIM, :]],
                                axis=1)
        otg = _dot(vband, pt_scr[gidx])
        for gi in range(ATTN_GROUP):
            hd = kh * ATTN_GROUP + gi
            ot_scr[jb, hd * ATTN_HEAD_DIM:(hd + 1) * ATTN_HEAD_DIM, :] = (
                otg[:, gi * WINDOW:(gi + 1) * WINDOW] * inv_scr[jb, hd:hd + 1, :])

    state_phase(2)

    parts = 2 if nblk % 2 == 0 else 1
    rows = _row_splits(tq, parts)
    ogs = []
    for r in rows:
        o = jnp.concatenate([ot_scr[jb].T for jb in range(r.start // WINDOW, r.stop // WINDOW)],
                            axis=0)
        ogs.append((o * _silu(g[r, :])).astype(BF16))
    _out_proj_and_residual(rows, ogs, x, emb, gpost_ref[...], wout_ref, wgate_ref, xo_ref)
    state_phase(3)

    kext_scr[0:WINDOW, :] = kext_scr[tq:tq + WINDOW, :]
    vt_scr[0] = vt_scr[nblk]

    @pl.when(step == n_steps - 1)
    def _():
        for slot in range(2):
            for i in range(per_phase):
                load_copy(last_seq, slot, i).wait()
                store_copy(last_seq, slot, i).wait()


def _const_spec(shape):
    nd = len(shape)
    return pl.BlockSpec(shape, lambda *_: (0,) * nd)


def _attn_prompt_layer(layer, x, p_all, tables, gpre, gpost, sinks, w_qt, w_k, w_vt, w_g,
                       w_out, w_gate, w_ple, ret_qkv, ret_state, dec_len):
    nb, seq, d = x.shape
    tq = min(512, seq)
    nblk = tq // WINDOW
    nt = seq // tq
    n_seq = ret_state.shape[0] // (nb * nt)
    assert n_seq * nb * nt == ret_state.shape[0] and n_seq % STATE_PHASES == 0, (
        ret_state.shape, nb, nt)
    ple = p_all.shape[-1]
    cos, sin, cost, sint = tables
    kernel = functools.partial(_attn_prompt_kernel, tq=tq, n_seq=n_seq, dec_len=dec_len)
    tok = lambda b, t: (b, t, 0)
    win_blk = lambda b, t: (b, 0, 0)
    stok = lambda b, t: (b * nt + t, 0)
    consts = (gpre, gpost, w_qt, w_k, w_vt, w_g, w_out, w_gate, w_ple)
    rq, rk, rkd, rv = ret_qkv
    st_shape = (2, n_seq // STATE_PHASES) + ret_state.shape[1:]
    return pl.pallas_call(
        kernel,
        grid=(nb, nt),
        in_specs=[
            pl.BlockSpec(memory_space=pltpu.SMEM),
            pl.BlockSpec((None, tq, d), tok),
            pl.BlockSpec((None, None, tq, ple), lambda b, t: (layer, b, t, 0)),
            pl.BlockSpec((tq, LANES), lambda b, t: (t, 0)),
            pl.BlockSpec((tq, LANES), lambda b, t: (t, 0)),
            pl.BlockSpec((ATTN_HEAD_DIM // 2, tq), lambda b, t: (0, t)),
            pl.BlockSpec((ATTN_HEAD_DIM // 2, tq), lambda b, t: (0, t)),
        ] + [_const_spec(c.shape) for c in consts] + [
            pl.BlockSpec((n_seq * dec_len, RET_QK_W), stok),
            pl.BlockSpec((n_seq * dec_len, RET_QK_W), stok),
            pl.BlockSpec((n_seq * dec_len, RET_QK_W), stok),
            pl.BlockSpec((n_seq * dec_len, RET_VW), stok),
            pl.BlockSpec(memory_space=pl.ANY),
        ],
        out_specs=[
            pl.BlockSpec((None, tq, d), tok),
            pl.BlockSpec((None, ATTN_KV_DIM, WINDOW), win_blk),
            pl.BlockSpec((None, ATTN_KV_DIM, WINDOW), win_blk),
            pl.BlockSpec((n_seq * dec_len, RET_VW), stok),
            pl.BlockSpec(memory_space=pl.ANY),
        ],
        out_shape=[
            jax.ShapeDtypeStruct(x.shape, F32),
            jax.ShapeDtypeStruct((nb, ATTN_KV_DIM, WINDOW), F32),
            jax.ShapeDtypeStruct((nb, ATTN_KV_DIM, WINDOW), F32),
            jax.ShapeDtypeStruct((rq.shape[0], RET_VW), F32),
            jax.ShapeDtypeStruct(ret_state.shape, F32),
        ],
        scratch_shapes=[
            pltpu.VMEM((nblk, ATTN_Q_DIM, WINDOW), BF16),
            pltpu.VMEM((tq + WINDOW, ATTN_KV_DIM), BF16),
            pltpu.VMEM((nblk + 1, ATTN_KV_DIM, WINDOW), BF16),
            pltpu.VMEM((nblk * ATTN_KV_HEADS, 2 * WINDOW, ATTN_GROUP * WINDOW), F32),
            pltpu.VMEM((nblk * ATTN_KV_HEADS, 2 * WINDOW, ATTN_GROUP * WINDOW), BF16),
            pltpu.VMEM((nblk, ATTN_HEADS, WINDOW), F32),
            pltpu.VMEM((nblk, ATTN_Q_DIM, WINDOW), F32),
            pltpu.VMEM(st_shape, F32),
            pltpu.VMEM(st_shape, F32),
            pltpu.SemaphoreType.DMA((2, n_seq // STATE_PHASES)),
            pltpu.SemaphoreType.DMA((2, n_seq // STATE_PHASES)),
        ],
        compiler_params=pltpu.CompilerParams(
            dimension_semantics=("arbitrary", "arbitrary"),
            vmem_limit_bytes=ATTN_PROMPT_VMEM_LIMIT_BYTES),
        name="attn_prompt_layer",
    )(sinks, x, p_all, cos, sin, cost, sint, *consts, rq, rk, rkd, rv, ret_state)


def _attn_sample_kernel(sink_ref, x_ref, p_ref, cos_ref, sin_ref, cost_ref, sint_ref,
                        gpre_ref, gpost_ref, wq_ref, wkt_ref, wvt_ref, wg_ref,
                        wout_ref, wgate_ref, wple_ref, kc_ref, vc_ref,
                        xo_ref, ko_ref, vo_ref,
                        q_scr, knt_scr, vnt_scr, o_scr, *, n_seq, dec_len):
    half = ATTN_HEAD_DIM // 2
    x = x_ref[...]
    h = _rms_norm(x, gpre_ref[...]).astype(BF16)
    q_scr[...] = (_rope_attn(_dot(h, wq_ref[...]), cos_ref[...], sin_ref[...])
                  * (ATTN_HEAD_DIM ** -0.5 * LOG2_E))
    knt = _dot_nt(wkt_ref[...], h)
    cost = cost_ref[...]
    sint = sint_ref[...]
    pieces = []
    for kh in range(ATTN_KV_HEADS):
        x1 = knt[kh * ATTN_HEAD_DIM:kh * ATTN_HEAD_DIM + half, :]
        x2 = knt[kh * ATTN_HEAD_DIM + half:(kh + 1) * ATTN_HEAD_DIM, :]
        pieces.append(x1 * cost - x2 * sint)
        pieces.append(x2 * cost + x1 * sint)
    knt_scr[...] = jnp.concatenate(pieces, axis=0)
    vnt_scr[...] = _dot_nt(wvt_ref[...], h)

    rows = 2 * dec_len
    tok = lax.broadcasted_iota(jnp.int32, (rows, 2 * WINDOW), 0) % dec_len
    col = lax.broadcasted_iota(jnp.int32, (rows, 2 * WINDOW), 1)
    upper_row = lax.broadcasted_iota(jnp.int32, (rows, 1), 0) >= dec_len
    lane = lax.broadcasted_iota(jnp.int32, (ATTN_HEAD_DIM, WINDOW), 1)
    keep_old = lane < WINDOW - dec_len
    zeros = jnp.zeros((ATTN_HEAD_DIM, 2 * WINDOW), BF16)

    def select_head(t, parity):
        return jnp.concatenate([t, zeros] if parity == 0 else [zeros, t], axis=0)

    def scores(s):
        tok_rows = slice(s * dec_len, (s + 1) * dec_len)
        new_col = col - (WINDOW + s * dec_len)
        mask = ((col < WINDOW) & (col > tok)) | ((new_col >= 0) & (new_col <= tok))
        new_shift = (WINDOW - dec_len - s * dec_len) % WINDOW
        out = []
        for kh in range(ATTN_KV_HEADS):
            hrows = slice(kh * ATTN_HEAD_DIM, (kh + 1) * ATTN_HEAD_DIM)
            kc = kc_ref[s, kh]
            vc = vc_ref[s, kh]
            kn = knt_scr[hrows, :]
            vn = vnt_scr[hrows, :]
            kn_at_end = pltpu.roll(kn, new_shift, axis=1) if new_shift else kn
            vn_at_end = pltpu.roll(vn, new_shift, axis=1) if new_shift else vn
            ko_ref[s, kh] = jnp.where(keep_old, pltpu.roll(kc, WINDOW - dec_len, axis=1), kn_at_end)
            vo_ref[s, kh] = jnp.where(keep_old, pltpu.roll(vc, WINDOW - dec_len, axis=1), vn_at_end)

            kt = jnp.concatenate([kc, kn], axis=1).astype(BF16)
            vt = jnp.concatenate([vc, vn], axis=1).astype(BF16)
            qp = jnp.concatenate([q_scr[tok_rows, (2 * kh + c) * LANES:(2 * kh + c + 1) * LANES]
                                  for c in range(2)], axis=0).astype(BF16)
            sc = [jnp.where(mask, _dot(qp, select_head(kt, parity)), NEG_INF)
                  for parity in range(2)]
            out.append((sc, vt))
        return out

    def values(s, per_head):
        tok_rows = slice(s * dec_len, (s + 1) * dec_len)
        probs = []
        for kh, (scs, vt) in enumerate(per_head):
            for parity, sc in enumerate(scs):
                hd_lo, hd_hi = 4 * kh + parity, 4 * kh + 2 + parity
                sink = jnp.where(upper_row, sink_ref[hd_hi], sink_ref[hd_lo]) * LOG2_E
                m = jnp.maximum(jnp.max(sc, axis=-1, keepdims=True), sink)
                pr = jnp.exp2(sc - m)
                den = jnp.sum(pr, axis=-1, keepdims=True) + jnp.exp2(sink - m)
                probs.append((pr / den).astype(BF16))
        for kh, (scs, vt) in enumerate(per_head):
            o_pair = (_dot_nt(probs[2 * kh], select_head(vt, 0))
                      + _dot_nt(probs[2 * kh + 1], select_head(vt, 1)))
            for c in range(2):
                o_scr[tok_rows, (2 * kh + c) * LANES:(2 * kh + c + 1) * LANES] = (
                    o_pair[c * dec_len:(c + 1) * dec_len, :])

    group = min(SAMPLE_ATTN_PIPELINE_GROUP, n_seq)
    pending = [scores(s) for s in range(group)]
    for s0 in range(0, n_seq, group):
        upcoming = [scores(s) for s in range(s0 + group, min(s0 + 2 * group, n_seq))]
        for i, per_head in enumerate(pending):
            values(s0 + i, per_head)
        pending = upcoming

    g = _dot(h, wg_ref[...])
    emb = _dot(p_ref[...].astype(BF16), wple_ref[...])
    og = (o_scr[...] * _silu(g)).astype(BF16)
    _out_proj_and_residual(_row_splits(x.shape[0], 1), [og], x, emb, gpost_ref[...],
                           wout_ref, wgate_ref, xo_ref)


def _attn_sample_layer(layer, x, p_all, tables, gpre, gpost, sinks, w_q, w_kt, w_vt, w_g,
                       w_out, w_gate, w_ple, cache_kt, cache_vt, dec_len):
    n_tok, d = x.shape
    n_all = cache_kt.shape[0]
    n_seq = WINDOW // dec_len
    tq = n_seq * dec_len
    assert tq == WINDOW and n_all % n_seq == 0, (dec_len, n_all)
    ple = p_all.shape[-1]
    cos, sin, cost, sint = tables
    kernel = functools.partial(_attn_sample_kernel, n_seq=n_seq, dec_len=dec_len)
    tok = lambda i: (i, 0)
    cache_shape = (n_seq, ATTN_KV_HEADS, ATTN_HEAD_DIM, WINDOW)
    cache_blk = lambda i: (i, 0, 0, 0)
    consts = (cos, sin, cost, sint, gpre, gpost, w_q, w_kt, w_vt, w_g, w_out, w_gate, w_ple)
    return pl.pallas_call(
        kernel,
        grid=(n_all // n_seq,),
        in_specs=[
            pl.BlockSpec(memory_space=pltpu.SMEM),
            pl.BlockSpec((tq, d), tok),
            pl.BlockSpec((None, tq, ple), lambda i: (layer, i, 0)),
        ] + [_const_spec(c.shape) for c in consts] + [
            pl.BlockSpec(cache_shape, cache_blk),
            pl.BlockSpec(cache_shape, cache_blk),
        ],
        out_specs=[
            pl.BlockSpec((tq, d), tok),
            pl.BlockSpec(cache_shape, cache_blk),
            pl.BlockSpec(cache_shape, cache_blk),
        ],
        out_shape=[
            jax.ShapeDtypeStruct(x.shape, F32),
            jax.ShapeDtypeStruct(cache_kt.shape, F32),
            jax.ShapeDtypeStruct(cache_vt.shape, F32),
        ],
        scratch_shapes=[
            pltpu.VMEM((tq, ATTN_Q_DIM), F32),
            pltpu.VMEM((ATTN_KV_DIM, tq), F32),
            pltpu.VMEM((ATTN_KV_DIM, tq), F32),
            pltpu.VMEM((tq, ATTN_Q_DIM), F32),
        ],
        compiler_params=pltpu.CompilerParams(
            dimension_semantics=("arbitrary",),
            vmem_limit_bytes=VMEM_LIMIT_BYTES),
        name="attn_sample_layer",
    )(sinks, x, p_all, *consts, cache_kt, cache_vt)


RET_GATE_COL0 = 2 * RET_QK_W + RET_VW


def _ret_project(h, cos, sin, win_ref, chunk_len):
    q0, k0, v0, g0 = 0, RET_QK_W, 2 * RET_QK_W, RET_GATE_COL0
    q = _rope_ret(_dot(h, win_ref[:, q0:k0]), cos, sin)
    k = _rope_ret(_dot(h, win_ref[:, k0:v0]), cos, sin) * (RET_QK_DIM ** -0.5)
    v = _dot(h, win_ref[:, v0:g0])
    n = h.shape[0]
    idx = (lax.broadcasted_iota(jnp.int32, (n, 1), 0) % chunk_len).astype(F32)
    kd = jnp.concatenate(
        [k[:, hd * RET_QK_DIM:(hd + 1) * RET_QK_DIM]
         * jnp.exp((chunk_len - 1.0 - idx) * RET_LOG_GAMMA[hd]) for hd in range(RET_HEADS)], axis=1)
    return q, k, kd, v


def _ret_decay_mask(n, chunk_len, hd):
    ri = lax.broadcasted_iota(jnp.int32, (n, n), 0)
    ci = lax.broadcasted_iota(jnp.int32, (n, n), 1)
    diff = ri - ci
    ok = (diff >= 0) & ((ri // chunk_len) == (ci // chunk_len))
    return jnp.where(ok, jnp.exp(jnp.maximum(diff, 0).astype(F32) * RET_LOG_GAMMA[hd]), 0.0)


def _ret_gated(o, g):
    outs = []
    for hd in range(RET_HEADS):
        oh = o[:, hd * RET_V_DIM:(hd + 1) * RET_V_DIM]
        mu = jnp.mean(oh, axis=-1, keepdims=True)
        var = jnp.mean(jnp.square(oh - mu), axis=-1, keepdims=True)
        outs.append((oh - mu) * lax.rsqrt(var + NORM_EPS))
    return (jnp.concatenate(outs, axis=1) * _silu(g)).astype(BF16)


def _ret_prompt_kernel(x_ref, p_ref, cos_ref, sin_ref, gpre_ref, gpost_ref,
                       win_ref, wout_ref, wgate_ref, wple_ref,
                       xo_ref, st_ref,
                       q_scr, k_scr, kd_scr, v_scr, o_scr, g_scr, dm_scr, *, tq):
    b = pl.program_id(0)
    t = pl.program_id(1)

    @pl.when((b == 0) & (t == 0))
    def _():
        for hd in range(RET_HEADS):
            dm_scr[hd] = _ret_decay_mask(RET_CHUNK, RET_CHUNK, hd)

    @pl.when(t == 0)
    def _():
        st_ref[...] = jnp.zeros(st_ref.shape, F32)

    emb = _dot(p_ref[...].astype(BF16), wple_ref[...])
    x = x_ref[...]
    h = _rms_norm(x, gpre_ref[...]).astype(BF16)
    q, k, kd, v = _ret_project(h, cos_ref[...], sin_ref[...], win_ref, RET_CHUNK)
    q_scr[...] = q.astype(BF16)
    k_scr[...] = k.astype(BF16)
    kd_scr[...] = kd.astype(BF16)
    v_scr[...] = v.astype(BF16)

    idx = lax.broadcasted_iota(jnp.int32, (RET_CHUNK, 1), 0).astype(F32)
    heads = range(RET_HEADS)
    qs = [slice(hd * RET_QK_DIM, (hd + 1) * RET_QK_DIM) for hd in heads]
    vs = [slice(hd * RET_V_DIM, (hd + 1) * RET_V_DIM) for hd in heads]
    chunks = _row_splits(tq, tq // RET_CHUNK)
    gate_cols = iter(_row_splits(RET_VW, 2 * len(chunks)))

    def gate_piece():
        c = next(gate_cols)
        g_scr[:, c] = _dot(h, win_ref[:, RET_GATE_COL0 + c.start:RET_GATE_COL0 + c.stop])

    for rows in chunks:
        scores = [_dot_nt(q_scr[rows, qs[hd]], k_scr[rows, qs[hd]]) for hd in heads]
        cross = [_dot(q_scr[rows, qs[hd]], st_ref[hd].astype(BF16)) for hd in heads]
        gate_piece()
        for hd in heads:
            st_ref[hd] = (math.exp(RET_CHUNK * RET_LOG_GAMMA[hd]) * st_ref[hd]
                          + _dot_tn(kd_scr[rows, qs[hd]], v_scr[rows, vs[hd]]))
        for hd in heads:
            inner = _dot((scores[hd] * dm_scr[hd]).astype(BF16), v_scr[rows, vs[hd]])
            o_scr[rows, vs[hd]] = inner + cross[hd] * jnp.exp((idx + 1.0) * RET_LOG_GAMMA[hd])
        gate_piece()

    ogs = [_ret_gated(o_scr[r, :], g_scr[r, :]) for r in chunks]
    _out_proj_and_residual(chunks, ogs, x, emb, gpost_ref[...], wout_ref, wgate_ref, xo_ref)


def _ret_prompt_layer(layer, x, p_all, cos, sin, gpre, gpost, w_in, w_out, w_gate, w_ple):
    nb, seq, d = x.shape
    tq = min(512, seq)
    ple = p_all.shape[-1]
    kernel = functools.partial(_ret_prompt_kernel, tq=tq)
    tok = lambda b, t: (b, t, 0)
    return pl.pallas_call(
        kernel,
        grid=(nb, seq // tq),
        in_specs=[
            pl.BlockSpec((None, tq, d), tok),
            pl.BlockSpec((None, None, tq, ple), lambda b, t: (layer, b, t, 0)),
            pl.BlockSpec((tq, LANES), lambda b, t: (t, 0)),
            pl.BlockSpec((tq, LANES), lambda b, t: (t, 0)),
            _const_spec(gpre.shape), _const_spec(gpost.shape),
            _const_spec(w_in.shape), _const_spec(w_out.shape),
            _const_spec(w_gate.shape), _const_spec(w_ple.shape),
        ],
        out_specs=[
            pl.BlockSpec((None, tq, d), tok),
            pl.BlockSpec((None, RET_HEADS, RET_QK_DIM, RET_V_DIM), lambda b, t: (b, 0, 0, 0)),
        ],
        out_shape=[
            jax.ShapeDtypeStruct(x.shape, F32),
            jax.ShapeDtypeStruct((nb, RET_HEADS, RET_QK_DIM, RET_V_DIM), F32),
        ],
        scratch_shapes=[
            pltpu.VMEM((tq, RET_QK_W), BF16),
            pltpu.VMEM((tq, RET_QK_W), BF16),
            pltpu.VMEM((tq, RET_QK_W), BF16),
            pltpu.VMEM((tq, RET_VW), BF16),
            pltpu.VMEM((tq, RET_VW), F32),
            pltpu.VMEM((tq, RET_VW), F32),
            pltpu.VMEM((RET_HEADS, RET_CHUNK, RET_CHUNK), F32),
        ],
        compiler_params=pltpu.CompilerParams(
            dimension_semantics=("arbitrary", "arbitrary"),
            vmem_limit_bytes=VMEM_LIMIT_BYTES),
        name="ret_prompt_layer",
    )(x, p_all, cos, sin, gpre, gpost, w_in, w_out, w_gate, w_ple)


def _ret_sample_proj_kernel(x_ref, cos_ref, sin_ref, gpre_ref, win_ref,
                            q_ref, k_ref, kd_ref, v_ref, g_ref, *, dec_len):
    h = _rms_norm(x_ref[...], gpre_ref[...]).astype(BF16)
    q, k, kd, v = _ret_project(h, cos_ref[...], sin_ref[...], win_ref, dec_len)
    q_ref[...] = q
    k_ref[...] = k
    kd_ref[...] = kd
    v_ref[...] = v
    g_ref[...] = _dot(h, win_ref[:, RET_GATE_COL0:])


def _ret_sample_proj(x, cos, sin, gpre, w_in, dec_len):
    n_tok, d = x.shape
    tq = min(RET_SAMPLE_TILE, n_tok)
    kernel = functools.partial(_ret_sample_proj_kernel, dec_len=dec_len)
    tok = lambda i: (i, 0)
    widths = (RET_QK_W, RET_QK_W, RET_QK_W, RET_VW, RET_VW)
    return pl.pallas_call(
        kernel,
        grid=(n_tok // tq,),
        in_specs=[
            pl.BlockSpec((tq, d), tok),
            _const_spec((tq, LANES)), _const_spec((tq, LANES)),
            _const_spec(gpre.shape), _const_spec(w_in.shape),
        ],
        out_specs=[pl.BlockSpec((tq, w), tok) for w in widths],
        out_shape=[jax.ShapeDtypeStruct((n_tok, w), F32) for w in widths],
        compiler_params=pltpu.CompilerParams(
            dimension_semantics=("arbitrary",),
            vmem_limit_bytes=VMEM_LIMIT_BYTES),
        name="ret_sample_proj",
    )(x, cos, sin, gpre, w_in)


def _ret_state_update_tile(q_ref, k_ref, kd_ref, v_ref, st_in, st_out, o_ref, *,
                           first_seq, n_seq, dec_len):
    pad_rows = 2 * dec_len
    idx = lax.broadcasted_iota(jnp.int32, (pad_rows, 1), 0).astype(F32)

    def tok_rows(s):
        return slice((first_seq + s) * dec_len, (first_seq + s + 1) * dec_len)

    def padded(ref, s, cols):
        val = ref[tok_rows(s), cols]
        return jnp.concatenate([val, jnp.zeros_like(val)], axis=0).astype(BF16)

    units = [(s, hd) for s in range(n_seq) for hd in range(RET_HEADS)]
    qs = [slice(hd * RET_QK_DIM, (hd + 1) * RET_QK_DIM) for hd in range(RET_HEADS)]
    vs = [slice(hd * RET_V_DIM, (hd + 1) * RET_V_DIM) for hd in range(RET_HEADS)]
    qh = {(s, hd): padded(q_ref, s, qs[hd]) for s, hd in units}
    vh = {(s, hd): padded(v_ref, s, vs[hd]) for s, hd in units}
    scores = {(s, hd): _dot_nt(qh[s, hd], padded(k_ref, s, qs[hd])) for s, hd in units}
    for s, hd in units:
        st_out[s, hd] = (math.exp(dec_len * RET_LOG_GAMMA[hd]) * st_in[s, hd]
                         + _dot_tn(padded(kd_ref, s, qs[hd]), vh[s, hd]))
    cross = {(s, hd): _dot(qh[s, hd], st_in[s, hd].astype(BF16)) for s, hd in units}
    masks = [_ret_decay_mask(pad_rows, dec_len, hd) for hd in range(RET_HEADS)]
    for s, hd in units:
        inner = _dot((scores[s, hd] * masks[hd]).astype(BF16), vh[s, hd])
        out = inner + cross[s, hd] * jnp.exp((idx + 1.0) * RET_LOG_GAMMA[hd])
        o_ref[tok_rows(s), vs[hd]] = out[0:dec_len, :]


def _ret_sample_finish_kernel(o_ref, g_ref, x_ref, p_ref, gpost_ref, wout_ref, wgate_ref, wple_ref,
                              xo_ref):
    emb = _dot(p_ref[...].astype(BF16), wple_ref[...])
    x = x_ref[...]
    rows = _row_splits(x.shape[0], 2)
    ogs = [_ret_gated(o_ref[r, :], g_ref[r, :]) for r in rows]
    _out_proj_and_residual(rows, ogs, x, emb, gpost_ref[...], wout_ref, wgate_ref, xo_ref)


def _ret_sample_finish(layer, o, g, x, p_all, gpost, w_out, w_gate, w_ple):
    n_tok, d = x.shape
    tq = min(RET_SAMPLE_TILE, n_tok)
    ple = p_all.shape[-1]
    tok = lambda i: (i, 0)
    return pl.pallas_call(
        _ret_sample_finish_kernel,
        grid=(n_tok // tq,),
        in_specs=[
            pl.BlockSpec((tq, RET_VW), tok), pl.BlockSpec((tq, RET_VW), tok),
            pl.BlockSpec((tq, d), tok),
            pl.BlockSpec((None, tq, ple), lambda i: (layer, i, 0)),
            _const_spec(gpost.shape), _const_spec(w_out.shape),
            _const_spec(w_gate.shape), _const_spec(w_ple.shape),
        ],
        out_specs=pl.BlockSpec((tq, d), tok),
        out_shape=jax.ShapeDtypeStruct(x.shape, F32),
        compiler_params=pltpu.CompilerParams(
            dimension_semantics=("arbitrary",),
            vmem_limit_bytes=VMEM_LIMIT_BYTES),
        name="ret_sample_finish",
    )(o, g, x, p_all, gpost, w_out, w_gate, w_ple)


def _rope_tables(pos, head_dim):
    half = head_dim // 2
    inv = ROPE_THETA ** (-np.arange(half, dtype=np.float64) / half)
    ang = np.asarray(pos, np.float64)[:, None] * inv[None, :]
    return np.cos(ang).astype(np.float32), np.sin(ang).astype(np.float32)


def _attn_tables(pos):
    cos, sin = _rope_tables(pos, ATTN_HEAD_DIM)
    reps = LANES // ATTN_HEAD_DIM
    return (np.tile(np.concatenate([cos, cos], axis=1), (1, reps)),
            np.tile(np.concatenate([-sin, sin], axis=1), (1, reps)))


def kernel(x_prompt, x_sample, cache_k_win, cache_v_win, state_ret, p_prompt, p_sample, pre_norm,
           post_norm, w_in_attn, attn_sinks, w_out_attn, w_in_ret, w_out_ret, w_ple, w_ple_gate):
    depth = p_prompt.shape[0]
    nb, seq, d = x_prompt.shape
    n_dec, dec_len, _ = x_sample.shape
    n_stok = n_dec * dec_len
    assert seq % 512 == 0 or seq in (128, 256), seq

    pos_p = np.arange(seq)
    pos_s = PAST_LEN + np.arange(dec_len)
    cos_ap, sin_ap = _attn_tables(pos_p)
    cos_apt, sin_apt = (np.ascontiguousarray(a.T) for a in _rope_tables(pos_p, ATTN_HEAD_DIM))
    cos_as, sin_as = _attn_tables(pos_s)
    cos_rp, sin_rp = _rope_tables(pos_p, RET_QK_DIM)
    cos_rs, sin_rs = _rope_tables(pos_s, RET_QK_DIM)
    attn_s_tile = WINDOW // dec_len
    ret_s_tile = min(RET_SAMPLE_TILE, n_stok) // dec_len
    cos_as, sin_as = (np.tile(a, (attn_s_tile, 1)) for a in (cos_as, sin_as))
    cos_ast, sin_ast = (np.ascontiguousarray(np.tile(a, (attn_s_tile, 1)).T)
                        for a in _rope_tables(pos_s, ATTN_HEAD_DIM))
    cos_rs, sin_rs = (np.tile(a, (ret_s_tile, 1)) for a in (cos_rs, sin_rs))
    (cos_ap, sin_ap, cos_apt, sin_apt, cos_as, sin_as, cos_ast, sin_ast,
     cos_rp, sin_rp, cos_rs, sin_rs) = (
        jnp.asarray(a) for a in
        (cos_ap, sin_ap, cos_apt, sin_apt, cos_as, sin_as, cos_ast, sin_ast,
         cos_rp, sin_rp, cos_rs, sin_rs))

    assert depth == 2, depth
    xs = x_sample.reshape(n_stok, d)
    p_s = p_sample.reshape(depth, n_stok, p_sample.shape[-1])
    bf = lambda w: w.astype(BF16)
    gpre = [pre_norm[i][None, :] for i in range(depth)]
    gpost = [post_norm[i][None, :] for i in range(depth)]
    w_gate = [bf(w_ple_gate[i]) for i in range(depth)]
    w_pl = [bf(w_ple[i]) for i in range(depth)]

    w_in = bf(w_in_attn[0])
    w_out = bf(w_out_attn[0])
    sinks = attn_sinks[0]
    k0, v0, g0 = ATTN_Q_DIM, ATTN_Q_DIM + ATTN_KV_DIM, ATTN_Q_DIM + 2 * ATTN_KV_DIM
    w_q, w_k, w_v, w_g = w_in[:, :k0], w_in[:, k0:v0], w_in[:, v0:g0], w_in[:, g0:]
    to_t = lambda a: a.transpose(0, 2, 3, 1)
    from_t = lambda a: a.transpose(0, 3, 1, 2)
    xs, knt, vnt = _attn_sample_layer(
        0, xs, p_s, (cos_as, sin_as, cos_ast, sin_ast), gpre[0], gpost[0], sinks,
        w_q, w_k.T, w_v.T, w_g, w_out, w_gate[0], w_pl[0],
        to_t(cache_k_win[0]), to_t(cache_v_win[0]), dec_len)

    w_in_r = bf(w_in_ret[0])
    w_out_r = bf(w_out_ret[0])
    rq, rk, rkd, rv, rg = _ret_sample_proj(xs, cos_rs, sin_rs, gpre[1], w_in_r, dec_len)
    xp, kpt, vpt, ro, ret_state_sample = _attn_prompt_layer(
        0, x_prompt, p_prompt, (cos_ap, sin_ap, cos_apt, sin_apt), gpre[0], gpost[0], sinks,
        w_q.T, w_k, w_v.T, w_g, w_out, w_gate[0], w_pl[0],
        (rq, rk, rkd, rv), state_ret[0], dec_len)
    head_shape = (nb, ATTN_KV_HEADS, ATTN_HEAD_DIM, WINDOW)

    xp, ret_state_prompt = _ret_prompt_layer(1, xp, p_prompt, cos_rp, sin_rp, gpre[1], gpost[1],
                                             w_in_r, w_out_r, w_gate[1], w_pl[1])
    xs = _ret_sample_finish(1, ro, rg, xs, p_s, gpost[1], w_out_r, w_gate[1], w_pl[1])
    return (xp, xs.reshape(n_dec, dec_len, d),
            from_t(kpt.reshape(head_shape))[None], from_t(vpt.reshape(head_shape))[None],
            from_t(knt)[None], from_t(vnt)[None], ret_state_prompt[None], ret_state_sample[None])
```

```python
import functools
import math

import numpy as np
import jax
import jax.numpy as jnp
from jax import lax
from jax.experimental import pallas as pl
from jax.experimental.pallas import tpu as pltpu

F32 = jnp.float32
BF16 = jnp.bfloat16

PAST_LEN = 16384
ROPE_THETA = 10000.0
NORM_EPS = 1e-6
NEG_INF = -1e30
WINDOW = 128
ATTN_HEADS = 16
ATTN_KV_HEADS = 4
ATTN_GROUP = ATTN_HEADS // ATTN_KV_HEADS
ATTN_HEAD_DIM = 64
ATTN_Q_DIM = ATTN_HEADS * ATTN_HEAD_DIM
ATTN_KV_DIM = ATTN_KV_HEADS * ATTN_HEAD_DIM
RET_HEADS = 4
RET_QK_DIM = 256
RET_V_DIM = 512
RET_QK_W = RET_HEADS * RET_QK_DIM
RET_VW = RET_HEADS * RET_V_DIM
RET_CHUNK = 256
LANES = 128

LOG2_E = math.log2(math.e)
RET_LOG_GAMMA = tuple(math.log1p(-(2.0 ** (-5.0 - h))) for h in range(RET_HEADS))

VMEM_LIMIT_BYTES = 56 * 1024 * 1024
ATTN_PROMPT_VMEM_LIMIT_BYTES = 50 * 1024 * 1024
STATE_PHASES = 4
RET_SAMPLE_TILE = 256
SAMPLE_ATTN_PIPELINE_GROUP = 8


def _dot(a, b):
    return jnp.dot(a, b, preferred_element_type=F32)


def _dot_nt(a, b):
    return lax.dot_general(a, b, (((1,), (1,)), ((), ())), preferred_element_type=F32)


def _dot_tn(a, b):
    return lax.dot_general(a, b, (((0,), (0,)), ((), ())), preferred_element_type=F32)


def _rms_norm(x, g):
    return x * lax.rsqrt(jnp.mean(x * x, axis=-1, keepdims=True) + NORM_EPS) * g


def _silu(g):
    return g * jax.nn.sigmoid(g)


def _row_splits(n, parts):
    step = n // parts
    return [slice(i * step, (i + 1) * step) for i in range(parts)]


def _out_proj_and_residual(rows, ogs, x, emb, gpost, wout_ref, wgate_ref, xo_ref):
    ys = [_dot(og, wout_ref[...]) for og in ogs]
    x1s = [x[r, :] + _rms_norm(y, gpost) for r, y in zip(rows, ys)]
    gates = [_dot(x1.astype(BF16), wgate_ref[...]) for x1 in x1s]
    for r, x1, gate in zip(rows, x1s, gates):
        xo_ref[r, :] = x1 + jax.nn.sigmoid(gate) * emb[r, :]


def _rope_attn(x, cos, sin_signed):
    lane = lax.broadcasted_iota(jnp.int32, (1, LANES), 1)
    first_half = (lane % ATTN_HEAD_DIM) < (ATTN_HEAD_DIM // 2)
    outs = []
    for j in range(x.shape[1] // LANES):
        xj = x[:, j * LANES:(j + 1) * LANES]
        fwd = pltpu.roll(xj, LANES - ATTN_HEAD_DIM // 2, axis=1)
        bwd = pltpu.roll(xj, ATTN_HEAD_DIM // 2, axis=1)
        outs.append(xj * cos + jnp.where(first_half, fwd, bwd) * sin_signed)
    return jnp.concatenate(outs, axis=1)


def _rope_ret(x, cos, sin):
    half = RET_QK_DIM // 2
    outs = []
    for h in range(x.shape[1] // RET_QK_DIM):
        x1 = x[:, h * RET_QK_DIM:h * RET_QK_DIM + half]
        x2 = x[:, h * RET_QK_DIM + half:(h + 1) * RET_QK_DIM]
        outs.append(x1 * cos - x2 * sin)
        outs.append(x2 * cos + x1 * sin)
    return jnp.concatenate(outs, axis=1)


def _attn_prompt_kernel(sink_ref, x_ref, p_ref, cos_ref, sin_ref, cost_ref, sint_ref,
                        gpre_ref, gpost_ref, wqt_ref, wk_ref, wvt_ref, wg_ref,
                        wout_ref, wgate_ref, wple_ref,
                        rq_ref, rk_ref, rkd_ref, rv_ref, rstate_hbm,
                        xo_ref, kw_ref, vw_ref, ro_ref, rstate_out_hbm,
                        qt_scr, kext_scr, vt_scr, st_scr, pt_scr, inv_scr, ot_scr,
                        rst_in, rst_out, sem_in, sem_out, *, tq, n_seq, dec_len):
    nblk = tq // WINDOW
    half = ATTN_HEAD_DIM // 2
    t = pl.program_id(1)
    nt = pl.num_programs(1)

    @pl.when(t == 0)
    def _():
        kext_scr[0:WINDOW, :] = jnp.zeros((WINDOW, ATTN_KV_DIM), BF16)
        vt_scr[0] = jnp.zeros((ATTN_KV_DIM, WINDOW), BF16)

    per_phase = n_seq // STATE_PHASES
    step = pl.program_id(0) * nt + t
    n_steps = pl.num_programs(0) * nt
    seq0 = step * n_seq
    last_seq = n_steps * n_seq - 1

    def load_copy(seq, slot, i):
        return pltpu.make_async_copy(rstate_hbm.at[seq], rst_in.at[slot, i], sem_in.at[slot, i])

    def store_copy(seq, slot, i):
        return pltpu.make_async_copy(rst_out.at[slot, i], rstate_out_hbm.at[seq],
                                     sem_out.at[slot, i])

    @pl.when(step == 0)
    def _():
        rst_out[...] = jnp.zeros(rst_out.shape, F32)
        for slot in range(2):
            for i in range(per_phase):
                load_copy(slot * per_phase + i, slot, i).start()
                store_copy(slot * per_phase + i, slot, i).start()

    def state_phase(phase):
        slot = phase % 2
        first = seq0 + phase * per_phase
        for i in range(per_phase):
            load_copy(first + i, slot, i).wait()
            store_copy(first + i, slot, i).wait()
        _ret_state_update_tile(rq_ref, rk_ref, rkd_ref, rv_ref, rst_in.at[slot], rst_out.at[slot],
                               ro_ref, first_seq=phase * per_phase, n_seq=per_phase,
                               dec_len=dec_len)
        for i in range(per_phase):
            store_copy(first + i, slot, i).start()
            load_copy(jnp.minimum(first + i + 2 * per_phase, last_seq), slot, i).start()

    state_phase(0)

    emb = _dot(p_ref[...].astype(BF16), wple_ref[...])
    x = x_ref[...]
    h = _rms_norm(x, gpre_ref[...]).astype(BF16)

    k = _rope_attn(_dot(h, wk_ref[...]), cos_ref[...], sin_ref[...])
    kext_scr[WINDOW:, :] = k.astype(BF16)

    qt = _dot_nt(wqt_ref[...], h)
    cost = cost_ref[...]
    sint = sint_ref[...]
    pieces = []
    for hd in range(ATTN_HEADS):
        x1 = qt[hd * ATTN_HEAD_DIM:hd * ATTN_HEAD_DIM + half, :]
        x2 = qt[hd * ATTN_HEAD_DIM + half:(hd + 1) * ATTN_HEAD_DIM, :]
        pieces.append(x1 * cost - x2 * sint)
        pieces.append(x2 * cost + x1 * sint)
    qrot = (jnp.concatenate(pieces, axis=0) * (ATTN_HEAD_DIM ** -0.5 * LOG2_E)).astype(BF16)
    for jb in range(nblk):
        qt_scr[jb] = qrot[:, jb * WINDOW:(jb + 1) * WINDOW]

    vt = _dot_nt(wvt_ref[...], h)
    vtb = vt.astype(BF16)
    for jb in range(nblk):
        vt_scr[jb + 1] = vtb[:, jb * WINDOW:(jb + 1) * WINDOW]

    @pl.when(t == nt - 1)
    def _():
        kw_ref[...] = k[tq - WINDOW:, :].T
        vw_ref[...] = vt[:, tq - WINDOW:]

    state_phase(1)
    zero_rows = jnp.zeros((ATTN_HEAD_DIM, ATTN_GROUP * WINDOW), BF16)
    groups = [(jb, kh) for jb in range(nblk) for kh in range(ATTN_KV_HEADS)]

    for gidx, (jb, kh) in enumerate(groups):
        heads = [kh * ATTN_GROUP + gi for gi in range(ATTN_GROUP)]
        pair = kh // 2
        kb = kext_scr[jb * WINDOW:(jb + 2) * WINDOW, pair * LANES:(pair + 1) * LANES]
        qg = jnp.concatenate(
            [qt_scr[jb, hd * ATTN_HEAD_DIM:(hd + 1) * ATTN_HEAD_DIM, :] for hd in heads], axis=1)
        qg = jnp.concatenate([qg, zero_rows] if kh % 2 == 0 else [zero_rows, qg], axis=0)
        st_scr[gidx] = _dot(kb, qg)

    g = _dot(h, wg_ref[...])

    key_r = lax.broadcasted_iota(jnp.int32, (WINDOW, WINDOW), 0)
    query_c = lax.broadcasted_iota(jnp.int32, (WINDOW, WINDOW), 1)
    from_prev = key_r > query_c
    prev_w = from_prev.astype(BF16)
    cur_w = 1.0 - prev_w
    for gidx, (jb, kh) in enumerate(groups):
        for gi in range(ATTN_GROUP):
            hd = kh * ATTN_GROUP + gi
            cols = slice(gi * WINDOW, (gi + 1) * WINDOW)
            s_prev = st_scr[gidx, 0:WINDOW, cols]
            if jb == 0:
                s_prev = jnp.where(t > 0, s_prev, NEG_INF)
            s = jnp.where(from_prev, s_prev, st_scr[gidx, WINDOW:, cols])
            sink = sink_ref[hd] * LOG2_E
            m = jnp.maximum(jnp.max(s, axis=0, keepdims=True), sink)
            pr = jnp.exp2(s - m)
            den = jnp.sum(pr, axis=0, keepdims=True) + jnp.exp2(sink - m)
            prb = pr.astype(BF16)
            pt_scr[gidx, 0:WINDOW, cols] = prb * prev_w
            pt_scr[gidx, WINDOW:, cols] = prb * cur_w
            inv_scr[jb, hd:hd + 1, :] = 1.0 / den

    for gidx, (jb, kh) in enumerate(groups):
        vband = jnp.concatenate([vt_scr[jb, kh * ATTN_HEAD_DIM:(kh + 1) * ATTN_HEAD_DIM, :],
                                 vt_scr[jb + 1, kh * ATTN_HEAD_DIM:(kh + 1) * ATTN_HEAD_DIM, :]],
                                axis=1)
        otg = _dot(vband, pt_scr[gidx])
        for gi in range(ATTN_GROUP):
            hd = kh * ATTN_GROUP + gi
            ot_scr[jb, hd * ATTN_HEAD_DIM:(hd + 1) * ATTN_HEAD_DIM, :] = (
                otg[:, gi * WINDOW:(gi + 1) * WINDOW] * inv_scr[jb, hd:hd + 1, :])

    state_phase(2)

    parts = 2 if nblk % 2 == 0 else 1
    rows = _row_splits(tq, parts)
    ogs = []
    for r in rows:
        o = jnp.concatenate([ot_scr[jb].T for jb in range(r.start // WINDOW, r.stop // WINDOW)],
                            axis=0)
        ogs.append((o * _silu(g[r, :])).astype(BF16))
    _out_proj_and_residual(rows, ogs, x, emb, gpost_ref[...], wout_ref, wgate_ref, xo_ref)
    state_phase(3)

    kext_scr[0:WINDOW, :] = kext_scr[tq:tq + WINDOW, :]
    vt_scr[0] = vt_scr[nblk]

    @pl.when(step == n_steps - 1)
    def _():
        for slot in range(2):
            for i in range(per_phase):
                load_copy(last_seq, slot, i).wait()
                store_copy(last_seq, slot, i).wait()


def _const_spec(shape):
    nd = len(shape)
    return pl.BlockSpec(shape, lambda *_: (0,) * nd)


def _attn_prompt_layer(layer, x, p_all, tables, gpre, gpost, sinks, w_qt, w_k, w_vt, w_g,
                       w_out, w_gate, w_ple, ret_qkv, ret_state, dec_len):
    nb, seq, d = x.shape
    tq = min(512, seq)
    nblk = tq // WINDOW
    nt = seq // tq
    n_seq = ret_state.shape[0] // (nb * nt)
    assert n_seq * nb * nt == ret_state.shape[0] and n_seq % STATE_PHASES == 0, (
        ret_state.shape, nb, nt)
    ple = p_all.shape[-1]
    cos, sin, cost, sint = tables
    kernel = functools.partial(_attn_prompt_kernel, tq=tq, n_seq=n_seq, dec_len=dec_len)
    tok = lambda b, t: (b, t, 0)
    win_blk = lambda b, t: (b, 0, 0)
    stok = lambda b, t: (b * nt + t, 0)
    consts = (gpre, gpost, w_qt, w_k, w_vt, w_g, w_out, w_gate, w_ple)
    rq, rk, rkd, rv = ret_qkv
    st_shape = (2, n_seq // STATE_PHASES) + ret_state.shape[1:]
    return pl.pallas_call(
        kernel,
        grid=(nb, nt),
        in_specs=[
            pl.BlockSpec(memory_space=pltpu.SMEM),
            pl.BlockSpec((None, tq, d), tok),
            pl.BlockSpec((None, None, tq, ple), lambda b, t: (layer, b, t, 0)),
            pl.BlockSpec((tq, LANES), lambda b, t: (t, 0)),
            pl.BlockSpec((tq, LANES), lambda b, t: (t, 0)),
            pl.BlockSpec((ATTN_HEAD_DIM // 2, tq), lambda b, t: (0, t)),
            pl.BlockSpec((ATTN_HEAD_DIM // 2, tq), lambda b, t: (0, t)),
        ] + [_const_spec(c.shape) for c in consts] + [
            pl.BlockSpec((n_seq * dec_len, RET_QK_W), stok),
            pl.BlockSpec((n_seq * dec_len, RET_QK_W), stok),
            pl.BlockSpec((n_seq * dec_len, RET_QK_W), stok),
            pl.BlockSpec((n_seq * dec_len, RET_VW), stok),
            pl.BlockSpec(memory_space=pl.ANY),
        ],
        out_specs=[
            pl.BlockSpec((None, tq, d), tok),
            pl.BlockSpec((None, ATTN_KV_DIM, WINDOW), win_blk),
            pl.BlockSpec((None, ATTN_KV_DIM, WINDOW), win_blk),
            pl.BlockSpec((n_seq * dec_len, RET_VW), stok),
            pl.BlockSpec(memory_space=pl.ANY),
        ],
        out_shape=[
            jax.ShapeDtypeStruct(x.shape, F32),
            jax.ShapeDtypeStruct((nb, ATTN_KV_DIM, WINDOW), F32),
            jax.ShapeDtypeStruct((nb, ATTN_KV_DIM, WINDOW), F32),
            jax.ShapeDtypeStruct((rq.shape[0], RET_VW), F32),
            jax.ShapeDtypeStruct(ret_state.shape, F32),
        ],
        scratch_shapes=[
            pltpu.VMEM((nblk, ATTN_Q_DIM, WINDOW), BF16),
            pltpu.VMEM((tq + WINDOW, ATTN_KV_DIM), BF16),
            pltpu.VMEM((nblk + 1, ATTN_KV_DIM, WINDOW), BF16),
            pltpu.VMEM((nblk * ATTN_KV_HEADS, 2 * WINDOW, ATTN_GROUP * WINDOW), F32),
            pltpu.VMEM((nblk * ATTN_KV_HEADS, 2 * WINDOW, ATTN_GROUP * WINDOW), BF16),
            pltpu.VMEM((nblk, ATTN_HEADS, WINDOW), F32),
            pltpu.VMEM((nblk, ATTN_Q_DIM, WINDOW), F32),
            pltpu.VMEM(st_shape, F32),
            pltpu.VMEM(st_shape, F32),
            pltpu.SemaphoreType.DMA((2, n_seq // STATE_PHASES)),
            pltpu.SemaphoreType.DMA((2, n_seq // STATE_PHASES)),
        ],
        compiler_params=pltpu.CompilerParams(
            dimension_semantics=("arbitrary", "arbitrary"),
            vmem_limit_bytes=ATTN_PROMPT_VMEM_LIMIT_BYTES),
        name="attn_prompt_layer",
    )(sinks, x, p_all, cos, sin, cost, sint, *consts, rq, rk, rkd, rv, ret_state)


def _attn_sample_kernel(sink_ref, x_ref, p_ref, cos_ref, sin_ref, cost_ref, sint_ref,
                        gpre_ref, gpost_ref, wq_ref, wkt_ref, wvt_ref, wg_ref,
                        wout_ref, wgate_ref, wple_ref, kc_ref, vc_ref,
                        xo_ref, ko_ref, vo_ref,
                        q_scr, knt_scr, vnt_scr, o_scr, *, n_seq, dec_len):
    half = ATTN_HEAD_DIM // 2
    x = x_ref[...]
    h = _rms_norm(x, gpre_ref[...]).astype(BF16)
    q_scr[...] = (_rope_attn(_dot(h, wq_ref[...]), cos_ref[...], sin_ref[...])
                  * (ATTN_HEAD_DIM ** -0.5 * LOG2_E))
    knt = _dot_nt(wkt_ref[...], h)
    cost = cost_ref[...]
    sint = sint_ref[...]
    pieces = []
    for kh in range(ATTN_KV_HEADS):
        x1 = knt[kh * ATTN_HEAD_DIM:kh * ATTN_HEAD_DIM + half, :]
        x2 = knt[kh * ATTN_HEAD_DIM + half:(kh + 1) * ATTN_HEAD_DIM, :]
        pieces.append(x1 * cost - x2 * sint)
        pieces.append(x2 * cost + x1 * sint)
    knt_scr[...] = jnp.concatenate(pieces, axis=0)
    vnt_scr[...] = _dot_nt(wvt_ref[...], h)

    rows = 2 * dec_len
    tok = lax.broadcasted_iota(jnp.int32, (rows, 2 * WINDOW), 0) % dec_len
    col = lax.broadcasted_iota(jnp.int32, (rows, 2 * WINDOW), 1)
    upper_row = lax.broadcasted_iota(jnp.int32, (rows, 1), 0) >= dec_len
    lane = lax.broadcasted_iota(jnp.int32, (ATTN_HEAD_DIM, WINDOW), 1)
    keep_old = lane < WINDOW - dec_len
    zeros = jnp.zeros((ATTN_HEAD_DIM, 2 * WINDOW), BF16)

    def select_head(t, parity):
        return jnp.concatenate([t, zeros] if parity == 0 else [zeros, t], axis=0)

    def scores(s):
        tok_rows = slice(s * dec_len, (s + 1) * dec_len)
        new_col = col - (WINDOW + s * dec_len)
        mask = ((col < WINDOW) & (col > tok)) | ((new_col >= 0) & (new_col <= tok))
        new_shift = (WINDOW - dec_len - s * dec_len) % WINDOW
        out = []
        for kh in range(ATTN_KV_HEADS):
            hrows = slice(kh * ATTN_HEAD_DIM, (kh + 1) * ATTN_HEAD_DIM)
            kc = kc_ref[s, kh]
            vc = vc_ref[s, kh]
            kn = knt_scr[hrows, :]
            vn = vnt_scr[hrows, :]
            kn_at_end = pltpu.roll(kn, new_shift, axis=1) if new_shift else kn
            vn_at_end = pltpu.roll(vn, new_shift, axis=1) if new_shift else vn
            ko_ref[s, kh] = jnp.where(keep_old, pltpu.roll(kc, WINDOW - dec_len, axis=1), kn_at_end)
            vo_ref[s, kh] = jnp.where(keep_old, pltpu.roll(vc, WINDOW - dec_len, axis=1), vn_at_end)

            kt = jnp.concatenate([kc, kn], axis=1).astype(BF16)
            vt = jnp.concatenate([vc, vn], axis=1).astype(BF16)
            qp = jnp.concatenate([q_scr[tok_rows, (2 * kh + c) * LANES:(2 * kh + c + 1) * LANES]
                                  for c in range(2)], axis=0).astype(BF16)
            sc = [jnp.where(mask, _dot(qp, select_head(kt, parity)), NEG_INF)
                  for parity in range(2)]
            out.append((sc, vt))
        return out

    def values(s, per_head):
        tok_rows = slice(s * dec_len, (s + 1) * dec_len)
        probs = []
        for kh, (scs, vt) in enumerate(per_head):
            for parity, sc in enumerate(scs):
                hd_lo, hd_hi = 4 * kh + parity, 4 * kh + 2 + parity
                sink = jnp.where(upper_row, sink_ref[hd_hi], sink_ref[hd_lo]) * LOG2_E
                m = jnp.maximum(jnp.max(sc, axis=-1, keepdims=True), sink)
                pr = jnp.exp2(sc - m)
                den = jnp.sum(pr, axis=-1, keepdims=True) + jnp.exp2(sink - m)
                probs.append((pr / den).astype(BF16))
        for kh, (scs, vt) in enumerate(per_head):
            o_pair = (_dot_nt(probs[2 * kh], select_head(vt, 0))
                      + _dot_nt(probs[2 * kh + 1], select_head(vt, 1)))
            for c in range(2):
                o_scr[tok_rows, (2 * kh + c) * LANES:(2 * kh + c + 1) * LANES] = (
                    o_pair[c * dec_len:(c + 1) * dec_len, :])

    group = min(SAMPLE_ATTN_PIPELINE_GROUP, n_seq)
    pending = [scores(s) for s in range(group)]
    for s0 in range(0, n_seq, group):
        upcoming = [scores(s) for s in range(s0 + group, min(s0 + 2 * group, n_seq))]
        for i, per_head in enumerate(pending):
            values(s0 + i, per_head)
        pending = upcoming

    g = _dot(h, wg_ref[...])
    emb = _dot(p_ref[...].astype(BF16), wple_ref[...])
    og = (o_scr[...] * _silu(g)).astype(BF16)
    _out_proj_and_residual(_row_splits(x.shape[0], 1), [og], x, emb, gpost_ref[...],
                           wout_ref, wgate_ref, xo_ref)


def _attn_sample_layer(layer, x, p_all, tables, gpre, gpost, sinks, w_q, w_kt, w_vt, w_g,
                       w_out, w_gate, w_ple, cache_kt, cache_vt, dec_len):
    n_tok, d = x.shape
    n_all = cache_kt.shape[0]
    n_seq = WINDOW // dec_len
    tq = n_seq * dec_len
    assert tq == WINDOW and n_all % n_seq == 0, (dec_len, n_all)
    ple = p_all.shape[-1]
    cos, sin, cost, sint = tables
    kernel = functools.partial(_attn_sample_kernel, n_seq=n_seq, dec_len=dec_len)
    tok = lambda i: (i, 0)
    cache_shape = (n_seq, ATTN_KV_HEADS, ATTN_HEAD_DIM, WINDOW)
    cache_blk = lambda i: (i, 0, 0, 0)
    consts = (cos, sin, cost, sint, gpre, gpost, w_q, w_kt, w_vt, w_g, w_out, w_gate, w_ple)
    return pl.pallas_call(
        kernel,
        grid=(n_all // n_seq,),
        in_specs=[
            pl.BlockSpec(memory_space=pltpu.SMEM),
            pl.BlockSpec((tq, d), tok),
            pl.BlockSpec((None, tq, ple), lambda i: (layer, i, 0)),
        ] + [_const_spec(c.shape) for c in consts] + [
            pl.BlockSpec(cache_shape, cache_blk),
            pl.BlockSpec(cache_shape, cache_blk),
        ],
        out_specs=[
            pl.BlockSpec((tq, d), tok),
            pl.BlockSpec(cache_shape, cache_blk),
            pl.BlockSpec(cache_shape, cache_blk),
        ],
        out_shape=[
            jax.ShapeDtypeStruct(x.shape, F32),
            jax.ShapeDtypeStruct(cache_kt.shape, F32),
            jax.ShapeDtypeStruct(cache_vt.shape, F32),
        ],
        scratch_shapes=[
            pltpu.VMEM((tq, ATTN_Q_DIM), F32),
            pltpu.VMEM((ATTN_KV_DIM, tq), F32),
            pltpu.VMEM((ATTN_KV_DIM, tq), F32),
            pltpu.VMEM((tq, ATTN_Q_DIM), F32),
        ],
        compiler_params=pltpu.CompilerParams(
            dimension_semantics=("arbitrary",),
            vmem_limit_bytes=VMEM_LIMIT_BYTES),
        name="attn_sample_layer",
    )(sinks, x, p_all, *consts, cache_kt, cache_vt)


RET_GATE_COL0 = 2 * RET_QK_W + RET_VW


def _ret_project(h, cos, sin, win_ref, chunk_len):
    q0, k0, v0, g0 = 0, RET_QK_W, 2 * RET_QK_W, RET_GATE_COL0
    q = _rope_ret(_dot(h, win_ref[:, q0:k0]), cos, sin)
    k = _rope_ret(_dot(h, win_ref[:, k0:v0]), cos, sin) * (RET_QK_DIM ** -0.5)
    v = _dot(h, win_ref[:, v0:g0])
    n = h.shape[0]
    idx = (lax.broadcasted_iota(jnp.int32, (n, 1), 0) % chunk_len).astype(F32)
    kd = jnp.concatenate(
        [k[:, hd * RET_QK_DIM:(hd + 1) * RET_QK_DIM]
         * jnp.exp((chunk_len - 1.0 - idx) * RET_LOG_GAMMA[hd]) for hd in range(RET_HEADS)], axis=1)
    return q, k, kd, v


def _ret_decay_mask(n, chunk_len, hd):
    ri = lax.broadcasted_iota(jnp.int32, (n, n), 0)
    ci = lax.broadcasted_iota(jnp.int32, (n, n), 1)
    diff = ri - ci
    ok = (diff >= 0) & ((ri // chunk_len) == (ci // chunk_len))
    return jnp.where(ok, jnp.exp(jnp.maximum(diff, 0).astype(F32) * RET_LOG_GAMMA[hd]), 0.0)


def _ret_gated(o, g):
    outs = []
    for hd in range(RET_HEADS):
        oh = o[:, hd * RET_V_DIM:(hd + 1) * RET_V_DIM]
        mu = jnp.mean(oh, axis=-1, keepdims=True)
        var = jnp.mean(jnp.square(oh - mu), axis=-1, keepdims=True)
        outs.append((oh - mu) * lax.rsqrt(var + NORM_EPS))
    return (jnp.concatenate(outs, axis=1) * _silu(g)).astype(BF16)


def _ret_prompt_kernel(x_ref, p_ref, cos_ref, sin_ref, gpre_ref, gpost_ref,
                       win_ref, wout_ref, wgate_ref, wple_ref,
                       xo_ref, st_ref,
                       q_scr, k_scr, kd_scr, v_scr, o_scr, g_scr, dm_scr, *, tq):
    b = pl.program_id(0)
    t = pl.program_id(1)

    @pl.when((b == 0) & (t == 0))
    def _():
        for hd in range(RET_HEADS):
            dm_scr[hd] = _ret_decay_mask(RET_CHUNK, RET_CHUNK, hd)

    @pl.when(t == 0)
    def _():
        st_ref[...] = jnp.zeros(st_ref.shape, F32)

    emb = _dot(p_ref[...].astype(BF16), wple_ref[...])
    x = x_ref[...]
    h = _rms_norm(x, gpre_ref[...]).astype(BF16)
    q, k, kd, v = _ret_project(h, cos_ref[...], sin_ref[...], win_ref, RET_CHUNK)
    q_scr[...] = q.astype(BF16)
    k_scr[...] = k.astype(BF16)
    kd_scr[...] = kd.astype(BF16)
    v_scr[...] = v.astype(BF16)

    idx = lax.broadcasted_iota(jnp.int32, (RET_CHUNK, 1), 0).astype(F32)
    heads = range(RET_HEADS)
    qs = [slice(hd * RET_QK_DIM, (hd + 1) * RET_QK_DIM) for hd in heads]
    vs = [slice(hd * RET_V_DIM, (hd + 1) * RET_V_DIM) for hd in heads]
    chunks = _row_splits(tq, tq // RET_CHUNK)
    gate_cols = iter(_row_splits(RET_VW, 4 * len(chunks)))

    def gate_piece():
        c = next(gate_cols)
        g_scr[:, c] = _dot(h, win_ref[:, RET_GATE_COL0 + c.start:RET_GATE_COL0 + c.stop])

    for rows in chunks:
        scores = [_dot_nt(q_scr[rows, qs[hd]], k_scr[rows, qs[hd]]) for hd in heads]
        gate_piece()
        cross = [_dot(q_scr[rows, qs[hd]], st_ref[hd].astype(BF16)) for hd in heads]
        gate_piece()
        for hd in heads:
            st_ref[hd] = (math.exp(RET_CHUNK * RET_LOG_GAMMA[hd]) * st_ref[hd]
                          + _dot_tn(kd_scr[rows, qs[hd]], v_scr[rows, vs[hd]]))
        gate_piece()
        for hd in heads:
            inner = _dot((scores[hd] * dm_scr[hd]).astype(BF16), v_scr[rows, vs[hd]])
            o_scr[rows, vs[hd]] = inner + cross[hd] * jnp.exp((idx + 1.0) * RET_LOG_GAMMA[hd])
        gate_piece()

    ogs = [_ret_gated(o_scr[r, :], g_scr[r, :]) for r in chunks]
    _out_proj_and_residual(chunks, ogs, x, emb, gpost_ref[...], wout_ref, wgate_ref, xo_ref)


def _ret_prompt_layer(layer, x, p_all, cos, sin, gpre, gpost, w_in, w_out, w_gate, w_ple):
    nb, seq, d = x.shape
    tq = min(512, seq)
    ple = p_all.shape[-1]
    kernel = functools.partial(_ret_prompt_kernel, tq=tq)
    tok = lambda b, t: (b, t, 0)
    return pl.pallas_call(
        kernel,
        grid=(nb, seq // tq),
        in_specs=[
            pl.BlockSpec((None, tq, d), tok),
            pl.BlockSpec((None, None, tq, ple), lambda b, t: (layer, b, t, 0)),
            pl.BlockSpec((tq, LANES), lambda b, t: (t, 0)),
            pl.BlockSpec((tq, LANES), lambda b, t: (t, 0)),
            _const_spec(gpre.shape), _const_spec(gpost.shape),
            _const_spec(w_in.shape), _const_spec(w_out.shape),
            _const_spec(w_gate.shape), _const_spec(w_ple.shape),
        ],
        out_specs=[
            pl.BlockSpec((None, tq, d), tok),
            pl.BlockSpec((None, RET_HEADS, RET_QK_DIM, RET_V_DIM), lambda b, t: (b, 0, 0, 0)),
        ],
        out_shape=[
            jax.ShapeDtypeStruct(x.shape, F32),
            jax.ShapeDtypeStruct((nb, RET_HEADS, RET_QK_DIM, RET_V_DIM), F32),
        ],
        scratch_shapes=[
            pltpu.VMEM((tq, RET_QK_W), BF16),
            pltpu.VMEM((tq, RET_QK_W), BF16),
            pltpu.VMEM((tq, RET_QK_W), BF16),
            pltpu.VMEM((tq, RET_VW), BF16),
            pltpu.VMEM((tq, RET_VW), F32),
            pltpu.VMEM((tq, RET_VW), F32),
            pltpu.VMEM((RET_HEADS, RET_CHUNK, RET_CHUNK), F32),
        ],
        compiler_params=pltpu.CompilerParams(
            dimension_semantics=("arbitrary", "arbitrary"),
            vmem_limit_bytes=VMEM_LIMIT_BYTES),
        name="ret_prompt_layer",
    )(x, p_all, cos, sin, gpre, gpost, w_in, w_out, w_gate, w_ple)


def _ret_sample_proj_kernel(x_ref, cos_ref, sin_ref, gpre_ref, win_ref,
                            q_ref, k_ref, kd_ref, v_ref, g_ref, *, dec_len):
    h = _rms_norm(x_ref[...], gpre_ref[...]).astype(BF16)
    q, k, kd, v = _ret_project(h, cos_ref[...], sin_ref[...], win_ref, dec_len)
    q_ref[...] = q
    k_ref[...] = k
    kd_ref[...] = kd
    v_ref[...] = v
    g_ref[...] = _dot(h, win_ref[:, RET_GATE_COL0:])


def _ret_sample_proj(x, cos, sin, gpre, w_in, dec_len):
    n_tok, d = x.shape
    tq = min(RET_SAMPLE_TILE, n_tok)
    kernel = functools.partial(_ret_sample_proj_kernel, dec_len=dec_len)
    tok = lambda i: (i, 0)
    widths = (RET_QK_W, RET_QK_W, RET_QK_W, RET_VW, RET_VW)
    return pl.pallas_call(
        kernel,
        grid=(n_tok // tq,),
        in_specs=[
            pl.BlockSpec((tq, d), tok),
            _const_spec((tq, LANES)), _const_spec((tq, LANES)),
            _const_spec(gpre.shape), _const_spec(w_in.shape),
        ],
        out_specs=[pl.BlockSpec((tq, w), tok) for w in widths],
        out_shape=[jax.ShapeDtypeStruct((n_tok, w), F32) for w in widths],
        compiler_params=pltpu.CompilerParams(
            dimension_semantics=("arbitrary",),
            vmem_limit_bytes=VMEM_LIMIT_BYTES),
        name="ret_sample_proj",
    )(x, cos, sin, gpre, w_in)


def _ret_state_update_tile(q_ref, k_ref, kd_ref, v_ref, st_in, st_out, o_ref, *,
                           first_seq, n_seq, dec_len):
    pad_rows = 2 * dec_len
    idx = lax.broadcasted_iota(jnp.int32, (pad_rows, 1), 0).astype(F32)

    def tok_rows(s):
        return slice((first_seq + s) * dec_len, (first_seq + s + 1) * dec_len)

    def padded(ref, s, cols):
        val = ref[tok_rows(s), cols]
        return jnp.concatenate([val, jnp.zeros_like(val)], axis=0).astype(BF16)

    units = [(s, hd) for s in range(n_seq) for hd in range(RET_HEADS)]
    qs = [slice(hd * RET_QK_DIM, (hd + 1) * RET_QK_DIM) for hd in range(RET_HEADS)]
    vs = [slice(hd * RET_V_DIM, (hd + 1) * RET_V_DIM) for hd in range(RET_HEADS)]
    qh = {(s, hd): padded(q_ref, s, qs[hd]) for s, hd in units}
    vh = {(s, hd): padded(v_ref, s, vs[hd]) for s, hd in units}
    scores = {(s, hd): _dot_nt(qh[s, hd], padded(k_ref, s, qs[hd])) for s, hd in units}
    cross = {(s, hd): _dot(qh[s, hd], st_in[s, hd].astype(BF16)) for s, hd in units}
    for s, hd in units:
        st_out[s, hd] = (math.exp(dec_len * RET_LOG_GAMMA[hd]) * st_in[s, hd]
                         + _dot_tn(padded(kd_ref, s, qs[hd]), vh[s, hd]))
    masks = [_ret_decay_mask(pad_rows, dec_len, hd) for hd in range(RET_HEADS)]
    for s, hd in units:
        inner = _dot((scores[s, hd] * masks[hd]).astype(BF16), vh[s, hd])
        out = inner + cross[s, hd] * jnp.exp((idx + 1.0) * RET_LOG_GAMMA[hd])
        o_ref[tok_rows(s), vs[hd]] = out[0:dec_len, :]


def _ret_sample_finish_kernel(o_ref, g_ref, x_ref, p_ref, gpost_ref, wout_ref, wgate_ref, wple_ref,
                              xo_ref):
    emb = _dot(p_ref[...].astype(BF16), wple_ref[...])
    x = x_ref[...]
    rows = _row_splits(x.shape[0], 2)
    ogs = [_ret_gated(o_ref[r, :], g_ref[r, :]) for r in rows]
    _out_proj_and_residual(rows, ogs, x, emb, gpost_ref[...], wout_ref, wgate_ref, xo_ref)


def _ret_sample_finish(layer, o, g, x, p_all, gpost, w_out, w_gate, w_ple):
    n_tok, d = x.shape
    tq = min(RET_SAMPLE_TILE, n_tok)
    ple = p_all.shape[-1]
    tok = lambda i: (i, 0)
    return pl.pallas_call(
        _ret_sample_finish_kernel,
        grid=(n_tok // tq,),
        in_specs=[
            pl.BlockSpec((tq, RET_VW), tok), pl.BlockSpec((tq, RET_VW), tok),
            pl.BlockSpec((tq, d), tok),
            pl.BlockSpec((None, tq, ple), lambda i: (layer, i, 0)),
            _const_spec(gpost.shape), _const_spec(w_out.shape),
            _const_spec(w_gate.shape), _const_spec(w_ple.shape),
        ],
        out_specs=pl.BlockSpec((tq, d), tok),
        out_shape=jax.ShapeDtypeStruct(x.shape, F32),
        compiler_params=pltpu.CompilerParams(
            dimension_semantics=("arbitrary",),
            vmem_limit_bytes=VMEM_LIMIT_BYTES),
        name="ret_sample_finish",
    )(o, g, x, p_all, gpost, w_out, w_gate, w_ple)


def _rope_tables(pos, head_dim):
    half = head_dim // 2
    inv = ROPE_THETA ** (-np.arange(half, dtype=np.float64) / half)
    ang = np.asarray(pos, np.float64)[:, None] * inv[None, :]
    return np.cos(ang).astype(np.float32), np.sin(ang).astype(np.float32)


def _attn_tables(pos):
    cos, sin = _rope_tables(pos, ATTN_HEAD_DIM)
    reps = LANES // ATTN_HEAD_DIM
    return (np.tile(np.concatenate([cos, cos], axis=1), (1, reps)),
            np.tile(np.concatenate([-sin, sin], axis=1), (1, reps)))


def kernel(x_prompt, x_sample, cache_k_win, cache_v_win, state_ret, p_prompt, p_sample, pre_norm,
           post_norm, w_in_attn, attn_sinks, w_out_attn, w_in_ret, w_out_ret, w_ple, w_ple_gate):
    depth = p_prompt.shape[0]
    nb, seq, d = x_prompt.shape
    n_dec, dec_len, _ = x_sample.shape
    n_stok = n_dec * dec_len
    assert seq % 512 == 0 or seq in (128, 256), seq

    pos_p = np.arange(seq)
    pos_s = PAST_LEN + np.arange(dec_len)
    cos_ap, sin_ap = _attn_tables(pos_p)
    cos_apt, sin_apt = (np.ascontiguousarray(a.T) for a in _rope_tables(pos_p, ATTN_HEAD_DIM))
    cos_as, sin_as = _attn_tables(pos_s)
    cos_rp, sin_rp = _rope_tables(pos_p, RET_QK_DIM)
    cos_rs, sin_rs = _rope_tables(pos_s, RET_QK_DIM)
    attn_s_tile = WINDOW // dec_len
    ret_s_tile = min(RET_SAMPLE_TILE, n_stok) // dec_len
    cos_as, sin_as = (np.tile(a, (attn_s_tile, 1)) for a in (cos_as, sin_as))
    cos_ast, sin_ast = (np.ascontiguousarray(np.tile(a, (attn_s_tile, 1)).T)
                        for a in _rope_tables(pos_s, ATTN_HEAD_DIM))
    cos_rs, sin_rs = (np.tile(a, (ret_s_tile, 1)) for a in (cos_rs, sin_rs))
    (cos_ap, sin_ap, cos_apt, sin_apt, cos_as, sin_as, cos_ast, sin_ast,
     cos_rp, sin_rp, cos_rs, sin_rs) = (
        jnp.asarray(a) for a in
        (cos_ap, sin_ap, cos_apt, sin_apt, cos_as, sin_as, cos_ast, sin_ast,
         cos_rp, sin_rp, cos_rs, sin_rs))

    assert depth == 2, depth
    xs = x_sample.reshape(n_stok, d)
    p_s = p_sample.reshape(depth, n_stok, p_sample.shape[-1])
    bf = lambda w: w.astype(BF16)
    gpre = [pre_norm[i][None, :] for i in range(depth)]
    gpost = [post_norm[i][None, :] for i in range(depth)]
    w_gate = [bf(w_ple_gate[i]) for i in range(depth)]
    w_pl = [bf(w_ple[i]) for i in range(depth)]

    w_in = bf(w_in_attn[0])
    w_out = bf(w_out_attn[0])
    sinks = attn_sinks[0]
    k0, v0, g0 = ATTN_Q_DIM, ATTN_Q_DIM + ATTN_KV_DIM, ATTN_Q_DIM + 2 * ATTN_KV_DIM
    w_q, w_k, w_v, w_g = w_in[:, :k0], w_in[:, k0:v0], w_in[:, v0:g0], w_in[:, g0:]
    to_t = lambda a: a.transpose(0, 2, 3, 1)
    from_t = lambda a: a.transpose(0, 3, 1, 2)
    xs, knt, vnt = _attn_sample_layer(
        0, xs, p_s, (cos_as, sin_as, cos_ast, sin_ast), gpre[0], gpost[0], sinks,
        w_q, w_k.T, w_v.T, w_g, w_out, w_gate[0], w_pl[0],
        to_t(cache_k_win[0]), to_t(cache_v_win[0]), dec_len)

    w_in_r = bf(w_in_ret[0])
    w_out_r = bf(w_out_ret[0])
    rq, rk, rkd, rv, rg = _ret_sample_proj(xs, cos_rs, sin_rs, gpre[1], w_in_r, dec_len)
    xp, kpt, vpt, ro, ret_state_sample = _attn_prompt_layer(
        0, x_prompt, p_prompt, (cos_ap, sin_ap, cos_apt, sin_apt), gpre[0], gpost[0], sinks,
        w_q.T, w_k, w_v.T, w_g, w_out, w_gate[0], w_pl[0],
        (rq, rk, rkd, rv), state_ret[0], dec_len)
    head_shape = (nb, ATTN_KV_HEADS, ATTN_HEAD_DIM, WINDOW)

    xp, ret_state_prompt = _ret_prompt_layer(1, xp, p_prompt, cos_rp, sin_rp, gpre[1], gpost[1],
                                             w_in_r, w_out_r, w_gate[1], w_pl[1])
    xs = _ret_sample_finish(1, ro, rg, xs, p_s, gpost[1], w_out_r, w_gate[1], w_pl[1])
    return (xp, xs.reshape(n_dec, dec_len, d),
            from_t(kpt.reshape(head_shape))[None], from_t(vpt.reshape(head_shape))[None],
            from_t(knt)[None], from_t(vnt)[None], ret_state_prompt[None], ret_state_sample[None])
```

```python
import functools
import math

import numpy as np
import jax
import jax.numpy as jnp
from jax import lax
from jax.experimental import pallas as pl
from jax.experimental.pallas import tpu as pltpu

F32 = jnp.float32
BF16 = jnp.bfloat16

PAST_LEN = 16384
ROPE_THETA = 10000.0
NORM_EPS = 1e-6
NEG_INF = -1e30
WINDOW = 128
ATTN_HEADS = 16
ATTN_KV_HEADS = 4
ATTN_GROUP = ATTN_HEADS // ATTN_KV_HEADS
ATTN_HEAD_DIM = 64
ATTN_Q_DIM = ATTN_HEADS * ATTN_HEAD_DIM
ATTN_KV_DIM = ATTN_KV_HEADS * ATTN_HEAD_DIM
RET_HEADS = 4
RET_QK_DIM = 256
RET_V_DIM = 512
RET_QK_W = RET_HEADS * RET_QK_DIM
RET_VW = RET_HEADS * RET_V_DIM
RET_CHUNK = 256
LANES = 128

LOG2_E = math.log2(math.e)
RET_LOG_GAMMA = tuple(math.log1p(-(2.0 ** (-5.0 - h))) for h in range(RET_HEADS))

VMEM_LIMIT_BYTES = 56 * 1024 * 1024
ATTN_PROMPT_VMEM_LIMIT_BYTES = 50 * 1024 * 1024
STATE_PHASES = 4
RET_SAMPLE_TILE = 256
SAMPLE_ATTN_PIPELINE_GROUP = 8


def _dot(a, b):
    return jnp.dot(a, b, preferred_element_type=F32)


def _dot_nt(a, b):
    return lax.dot_general(a, b, (((1,), (1,)), ((), ())), preferred_element_type=F32)


def _dot_tn(a, b):
    return lax.dot_general(a, b, (((0,), (0,)), ((), ())), preferred_element_type=F32)


def _rms_norm(x, g):
    return x * lax.rsqrt(jnp.mean(x * x, axis=-1, keepdims=True) + NORM_EPS) * g


def _silu(g):
    return g * jax.nn.sigmoid(g)


def _row_splits(n, parts):
    step = n // parts
    return [slice(i * step, (i + 1) * step) for i in range(parts)]


def _out_proj_and_residual(rows, ogs, x, emb, gpost, wout_ref, wgate_ref, xo_ref):
    ys = [_dot(og, wout_ref[...]) for og in ogs]
    x1s = [x[r, :] + _rms_norm(y, gpost) for r, y in zip(rows, ys)]
    gates = [_dot(x1.astype(BF16), wgate_ref[...]) for x1 in x1s]
    for r, x1, gate in zip(rows, x1s, gates):
        xo_ref[r, :] = x1 + jax.nn.sigmoid(gate) * emb[r, :]


def _rope_attn(x, cos, sin_signed):
    lane = lax.broadcasted_iota(jnp.int32, (1, LANES), 1)
    first_half = (lane % ATTN_HEAD_DIM) < (ATTN_HEAD_DIM // 2)
    outs = []
    for j in range(x.shape[1] // LANES):
        xj = x[:, j * LANES:(j + 1) * LANES]
        fwd = pltpu.roll(xj, LANES - ATTN_HEAD_DIM // 2, axis=1)
        bwd = pltpu.roll(xj, ATTN_HEAD_DIM // 2, axis=1)
        outs.append(xj * cos + jnp.where(first_half, fwd, bwd) * sin_signed)
    return jnp.concatenate(outs, axis=1)


def _rope_ret(x, cos, sin):
    half = RET_QK_DIM // 2
    outs = []
    for h in range(x.shape[1] // RET_QK_DIM):
        x1 = x[:, h * RET_QK_DIM:h * RET_QK_DIM + half]
        x2 = x[:, h * RET_QK_DIM + half:(h + 1) * RET_QK_DIM]
        outs.append(x1 * cos - x2 * sin)
        outs.append(x2 * cos + x1 * sin)
    return jnp.concatenate(outs, axis=1)


def _attn_prompt_kernel(sink_ref, x_ref, p_ref, cos_ref, sin_ref, cost_ref, sint_ref,
                        gpre_ref, gpost_ref, wqt_ref, wk_ref, wvt_ref, wg_ref,
                        wout_ref, wgate_ref, wple_ref,
                        rq_ref, rk_ref, rkd_ref, rv_ref, rstate_hbm,
                        xo_ref, kw_ref, vw_ref, ro_ref, rstate_out_hbm,
                        qt_scr, kext_scr, vt_scr, st_scr, pt_scr, inv_scr, ot_scr,
                        rst_in, rst_out, sem_in, sem_out, *, tq, n_seq, dec_len):
    nblk = tq // WINDOW
    half = ATTN_HEAD_DIM // 2
    t = pl.program_id(1)
    nt = pl.num_programs(1)

    @pl.when(t == 0)
    def _():
        kext_scr[0:WINDOW, :] = jnp.zeros((WINDOW, ATTN_KV_DIM), BF16)
        vt_scr[0] = jnp.zeros((ATTN_KV_DIM, WINDOW), BF16)

    per_phase = n_seq // STATE_PHASES
    step = pl.program_id(0) * nt + t
    n_steps = pl.num_programs(0) * nt
    seq0 = step * n_seq
    last_seq = n_steps * n_seq - 1

    def load_copy(seq, slot, i):
        return pltpu.make_async_copy(rstate_hbm.at[seq], rst_in.at[slot, i], sem_in.at[slot, i])

    def store_copy(seq, slot, i):
        return pltpu.make_async_copy(rst_out.at[slot, i], rstate_out_hbm.at[seq],
                                     sem_out.at[slot, i])

    @pl.when(step == 0)
    def _():
        rst_out[...] = jnp.zeros(rst_out.shape, F32)
        for slot in range(2):
            for i in range(per_phase):
                load_copy(slot * per_phase + i, slot, i).start()
                store_copy(slot * per_phase + i, slot, i).start()

    def state_phase(phase):
        slot = phase % 2
        first = seq0 + phase * per_phase
        for i in range(per_phase):
            load_copy(first + i, slot, i).wait()
            store_copy(first + i, slot, i).wait()
        _ret_state_update_tile(rq_ref, rk_ref, rkd_ref, rv_ref, rst_in.at[slot], rst_out.at[slot],
                               ro_ref, first_seq=phase * per_phase, n_seq=per_phase,
                               dec_len=dec_len)
        for i in range(per_phase):
            store_copy(first + i, slot, i).start()
            load_copy(jnp.minimum(first + i + 2 * per_phase, last_seq), slot, i).start()

    state_phase(0)

    emb = _dot(p_ref[...].astype(BF16), wple_ref[...])
    x = x_ref[...]
    h = _rms_norm(x, gpre_ref[...]).astype(BF16)

    k = _rope_attn(_dot(h, wk_ref[...]), cos_ref[...], sin_ref[...])
    kext_scr[WINDOW:, :] = k.astype(BF16)

    qt = _dot_nt(wqt_ref[...], h)
    cost = cost_ref[...]
    sint = sint_ref[...]
    pieces = []
    for hd in range(ATTN_HEADS):
        x1 = qt[hd * ATTN_HEAD_DIM:hd * ATTN_HEAD_DIM + half, :]
        x2 = qt[hd * ATTN_HEAD_DIM + half:(hd + 1) * ATTN_HEAD_DIM, :]
        pieces.append(x1 * cost - x2 * sint)
        pieces.append(x2 * cost + x1 * sint)
    qrot = (jnp.concatenate(pieces, axis=0) * (ATTN_HEAD_DIM ** -0.5 * LOG2_E)).astype(BF16)
    for jb in range(nblk):
        qt_scr[jb] = qrot[:, jb * WINDOW:(jb + 1) * WINDOW]

    vt = _dot_nt(wvt_ref[...], h)
    vtb = vt.astype(BF16)
    for jb in range(nblk):
        vt_scr[jb + 1] = vtb[:, jb * WINDOW:(jb + 1) * WINDOW]

    @pl.when(t == nt - 1)
    def _():
        kw_ref[...] = k[tq - WINDOW:, :].T
        vw_ref[...] = vt[:, tq - WINDOW:]

    state_phase(1)
    zero_rows = jnp.zeros((ATTN_HEAD_DIM, ATTN_GROUP * WINDOW), BF16)
    groups = [(jb, kh) for jb in range(nblk) for kh in range(ATTN_KV_HEADS)]

    for gidx, (jb, kh) in enumerate(groups):
        heads = [kh * ATTN_GROUP + gi for gi in range(ATTN_GROUP)]
        pair = kh // 2
        kb = kext_scr[jb * WINDOW:(jb + 2) * WINDOW, pair * LANES:(pair + 1) * LANES]
        qg = jnp.concatenate(
            [qt_scr[jb, hd * ATTN_HEAD_DIM:(hd + 1) * ATTN_HEAD_DIM, :] for hd in heads], axis=1)
        qg = jnp.concatenate([qg, zero_rows] if kh % 2 == 0 else [zero_rows, qg], axis=0)
        st_scr[gidx] = _dot(kb, qg)

    g = _dot(h, wg_ref[...])

    key_r = lax.broadcasted_iota(jnp.int32, (WINDOW, WINDOW), 0)
    query_c = lax.broadcasted_iota(jnp.int32, (WINDOW, WINDOW), 1)
    from_prev = key_r > query_c
    prev_w = from_prev.astype(BF16)
    cur_w = 1.0 - prev_w
    for gidx, (jb, kh) in enumerate(groups):
        for gi in range(ATTN_GROUP):
            hd = kh * ATTN_GROUP + gi
            cols = slice(gi * WINDOW, (gi + 1) * WINDOW)
            s_prev = st_scr[gidx, 0:WINDOW, cols]
            if jb == 0:
                s_prev = jnp.where(t > 0, s_prev, NEG_INF)
            s = jnp.where(from_prev, s_prev, st_scr[gidx, WINDOW:, cols])
            sink = sink_ref[hd] * LOG2_E
            m = jnp.maximum(jnp.max(s, axis=0, keepdims=True), sink)
            pr = jnp.exp2(s - m)
            den = jnp.sum(pr, axis=0, keepdims=True) + jnp.exp2(sink - m)
            prb = pr.astype(BF16)
            pt_scr[gidx, 0:WINDOW, cols] = prb * prev_w
            pt_scr[gidx, WINDOW:, cols] = prb * cur_w
            inv_scr[jb, hd:hd + 1, :] = 1.0 / den

    for gidx, (jb, kh) in enumerate(groups):
        vband = jnp.concatenate([vt_scr[jb, kh * ATTN_HEAD_DIM:(kh + 1) * ATTN_HEAD_DIM, :],
                                 vt_scr[jb + 1, kh * ATTN_HEAD_DIM:(kh + 1) * ATTN_HEAD_DIM, :]],
                                axis=1)
        otg = _dot(vband, pt_scr[gidx])
        for gi in range(ATTN_GROUP):
            hd = kh * ATTN_GROUP + gi
            ot_scr[jb, hd * ATTN_HEAD_DIM:(hd + 1) * ATTN_HEAD_DIM, :] = (
                otg[:, gi * WINDOW:(gi + 1) * WINDOW] * inv_scr[jb, hd:hd + 1, :])

    state_phase(2)

    parts = 2 if nblk % 2 == 0 else 1
    rows = _row_splits(tq, parts)
    ogs = []
    for r in rows:
        o = jnp.concatenate([ot_scr[jb].T for jb in range(r.start // WINDOW, r.stop // WINDOW)],
                            axis=0)
        ogs.append((o * _silu(g[r, :])).astype(BF16))
    _out_proj_and_residual(rows, ogs, x, emb, gpost_ref[...], wout_ref, wgate_ref, xo_ref)
    state_phase(3)

    kext_scr[0:WINDOW, :] = kext_scr[tq:tq + WINDOW, :]
    vt_scr[0] = vt_scr[nblk]

    @pl.when(step == n_steps - 1)
    def _():
        for slot in range(2):
            for i in range(per_phase):
                load_copy(last_seq, slot, i).wait()
                store_copy(last_seq, slot, i).wait()


def _const_spec(shape):
    nd = len(shape)
    return pl.BlockSpec(shape, lambda *_: (0,) * nd)


def _attn_prompt_layer(layer, x, p_all, tables, gpre, gpost, sinks, w_qt, w_k, w_vt, w_g,
                       w_out, w_gate, w_ple, ret_qkv, ret_state, dec_len):
    nb, seq, d = x.shape
    tq = min(512, seq)
    nblk = tq // WINDOW
    nt = seq // tq
    n_seq = ret_state.shape[0] // (nb * nt)
    assert n_seq * nb * nt == ret_state.shape[0] and n_seq % STATE_PHASES == 0, (
        ret_state.shape, nb, nt)
    ple = p_all.shape[-1]
    cos, sin, cost, sint = tables
    kernel = functools.partial(_attn_prompt_kernel, tq=tq, n_seq=n_seq, dec_len=dec_len)
    tok = lambda b, t: (b, t, 0)
    win_blk = lambda b, t: (b, 0, 0)
    stok = lambda b, t: (b * nt + t, 0)
    consts = (gpre, gpost, w_qt, w_k, w_vt, w_g, w_out, w_gate, w_ple)
    rq, rk, rkd, rv = ret_qkv
    st_shape = (2, n_seq // STATE_PHASES) + ret_state.shape[1:]
    return pl.pallas_call(
        kernel,
        grid=(nb, nt),
        in_specs=[
            pl.BlockSpec(memory_space=pltpu.SMEM),
            pl.BlockSpec((None, tq, d), tok),
            pl.BlockSpec((None, None, tq, ple), lambda b, t: (layer, b, t, 0)),
            pl.BlockSpec((tq, LANES), lambda b, t: (t, 0)),
            pl.BlockSpec((tq, LANES), lambda b, t: (t, 0)),
            pl.BlockSpec((ATTN_HEAD_DIM // 2, tq), lambda b, t: (0, t)),
            pl.BlockSpec((ATTN_HEAD_DIM // 2, tq), lambda b, t: (0, t)),
        ] + [_const_spec(c.shape) for c in consts] + [
            pl.BlockSpec((2 * n_seq * dec_len, RET_QK_W), stok),
            pl.BlockSpec((2 * n_seq * dec_len, RET_QK_W), stok),
            pl.BlockSpec((2 * n_seq * dec_len, RET_QK_W), stok),
            pl.BlockSpec((2 * n_seq * dec_len, RET_VW), stok),
            pl.BlockSpec(memory_space=pl.ANY),
        ],
        out_specs=[
            pl.BlockSpec((None, tq, d), tok),
            pl.BlockSpec((None, ATTN_KV_DIM, WINDOW), win_blk),
            pl.BlockSpec((None, ATTN_KV_DIM, WINDOW), win_blk),
            pl.BlockSpec((n_seq * dec_len, RET_VW), stok),
            pl.BlockSpec(memory_space=pl.ANY),
        ],
        out_shape=[
            jax.ShapeDtypeStruct(x.shape, F32),
            jax.ShapeDtypeStruct((nb, ATTN_KV_DIM, WINDOW), F32),
            jax.ShapeDtypeStruct((nb, ATTN_KV_DIM, WINDOW), F32),
            jax.ShapeDtypeStruct((rq.shape[0] // 2, RET_VW), F32),
            jax.ShapeDtypeStruct(ret_state.shape, F32),
        ],
        scratch_shapes=[
            pltpu.VMEM((nblk, ATTN_Q_DIM, WINDOW), BF16),
            pltpu.VMEM((tq + WINDOW, ATTN_KV_DIM), BF16),
            pltpu.VMEM((nblk + 1, ATTN_KV_DIM, WINDOW), BF16),
            pltpu.VMEM((nblk * ATTN_KV_HEADS, 2 * WINDOW, ATTN_GROUP * WINDOW), F32),
            pltpu.VMEM((nblk * ATTN_KV_HEADS, 2 * WINDOW, ATTN_GROUP * WINDOW), BF16),
            pltpu.VMEM((nblk, ATTN_HEADS, WINDOW), F32),
            pltpu.VMEM((nblk, ATTN_Q_DIM, WINDOW), F32),
            pltpu.VMEM(st_shape, F32),
            pltpu.VMEM(st_shape, F32),
            pltpu.SemaphoreType.DMA((2, n_seq // STATE_PHASES)),
            pltpu.SemaphoreType.DMA((2, n_seq // STATE_PHASES)),
        ],
        compiler_params=pltpu.CompilerParams(
            dimension_semantics=("arbitrary", "arbitrary"),
            vmem_limit_bytes=ATTN_PROMPT_VMEM_LIMIT_BYTES),
        name="attn_prompt_layer",
    )(sinks, x, p_all, cos, sin, cost, sint, *consts, rq, rk, rkd, rv, ret_state)


def _attn_sample_kernel(sink_ref, x_ref, p_ref, cos_ref, sin_ref, cost_ref, sint_ref,
                        gpre_ref, gpost_ref, wq_ref, wkt_ref, wvt_ref, wg_ref,
                        wout_ref, wgate_ref, wple_ref, kc_ref, vc_ref,
                        xo_ref, ko_ref, vo_ref,
                        q_scr, knt_scr, vnt_scr, o_scr, *, n_seq, dec_len):
    half = ATTN_HEAD_DIM // 2
    x = x_ref[...]
    h = _rms_norm(x, gpre_ref[...]).astype(BF16)
    q_scr[...] = (_rope_attn(_dot(h, wq_ref[...]), cos_ref[...], sin_ref[...])
                  * (ATTN_HEAD_DIM ** -0.5 * LOG2_E))
    knt = _dot_nt(wkt_ref[...], h)
    cost = cost_ref[...]
    sint = sint_ref[...]
    pieces = []
    for kh in range(ATTN_KV_HEADS):
        x1 = knt[kh * ATTN_HEAD_DIM:kh * ATTN_HEAD_DIM + half, :]
        x2 = knt[kh * ATTN_HEAD_DIM + half:(kh + 1) * ATTN_HEAD_DIM, :]
        pieces.append(x1 * cost - x2 * sint)
        pieces.append(x2 * cost + x1 * sint)
    knt_scr[...] = jnp.concatenate(pieces, axis=0)
    vnt_scr[...] = _dot_nt(wvt_ref[...], h)

    rows = 2 * dec_len
    tok = lax.broadcasted_iota(jnp.int32, (rows, 2 * WINDOW), 0) % dec_len
    col = lax.broadcasted_iota(jnp.int32, (rows, 2 * WINDOW), 1)
    upper_row = lax.broadcasted_iota(jnp.int32, (rows, 1), 0) >= dec_len
    lane = lax.broadcasted_iota(jnp.int32, (ATTN_HEAD_DIM, WINDOW), 1)
    keep_old = lane < WINDOW - dec_len
    zeros = jnp.zeros((ATTN_HEAD_DIM, 2 * WINDOW), BF16)

    def select_head(t, parity):
        return jnp.concatenate([t, zeros] if parity == 0 else [zeros, t], axis=0)

    def scores(s):
        tok_rows = slice(s * dec_len, (s + 1) * dec_len)
        new_col = col - (WINDOW + s * dec_len)
        mask = ((col < WINDOW) & (col > tok)) | ((new_col >= 0) & (new_col <= tok))
        new_shift = (WINDOW - dec_len - s * dec_len) % WINDOW
        out = []
        for kh in range(ATTN_KV_HEADS):
            hrows = slice(kh * ATTN_HEAD_DIM, (kh + 1) * ATTN_HEAD_DIM)
            kc = kc_ref[s, kh]
            vc = vc_ref[s, kh]
            kn = knt_scr[hrows, :]
            vn = vnt_scr[hrows, :]
            kn_at_end = pltpu.roll(kn, new_shift, axis=1) if new_shift else kn
            vn_at_end = pltpu.roll(vn, new_shift, axis=1) if new_shift else vn
            ko_ref[s, kh] = jnp.where(keep_old, pltpu.roll(kc, WINDOW - dec_len, axis=1), kn_at_end)
            vo_ref[s, kh] = jnp.where(keep_old, pltpu.roll(vc, WINDOW - dec_len, axis=1), vn_at_end)

            kt = jnp.concatenate([kc, kn], axis=1).astype(BF16)
            vt = jnp.concatenate([vc, vn], axis=1).astype(BF16)
            qp = jnp.concatenate([q_scr[tok_rows, (2 * kh + c) * LANES:(2 * kh + c + 1) * LANES]
                                  for c in range(2)], axis=0).astype(BF16)
            sc = [jnp.where(mask, _dot(qp, select_head(kt, parity)), NEG_INF)
                  for parity in range(2)]
            out.append((sc, vt))
        return out

    def values(s, per_head):
        tok_rows = slice(s * dec_len, (s + 1) * dec_len)
        probs = []
        for kh, (scs, vt) in enumerate(per_head):
            for parity, sc in enumerate(scs):
                hd_lo, hd_hi = 4 * kh + parity, 4 * kh + 2 + parity
                sink = jnp.where(upper_row, sink_ref[hd_hi], sink_ref[hd_lo]) * LOG2_E
                m = jnp.maximum(jnp.max(sc, axis=-1, keepdims=True), sink)
                pr = jnp.exp2(sc - m)
                den = jnp.sum(pr, axis=-1, keepdims=True) + jnp.exp2(sink - m)
                probs.append((pr / den).astype(BF16))
        for kh, (scs, vt) in enumerate(per_head):
            o_pair = (_dot_nt(probs[2 * kh], select_head(vt, 0))
                      + _dot_nt(probs[2 * kh + 1], select_head(vt, 1)))
            for c in range(2):
                o_scr[tok_rows, (2 * kh + c) * LANES:(2 * kh + c + 1) * LANES] = (
                    o_pair[c * dec_len:(c + 1) * dec_len, :])

    group = min(SAMPLE_ATTN_PIPELINE_GROUP, n_seq)
    pending = [scores(s) for s in range(group)]
    for s0 in range(0, n_seq, group):
        upcoming = [scores(s) for s in range(s0 + group, min(s0 + 2 * group, n_seq))]
        for i, per_head in enumerate(pending):
            values(s0 + i, per_head)
        pending = upcoming

    g = _dot(h, wg_ref[...])
    emb = _dot(p_ref[...].astype(BF16), wple_ref[...])
    og = (o_scr[...] * _silu(g)).astype(BF16)
    _out_proj_and_residual(_row_splits(x.shape[0], 1), [og], x, emb, gpost_ref[...],
                           wout_ref, wgate_ref, xo_ref)


def _attn_sample_layer(layer, x, p_all, tables, gpre, gpost, sinks, w_q, w_kt, w_vt, w_g,
                       w_out, w_gate, w_ple, cache_kt, cache_vt, dec_len):
    n_tok, d = x.shape
    n_all = cache_kt.shape[0]
    n_seq = WINDOW // dec_len
    tq = n_seq * dec_len
    assert tq == WINDOW and n_all % n_seq == 0, (dec_len, n_all)
    ple = p_all.shape[-1]
    cos, sin, cost, sint = tables
    kernel = functools.partial(_attn_sample_kernel, n_seq=n_seq, dec_len=dec_len)
    tok = lambda i: (i, 0)
    cache_shape = (n_seq, ATTN_KV_HEADS, ATTN_HEAD_DIM, WINDOW)
    cache_blk = lambda i: (i, 0, 0, 0)
    consts = (cos, sin, cost, sint, gpre, gpost, w_q, w_kt, w_vt, w_g, w_out, w_gate, w_ple)
    return pl.pallas_call(
        kernel,
        grid=(n_all // n_seq,),
        in_specs=[
            pl.BlockSpec(memory_space=pltpu.SMEM),
            pl.BlockSpec((tq, d), tok),
            pl.BlockSpec((None, tq, ple), lambda i: (layer, i, 0)),
        ] + [_const_spec(c.shape) for c in consts] + [
            pl.BlockSpec(cache_shape, cache_blk),
            pl.BlockSpec(cache_shape, cache_blk),
        ],
        out_specs=[
            pl.BlockSpec((tq, d), tok),
            pl.BlockSpec(cache_shape, cache_blk),
            pl.BlockSpec(cache_shape, cache_blk),
        ],
        out_shape=[
            jax.ShapeDtypeStruct(x.shape, F32),
            jax.ShapeDtypeStruct(cache_kt.shape, F32),
            jax.ShapeDtypeStruct(cache_vt.shape, F32),
        ],
        scratch_shapes=[
            pltpu.VMEM((tq, ATTN_Q_DIM), F32),
            pltpu.VMEM((ATTN_KV_DIM, tq), F32),
            pltpu.VMEM((ATTN_KV_DIM, tq), F32),
            pltpu.VMEM((tq, ATTN_Q_DIM), F32),
        ],
        compiler_params=pltpu.CompilerParams(
            dimension_semantics=("arbitrary",),
            vmem_limit_bytes=VMEM_LIMIT_BYTES),
        name="attn_sample_layer",
    )(sinks, x, p_all, *consts, cache_kt, cache_vt)


RET_GATE_COL0 = 2 * RET_QK_W + RET_VW


def _ret_project(h, cos, sin, win_ref, chunk_len):
    q0, k0, v0, g0 = 0, RET_QK_W, 2 * RET_QK_W, RET_GATE_COL0
    q = _rope_ret(_dot(h, win_ref[:, q0:k0]), cos, sin)
    k = _rope_ret(_dot(h, win_ref[:, k0:v0]), cos, sin) * (RET_QK_DIM ** -0.5)
    v = _dot(h, win_ref[:, v0:g0])
    n = h.shape[0]
    idx = (lax.broadcasted_iota(jnp.int32, (n, 1), 0) % chunk_len).astype(F32)
    kd = jnp.concatenate(
        [k[:, hd * RET_QK_DIM:(hd + 1) * RET_QK_DIM]
         * jnp.exp((chunk_len - 1.0 - idx) * RET_LOG_GAMMA[hd]) for hd in range(RET_HEADS)], axis=1)
    return q, k, kd, v


def _ret_decay_mask(n, chunk_len, hd):
    ri = lax.broadcasted_iota(jnp.int32, (n, n), 0)
    ci = lax.broadcasted_iota(jnp.int32, (n, n), 1)
    diff = ri - ci
    ok = (diff >= 0) & ((ri // chunk_len) == (ci // chunk_len))
    return jnp.where(ok, jnp.exp(jnp.maximum(diff, 0).astype(F32) * RET_LOG_GAMMA[hd]), 0.0)


def _ret_gated(o, g):
    outs = []
    for hd in range(RET_HEADS):
        oh = o[:, hd * RET_V_DIM:(hd + 1) * RET_V_DIM]
        mu = jnp.mean(oh, axis=-1, keepdims=True)
        var = jnp.mean(jnp.square(oh - mu), axis=-1, keepdims=True)
        outs.append((oh - mu) * lax.rsqrt(var + NORM_EPS))
    return (jnp.concatenate(outs, axis=1) * _silu(g)).astype(BF16)


def _ret_prompt_kernel(x_ref, p_ref, cos_ref, sin_ref, gpre_ref, gpost_ref,
                       win_ref, wout_ref, wgate_ref, wple_ref,
                       xo_ref, st_ref,
                       q_scr, k_scr, kd_scr, v_scr, o_scr, g_scr, dm_scr, *, tq):
    b = pl.program_id(0)
    t = pl.program_id(1)

    @pl.when((b == 0) & (t == 0))
    def _():
        for hd in range(RET_HEADS):
            dm_scr[hd] = _ret_decay_mask(RET_CHUNK, RET_CHUNK, hd)

    @pl.when(t == 0)
    def _():
        st_ref[...] = jnp.zeros(st_ref.shape, F32)

    emb = _dot(p_ref[...].astype(BF16), wple_ref[...])
    x = x_ref[...]
    h = _rms_norm(x, gpre_ref[...]).astype(BF16)
    q, k, kd, v = _ret_project(h, cos_ref[...], sin_ref[...], win_ref, RET_CHUNK)
    q_scr[...] = q.astype(BF16)
    k_scr[...] = k.astype(BF16)
    kd_scr[...] = kd.astype(BF16)
    v_scr[...] = v.astype(BF16)

    idx = lax.broadcasted_iota(jnp.int32, (RET_CHUNK, 1), 0).astype(F32)
    heads = range(RET_HEADS)
    qs = [slice(hd * RET_QK_DIM, (hd + 1) * RET_QK_DIM) for hd in heads]
    vs = [slice(hd * RET_V_DIM, (hd + 1) * RET_V_DIM) for hd in heads]
    chunks = _row_splits(tq, tq // RET_CHUNK)
    gate_cols = iter(_row_splits(RET_VW, 4 * len(chunks)))

    def gate_piece():
        c = next(gate_cols)
        g_scr[:, c] = _dot(h, win_ref[:, RET_GATE_COL0 + c.start:RET_GATE_COL0 + c.stop])

    for rows in chunks:
        scores = [_dot_nt(q_scr[rows, qs[hd]], k_scr[rows, qs[hd]]) for hd in heads]
        gate_piece()
        cross = [_dot(q_scr[rows, qs[hd]], st_ref[hd].astype(BF16)) for hd in heads]
        gate_piece()
        for hd in heads:
            st_ref[hd] = (math.exp(RET_CHUNK * RET_LOG_GAMMA[hd]) * st_ref[hd]
                          + _dot_tn(kd_scr[rows, qs[hd]], v_scr[rows, vs[hd]]))
        gate_piece()
        for hd in heads:
            inner = _dot((scores[hd] * dm_scr[hd]).astype(BF16), v_scr[rows, vs[hd]])
            o_scr[rows, vs[hd]] = inner + cross[hd] * jnp.exp((idx + 1.0) * RET_LOG_GAMMA[hd])
        gate_piece()

    ogs = [_ret_gated(o_scr[r, :], g_scr[r, :]) for r in chunks]
    _out_proj_and_residual(chunks, ogs, x, emb, gpost_ref[...], wout_ref, wgate_ref, xo_ref)


def _ret_prompt_layer(layer, x, p_all, cos, sin, gpre, gpost, w_in, w_out, w_gate, w_ple):
    nb, seq, d = x.shape
    tq = min(512, seq)
    ple = p_all.shape[-1]
    kernel = functools.partial(_ret_prompt_kernel, tq=tq)
    tok = lambda b, t: (b, t, 0)
    return pl.pallas_call(
        kernel,
        grid=(nb, seq // tq),
        in_specs=[
            pl.BlockSpec((None, tq, d), tok),
            pl.BlockSpec((None, None, tq, ple), lambda b, t: (layer, b, t, 0)),
            pl.BlockSpec((tq, LANES), lambda b, t: (t, 0)),
            pl.BlockSpec((tq, LANES), lambda b, t: (t, 0)),
            _const_spec(gpre.shape), _const_spec(gpost.shape),
            _const_spec(w_in.shape), _const_spec(w_out.shape),
            _const_spec(w_gate.shape), _const_spec(w_ple.shape),
        ],
        out_specs=[
            pl.BlockSpec((None, tq, d), tok),
            pl.BlockSpec((None, RET_HEADS, RET_QK_DIM, RET_V_DIM), lambda b, t: (b, 0, 0, 0)),
        ],
        out_shape=[
            jax.ShapeDtypeStruct(x.shape, F32),
            jax.ShapeDtypeStruct((nb, RET_HEADS, RET_QK_DIM, RET_V_DIM), F32),
        ],
        scratch_shapes=[
            pltpu.VMEM((tq, RET_QK_W), BF16),
            pltpu.VMEM((tq, RET_QK_W), BF16),
            pltpu.VMEM((tq, RET_QK_W), BF16),
            pltpu.VMEM((tq, RET_VW), BF16),
            pltpu.VMEM((tq, RET_VW), F32),
            pltpu.VMEM((tq, RET_VW), F32),
            pltpu.VMEM((RET_HEADS, RET_CHUNK, RET_CHUNK), F32),
        ],
        compiler_params=pltpu.CompilerParams(
            dimension_semantics=("arbitrary", "arbitrary"),
            vmem_limit_bytes=VMEM_LIMIT_BYTES),
        name="ret_prompt_layer",
    )(x, p_all, cos, sin, gpre, gpost, w_in, w_out, w_gate, w_ple)


def _ret_sample_proj_kernel(x_ref, cos_ref, sin_ref, gpre_ref, win_ref,
                            q_ref, k_ref, kd_ref, v_ref, g_ref, *, dec_len):
    h = _rms_norm(x_ref[...], gpre_ref[...]).astype(BF16)
    q, k, kd, v = _ret_project(h, cos_ref[...], sin_ref[...], win_ref, dec_len)

    def padded(a):
        n, w = a.shape
        a3 = a.reshape(n // dec_len, dec_len, w)
        return jnp.concatenate([a3, jnp.zeros_like(a3)], axis=1).reshape(2 * n, w).astype(BF16)

    q_ref[...] = padded(q)
    k_ref[...] = padded(k)
    kd_ref[...] = padded(kd)
    v_ref[...] = padded(v)
    g_ref[...] = _dot(h, win_ref[:, RET_GATE_COL0:])


def _ret_sample_proj(x, cos, sin, gpre, w_in, dec_len):
    n_tok, d = x.shape
    tq = min(RET_SAMPLE_TILE, n_tok)
    kernel = functools.partial(_ret_sample_proj_kernel, dec_len=dec_len)
    tok = lambda i: (i, 0)
    widths = (RET_QK_W, RET_QK_W, RET_QK_W, RET_VW)
    return pl.pallas_call(
        kernel,
        grid=(n_tok // tq,),
        in_specs=[
            pl.BlockSpec((tq, d), tok),
            _const_spec((tq, LANES)), _const_spec((tq, LANES)),
            _const_spec(gpre.shape), _const_spec(w_in.shape),
        ],
        out_specs=([pl.BlockSpec((2 * tq, w), tok) for w in widths]
                   + [pl.BlockSpec((tq, RET_VW), tok)]),
        out_shape=([jax.ShapeDtypeStruct((2 * n_tok, w), BF16) for w in widths]
                   + [jax.ShapeDtypeStruct((n_tok, RET_VW), F32)]),
        compiler_params=pltpu.CompilerParams(
            dimension_semantics=("arbitrary",),
            vmem_limit_bytes=VMEM_LIMIT_BYTES),
        name="ret_sample_proj",
    )(x, cos, sin, gpre, w_in)


def _ret_state_update_tile(q_ref, k_ref, kd_ref, v_ref, st_in, st_out, o_ref, *,
                           first_seq, n_seq, dec_len):
    pad_rows = 2 * dec_len
    idx = lax.broadcasted_iota(jnp.int32, (pad_rows, 1), 0).astype(F32)

    def tok_rows(s):
        return slice((first_seq + s) * dec_len, (first_seq + s + 1) * dec_len)

    def padded(ref, s, cols):
        return ref[(first_seq + s) * pad_rows:(first_seq + s + 1) * pad_rows, cols]

    units = [(s, hd) for s in range(n_seq) for hd in range(RET_HEADS)]
    qs = [slice(hd * RET_QK_DIM, (hd + 1) * RET_QK_DIM) for hd in range(RET_HEADS)]
    vs = [slice(hd * RET_V_DIM, (hd + 1) * RET_V_DIM) for hd in range(RET_HEADS)]
    qh = {(s, hd): padded(q_ref, s, qs[hd]) for s, hd in units}
    vh = {(s, hd): padded(v_ref, s, vs[hd]) for s, hd in units}
    scores = {(s, hd): _dot_nt(qh[s, hd], padded(k_ref, s, qs[hd])) for s, hd in units}
    cross = {(s, hd): _dot(qh[s, hd], st_in[s, hd].astype(BF16)) for s, hd in units}
    for s, hd in units:
        st_out[s, hd] = (math.exp(dec_len * RET_LOG_GAMMA[hd]) * st_in[s, hd]
                         + _dot_tn(padded(kd_ref, s, qs[hd]), vh[s, hd]))
    masks = [_ret_decay_mask(pad_rows, dec_len, hd) for hd in range(RET_HEADS)]
    for s, hd in units:
        inner = _dot((scores[s, hd] * masks[hd]).astype(BF16), vh[s, hd])
        out = inner + cross[s, hd] * jnp.exp((idx + 1.0) * RET_LOG_GAMMA[hd])
        o_ref[tok_rows(s), vs[hd]] = out[0:dec_len, :]


def _ret_sample_finish_kernel(o_ref, g_ref, x_ref, p_ref, gpost_ref, wout_ref, wgate_ref, wple_ref,
                              xo_ref):
    emb = _dot(p_ref[...].astype(BF16), wple_ref[...])
    x = x_ref[...]
    rows = _row_splits(x.shape[0], 2)
    ogs = [_ret_gated(o_ref[r, :], g_ref[r, :]) for r in rows]
    _out_proj_and_residual(rows, ogs, x, emb, gpost_ref[...], wout_ref, wgate_ref, xo_ref)


def _ret_sample_finish(layer, o, g, x, p_all, gpost, w_out, w_gate, w_ple):
    n_tok, d = x.shape
    tq = min(RET_SAMPLE_TILE, n_tok)
    ple = p_all.shape[-1]
    tok = lambda i: (i, 0)
    return pl.pallas_call(
        _ret_sample_finish_kernel,
        grid=(n_tok // tq,),
        in_specs=[
            pl.BlockSpec((tq, RET_VW), tok), pl.BlockSpec((tq, RET_VW), tok),
            pl.BlockSpec((tq, d), tok),
            pl.BlockSpec((None, tq, ple), lambda i: (layer, i, 0)),
            _const_spec(gpost.shape), _const_spec(w_out.shape),
            _const_spec(w_gate.shape), _const_spec(w_ple.shape),
        ],
        out_specs=pl.BlockSpec((tq, d), tok),
        out_shape=jax.ShapeDtypeStruct(x.shape, F32),
        compiler_params=pltpu.CompilerParams(
            dimension_semantics=("arbitrary",),
            vmem_limit_bytes=VMEM_LIMIT_BYTES),
        name="ret_sample_finish",
    )(o, g, x, p_all, gpost, w_out, w_gate, w_ple)


def _rope_tables(pos, head_dim):
    half = head_dim // 2
    inv = ROPE_THETA ** (-np.arange(half, dtype=np.float64) / half)
    ang = np.asarray(pos, np.float64)[:, None] * inv[None, :]
    return np.cos(ang).astype(np.float32), np.sin(ang).astype(np.float32)


def _attn_tables(pos):
    cos, sin = _rope_tables(pos, ATTN_HEAD_DIM)
    reps = LANES // ATTN_HEAD_DIM
    return (np.tile(np.concatenate([cos, cos], axis=1), (1, reps)),
            np.tile(np.concatenate([-sin, sin], axis=1), (1, reps)))


def kernel(x_prompt, x_sample, cache_k_win, cache_v_win, state_ret, p_prompt, p_sample, pre_norm,
           post_norm, w_in_attn, attn_sinks, w_out_attn, w_in_ret, w_out_ret, w_ple, w_ple_gate):
    depth = p_prompt.shape[0]
    nb, seq, d = x_prompt.shape
    n_dec, dec_len, _ = x_sample.shape
    n_stok = n_dec * dec_len
    assert seq % 512 == 0 or seq in (128, 256), seq

    pos_p = np.arange(seq)
    pos_s = PAST_LEN + np.arange(dec_len)
    cos_ap, sin_ap = _attn_tables(pos_p)
    cos_apt, sin_apt = (np.ascontiguousarray(a.T) for a in _rope_tables(pos_p, ATTN_HEAD_DIM))
    cos_as, sin_as = _attn_tables(pos_s)
    cos_rp, sin_rp = _rope_tables(pos_p, RET_QK_DIM)
    cos_rs, sin_rs = _rope_tables(pos_s, RET_QK_DIM)
    attn_s_tile = WINDOW // dec_len
    ret_s_tile = min(RET_SAMPLE_TILE, n_stok) // dec_len
    cos_as, sin_as = (np.tile(a, (attn_s_tile, 1)) for a in (cos_as, sin_as))
    cos_ast, sin_ast = (np.ascontiguousarray(np.tile(a, (attn_s_tile, 1)).T)
                        for a in _rope_tables(pos_s, ATTN_HEAD_DIM))
    cos_rs, sin_rs = (np.tile(a, (ret_s_tile, 1)) for a in (cos_rs, sin_rs))
    (cos_ap, sin_ap, cos_apt, sin_apt, cos_as, sin_as, cos_ast, sin_ast,
     cos_rp, sin_rp, cos_rs, sin_rs) = (
        jnp.asarray(a) for a in
        (cos_ap, sin_ap, cos_apt, sin_apt, cos_as, sin_as, cos_ast, sin_ast,
         cos_rp, sin_rp, cos_rs, sin_rs))

    assert depth == 2, depth
    xs = x_sample.reshape(n_stok, d)
    p_s = p_sample.reshape(depth, n_stok, p_sample.shape[-1])
    bf = lambda w: w.astype(BF16)
    gpre = [pre_norm[i][None, :] for i in range(depth)]
    gpost = [post_norm[i][None, :] for i in range(depth)]
    w_gate = [bf(w_ple_gate[i]) for i in range(depth)]
    w_pl = [bf(w_ple[i]) for i in range(depth)]

    w_in = bf(w_in_attn[0])
    w_out = bf(w_out_attn[0])
    sinks = attn_sinks[0]
    k0, v0, g0 = ATTN_Q_DIM, ATTN_Q_DIM + ATTN_KV_DIM, ATTN_Q_DIM + 2 * ATTN_KV_DIM
    w_q, w_k, w_v, w_g = w_in[:, :k0], w_in[:, k0:v0], w_in[:, v0:g0], w_in[:, g0:]
    to_t = lambda a: a.transpose(0, 2, 3, 1)
    from_t = lambda a: a.transpose(0, 3, 1, 2)
    xs, knt, vnt = _attn_sample_layer(
        0, xs, p_s, (cos_as, sin_as, cos_ast, sin_ast), gpre[0], gpost[0], sinks,
        w_q, w_k.T, w_v.T, w_g, w_out, w_gate[0], w_pl[0],
        to_t(cache_k_win[0]), to_t(cache_v_win[0]), dec_len)

    w_in_r = bf(w_in_ret[0])
    w_out_r = bf(w_out_ret[0])
    rq, rk, rkd, rv, rg = _ret_sample_proj(xs, cos_rs, sin_rs, gpre[1], w_in_r, dec_len)
    xp, kpt, vpt, ro, ret_state_sample = _attn_prompt_layer(
        0, x_prompt, p_prompt, (cos_ap, sin_ap, cos_apt, sin_apt), gpre[0], gpost[0], sinks,
        w_q.T, w_k, w_v.T, w_g, w_out, w_gate[0], w_pl[0],
        (rq, rk, rkd, rv), state_ret[0], dec_len)
    head_shape = (nb, ATTN_KV_HEADS, ATTN_HEAD_DIM, WINDOW)

    xp, ret_state_prompt = _ret_prompt_layer(1, xp, p_prompt, cos_rp, sin_rp, gpre[1], gpost[1],
                                             w_in_r, w_out_r, w_gate[1], w_pl[1])
    xs = _ret_sample_finish(1, ro, rg, xs, p_s, gpost[1], w_out_r, w_gate[1], w_pl[1])
    return (xp, xs.reshape(n_dec, dec_len, d),
            from_t(kpt.reshape(head_shape))[None], from_t(vpt.reshape(head_shape))[None],
            from_t(knt)[None], from_t(vnt)[None], ret_state_prompt[None], ret_state_sample[None])
```
